```python
import math
import jax, jax.numpy as jnp
from jax import lax
import numpy as np

D_MODEL = 1024
BATCH = 8
SEQ = 4096
DEPTH = 2

MIX_WIDTH = D_MODEL
GM_WIDTH = MIX_WIDTH // 2
RW_WIDTH = MIX_WIDTH - GM_WIDTH
HEAD_DIM = 64
GM_HEADS = GM_WIDTH // HEAD_DIM
RW_HEADS = RW_WIDTH // HEAD_DIM
CHUNK = 128
DECAY_RANK = 64
AAA_RANK = 64
GATE_RANK = 128
RW_SHIFT_WIDTH = 3 * RW_WIDTH + DECAY_RANK + AAA_RANK + GATE_RANK
IN_WIDTH = 2 * GM_WIDTH + RW_SHIFT_WIDTH
D_FF = 2816
N_EXPERTS = 8
TOP_K = 2
D_FF_EXPERT = 3584
MOE_BLOCK = 128
N_DENSE = (DEPTH + 1) // 2
N_MOE = DEPTH // 2
RMS_EPS = 1e-6
LN_EPS = 1e-5
GN_EPS = 64e-5

kernel_name = "hybrid_gmlp_rwkv7_moe_trunk"


def rmsnorm(x, g):
    xf = x.astype(jnp.float32)
    y = xf * lax.rsqrt(jnp.mean(xf * xf, axis=-1, keepdims=True) + RMS_EPS)
    return (y * g.astype(jnp.float32)).astype(x.dtype)


def head_layernorm(x, w, b, eps):
    h, n = x.shape[-2], x.shape[-1]
    xf = x.astype(jnp.float32)
    mu = jnp.mean(xf, axis=-1, keepdims=True)
    xc = xf - mu
    var = jnp.mean(xc * xc, axis=-1, keepdims=True)
    y = xc * lax.rsqrt(var + eps)
    y = y * w.astype(jnp.float32).reshape(h, n) + b.astype(jnp.float32).reshape(h, n)
    return y.astype(x.dtype)


def token_shift_mix(p, mu):
    prev = jnp.pad(p, ((0, 0), (1, 0), (0, 0)))[:, :-1]
    return p + (prev - p) * mu


def gmlp_group(pu, pv, ln_w, ln_b, ws, bs):
    b, t, _ = pu.shape
    u = jax.nn.gelu(pu, approximate=False)
    v = jax.nn.gelu(pv, approximate=False)
    v = v.reshape(b, t // CHUNK, CHUNK, GM_HEADS, HEAD_DIM)
    v = head_layernorm(v, ln_w, ln_b, LN_EPS)
    causal = jnp.tril(jnp.ones((CHUNK, CHUNK), dtype=bool))
    w_masked = jnp.where(causal[None], ws, jnp.zeros((), ws.dtype))
    mixed = jnp.einsum('hts,bcshn->bcthn', w_masked, v)
    mixed = mixed + bs.T[None, None, :, :, None]
    return u * mixed.reshape(b, t, GM_WIDTH)


def wkv7_scan(r, w, k, v, a, bb):
    b, t, h, n = r.shape
    xs = tuple(jnp.moveaxis(z, 1, 0) for z in (r, w, k, v, a, bb))

    def step(S, inp):
        r_t, w_t, k_t, v_t, a_t, b_t = inp
        sa = jnp.einsum('bhvk,bhk->bhv', S, a_t)
        S = S * w_t[:, :, None, :] + sa[..., None] * b_t[:, :, None, :] + v_t[..., None] * k_t[:, :, None, :]
        y_t = jnp.einsum('bhvk,bhk->bhv', S, r_t)
        return S, y_t

    S0 = jnp.zeros((b, h, n, n), jnp.float32)
    _, ys = lax.scan(step, S0, xs)
    return jnp.moveaxis(ys, 0, 1)


def rwkv7_group(p, mu, w_up, w0, a_up, a0, g_up, k_k, k_a, r_k, lnx_w, lnx_b):
    b, t, _ = p.shape
    dt = p.dtype
    p = token_shift_mix(p, mu)
    i1, i2, i3 = RW_WIDTH, 2 * RW_WIDTH, 3 * RW_WIDTH
    i4, i5 = i3 + DECAY_RANK, i3 + DECAY_RANK + AAA_RANK
    r, k, v = p[..., :i1], p[..., i1:i2], p[..., i2:i3]
    wd, ad, gd = p[..., i3:i4], p[..., i4:i5], p[..., i5:]
    log_w = -jax.nn.softplus(-(w0 + jnp.tanh(wd) @ w_up)) - 0.5
    decay = jnp.exp(-jnp.exp(log_w.astype(jnp.float32)))
    a = jax.nn.sigmoid(a0 + ad @ a_up)
    g = jax.nn.sigmoid(gd) @ g_up
    heads = lambda z: z.reshape(b, t, RW_HEADS, HEAD_DIM).astype(jnp.float32)
    kk = heads(k * k_k)
    kk = kk / jnp.maximum(jnp.sqrt(jnp.sum(kk * kk, axis=-1, keepdims=True)), 1e-12)
    k = k * (1.0 + (a - 1.0) * k_a)
    rh, kh, vh, ah, wh = heads(r), heads(k), heads(v), heads(a), decay.reshape(b, t, RW_HEADS, HEAD_DIM)
    y = wkv7_scan(rh, wh, kh, vh, -kk, kk * ah)
    y = head_layernorm(y, lnx_w, lnx_b, GN_EPS)
    bonus = jnp.sum(rh * kh * r_k.astype(jnp.float32), axis=-1, keepdims=True) * vh
    y = (y + bonus).reshape(b, t, RW_WIDTH).astype(dt)
    return y * g


def hybrid_mixer(h, w_in, w_out, shift_mu, gm_ln_w, gm_ln_b, gm_ws, gm_bs,
                 rw_w_up, rw_w0, rw_a_up, rw_a0, rw_g_up, rw_k_k, rw_k_a, rw_r_k, rw_lnx_w, rw_lnx_b):
    p = h @ w_in
    y_gm = gmlp_group(p[..., :GM_WIDTH], p[..., GM_WIDTH:2 * GM_WIDTH], gm_ln_w, gm_ln_b, gm_ws, gm_bs)
    y_rw = rwkv7_group(p[..., 2 * GM_WIDTH:], shift_mu, rw_w_up, rw_w0, rw_a_up, rw_a0, rw_g_up,
                       rw_k_k, rw_k_a, rw_r_k, rw_lnx_w, rw_lnx_b)
    y = jnp.concatenate([y_gm, y_rw], axis=-1)
    return y @ w_out


def swiglu(h, wg, wu, wd):
    return (jax.nn.silu(h @ wg) * (h @ wu)) @ wd


def moe_swiglu(xf, router, wg, wu, wd):
    n, d = xf.shape
    logits = (xf @ router).astype(jnp.float32)
    top_val, top_idx = lax.top_k(logits, TOP_K)
    gate = jax.nn.softmax(top_val, axis=-1)
    n_assign = n * TOP_K
    n_rows = ((n_assign + MOE_BLOCK - 1) // MOE_BLOCK + N_EXPERTS) * MOE_BLOCK
    flat_e = top_idx.reshape(-1).astype(jnp.int32)
    flat_tok = jnp.repeat(jnp.arange(n, dtype=jnp.int32), TOP_K)
    flat_gate = gate.reshape(-1)
    order = jnp.argsort(flat_e, stable=True)
    e_sorted = flat_e[order]
    counts = jnp.bincount(flat_e, length=N_EXPERTS)
    padded = (counts + MOE_BLOCK - 1) // MOE_BLOCK * MOE_BLOCK
    pad_end = jnp.cumsum(padded)
    pad_start = pad_end - padded
    grp_start = jnp.cumsum(counts) - counts
    rank = jnp.arange(n_assign, dtype=jnp.int32) - grp_start[e_sorted]
    dest = pad_start[e_sorted] + rank
    row_tok = jnp.full((n_rows,), n, jnp.int32).at[dest].set(flat_tok[order])
    row_gate = jnp.zeros((n_rows,), xf.dtype).at[dest].set(flat_gate[order].astype(xf.dtype))
    block_start = jnp.arange(n_rows // MOE_BLOCK, dtype=jnp.int32) * MOE_BLOCK
    block_expert = jnp.minimum(jnp.searchsorted(pad_end, block_start, side='right'), N_EXPERTS - 1)
    x_pad = jnp.concatenate([xf, jnp.zeros((1, d), xf.dtype)], axis=0)
    xb = x_pad[row_tok].reshape(-1, MOE_BLOCK, d)

    def expert_block(args):
        xblk, e = args
        return (jax.nn.silu(xblk @ wg[e]) * (xblk @ wu[e])) @ wd[e]

    ys = lax.map(expert_block, (xb, block_expert)).reshape(n_rows, d) * row_gate[:, None]
    return jnp.zeros_like(x_pad).at[row_tok].add(ys)[:n]


def setup_inputs(seed: int = 0) -> dict:
    key = jax.random.key(seed)
    ks = jax.random.split(key, 32)
    f32 = jnp.float32
    nrm = lambda k, shape, s: jax.random.normal(k, shape, f32) * s
    L = DEPTH
    return {
        "x": nrm(ks[0], (BATCH, SEQ, D_MODEL), 1.0),
        "norm_mix": 1.0 + nrm(ks[1], (L, D_MODEL), 0.1),
        "w_in": nrm(ks[2], (L, D_MODEL, IN_WIDTH), D_MODEL ** -0.5),
        "w_out": nrm(ks[3], (L, MIX_WIDTH, D_MODEL), MIX_WIDTH ** -0.5),
        "shift_mu": jax.random.uniform(ks[4], (L, RW_SHIFT_WIDTH), f32),
        "gm_ln_w": 1.0 + nrm(ks[5], (L, GM_WIDTH), 0.1),
        "gm_ln_b": nrm(ks[6], (L, GM_WIDTH), 0.1),
        "gm_ws": nrm(ks[7], (L, GM_HEADS, CHUNK, CHUNK), CHUNK ** -0.5),
        "gm_bs": 1.0 + nrm(ks[8], (L, GM_HEADS, CHUNK), 0.1),
        "rw_w_up": nrm(ks[9], (L, DECAY_RANK, RW_WIDTH), 0.1),
        "rw_w0": jax.random.uniform(ks[10], (L, RW_WIDTH), f32, -6.0, -1.0),
        "rw_a_up": nrm(ks[11], (L, AAA_RANK, RW_WIDTH), AAA_RANK ** -0.5),
        "rw_a0": nrm(ks[12], (L, RW_WIDTH), 0.1),
        "rw_g_up": nrm(ks[13], (L, GATE_RANK, RW_WIDTH), GATE_RANK ** -0.5),
        "rw_k_k": 0.85 + nrm(ks[14], (L, RW_WIDTH), 0.1),
        "rw_k_a": 1.0 + nrm(ks[15], (L, RW_WIDTH), 0.1),
        "rw_r_k": nrm(ks[16], (L, RW_HEADS, HEAD_DIM), 0.5),
        "rw_lnx_w": 1.0 + nrm(ks[17], (L, RW_WIDTH), 0.1),
        "rw_lnx_b": nrm(ks[18], (L, RW_WIDTH), 0.1),
        "norm_ffn": 1.0 + nrm(ks[19], (L, D_MODEL), 0.1),
        "ffn_w_gate": nrm(ks[20], (N_DENSE, D_MODEL, D_FF), D_MODEL ** -0.5),
        "ffn_w_up": nrm(ks[21], (N_DENSE, D_MODEL, D_FF), D_MODEL ** -0.5),
        "ffn_w_down": nrm(ks[22], (N_DENSE, D_FF, D_MODEL), D_FF ** -0.5),
        "moe_router": nrm(ks[23], (N_MOE, D_MODEL, N_EXPERTS), D_MODEL ** -0.5),
        "moe_w_gate": nrm(ks[24], (N_MOE, N_EXPERTS, D_MODEL, D_FF_EXPERT), D_MODEL ** -0.5),
        "moe_w_up": nrm(ks[25], (N_MOE, N_EXPERTS, D_MODEL, D_FF_EXPERT), D_MODEL ** -0.5),
        "moe_w_down": nrm(ks[26], (N_MOE, N_EXPERTS, D_FF_EXPERT, D_MODEL), D_FF_EXPERT ** -0.5),
        "norm_final": 1.0 + nrm(ks[27], (D_MODEL,), 0.1),
    }


def reference(x, norm_mix, w_in, w_out, shift_mu, gm_ln_w, gm_ln_b, gm_ws, gm_bs,
              rw_w_up, rw_w0, rw_a_up, rw_a0, rw_g_up, rw_k_k, rw_k_a, rw_r_k, rw_lnx_w, rw_lnx_b,
              norm_ffn, ffn_w_gate, ffn_w_up, ffn_w_down,
              moe_router, moe_w_gate, moe_w_up, moe_w_down, norm_final):
    b, t, d = x.shape
    for i in range(DEPTH):
        h = rmsnorm(x, norm_mix[i])
        x = x + hybrid_mixer(h, w_in[i], w_out[i], shift_mu[i], gm_ln_w[i], gm_ln_b[i], gm_ws[i], gm_bs[i],
                             rw_w_up[i], rw_w0[i], rw_a_up[i], rw_a0[i], rw_g_up[i], rw_k_k[i], rw_k_a[i],
                             rw_r_k[i], rw_lnx_w[i], rw_lnx_b[i])
        h = rmsnorm(x, norm_ffn[i])
        j = i // 2
        if i % 2 == 0:
            x = x + swiglu(h, ffn_w_gate[j], ffn_w_up[j], ffn_w_down[j])
        else:
            y = moe_swiglu(h.reshape(b * t, d), moe_router[j], moe_w_gate[j], moe_w_up[j], moe_w_down[j])
            x = x + y.reshape(b, t, d)
    return rmsnorm(x, norm_final)
```

```python
import functools

import jax
import jax.numpy as jnp
from jax import lax
from jax.experimental import pallas as pl
from jax.experimental.pallas import tpu as pltpu

F32 = jnp.float32
BF16 = jnp.bfloat16

HEAD_DIM = 64
LANES = 128
GM_CHUNK = 128
WKV_CHUNK = 64
N_EXPERTS = 8
RMS_EPS = 1e-6
LN_EPS = 1e-5
GN_EPS = 64e-5
NEG_BIG = -1e30

ROW_TILE = 512
WKV_STEP = 128
MOE_BLOCK = 512
GATHER_ROWS = 1024
VMEM_LIMIT = 56 * 1024 * 1024


def _cparams(*sem):
    return pltpu.CompilerParams(dimension_semantics=sem, vmem_limit_bytes=VMEM_LIMIT)


def _bf(x):
    return x.astype(BF16)


def _dot(a, b):
    return jnp.dot(a, b, preferred_element_type=F32)


def _dot_nt(a, b):
    return lax.dot_general(a, b, (((1,), (1,)), ((), ())), preferred_element_type=F32)


def _dot_tn(a, b):
    return lax.dot_general(a, b, (((0,), (0,)), ((), ())), preferred_element_type=F32)


def _split_dot(x, m, parts=3):
    acc = None
    rem = x
    for p in range(parts):
        hi = _bf(rem)
        d = _dot(hi, m)
        acc = d if acc is None else acc + d
        if p + 1 < parts:
            rem = rem - hi.astype(F32)
    return acc


def _split_dot_left(m, x, parts=3):
    acc = None
    rem = x
    for p in range(parts):
        hi = _bf(rem)
        d = _dot(m, hi)
        acc = d if acc is None else acc + d
        if p + 1 < parts:
            rem = rem - hi.astype(F32)
    return acc


def _rmsnorm(x, g):
    return x * lax.rsqrt(jnp.mean(x * x, axis=-1, keepdims=True) + RMS_EPS) * g


def _gelu(x):
    return 0.5 * x * (1.0 + lax.erf(x * (2.0 ** -0.5)))


def _head_avg_matrix(width):
    r = lax.broadcasted_iota(jnp.int32, (width, width), 0) // HEAD_DIM
    c = lax.broadcasted_iota(jnp.int32, (width, width), 1) // HEAD_DIM
    return jnp.where(r == c, 1.0 / HEAD_DIM, 0.0).astype(BF16)


def _head_layernorm(x, avg, w, b, eps):
    mu = _split_dot(x, avg)
    xc = x - mu
    var = _split_dot(xc * xc, avg)
    return xc * lax.rsqrt(var + eps) * w + b


def _norm_inproj_kernel(x_ref, g_ref, wgm_ref, wrw_ref, pgm_ref, prw_ref):
    h = _bf(_rmsnorm(x_ref[...], g_ref[...]))
    pgm_ref[...] = _dot(h, wgm_ref[...])
    prw_ref[...] = _dot(h, wrw_ref[...])


def _norm_inproj(x2, gain, w_gm, w_rw):
    n, d = x2.shape
    wg, wr = w_gm.shape[1], w_rw.shape[1]
    return pl.pallas_call(
        _norm_inproj_kernel,
        grid=(n // ROW_TILE,),
        in_specs=[pl.BlockSpec((ROW_TILE, d), lambda i: (i, 0)),
                  pl.BlockSpec((1, d), lambda i: (0, 0)),
                  pl.BlockSpec((d, wg), lambda i: (0, 0)),
                  pl.BlockSpec((d, wr), lambda i: (0, 0))],
        out_specs=[pl.BlockSpec((ROW_TILE, wg), lambda i: (i, 0)),
                   pl.BlockSpec((ROW_TILE, wr), lambda i: (i, 0))],
        out_shape=[jax.ShapeDtypeStruct((n, wg), F32), jax.ShapeDtypeStruct((n, wr), F32)],
        compiler_params=_cparams("parallel"),
        name="norm_inproj",
    )(x2, gain.reshape(1, d), w_gm, w_rw)


def _gmlp_kernel(p_ref, lnw_ref, lnb_ref, ws_ref, bias_ref, o_ref):
    width = o_ref.shape[1]
    p = p_ref[...]
    u = _gelu(p[:, :width])
    v = _gelu(p[:, width:])
    vn = _head_layernorm(v, _head_avg_matrix(width), lnw_ref[...], lnb_ref[...], LN_EPS)
    t = lax.broadcasted_iota(jnp.int32, (GM_CHUNK, GM_CHUNK), 0)
    s = lax.broadcasted_iota(jnp.int32, (GM_CHUNK, GM_CHUNK), 1)
    causal = s <= t
    head0 = lax.broadcasted_iota(jnp.int32, (GM_CHUNK, LANES), 1) < HEAD_DIM
    mixed = []
    for pair in range(width // LANES):
        vp = vn[:, pair * LANES:(pair + 1) * LANES]
        v_st = _bf(jnp.concatenate([jnp.where(head0, vp, 0.0), jnp.where(head0, 0.0, vp)], axis=0))
        w2 = _bf(jnp.concatenate([jnp.where(causal, ws_ref[2 * pair], 0.0),
                                  jnp.where(causal, ws_ref[2 * pair + 1], 0.0)], axis=1))
        mixed.append(_dot(w2, v_st))
    o_ref[...] = u * (jnp.concatenate(mixed, axis=1) + bias_ref[...])


def _gmlp(p_gm, ln_w, ln_b, ws, bs):
    n, w2 = p_gm.shape
    width = w2 // 2
    bias = jnp.repeat(bs.T, HEAD_DIM, axis=1)
    return pl.pallas_call(
        _gmlp_kernel,
        grid=(n // GM_CHUNK,),
        in_specs=[pl.BlockSpec((GM_CHUNK, w2), lambda i: (i, 0)),
                  pl.BlockSpec((1, width), lambda i: (0, 0)),
                  pl.BlockSpec((1, width), lambda i: (0, 0)),
                  pl.BlockSpec(ws.shape, lambda i: (0, 0, 0)),
                  pl.BlockSpec((GM_CHUNK, width), lambda i: (0, 0))],
        out_specs=pl.BlockSpec((GM_CHUNK, width), lambda i: (i, 0)),
        out_shape=jax.ShapeDtypeStruct((n, width), F32),
        compiler_params=_cparams("parallel"),
        name="gmlp",
    )(p_gm, ln_w.reshape(1, width), ln_b.reshape(1, width), ws, bias)


def _rwkv_prep_kernel(p_ref, mu_ref, wup_ref, w0_ref, aup_ref, a0_ref, gup_ref, kk_ref, ka_ref, rk_ref,
                      r_ref, lw_ref, k_ref, v_ref, a_ref, b_ref, g_ref, bonus_ref, carry_ref):
    width = r_ref.shape[2]

    @pl.when(pl.program_id(1) == 0)
    def _():
        carry_ref[...] = jnp.zeros_like(carry_ref)

    p = p_ref[0]
    rows = p.shape[0]
    first = lax.broadcasted_iota(jnp.int32, p.shape, 0) == 0
    prev = jnp.where(first, carry_ref[0:1, :], pltpu.roll(p, 1, axis=0))
    carry_ref[0:1, :] = p[rows - 1:rows, :]
    ps = p + (prev - p) * mu_ref[...]
    r = ps[:, :width]
    k = ps[:, width:2 * width]
    v = ps[:, 2 * width:3 * width]
    lora_in = ps[:, 3 * width:3 * width + LANES]
    gd = ps[:, 3 * width + LANES:]
    log_w = -jax.nn.softplus(-(w0_ref[...] + _dot(_bf(jnp.tanh(lora_in)), wup_ref[...]))) - 0.5
    lw = -jnp.exp(log_w)
    a_lr = jax.nn.sigmoid(a0_ref[...] + _dot(_bf(lora_in), aup_ref[...]))
    g = _dot(_bf(jax.nn.sigmoid(gd)), gup_ref[...])
    ones = _head_avg_matrix(width) * HEAD_DIM
    kk = k * kk_ref[...]
    kk = kk / jnp.maximum(jnp.sqrt(_split_dot(kk * kk, ones)), 1e-12)
    kmod = k * (1.0 + (a_lr - 1.0) * ka_ref[...])
    r_ref[0] = r
    lw_ref[0] = lw
    k_ref[0] = kmod
    v_ref[0] = v
    a_ref[0] = -kk
    b_ref[0] = kk * a_lr
    g_ref[0] = g
    bonus_ref[0] = _split_dot(r * kmod * rk_ref[...], ones) * v


def _rwkv_prep(p_rw, batch, mu, w_up, w0, a_up, a0, g_up, k_k, k_a, r_k):
    n, win = p_rw.shape
    t = n // batch
    width = w0.shape[0]
    rank = w_up.shape[0]
    zeros = jnp.zeros((LANES - rank, width), F32)
    wup_pad = _bf(jnp.concatenate([w_up, zeros], axis=0))
    aup_pad = _bf(jnp.concatenate([zeros, a_up], axis=0))
    row = lambda z: z.reshape(1, -1)
    vec = pl.BlockSpec((1, width), lambda b, i: (0, 0))
    out = pl.BlockSpec((1, ROW_TILE, width), lambda b, i: (b, i, 0))
    return pl.pallas_call(
        _rwkv_prep_kernel,
        grid=(batch, t // ROW_TILE),
        in_specs=[pl.BlockSpec((1, ROW_TILE, win), lambda b, i: (b, i, 0)),
                  pl.BlockSpec((1, win), lambda b, i: (0, 0)),
                  pl.BlockSpec((LANES, width), lambda b, i: (0, 0)), vec,
                  pl.BlockSpec((LANES, width), lambda b, i: (0, 0)), vec,
                  pl.BlockSpec(g_up.shape, lambda b, i: (0, 0)), vec, vec, vec],
        out_specs=[out] * 8,
        out_shape=[jax.ShapeDtypeStruct((batch, t, width), F32)] * 8,
        scratch_shapes=[pltpu.VMEM((8, win), F32)],
        compiler_params=_cparams("parallel", "arbitrary"),
        name="rwkv_prep",
    )(p_rw.reshape(batch, t, win), row(mu), wup_pad, row(w0), aup_pad, row(a0), _bf(g_up),
      row(k_k), row(k_a), row(r_k))


def _wkv_chunk_pair(r, lw, k, v, a, b, h, consts):
    tri, head0, strict, incl, eye = consts
    n = WKV_CHUNK
    c = _split_dot_left(tri, lw)
    cex = c - lw
    mid = c[n // 2 - 1:n // 2, :]
    last = c[n - 1:n, :]
    g_inv = jnp.exp(mid - c)
    r_t = r * jnp.exp(c - mid)
    a_t = a * jnp.exp(cex - mid)
    k_t = k * g_inv
    b_t = b * g_inv
    r_h = r * jnp.exp(c)
    a_h = a * jnp.exp(cex)
    g_end = jnp.exp(last - c)
    b_c = b * g_end
    k_c = k * g_end

    def stack(z):
        return _bf(jnp.concatenate([jnp.where(head0, z, 0.0), jnp.where(head0, 0.0, z)], axis=0))

    def dup(z):
        zb = _bf(z)
        return jnp.concatenate([zb, zb], axis=0)

    a_st, r_st = stack(a_t), stack(r_t)
    b_dup, k_dup = dup(b_t), dup(k_t)
    nil = jnp.where(strict, _dot_nt(a_st, b_dup), 0.0)
    a_k = jnp.where(strict, _dot_nt(a_st, k_dup), 0.0)
    r_b = jnp.where(incl, _dot_nt(r_st, b_dup), 0.0)
    r_k = jnp.where(incl, _dot_nt(r_st, k_dup), 0.0)
    v_st = stack(v)
    y = jnp.concatenate([stack(a_h).astype(F32), _dot(_bf(a_k), v_st)], axis=1)
    pw = _bf(nil)
    steps = n.bit_length() - 1
    for i in range(steps):
        y = y + _dot(pw, _bf(y))
        if i + 1 < steps:
            pw = _bf(_dot(pw, pw))
    w = y[:, :LANES]
    u0 = y[:, LANES:]
    hb = _bf(h)
    gmat = _dot(jnp.concatenate([_bf(w), stack(r_h)], axis=0), hb)
    u = _bf(gmat[:2 * n] + u0)
    y_st = gmat[2 * n:] + _dot(_bf(r_b), u) + _dot(_bf(r_k), v_st)
    out = y_st[:n] + y_st[n:]
    e_col = jnp.sum(jnp.where(eye, jnp.exp(last), 0.0), axis=1, keepdims=True)
    h_new = e_col * h + _dot_tn(stack(b_c), u) + _dot_tn(stack(k_c), v_st)
    return out, h_new


def _wkv_kernel(r_ref, lw_ref, k_ref, v_ref, a_ref, b_ref, y_ref, h_ref):
    @pl.when(pl.program_id(1) == 0)
    def _():
        h_ref[...] = jnp.zeros_like(h_ref)

    n = WKV_CHUNK
    two = 2 * n
    tri = (lax.broadcasted_iota(jnp.int32, (n, n), 1) <= lax.broadcasted_iota(jnp.int32, (n, n), 0)).astype(BF16)
    head0 = lax.broadcasted_iota(jnp.int32, (n, LANES), 1) < HEAD_DIM
    row = lax.broadcasted_iota(jnp.int32, (two, two), 0)
    col = lax.broadcasted_iota(jnp.int32, (two, two), 1)
    same = (row // n) == (col // n)
    strict = same & ((col % n) < (row % n))
    incl = same & ((col % n) <= (row % n))
    eye = row == col
    consts = (tri, head0, strict, incl, eye)
    width = y_ref.shape[2]
    for pair in range(width // LANES):
        ls = slice(pair * LANES, (pair + 1) * LANES)
        h = h_ref[pair]
        for ch in range(r_ref.shape[1] // n):
            ts = slice(ch * n, (ch + 1) * n)
            y, h = _wkv_chunk_pair(r_ref[0, ts, ls], lw_ref[0, ts, ls], k_ref[0, ts, ls], v_ref[0, ts, ls],
                                   a_ref[0, ts, ls], b_ref[0, ts, ls], h, consts)
            y_ref[0, ts, ls] = y
        h_ref[pair] = h


def _wkv(r, lw, k, v, a, b):
    batch, t, width = r.shape
    spec = pl.BlockSpec((1, WKV_STEP, width), lambda bi, i: (bi, i, 0))
    return pl.pallas_call(
        _wkv_kernel,
        grid=(batch, t // WKV_STEP),
        in_specs=[spec] * 6,
        out_specs=spec,
        out_shape=jax.ShapeDtypeStruct((batch, t, width), F32),
        scratch_shapes=[pltpu.VMEM((width // LANES, LANES, LANES), F32)],
        compiler_params=_cparams("parallel", "arbitrary"),
        name="wkv7",
    )(r, lw, k, v, a, b)


def _post_outproj_kernel(x_ref, ygm_ref, ys_ref, bonus_ref, g_ref, lnw_ref, lnb_ref, wtop_ref, wbot_ref, o_ref):
    width = ys_ref.shape[1]
    yn = _head_layernorm(ys_ref[...], _head_avg_matrix(width), lnw_ref[...], lnb_ref[...], GN_EPS)
    y_rw = (yn + bonus_ref[...]) * g_ref[...]
    o_ref[...] = x_ref[...] + _dot(_bf(ygm_ref[...]), wtop_ref[...]) + _dot(_bf(y_rw), wbot_ref[...])


def _post_outproj(x2, y_gm, y_scan, bonus, g, lnx_w, lnx_b, w_out):
    n, d = x2.shape
    width = y_gm.shape[1]
    rows = lambda w: pl.BlockSpec((ROW_TILE, w), lambda i: (i, 0))
    vec = pl.BlockSpec((1, width), lambda i: (0, 0))
    wspec = pl.BlockSpec((width, d), lambda i: (0, 0))
    w_out = _bf(w_out)
    return pl.pallas_call(
        _post_outproj_kernel,
        grid=(n // ROW_TILE,),
        in_specs=[rows(d), rows(width), rows(width), rows(width), rows(width), vec, vec, wspec, wspec],
        out_specs=rows(d),
        out_shape=jax.ShapeDtypeStruct((n, d), F32),
        compiler_params=_cparams("parallel"),
        name="post_outproj",
    )(x2, y_gm, y_scan, bonus, g, lnx_w.reshape(1, width), lnx_b.reshape(1, width),
      w_out[:width], w_out[width:])


def _swiglu_kernel(x_ref, g_ref, wg_ref, wu_ref, wd_ref, o_ref, h_ref):
    j = pl.program_id(1)

    @pl.when(j == 0)
    def _():
        x = x_ref[...]
        h_ref[...] = _bf(_rmsnorm(x, g_ref[...]))
        o_ref[...] = x

    h = h_ref[...]
    act = jax.nn.silu(_dot(h, wg_ref[...])) * _dot(h, wu_ref[...])
    o_ref[...] += _dot(_bf(act), wd_ref[...])


def _ffn_tile(d_ff, parts):
    assert d_ff % (parts * LANES) == 0, f"d_ff={d_ff} does not split into {parts} lane-aligned tiles"
    return d_ff // parts


def _swiglu(x2, gain, wg, wu, wd):
    n, d = x2.shape
    d_ff = wg.shape[1]
    tf = _ffn_tile(d_ff, 2)
    return pl.pallas_call(
        _swiglu_kernel,
        grid=(n // ROW_TILE, d_ff // tf),
        in_specs=[pl.BlockSpec((ROW_TILE, d), lambda i, j: (i, 0)),
                  pl.BlockSpec((1, d), lambda i, j: (0, 0)),
                  pl.BlockSpec((d, tf), lambda i, j: (0, j)),
                  pl.BlockSpec((d, tf), lambda i, j: (0, j)),
                  pl.BlockSpec((tf, d), lambda i, j: (j, 0))],
        out_specs=pl.BlockSpec((ROW_TILE, d), lambda i, j: (i, 0)),
        out_shape=jax.ShapeDtypeStruct((n, d), F32),
        scratch_shapes=[pltpu.VMEM((ROW_TILE, d), BF16)],
        compiler_params=_cparams("parallel", "arbitrary"),
        name="dense_swiglu",
    )(x2, gain.reshape(1, d), _bf(wg), _bf(wu), _bf(wd))


META_E, META_G, META_RANK = 0, 2, 4


def _router_kernel(x_ref, g_ref, wr_ref, h_ref, meta_ref, cnt_ref, run_ref):
    @pl.when(pl.program_id(0) == 0)
    def _():
        run_ref[...] = jnp.zeros_like(run_ref)

    h = _rmsnorm(x_ref[...], g_ref[...])
    h_ref[...] = h
    rows = h.shape[0]
    lane = lax.broadcasted_iota(jnp.int32, (rows, LANES), 1)
    logits = jnp.dot(h, wr_ref[...], preferred_element_type=F32, precision=lax.Precision.HIGHEST)
    logits = jnp.where(lane < N_EXPERTS, logits, NEG_BIG)
    v1 = jnp.max(logits, axis=-1, keepdims=True)
    e1 = jnp.min(jnp.where(logits == v1, lane, LANES), axis=-1, keepdims=True)
    oh1 = lane == e1
    rest = jnp.where(oh1, NEG_BIG, logits)
    v2 = jnp.max(rest, axis=-1, keepdims=True)
    e2 = jnp.min(jnp.where(rest == v2, lane, LANES), axis=-1, keepdims=True)
    oh2 = lane == e2
    ex = jnp.exp(v2 - v1)
    g1 = 1.0 / (1.0 + ex)
    g2 = ex / (1.0 + ex)
    cnt = jnp.where(oh1 | oh2, 1.0, 0.0)
    t = lax.broadcasted_iota(jnp.int32, (rows, rows), 0)
    s = lax.broadcasted_iota(jnp.int32, (rows, rows), 1)
    before = _dot(jnp.where(s < t, 1.0, 0.0).astype(BF16), _bf(cnt)) + run_ref[0:1, :]
    rank1 = jnp.sum(jnp.where(oh1, before, 0.0), axis=-1, keepdims=True)
    rank2 = jnp.sum(jnp.where(oh2, before, 0.0), axis=-1, keepdims=True)
    fields = (e1.astype(F32), e2.astype(F32), g1, g2, rank1, rank2)
    meta = jnp.zeros((rows, LANES), F32)
    for idx, val in enumerate(fields):
        meta = jnp.where(lane == idx, val, meta)
    meta_ref[...] = meta
    run = run_ref[0:1, :] + jnp.sum(cnt, axis=0, keepdims=True)
    run_ref[...] = jnp.broadcast_to(run, run_ref.shape)
    cnt_ref[...] = jnp.broadcast_to(run, cnt_ref.shape)


def _router(x2, gain, w_router):
    n, d = x2.shape
    wr = jnp.concatenate([w_router, jnp.zeros((d, LANES - w_router.shape[1]), F32)], axis=1)
    return pl.pallas_call(
        _router_kernel,
        grid=(n // ROW_TILE,),
        in_specs=[pl.BlockSpec((ROW_TILE, d), lambda i: (i, 0)),
                  pl.BlockSpec((1, d), lambda i: (0, 0)),
                  pl.BlockSpec((d, LANES), lambda i: (0, 0))],
        out_specs=[pl.BlockSpec((ROW_TILE, d), lambda i: (i, 0)),
                   pl.BlockSpec((ROW_TILE, LANES), lambda i: (i, 0)),
                   pl.BlockSpec((8, LANES), lambda i: (0, 0))],
        out_shape=[jax.ShapeDtypeStruct((n, d), F32),
                   jax.ShapeDtypeStruct((n, LANES), F32),
                   jax.ShapeDtypeStruct((8, LANES), F32)],
        scratch_shapes=[pltpu.VMEM((8, LANES), F32)],
        compiler_params=_cparams("arbitrary"),
        name="moe_router",
    )(x2, gain.reshape(1, d), wr)


def _gather_kernel(idx_ref, src_ref, out_ref, sem):
    base = pl.program_id(0) * GATHER_ROWS

    def issue(j, carry):
        pltpu.make_async_copy(src_ref.at[pl.ds(idx_ref[0, 0, j], 1)], out_ref.at[pl.ds(base + j, 1)], sem).start()
        return carry

    lax.fori_loop(0, GATHER_ROWS, issue, 0)
    pltpu.make_async_copy(src_ref.at[pl.ds(0, GATHER_ROWS)], out_ref.at[pl.ds(base, GATHER_ROWS)], sem).wait()


def _gather_rows(src, idx):
    m = idx.shape[0]
    d = src.shape[1]
    assert src.shape[0] >= GATHER_ROWS and m % GATHER_ROWS == 0
    nblk = m // GATHER_ROWS
    return pl.pallas_call(
        _gather_kernel,
        grid=(nblk,),
        in_specs=[pl.BlockSpec((1, 1, GATHER_ROWS), lambda i: (i, 0, 0), memory_space=pltpu.SMEM),
                  pl.BlockSpec(memory_space=pl.ANY)],
        out_specs=pl.BlockSpec(memory_space=pl.ANY),
        out_shape=jax.ShapeDtypeStruct((m, d), src.dtype),
        scratch_shapes=[pltpu.SemaphoreType.DMA(())],
        compiler_params=pltpu.CompilerParams(dimension_semantics=("arbitrary",)),
        name="gather_rows",
    )(idx.reshape(nblk, 1, GATHER_ROWS), src)


def _expert_kernel(be_ref, nu_ref, x_ref, wg_ref, wu_ref, wd_ref, o_ref, h_ref):
    i, j = pl.program_id(0), pl.program_id(1)

    @pl.when(i < nu_ref[0])
    def _():
        @pl.when(j == 0)
        def _():
            h_ref[...] = _bf(x_ref[...])
            o_ref[...] = jnp.zeros_like(o_ref)

        h = h_ref[...]
        act = jax.nn.silu(_dot(h, wg_ref[0])) * _dot(h, wu_ref[0])
        o_ref[...] += _dot(_bf(act), wd_ref[0])

    @pl.when((i >= nu_ref[0]) & (j == 0))
    def _():
        o_ref[...] = jnp.zeros_like(o_ref)


def _expert_swiglu(xb, block_expert, n_used, wg, wu, wd):
    n_rows, d = xb.shape
    d_ff = wg.shape[2]
    tf = _ffn_tile(d_ff, 4)
    nf = d_ff // tf
    nblk = n_rows // MOE_BLOCK

    def row(i, j, be, nu):
        return jnp.minimum(i, nu[0] - 1)

    def col(i, j, be, nu):
        return jnp.where(i < nu[0], j, nf - 1)

    grid_spec = pltpu.PrefetchScalarGridSpec(
        num_scalar_prefetch=2,
        grid=(nblk, nf),
        in_specs=[pl.BlockSpec((MOE_BLOCK, d), lambda i, j, be, nu: (row(i, j, be, nu), 0)),
                  pl.BlockSpec((1, d, tf), lambda i, j, be, nu: (be[row(i, j, be, nu)], 0, col(i, j, be, nu))),
                  pl.BlockSpec((1, d, tf), lambda i, j, be, nu: (be[row(i, j, be, nu)], 0, col(i, j, be, nu))),
                  pl.BlockSpec((1, tf, d), lambda i, j, be, nu: (be[row(i, j, be, nu)], col(i, j, be, nu), 0))],
        out_specs=pl.BlockSpec((MOE_BLOCK, d), lambda i, j, be, nu: (i, 0)),
        scratch_shapes=[pltpu.VMEM((MOE_BLOCK, d), BF16)],
    )
    return pl.pallas_call(
        _expert_kernel,
        grid_spec=grid_spec,
        out_shape=jax.ShapeDtypeStruct((n_rows, d), F32),
        compiler_params=_cparams("arbitrary", "arbitrary"),
        name="expert_swiglu",
    )(block_expert, n_used, xb, _bf(wg), _bf(wu), _bf(wd))


def _combine_kernel(x_ref, yg_ref, meta_ref, g_ref, o_ref):
    d = x_ref.shape[1]
    meta = meta_ref[...]
    lane = lax.broadcasted_iota(jnp.int32, meta.shape, 1)
    g1 = jnp.sum(jnp.where(lane == META_G, meta, 0.0), axis=-1, keepdims=True)
    g2 = jnp.sum(jnp.where(lane == META_G + 1, meta, 0.0), axis=-1, keepdims=True)
    y = x_ref[...] + (g1 * yg_ref[:, :d] + g2 * yg_ref[:, d:])
    o_ref[...] = _rmsnorm(y, g_ref[...])


def _combine_final_norm(x2, yg, meta, gain):
    n, d = x2.shape
    return pl.pallas_call(
        _combine_kernel,
        grid=(n // ROW_TILE,),
        in_specs=[pl.BlockSpec((ROW_TILE, d), lambda i: (i, 0)),
                  pl.BlockSpec((ROW_TILE, 2 * d), lambda i: (i, 0)),
                  pl.BlockSpec((ROW_TILE, LANES), lambda i: (i, 0)),
                  pl.BlockSpec((1, d), lambda i: (0, 0))],
        out_specs=pl.BlockSpec((ROW_TILE, d), lambda i: (i, 0)),
        out_shape=jax.ShapeDtypeStruct((n, d), F32),
        compiler_params=_cparams("parallel"),
        name="moe_combine_norm",
    )(x2, yg, meta, gain.reshape(1, d))


def _moe_layer(x2, gain, w_router, wg, wu, wd, final_gain):
    n, d = x2.shape
    h, meta, counts = _router(x2, gain, w_router)
    expert = meta[:, META_E:META_E + 2].astype(jnp.int32)
    rank = meta[:, META_RANK:META_RANK + 2].astype(jnp.int32)
    count = counts[0, :N_EXPERTS].astype(jnp.int32)
    padded = (count + MOE_BLOCK - 1) // MOE_BLOCK * MOE_BLOCK
    pad_end = jnp.cumsum(padded)
    pad_start = pad_end - padded
    dest = (pad_start[expert] + rank).reshape(-1)
    n_rows = (2 * n // MOE_BLOCK + N_EXPERTS) * MOE_BLOCK
    row_tok = jnp.zeros((n_rows,), jnp.int32).at[dest].set(jnp.repeat(jnp.arange(n, dtype=jnp.int32), 2))
    block_start = jnp.arange(n_rows // MOE_BLOCK, dtype=jnp.int32) * MOE_BLOCK
    block_expert = jnp.minimum(jnp.searchsorted(pad_end, block_start, side='right'), N_EXPERTS - 1).astype(jnp.int32)
    n_used = (pad_end[-1:] // MOE_BLOCK).astype(jnp.int32)
    xb = _gather_rows(h, row_tok)
    ys = _expert_swiglu(xb, block_expert, n_used, wg, wu, wd)
    yg = _gather_rows(ys, dest).reshape(n, 2 * d)
    return _combine_final_norm(x2, yg, meta, final_gain)


def _mixer_layer(x2, batch, norm_mix, w_in, w_out, shift_mu, gm_ln_w, gm_ln_b, gm_ws, gm_bs,
                 rw_w_up, rw_w0, rw_a_up, rw_a0, rw_g_up, rw_k_k, rw_k_a, rw_r_k, rw_lnx_w, rw_lnx_b):
    n, d = x2.shape
    gm2 = 2 * gm_ln_w.shape[0]
    w_in = _bf(w_in)
    p_gm, p_rw = _norm_inproj(x2, norm_mix, w_in[:, :gm2], w_in[:, gm2:])
    y_gm = _gmlp(p_gm, gm_ln_w, gm_ln_b, gm_ws, gm_bs)
    r, lw, k, v, a, b, g, bonus = _rwkv_prep(p_rw, batch, shift_mu, rw_w_up, rw_w0, rw_a_up, rw_a0, rw_g_up,
                                             rw_k_k, rw_k_a, rw_r_k.reshape(-1))
    y_scan = _wkv(r, lw, k, v, a, b)
    flat = lambda z: z.reshape(n, -1)
    return _post_outproj(x2, y_gm, flat(y_scan), flat(bonus), flat(g), rw_lnx_w, rw_lnx_b, w_out)


def kernel(x, norm_mix, w_in, w_out, shift_mu, gm_ln_w, gm_ln_b, gm_ws, gm_bs, rw_w_up, rw_w0, rw_a_up, rw_a0,
           rw_g_up, rw_k_k, rw_k_a, rw_r_k, rw_lnx_w, rw_lnx_b, norm_ffn, ffn_w_gate, ffn_w_up, ffn_w_down,
           moe_router, moe_w_gate, moe_w_up, moe_w_down, norm_final):
    batch, t, d = x.shape
    depth = norm_mix.shape[0]
    assert depth == 2 and t % ROW_TILE == 0, "two layers (dense then MoE), sequence a multiple of the row tile"
    x2 = x.reshape(batch * t, d)
    for i in range(depth):
        x2 = _mixer_layer(x2, batch, norm_mix[i], w_in[i], w_out[i], shift_mu[i], gm_ln_w[i], gm_ln_b[i],
                          gm_ws[i], gm_bs[i], rw_w_up[i], rw_w0[i], rw_a_up[i], rw_a0[i], rw_g_up[i],
                          rw_k_k[i], rw_k_a[i], rw_r_k[i], rw_lnx_w[i], rw_lnx_b[i])
        if i % 2 == 0:
            x2 = _swiglu(x2, norm_ffn[i], ffn_w_gate[i // 2], ffn_w_up[i // 2], ffn_w_down[i // 2])
        else:
            x2 = _moe_layer(x2, norm_ffn[i], moe_router[i // 2], moe_w_gate[i // 2], moe_w_up[i // 2],
                            moe_w_down[i // 2], norm_final)
    return x2.reshape(batch, t, d)
```

```python
import functools

import jax
import jax.numpy as jnp
from jax import lax
from jax.experimental import pallas as pl
from jax.experimental.pallas import tpu as pltpu
from jax.experimental.pallas import tpu_sc as plsc

F32 = jnp.float32
BF16 = jnp.bfloat16

HEAD_DIM = 64
LANES = 128
GM_CHUNK = 128
WKV_CHUNK = 64
N_EXPERTS = 8
RMS_EPS = 1e-6
LN_EPS = 1e-5
GN_EPS = 64e-5
NEG_BIG = -1e30

ROW_TILE = 512
WKV_STEP = 128
MOE_BLOCK = 512
SC_WINDOW = 32
VMEM_LIMIT = 56 * 1024 * 1024


def _cparams(*sem):
    return pltpu.CompilerParams(dimension_semantics=sem, vmem_limit_bytes=VMEM_LIMIT)


def _bf(x):
    return x.astype(BF16)


def _dot(a, b):
    return jnp.dot(a, b, preferred_element_type=F32)


def _dot_nt(a, b):
    return lax.dot_general(a, b, (((1,), (1,)), ((), ())), preferred_element_type=F32)


def _dot_tn(a, b):
    return lax.dot_general(a, b, (((0,), (0,)), ((), ())), preferred_element_type=F32)


def _split_dot(x, m, parts=3):
    acc = None
    rem = x
    for p in range(parts):
        hi = _bf(rem)
        d = _dot(hi, m)
        acc = d if acc is None else acc + d
        if p + 1 < parts:
            rem = rem - hi.astype(F32)
    return acc


def _split_dot_left(m, x, parts=3):
    acc = None
    rem = x
    for p in range(parts):
        hi = _bf(rem)
        d = _dot(m, hi)
        acc = d if acc is None else acc + d
        if p + 1 < parts:
            rem = rem - hi.astype(F32)
    return acc


def _rmsnorm(x, g):
    return x * lax.rsqrt(jnp.mean(x * x, axis=-1, keepdims=True) + RMS_EPS) * g


def _gelu(x):
    return 0.5 * x * (1.0 + lax.erf(x * (2.0 ** -0.5)))


def _head_avg_matrix(width):
    r = lax.broadcasted_iota(jnp.int32, (width, width), 0) // HEAD_DIM
    c = lax.broadcasted_iota(jnp.int32, (width, width), 1) // HEAD_DIM
    return jnp.where(r == c, 1.0 / HEAD_DIM, 0.0).astype(BF16)


def _head_layernorm(x, avg, w, b, eps):
    mu = _split_dot(x, avg)
    xc = x - mu
    var = _split_dot(xc * xc, avg)
    return xc * lax.rsqrt(var + eps) * w + b


def _norm_inproj_kernel(x_ref, g_ref, wgm_ref, wrw_ref, pgm_ref, prw_ref):
    h = _bf(_rmsnorm(x_ref[...], g_ref[...]))
    pgm_ref[...] = _dot(h, wgm_ref[...])
    prw_ref[...] = _dot(h, wrw_ref[...])


def _norm_inproj(x2, gain, w_gm, w_rw):
    n, d = x2.shape
    wg, wr = w_gm.shape[1], w_rw.shape[1]
    return pl.pallas_call(
        _norm_inproj_kernel,
        grid=(n // ROW_TILE,),
        in_specs=[pl.BlockSpec((ROW_TILE, d), lambda i: (i, 0)),
                  pl.BlockSpec((1, d), lambda i: (0, 0)),
                  pl.BlockSpec((d, wg), lambda i: (0, 0)),
                  pl.BlockSpec((d, wr), lambda i: (0, 0))],
        out_specs=[pl.BlockSpec((ROW_TILE, wg), lambda i: (i, 0)),
                   pl.BlockSpec((ROW_TILE, wr), lambda i: (i, 0))],
        out_shape=[jax.ShapeDtypeStruct((n, wg), F32), jax.ShapeDtypeStruct((n, wr), F32)],
        compiler_params=_cparams("parallel"),
        name="norm_inproj",
    )(x2, gain.reshape(1, d), w_gm, w_rw)


def _gmlp_kernel(p_ref, lnw_ref, lnb_ref, ws_ref, bias_ref, o_ref):
    width = o_ref.shape[1]
    p = p_ref[...]
    u = _gelu(p[:, :width])
    v = _gelu(p[:, width:])
    vn = _head_layernorm(v, _head_avg_matrix(width), lnw_ref[...], lnb_ref[...], LN_EPS)
    t = lax.broadcasted_iota(jnp.int32, (GM_CHUNK, GM_CHUNK), 0)
    s = lax.broadcasted_iota(jnp.int32, (GM_CHUNK, GM_CHUNK), 1)
    causal = s <= t
    head0 = lax.broadcasted_iota(jnp.int32, (GM_CHUNK, LANES), 1) < HEAD_DIM
    mixed = []
    for pair in range(width // LANES):
        vp = vn[:, pair * LANES:(pair + 1) * LANES]
        v_st = _bf(jnp.concatenate([jnp.where(head0, vp, 0.0), jnp.where(head0, 0.0, vp)], axis=0))
        w2 = _bf(jnp.concatenate([jnp.where(causal, ws_ref[2 * pair], 0.0),
                                  jnp.where(causal, ws_ref[2 * pair + 1], 0.0)], axis=1))
        mixed.append(_dot(w2, v_st))
    o_ref[...] = u * (jnp.concatenate(mixed, axis=1) + bias_ref[...])


def _gmlp(p_gm, ln_w, ln_b, ws, bs):
    n, w2 = p_gm.shape
    width = w2 // 2
    bias = jnp.repeat(bs.T, HEAD_DIM, axis=1)
    return pl.pallas_call(
        _gmlp_kernel,
        grid=(n // GM_CHUNK,),
        in_specs=[pl.BlockSpec((GM_CHUNK, w2), lambda i: (i, 0)),
                  pl.BlockSpec((1, width), lambda i: (0, 0)),
                  pl.BlockSpec((1, width), lambda i: (0, 0)),
                  pl.BlockSpec(ws.shape, lambda i: (0, 0, 0)),
                  pl.BlockSpec((GM_CHUNK, width), lambda i: (0, 0))],
        out_specs=pl.BlockSpec((GM_CHUNK, width), lambda i: (i, 0)),
        out_shape=jax.ShapeDtypeStruct((n, width), F32),
        compiler_params=_cparams("parallel"),
        name="gmlp",
    )(p_gm, ln_w.reshape(1, width), ln_b.reshape(1, width), ws, bias)


def _rwkv_prep_kernel(p_ref, mu_ref, wup_ref, w0_ref, aup_ref, a0_ref, gup_ref, kk_ref, ka_ref, rk_ref,
                      r_ref, lw_ref, k_ref, v_ref, a_ref, b_ref, g_ref, bonus_ref, carry_ref):
    width = r_ref.shape[2]

    @pl.when(pl.program_id(1) == 0)
    def _():
        carry_ref[...] = jnp.zeros_like(carry_ref)

    p = p_ref[0]
    rows = p.shape[0]
    first = lax.broadcasted_iota(jnp.int32, p.shape, 0) == 0
    prev = jnp.where(first, carry_ref[0:1, :], pltpu.roll(p, 1, axis=0))
    carry_ref[0:1, :] = p[rows - 1:rows, :]
    ps = p + (prev - p) * mu_ref[...]
    r = ps[:, :width]
    k = ps[:, width:2 * width]
    v = ps[:, 2 * width:3 * width]
    lora_in = ps[:, 3 * width:3 * width + LANES]
    gd = ps[:, 3 * width + LANES:]
    log_w = -jax.nn.softplus(-(w0_ref[...] + _dot(_bf(jnp.tanh(lora_in)), wup_ref[...]))) - 0.5
    lw = -jnp.exp(log_w)
    a_lr = jax.nn.sigmoid(a0_ref[...] + _dot(_bf(lora_in), aup_ref[...]))
    g = _dot(_bf(jax.nn.sigmoid(gd)), gup_ref[...])
    ones = _head_avg_matrix(width) * HEAD_DIM
    kk = k * kk_ref[...]
    kk = kk / jnp.maximum(jnp.sqrt(_split_dot(kk * kk, ones)), 1e-12)
    kmod = k * (1.0 + (a_lr - 1.0) * ka_ref[...])
    r_ref[0] = r
    lw_ref[0] = lw
    k_ref[0] = kmod
    v_ref[0] = v
    a_ref[0] = -kk
    b_ref[0] = kk * a_lr
    g_ref[0] = g
    bonus_ref[0] = _split_dot(r * kmod * rk_ref[...], ones) * v


def _rwkv_prep(p_rw, batch, mu, w_up, w0, a_up, a0, g_up, k_k, k_a, r_k):
    n, win = p_rw.shape
    t = n // batch
    width = w0.shape[0]
    rank = w_up.shape[0]
    zeros = jnp.zeros((LANES - rank, width), F32)
    wup_pad = _bf(jnp.concatenate([w_up, zeros], axis=0))
    aup_pad = _bf(jnp.concatenate([zeros, a_up], axis=0))
    row = lambda z: z.reshape(1, -1)
    vec = pl.BlockSpec((1, width), lambda b, i: (0, 0))
    out = pl.BlockSpec((1, ROW_TILE, width), lambda b, i: (b, i, 0))
    return pl.pallas_call(
        _rwkv_prep_kernel,
        grid=(batch, t // ROW_TILE),
        in_specs=[pl.BlockSpec((1, ROW_TILE, win), lambda b, i: (b, i, 0)),
                  pl.BlockSpec((1, win), lambda b, i: (0, 0)),
                  pl.BlockSpec((LANES, width), lambda b, i: (0, 0)), vec,
                  pl.BlockSpec((LANES, width), lambda b, i: (0, 0)), vec,
                  pl.BlockSpec(g_up.shape, lambda b, i: (0, 0)), vec, vec, vec],
        out_specs=[out] * 8,
        out_shape=[jax.ShapeDtypeStruct((batch, t, width), F32)] * 8,
        scratch_shapes=[pltpu.VMEM((8, win), F32)],
        compiler_params=_cparams("parallel", "arbitrary"),
        name="rwkv_prep",
    )(p_rw.reshape(batch, t, win), row(mu), wup_pad, row(w0), aup_pad, row(a0), _bf(g_up),
      row(k_k), row(k_a), row(r_k))


def _wkv_chunk_pair(r, lw, k, v, a, b, h, consts):
    tri, head0, strict, incl, eye = consts
    n = WKV_CHUNK
    c = _split_dot_left(tri, lw)
    cex = c - lw
    mid = c[n // 2 - 1:n // 2, :]
    last = c[n - 1:n, :]
    g_inv = jnp.exp(mid - c)
    r_t = r * jnp.exp(c - mid)
    a_t = a * jnp.exp(cex - mid)
    k_t = k * g_inv
    b_t = b * g_inv
    r_h = r * jnp.exp(c)
    a_h = a * jnp.exp(cex)
    g_end = jnp.exp(last - c)
    b_c = b * g_end
    k_c = k * g_end

    def stack(z):
        return _bf(jnp.concatenate([jnp.where(head0, z, 0.0), jnp.where(head0, 0.0, z)], axis=0))

    def dup(z):
        zb = _bf(z)
        return jnp.concatenate([zb, zb], axis=0)

    a_st, r_st = stack(a_t), stack(r_t)
    b_dup, k_dup = dup(b_t), dup(k_t)
    nil = jnp.where(strict, _dot_nt(a_st, b_dup), 0.0)
    a_k = jnp.where(strict, _dot_nt(a_st, k_dup), 0.0)
    r_b = jnp.where(incl, _dot_nt(r_st, b_dup), 0.0)
    r_k = jnp.where(incl, _dot_nt(r_st, k_dup), 0.0)
    v_st = stack(v)
    y = jnp.concatenate([stack(a_h).astype(F32), _dot(_bf(a_k), v_st)], axis=1)
    pw = _bf(nil)
    steps = n.bit_length() - 1
    for i in range(steps):
        y = y + _dot(pw, _bf(y))
        if i + 1 < steps:
            pw = _bf(_dot(pw, pw))
    w = y[:, :LANES]
    u0 = y[:, LANES:]
    hb = _bf(h)
    gmat = _dot(jnp.concatenate([_bf(w), stack(r_h)], axis=0), hb)
    u = _bf(gmat[:2 * n] + u0)
    y_st = gmat[2 * n:] + _dot(_bf(r_b), u) + _dot(_bf(r_k), v_st)
    out = y_st[:n] + y_st[n:]
    e_col = jnp.sum(jnp.where(eye, jnp.exp(last), 0.0), axis=1, keepdims=True)
    h_new = e_col * h + _dot_tn(stack(b_c), u) + _dot_tn(stack(k_c), v_st)
    return out, h_new


def _wkv_kernel(r_ref, lw_ref, k_ref, v_ref, a_ref, b_ref, y_ref, h_ref):
    @pl.when(pl.program_id(1) == 0)
    def _():
        h_ref[...] = jnp.zeros_like(h_ref)

    n = WKV_CHUNK
    two = 2 * n
    tri = (lax.broadcasted_iota(jnp.int32, (n, n), 1) <= lax.broadcasted_iota(jnp.int32, (n, n), 0)).astype(BF16)
    head0 = lax.broadcasted_iota(jnp.int32, (n, LANES), 1) < HEAD_DIM
    row = lax.broadcasted_iota(jnp.int32, (two, two), 0)
    col = lax.broadcasted_iota(jnp.int32, (two, two), 1)
    same = (row // n) == (col // n)
    strict = same & ((col % n) < (row % n))
    incl = same & ((col % n) <= (row % n))
    eye = row == col
    consts = (tri, head0, strict, incl, eye)
    width = y_ref.shape[2]
    for pair in range(width // LANES):
        ls = slice(pair * LANES, (pair + 1) * LANES)
        h = h_ref[pair]
        for ch in range(r_ref.shape[1] // n):
            ts = slice(ch * n, (ch + 1) * n)
            y, h = _wkv_chunk_pair(r_ref[0, ts, ls], lw_ref[0, ts, ls], k_ref[0, ts, ls], v_ref[0, ts, ls],
                                   a_ref[0, ts, ls], b_ref[0, ts, ls], h, consts)
            y_ref[0, ts, ls] = y
        h_ref[pair] = h


def _wkv(r, lw, k, v, a, b):
    batch, t, width = r.shape
    spec = pl.BlockSpec((1, WKV_STEP, width), lambda bi, i: (bi, i, 0))
    return pl.pallas_call(
        _wkv_kernel,
        grid=(batch, t // WKV_STEP),
        in_specs=[spec] * 6,
        out_specs=spec,
        out_shape=jax.ShapeDtypeStruct((batch, t, width), F32),
        scratch_shapes=[pltpu.VMEM((width // LANES, LANES, LANES), F32)],
        compiler_params=_cparams("parallel", "arbitrary"),
        name="wkv7",
    )(r, lw, k, v, a, b)


def _post_outproj_kernel(x_ref, ygm_ref, ys_ref, bonus_ref, g_ref, lnw_ref, lnb_ref, wtop_ref, wbot_ref, o_ref):
    width = ys_ref.shape[1]
    yn = _head_layernorm(ys_ref[...], _head_avg_matrix(width), lnw_ref[...], lnb_ref[...], GN_EPS)
    y_rw = (yn + bonus_ref[...]) * g_ref[...]
    o_ref[...] = x_ref[...] + _dot(_bf(ygm_ref[...]), wtop_ref[...]) + _dot(_bf(y_rw), wbot_ref[...])


def _post_outproj(x2, y_gm, y_scan, bonus, g, lnx_w, lnx_b, w_out):
    n, d = x2.shape
    width = y_gm.shape[1]
    rows = lambda w: pl.BlockSpec((ROW_TILE, w), lambda i: (i, 0))
    vec = pl.BlockSpec((1, width), lambda i: (0, 0))
    wspec = pl.BlockSpec((width, d), lambda i: (0, 0))
    w_out = _bf(w_out)
    return pl.pallas_call(
        _post_outproj_kernel,
        grid=(n // ROW_TILE,),
        in_specs=[rows(d), rows(width), rows(width), rows(width), rows(width), vec, vec, wspec, wspec],
        out_specs=rows(d),
        out_shape=jax.ShapeDtypeStruct((n, d), F32),
        compiler_params=_cparams("parallel"),
        name="post_outproj",
    )(x2, y_gm, y_scan, bonus, g, lnx_w.reshape(1, width), lnx_b.reshape(1, width),
      w_out[:width], w_out[width:])


def _swiglu_kernel(x_ref, g_ref, wg_ref, wu_ref, wd_ref, o_ref, h_ref):
    j = pl.program_id(1)

    @pl.when(j == 0)
    def _():
        x = x_ref[...]
        h_ref[...] = _bf(_rmsnorm(x, g_ref[...]))
        o_ref[...] = x

    h = h_ref[...]
    act = jax.nn.silu(_dot(h, wg_ref[...])) * _dot(h, wu_ref[...])
    o_ref[...] += _dot(_bf(act), wd_ref[...])


def _ffn_tile(d_ff, parts):
    assert d_ff % (parts * LANES) == 0, f"d_ff={d_ff} does not split into {parts} lane-aligned tiles"
    return d_ff // parts


def _swiglu(x2, gain, wg, wu, wd):
    n, d = x2.shape
    d_ff = wg.shape[1]
    tf = _ffn_tile(d_ff, 2)
    return pl.pallas_call(
        _swiglu_kernel,
        grid=(n // ROW_TILE, d_ff // tf),
        in_specs=[pl.BlockSpec((ROW_TILE, d), lambda i, j: (i, 0)),
                  pl.BlockSpec((1, d), lambda i, j: (0, 0)),
                  pl.BlockSpec((d, tf), lambda i, j: (0, j)),
                  pl.BlockSpec((d, tf), lambda i, j: (0, j)),
                  pl.BlockSpec((tf, d), lambda i, j: (j, 0))],
        out_specs=pl.BlockSpec((ROW_TILE, d), lambda i, j: (i, 0)),
        out_shape=jax.ShapeDtypeStruct((n, d), F32),
        scratch_shapes=[pltpu.VMEM((ROW_TILE, d), BF16)],
        compiler_params=_cparams("parallel", "arbitrary"),
        name="dense_swiglu",
    )(x2, gain.reshape(1, d), _bf(wg), _bf(wu), _bf(wd))


META_E, META_G, META_RANK = 0, 2, 4


def _router_kernel(x_ref, g_ref, wr_ref, h_ref, meta_ref, cnt_ref, run_ref):
    @pl.when(pl.program_id(0) == 0)
    def _():
        run_ref[...] = jnp.zeros_like(run_ref)

    h = _rmsnorm(x_ref[...], g_ref[...])
    h_ref[...] = h
    rows = h.shape[0]
    lane = lax.broadcasted_iota(jnp.int32, (rows, LANES), 1)
    logits = jnp.dot(h, wr_ref[...], preferred_element_type=F32, precision=lax.Precision.HIGHEST)
    logits = jnp.where(lane < N_EXPERTS, logits, NEG_BIG)
    v1 = jnp.max(logits, axis=-1, keepdims=True)
    e1 = jnp.min(jnp.where(logits == v1, lane, LANES), axis=-1, keepdims=True)
    oh1 = lane == e1
    rest = jnp.where(oh1, NEG_BIG, logits)
    v2 = jnp.max(rest, axis=-1, keepdims=True)
    e2 = jnp.min(jnp.where(rest == v2, lane, LANES), axis=-1, keepdims=True)
    oh2 = lane == e2
    ex = jnp.exp(v2 - v1)
    g1 = 1.0 / (1.0 + ex)
    g2 = ex / (1.0 + ex)
    cnt = jnp.where(oh1 | oh2, 1.0, 0.0)
    t = lax.broadcasted_iota(jnp.int32, (rows, rows), 0)
    s = lax.broadcasted_iota(jnp.int32, (rows, rows), 1)
    before = _dot(jnp.where(s < t, 1.0, 0.0).astype(BF16), _bf(cnt)) + run_ref[0:1, :]
    rank1 = jnp.sum(jnp.where(oh1, before, 0.0), axis=-1, keepdims=True)
    rank2 = jnp.sum(jnp.where(oh2, before, 0.0), axis=-1, keepdims=True)
    fields = (e1.astype(F32), e2.astype(F32), g1, g2, rank1, rank2)
    meta = jnp.zeros((rows, LANES), F32)
    for idx, val in enumerate(fields):
        meta = jnp.where(lane == idx, val, meta)
    meta_ref[...] = meta
    run = run_ref[0:1, :] + jnp.sum(cnt, axis=0, keepdims=True)
    run_ref[...] = jnp.broadcast_to(run, run_ref.shape)
    cnt_ref[...] = jnp.broadcast_to(run, cnt_ref.shape)


def _router(x2, gain, w_router):
    n, d = x2.shape
    wr = jnp.concatenate([w_router, jnp.zeros((d, LANES - w_router.shape[1]), F32)], axis=1)
    return pl.pallas_call(
        _router_kernel,
        grid=(n // ROW_TILE,),
        in_specs=[pl.BlockSpec((ROW_TILE, d), lambda i: (i, 0)),
                  pl.BlockSpec((1, d), lambda i: (0, 0)),
                  pl.BlockSpec((d, LANES), lambda i: (0, 0))],
        out_specs=[pl.BlockSpec((ROW_TILE, d), lambda i: (i, 0)),
                   pl.BlockSpec((ROW_TILE, LANES), lambda i: (i, 0)),
                   pl.BlockSpec((8, LANES), lambda i: (0, 0))],
        out_shape=[jax.ShapeDtypeStruct((n, d), F32),
                   jax.ShapeDtypeStruct((n, LANES), F32),
                   jax.ShapeDtypeStruct((8, LANES), F32)],
        scratch_shapes=[pltpu.VMEM((8, LANES), F32)],
        compiler_params=_cparams("arbitrary"),
        name="moe_router",
    )(x2, gain.reshape(1, d), wr)


def _gather_rows(src, idx):
    info = plsc.get_sparse_core_info()
    nc, ns = info.num_cores, info.num_subcores
    workers = nc * ns
    m, d = idx.shape[0], src.shape[1]
    per_worker = m // workers
    nchunk = per_worker // SC_WINDOW
    assert m == workers * nchunk * SC_WINDOW and nchunk % 2 == 0, "row count must split evenly over subcores"
    mesh = plsc.VectorSubcoreMesh(core_axis_name="c", subcore_axis_name="s")

    @functools.partial(
        pl.kernel, mesh=mesh,
        out_type=jax.ShapeDtypeStruct((m, d), src.dtype),
        scratch_types=[pltpu.VMEM((nchunk, SC_WINDOW), jnp.int32),
                       pltpu.VMEM((SC_WINDOW, d), src.dtype), pltpu.VMEM((SC_WINDOW, d), src.dtype),
                       pltpu.SemaphoreType.DMA, pltpu.SemaphoreType.DMA,
                       pltpu.SemaphoreType.DMA, pltpu.SemaphoreType.DMA],
    )
    def gather_kernel(src_hbm, idx_hbm, out_hbm, idx_v, buf0, buf1, g0, g1, w0, w1):
        wid = lax.axis_index("s") * nc + lax.axis_index("c")
        base = wid * per_worker
        pltpu.sync_copy(idx_hbm.at[wid], idx_v)
        bufs, gsem, wsem = (buf0, buf1), (g0, g1), (w0, w1)

        def gather(c, slot):
            return pltpu.make_async_copy(src_hbm.at[idx_v.at[c]], bufs[slot], gsem[slot])

        def write(c, slot):
            return pltpu.make_async_copy(bufs[slot], out_hbm.at[pl.ds(base + c * SC_WINDOW, SC_WINDOW)], wsem[slot])

        gather(0, 0).start()
        gather(1, 1).start()

        @pl.loop(0, nchunk, step=2)
        def _(c):
            for slot in range(2):
                cc = c + slot
                gather(cc, slot).wait()
                write(cc, slot).start()
                write(cc, slot).wait()

                @pl.when(cc + 2 < nchunk)
                def _():
                    gather(cc + 2, slot).start()

    return gather_kernel(src, idx.reshape(workers, nchunk, SC_WINDOW))


def _expert_kernel(be_ref, nu_ref, x_ref, wg_ref, wu_ref, wd_ref, o_ref, h_ref):
    i, j = pl.program_id(0), pl.program_id(1)

    @pl.when(i < nu_ref[0])
    def _():
        @pl.when(j == 0)
        def _():
            h_ref[...] = _bf(x_ref[...])
            o_ref[...] = jnp.zeros_like(o_ref)

        h = h_ref[...]
        act = jax.nn.silu(_dot(h, wg_ref[0])) * _dot(h, wu_ref[0])
        o_ref[...] += _dot(_bf(act), wd_ref[0])

    @pl.when((i >= nu_ref[0]) & (j == 0))
    def _():
        o_ref[...] = jnp.zeros_like(o_ref)


def _expert_swiglu(xb, block_expert, n_used, wg, wu, wd):
    n_rows, d = xb.shape
    d_ff = wg.shape[2]
    tf = _ffn_tile(d_ff, 4)
    nf = d_ff // tf
    nblk = n_rows // MOE_BLOCK

    def row(i, j, be, nu):
        return jnp.minimum(i, nu[0] - 1)

    def col(i, j, be, nu):
        return jnp.where(i < nu[0], j, nf - 1)

    grid_spec = pltpu.PrefetchScalarGridSpec(
        num_scalar_prefetch=2,
        grid=(nblk, nf),
        in_specs=[pl.BlockSpec((MOE_BLOCK, d), lambda i, j, be, nu: (row(i, j, be, nu), 0)),
                  pl.BlockSpec((1, d, tf), lambda i, j, be, nu: (be[row(i, j, be, nu)], 0, col(i, j, be, nu))),
                  pl.BlockSpec((1, d, tf), lambda i, j, be, nu: (be[row(i, j, be, nu)], 0, col(i, j, be, nu))),
                  pl.BlockSpec((1, tf, d), lambda i, j, be, nu: (be[row(i, j, be, nu)], col(i, j, be, nu), 0))],
        out_specs=pl.BlockSpec((MOE_BLOCK, d), lambda i, j, be, nu: (i, 0)),
        scratch_shapes=[pltpu.VMEM((MOE_BLOCK, d), BF16)],
    )
    return pl.pallas_call(
        _expert_kernel,
        grid_spec=grid_spec,
        out_shape=jax.ShapeDtypeStruct((n_rows, d), F32),
        compiler_params=_cparams("arbitrary", "arbitrary"),
        name="expert_swiglu",
    )(block_expert, n_used, xb, _bf(wg), _bf(wu), _bf(wd))


def _combine_kernel(x_ref, yg_ref, meta_ref, g_ref, o_ref):
    d = x_ref.shape[1]
    meta = meta_ref[...]
    lane = lax.broadcasted_iota(jnp.int32, meta.shape, 1)
    g1 = jnp.sum(jnp.where(lane == META_G, meta, 0.0), axis=-1, keepdims=True)
    g2 = jnp.sum(jnp.where(lane == META_G + 1, meta, 0.0), axis=-1, keepdims=True)
    y = x_ref[...] + (g1 * yg_ref[:, :d] + g2 * yg_ref[:, d:])
    o_ref[...] = _rmsnorm(y, g_ref[...])


def _combine_final_norm(x2, yg, meta, gain):
    n, d = x2.shape
    return pl.pallas_call(
        _combine_kernel,
        grid=(n // ROW_TILE,),
        in_specs=[pl.BlockSpec((ROW_TILE, d), lambda i: (i, 0)),
                  pl.BlockSpec((ROW_TILE, 2 * d), lambda i: (i, 0)),
                  pl.BlockSpec((ROW_TILE, LANES), lambda i: (i, 0)),
                  pl.BlockSpec((1, d), lambda i: (0, 0))],
        out_specs=pl.BlockSpec((ROW_TILE, d), lambda i: (i, 0)),
        out_shape=jax.ShapeDtypeStruct((n, d), F32),
        compiler_params=_cparams("parallel"),
        name="moe_combine_norm",
    )(x2, yg, meta, gain.reshape(1, d))


def _moe_layer(x2, gain, w_router, wg, wu, wd, final_gain):
    n, d = x2.shape
    h, meta, counts = _router(x2, gain, w_router)
    expert = meta[:, META_E:META_E + 2].astype(jnp.int32)
    rank = meta[:, META_RANK:META_RANK + 2].astype(jnp.int32)
    count = counts[0, :N_EXPERTS].astype(jnp.int32)
    padded = (count + MOE_BLOCK - 1) // MOE_BLOCK * MOE_BLOCK
    pad_end = jnp.cumsum(padded)
    pad_start = pad_end - padded
    dest = (pad_start[expert] + rank).reshape(-1)
    n_rows = (2 * n // MOE_BLOCK + N_EXPERTS) * MOE_BLOCK
    row_tok = jnp.zeros((n_rows,), jnp.int32).at[dest].set(jnp.repeat(jnp.arange(n, dtype=jnp.int32), 2))
    block_start = jnp.arange(n_rows // MOE_BLOCK, dtype=jnp.int32) * MOE_BLOCK
    block_expert = jnp.minimum(jnp.searchsorted(pad_end, block_start, side='right'), N_EXPERTS - 1).astype(jnp.int32)
    n_used = (pad_end[-1:] // MOE_BLOCK).astype(jnp.int32)
    xb = _gather_rows(h, row_tok)
    ys = _expert_swiglu(xb, block_expert, n_used, wg, wu, wd)
    yg = _gather_rows(ys, dest).reshape(n, 2 * d)
    return _combine_final_norm(x2, yg, meta, final_gain)


def _mixer_layer(x2, batch, norm_mix, w_in, w_out, shift_mu, gm_ln_w, gm_ln_b, gm_ws, gm_bs,
                 rw_w_up, rw_w0, rw_a_up, rw_a0, rw_g_up, rw_k_k, rw_k_a, rw_r_k, rw_lnx_w, rw_lnx_b):
    n, d = x2.shape
    gm2 = 2 * gm_ln_w.shape[0]
    w_in = _bf(w_in)
    p_gm, p_rw = _norm_inproj(x2, norm_mix, w_in[:, :gm2], w_in[:, gm2:])
    y_gm = _gmlp(p_gm, gm_ln_w, gm_ln_b, gm_ws, gm_bs)
    r, lw, k, v, a, b, g, bonus = _rwkv_prep(p_rw, batch, shift_mu, rw_w_up, rw_w0, rw_a_up, rw_a0, rw_g_up,
                                             rw_k_k, rw_k_a, rw_r_k.reshape(-1))
    y_scan = _wkv(r, lw, k, v, a, b)
    flat = lambda z: z.reshape(n, -1)
    return _post_outproj(x2, y_gm, flat(y_scan), flat(bonus), flat(g), rw_lnx_w, rw_lnx_b, w_out)


def kernel(x, norm_mix, w_in, w_out, shift_mu, gm_ln_w, gm_ln_b, gm_ws, gm_bs, rw_w_up, rw_w0, rw_a_up, rw_a0,
           rw_g_up, rw_k_k, rw_k_a, rw_r_k, rw_lnx_w, rw_lnx_b, norm_ffn, ffn_w_gate, ffn_w_up, ffn_w_down,
           moe_router, moe_w_gate, moe_w_up, moe_w_down, norm_final):
    batch, t, d = x.shape
    depth = norm_mix.shape[0]
    assert depth == 2 and t % ROW_TILE == 0, "two layers (dense then MoE), sequence a multiple of the row tile"
    x2 = x.reshape(batch * t, d)
    for i in range(depth):
        x2 = _mixer_layer(x2, batch, norm_mix[i], w_in[i], w_out[i], shift_mu[i], gm_ln_w[i], gm_ln_b[i],
                          gm_ws[i], gm_bs[i], rw_w_up[i], rw_w0[i], rw_a_up[i], rw_a0[i], rw_g_up[i],
                          rw_k_k[i], rw_k_a[i], rw_r_k[i], rw_lnx_w[i], rw_lnx_b[i])
        if i % 2 == 0:
            x2 = _swiglu(x2, norm_ffn[i], ffn_w_gate[i // 2], ffn_w_up[i // 2], ffn_w_down[i // 2])
        else:
            x2 = _moe_layer(x2, norm_ffn[i], moe_router[i // 2], moe_w_gate[i // 2], moe_w_up[i // 2],
                            moe_w_down[i // 2], norm_final)
    return x2.reshape(batch, t, d)
```

```python
import functools

import jax
import jax.numpy as jnp
from jax import lax
from jax.experimental import pallas as pl
from jax.experimental.pallas import tpu as pltpu
from jax.experimental.pallas import tpu_sc as plsc

F32 = jnp.float32
BF16 = jnp.bfloat16

HEAD_DIM = 64
LANES = 128
GM_CHUNK = 128
WKV_CHUNK = 64
N_EXPERTS = 8
RMS_EPS = 1e-6
LN_EPS = 1e-5
GN_EPS = 64e-5
NEG_BIG = -1e30

ROW_TILE = 512
WKV_STEP = 256
MOE_BLOCK = 512
SC_WINDOW = 32
VMEM_LIMIT = 56 * 1024 * 1024


def _cparams(*sem):
    return pltpu.CompilerParams(dimension_semantics=sem, vmem_limit_bytes=VMEM_LIMIT)


def _bf(x):
    return x.astype(BF16)


def _dot(a, b):
    return jnp.dot(a, b, preferred_element_type=F32)


def _dot_nt(a, b):
    return lax.dot_general(a, b, (((1,), (1,)), ((), ())), preferred_element_type=F32)


def _dot_tn(a, b):
    return lax.dot_general(a, b, (((0,), (0,)), ((), ())), preferred_element_type=F32)


def _split_dot(x, m, parts=3):
    acc = None
    rem = x
    for p in range(parts):
        hi = _bf(rem)
        d = _dot(hi, m)
        acc = d if acc is None else acc + d
        if p + 1 < parts:
            rem = rem - hi.astype(F32)
    return acc


def _split_dot_left(m, x, parts=3):
    acc = None
    rem = x
    for p in range(parts):
        hi = _bf(rem)
        d = _dot(m, hi)
        acc = d if acc is None else acc + d
        if p + 1 < parts:
            rem = rem - hi.astype(F32)
    return acc


def _rmsnorm(x, g):
    return x * lax.rsqrt(jnp.mean(x * x, axis=-1, keepdims=True) + RMS_EPS) * g


def _gelu(x):
    return 0.5 * x * (1.0 + lax.erf(x * (2.0 ** -0.5)))


def _head_avg_matrix(width):
    r = lax.broadcasted_iota(jnp.int32, (width, width), 0) // HEAD_DIM
    c = lax.broadcasted_iota(jnp.int32, (width, width), 1) // HEAD_DIM
    return jnp.where(r == c, 1.0 / HEAD_DIM, 0.0).astype(BF16)


def _head_layernorm(x, avg, w, b, eps):
    mu = _split_dot(x, avg)
    xc = x - mu
    var = _split_dot(xc * xc, avg)
    return xc * lax.rsqrt(var + eps) * w + b


def _norm_inproj_kernel(x_ref, g_ref, wgm_ref, wrw_ref, pgm_ref, prw_ref):
    h = _bf(_rmsnorm(x_ref[...], g_ref[...]))
    pgm_ref[...] = _dot(h, wgm_ref[...])
    prw_ref[...] = _dot(h, wrw_ref[...])


def _norm_inproj(x2, gain, w_gm, w_rw):
    n, d = x2.shape
    wg, wr = w_gm.shape[1], w_rw.shape[1]
    return pl.pallas_call(
        _norm_inproj_kernel,
        grid=(n // ROW_TILE,),
        in_specs=[pl.BlockSpec((ROW_TILE, d), lambda i: (i, 0)),
                  pl.BlockSpec((1, d), lambda i: (0, 0)),
                  pl.BlockSpec((d, wg), lambda i: (0, 0)),
                  pl.BlockSpec((d, wr), lambda i: (0, 0))],
        out_specs=[pl.BlockSpec((ROW_TILE, wg), lambda i: (i, 0)),
                   pl.BlockSpec((ROW_TILE, wr), lambda i: (i, 0))],
        out_shape=[jax.ShapeDtypeStruct((n, wg), F32), jax.ShapeDtypeStruct((n, wr), F32)],
        compiler_params=_cparams("parallel"),
        name="norm_inproj",
    )(x2, gain.reshape(1, d), w_gm, w_rw)


def _gmlp_kernel(p_ref, lnw_ref, lnb_ref, ws_ref, bias_ref, o_ref):
    width = o_ref.shape[1]
    p = p_ref[...]
    u = _gelu(p[:, :width])
    v = _gelu(p[:, width:])
    vn = _head_layernorm(v, _head_avg_matrix(width), lnw_ref[...], lnb_ref[...], LN_EPS)
    t = lax.broadcasted_iota(jnp.int32, (GM_CHUNK, GM_CHUNK), 0)
    s = lax.broadcasted_iota(jnp.int32, (GM_CHUNK, GM_CHUNK), 1)
    causal = s <= t
    head0 = lax.broadcasted_iota(jnp.int32, (GM_CHUNK, LANES), 1) < HEAD_DIM
    mixed = []
    for pair in range(width // LANES):
        vp = vn[:, pair * LANES:(pair + 1) * LANES]
        v_st = _bf(jnp.concatenate([jnp.where(head0, vp, 0.0), jnp.where(head0, 0.0, vp)], axis=0))
        w2 = _bf(jnp.concatenate([jnp.where(causal, ws_ref[2 * pair], 0.0),
                                  jnp.where(causal, ws_ref[2 * pair + 1], 0.0)], axis=1))
        mixed.append(_dot(w2, v_st))
    o_ref[...] = u * (jnp.concatenate(mixed, axis=1) + bias_ref[...])


def _gmlp(p_gm, ln_w, ln_b, ws, bs):
    n, w2 = p_gm.shape
    width = w2 // 2
    bias = jnp.repeat(bs.T, HEAD_DIM, axis=1)
    return pl.pallas_call(
        _gmlp_kernel,
        grid=(n // GM_CHUNK,),
        in_specs=[pl.BlockSpec((GM_CHUNK, w2), lambda i: (i, 0)),
                  pl.BlockSpec((1, width), lambda i: (0, 0)),
                  pl.BlockSpec((1, width), lambda i: (0, 0)),
                  pl.BlockSpec(ws.shape, lambda i: (0, 0, 0)),
                  pl.BlockSpec((GM_CHUNK, width), lambda i: (0, 0))],
        out_specs=pl.BlockSpec((GM_CHUNK, width), lambda i: (i, 0)),
        out_shape=jax.ShapeDtypeStruct((n, width), F32),
        compiler_params=_cparams("parallel"),
        name="gmlp",
    )(p_gm, ln_w.reshape(1, width), ln_b.reshape(1, width), ws, bias)


def _rwkv_prep_kernel(p_ref, mu_ref, wup_ref, w0_ref, aup_ref, a0_ref, gup_ref, kk_ref, ka_ref, rk_ref,
                      r_ref, c_ref, lw_ref, k_ref, v_ref, a_ref, b_ref, g_ref, bonus_ref, carry_ref):
    width = r_ref.shape[2]

    @pl.when(pl.program_id(1) == 0)
    def _():
        carry_ref[...] = jnp.zeros_like(carry_ref)

    p = p_ref[0]
    rows = p.shape[0]
    first = lax.broadcasted_iota(jnp.int32, p.shape, 0) == 0
    prev = jnp.where(first, carry_ref[0:1, :], pltpu.roll(p, 1, axis=0))
    carry_ref[0:1, :] = p[rows - 1:rows, :]
    ps = p + (prev - p) * mu_ref[...]
    r = ps[:, :width]
    k = ps[:, width:2 * width]
    v = ps[:, 2 * width:3 * width]
    lora_in = ps[:, 3 * width:3 * width + LANES]
    gd = ps[:, 3 * width + LANES:]
    log_w = -jax.nn.softplus(-(w0_ref[...] + _dot(_bf(jnp.tanh(lora_in)), wup_ref[...]))) - 0.5
    lw = -jnp.exp(log_w)
    a_lr = jax.nn.sigmoid(a0_ref[...] + _dot(_bf(lora_in), aup_ref[...]))
    g = _dot(_bf(jax.nn.sigmoid(gd)), gup_ref[...])
    ones = _head_avg_matrix(width) * HEAD_DIM
    kk = k * kk_ref[...]
    kk = kk / jnp.maximum(jnp.sqrt(_split_dot(kk * kk, ones)), 1e-12)
    kmod = k * (1.0 + (a_lr - 1.0) * ka_ref[...])
    r_ref[0] = r
    lw_ref[0] = lw
    t = lax.broadcasted_iota(jnp.int32, (rows, rows), 0)
    s = lax.broadcasted_iota(jnp.int32, (rows, rows), 1)
    tri = ((t // WKV_CHUNK == s // WKV_CHUNK) & (s <= t)).astype(BF16)
    c_ref[0] = _split_dot_left(tri, lw)
    k_ref[0] = kmod
    v_ref[0] = v
    a_ref[0] = -kk
    b_ref[0] = kk * a_lr
    g_ref[0] = g
    bonus_ref[0] = _split_dot(r * kmod * rk_ref[...], ones) * v


def _rwkv_prep(p_rw, batch, mu, w_up, w0, a_up, a0, g_up, k_k, k_a, r_k):
    n, win = p_rw.shape
    t = n // batch
    width = w0.shape[0]
    rank = w_up.shape[0]
    zeros = jnp.zeros((LANES - rank, width), F32)
    wup_pad = _bf(jnp.concatenate([w_up, zeros], axis=0))
    aup_pad = _bf(jnp.concatenate([zeros, a_up], axis=0))
    row = lambda z: z.reshape(1, -1)
    vec = pl.BlockSpec((1, width), lambda b, i: (0, 0))
    out = pl.BlockSpec((1, ROW_TILE, width), lambda b, i: (b, i, 0))
    return pl.pallas_call(
        _rwkv_prep_kernel,
        grid=(batch, t // ROW_TILE),
        in_specs=[pl.BlockSpec((1, ROW_TILE, win), lambda b, i: (b, i, 0)),
                  pl.BlockSpec((1, win), lambda b, i: (0, 0)),
                  pl.BlockSpec((LANES, width), lambda b, i: (0, 0)), vec,
                  pl.BlockSpec((LANES, width), lambda b, i: (0, 0)), vec,
                  pl.BlockSpec(g_up.shape, lambda b, i: (0, 0)), vec, vec, vec],
        out_specs=[out] * 9,
        out_shape=[jax.ShapeDtypeStruct((batch, t, width), F32)] * 9,
        scratch_shapes=[pltpu.VMEM((8, win), F32)],
        compiler_params=_cparams("parallel", "arbitrary"),
        name="rwkv_prep",
    )(p_rw.reshape(batch, t, win), row(mu), wup_pad, row(w0), aup_pad, row(a0), _bf(g_up),
      row(k_k), row(k_a), row(r_k))


def _wkv_kernel(r_ref, c_ref, lw_ref, k_ref, v_ref, a_ref, b_ref, y_ref, h_ref,
                pw_s, akm_s, rbm_s, rkm_s, v_s, kc_s, rh_s, bc_s, sol_s, rkv_s, kv_s, dec_s):
    @pl.when(pl.program_id(1) == 0)
    def _():
        h_ref[...] = jnp.zeros_like(h_ref)

    n = WKV_CHUNK
    two = 2 * n
    head0 = lax.broadcasted_iota(jnp.int32, (n, LANES), 1) < HEAD_DIM
    row = lax.broadcasted_iota(jnp.int32, (two, two), 0)
    col = lax.broadcasted_iota(jnp.int32, (two, two), 1)
    same = (row // n) == (col // n)
    strict = same & ((col % n) < (row % n))
    incl = same & ((col % n) <= (row % n))
    eye = row == col
    npairs = y_ref.shape[2] // LANES
    nchunks = y_ref.shape[1] // n
    items = [(ch, pair) for ch in range(nchunks) for pair in range(npairs)]

    def window(ch, pair):
        return slice(ch * n, (ch + 1) * n), slice(pair * LANES, (pair + 1) * LANES)

    def stack(z):
        return jnp.concatenate([jnp.where(head0, z, 0.0), jnp.where(head0, 0.0, z)], axis=0)

    def dup(z):
        zb = _bf(z)
        return jnp.concatenate([zb, zb], axis=0)

    for it, (ch, pair) in enumerate(items):
        ts, ls = window(ch, pair)
        r, k, v, a, b = (ref[0, ts, ls] for ref in (r_ref, k_ref, v_ref, a_ref, b_ref))
        c = c_ref[0, ts, ls]
        cex = c - lw_ref[0, ts, ls]
        mid = c[n // 2 - 1:n // 2, :]
        last = c[n - 1:n, :]
        g_inv = jnp.exp(mid - c)
        g_end = jnp.exp(last - c)
        a_st = _bf(stack(a * jnp.exp(cex - mid)))
        r_st = _bf(stack(r * jnp.exp(c - mid)))
        b_dup, k_dup = dup(b * g_inv), dup(k * g_inv)
        pw_s[it] = _bf(jnp.where(strict, _dot_nt(a_st, b_dup), 0.0))
        akm_s[it] = _bf(jnp.where(strict, _dot_nt(a_st, k_dup), 0.0))
        rbm_s[it] = _bf(jnp.where(incl, _dot_nt(r_st, b_dup), 0.0))
        rkm_s[it] = _bf(jnp.where(incl, _dot_nt(r_st, k_dup), 0.0))
        v_s[it] = _bf(stack(v))
        kc_s[it] = _bf(stack(k * g_end))
        bc_s[it] = _bf(stack(b * g_end))
        rh_s[it] = _bf(stack(r * jnp.exp(c)))
        sol_s[it, :, :LANES] = stack(a * jnp.exp(cex))
        e_col = jnp.sum(jnp.where(eye, jnp.exp(last), 0.0), axis=1, keepdims=True)
        dec_s[it] = jnp.broadcast_to(e_col, (two, LANES))

    for it in range(len(items)):
        v_st = v_s[it]
        sol_s[it, :, LANES:] = _dot(akm_s[it], v_st)
        rkv_s[it] = _dot(rkm_s[it], v_st)
        kv_s[it] = _dot_tn(kc_s[it], v_st)

    levels = n.bit_length() - 1
    for level in range(levels):
        for it in range(len(items)):
            pw = pw_s[it]
            sol = sol_s[it]
            sol_s[it] = sol + _dot(pw, _bf(sol))
            if level + 1 < levels:
                pw_s[it] = _bf(_dot(pw, pw))

    for it, (ch, pair) in enumerate(items):
        ts, ls = window(ch, pair)
        h = h_ref[pair]
        sol = sol_s[it]
        gmat = _dot(jnp.concatenate([_bf(sol[:, :LANES]), rh_s[it]], axis=0), _bf(h))
        u = _bf(gmat[:two] + sol[:, LANES:])
        y_st = gmat[two:] + _dot(rbm_s[it], u) + rkv_s[it]
        y_ref[0, ts, ls] = y_st[:n] + y_st[n:]
        h_ref[pair] = dec_s[it] * h + _dot_tn(bc_s[it], u) + kv_s[it]


def _wkv(r, c, lw, k, v, a, b):
    batch, t, width = r.shape
    npairs = width // LANES
    items = (WKV_STEP // WKV_CHUNK) * npairs
    spec = pl.BlockSpec((1, WKV_STEP, width), lambda bi, i: (bi, i, 0))
    sq = lambda dtype: pltpu.VMEM((items, LANES, LANES), dtype)
    return pl.pallas_call(
        _wkv_kernel,
        grid=(batch, t // WKV_STEP),
        in_specs=[spec] * 7,
        out_specs=spec,
        out_shape=jax.ShapeDtypeStruct((batch, t, width), F32),
        scratch_shapes=[pltpu.VMEM((npairs, LANES, LANES), F32)] + [sq(BF16)] * 8
                       + [pltpu.VMEM((items, LANES, 2 * LANES), F32), sq(F32), sq(F32), sq(F32)],
        compiler_params=_cparams("parallel", "arbitrary"),
        name="wkv7",
    )(r, c, lw, k, v, a, b)


def _post_outproj_kernel(x_ref, ygm_ref, ys_ref, bonus_ref, g_ref, lnw_ref, lnb_ref, wtop_ref, wbot_ref, o_ref):
    width = ys_ref.shape[1]
    yn = _head_layernorm(ys_ref[...], _head_avg_matrix(width), lnw_ref[...], lnb_ref[...], GN_EPS)
    y_rw = (yn + bonus_ref[...]) * g_ref[...]
    o_ref[...] = x_ref[...] + _dot(_bf(ygm_ref[...]), wtop_ref[...]) + _dot(_bf(y_rw), wbot_ref[...])


def _post_outproj(x2, y_gm, y_scan, bonus, g, lnx_w, lnx_b, w_out):
    n, d = x2.shape
    width = y_gm.shape[1]
    rows = lambda w: pl.BlockSpec((ROW_TILE, w), lambda i: (i, 0))
    vec = pl.BlockSpec((1, width), lambda i: (0, 0))
    wspec = pl.BlockSpec((width, d), lambda i: (0, 0))
    w_out = _bf(w_out)
    return pl.pallas_call(
        _post_outproj_kernel,
        grid=(n // ROW_TILE,),
        in_specs=[rows(d), rows(width), rows(width), rows(width), rows(width), vec, vec, wspec, wspec],
        out_specs=rows(d),
        out_shape=jax.ShapeDtypeStruct((n, d), F32),
        compiler_params=_cparams("parallel"),
        name="post_outproj",
    )(x2, y_gm, y_scan, bonus, g, lnx_w.reshape(1, width), lnx_b.reshape(1, width),
      w_out[:width], w_out[width:])


def _swiglu_kernel(x_ref, g_ref, wg_ref, wu_ref, wd_ref, o_ref, h_ref):
    j = pl.program_id(1)

    @pl.when(j == 0)
    def _():
        x = x_ref[...]
        h_ref[...] = _bf(_rmsnorm(x, g_ref[...]))
        o_ref[...] = x

    h = h_ref[...]
    act = jax.nn.silu(_dot(h, wg_ref[...])) * _dot(h, wu_ref[...])
    o_ref[...] += _dot(_bf(act), wd_ref[...])


def _ffn_tile(d_ff, parts):
    assert d_ff % (parts * LANES) == 0, f"d_ff={d_ff} does not split into {parts} lane-aligned tiles"
    return d_ff // parts


def _swiglu(x2, gain, wg, wu, wd):
    n, d = x2.shape
    d_ff = wg.shape[1]
    tf = _ffn_tile(d_ff, 2)
    return pl.pallas_call(
        _swiglu_kernel,
        grid=(n // ROW_TILE, d_ff // tf),
        in_specs=[pl.BlockSpec((ROW_TILE, d), lambda i, j: (i, 0)),
                  pl.BlockSpec((1, d), lambda i, j: (0, 0)),
                  pl.BlockSpec((d, tf), lambda i, j: (0, j)),
                  pl.BlockSpec((d, tf), lambda i, j: (0, j)),
                  pl.BlockSpec((tf, d), lambda i, j: (j, 0))],
        out_specs=pl.BlockSpec((ROW_TILE, d), lambda i, j: (i, 0)),
        out_shape=jax.ShapeDtypeStruct((n, d), F32),
        scratch_shapes=[pltpu.VMEM((ROW_TILE, d), BF16)],
        compiler_params=_cparams("parallel", "arbitrary"),
        name="dense_swiglu",
    )(x2, gain.reshape(1, d), _bf(wg), _bf(wu), _bf(wd))


META_E, META_G, META_RANK = 0, 2, 4


def _router_kernel(x_ref, g_ref, wr_ref, h_ref, meta_ref, cnt_ref, run_ref):
    @pl.when(pl.program_id(0) == 0)
    def _():
        run_ref[...] = jnp.zeros_like(run_ref)

    h = _rmsnorm(x_ref[...], g_ref[...])
    h_ref[...] = h
    rows = h.shape[0]
    lane = lax.broadcasted_iota(jnp.int32, (rows, LANES), 1)
    logits = jnp.dot(h, wr_ref[...], preferred_element_type=F32, precision=lax.Precision.HIGHEST)
    logits = jnp.where(lane < N_EXPERTS, logits, NEG_BIG)
    v1 = jnp.max(logits, axis=-1, keepdims=True)
    e1 = jnp.min(jnp.where(logits == v1, lane, LANES), axis=-1, keepdims=True)
    oh1 = lane == e1
    rest = jnp.where(oh1, NEG_BIG, logits)
    v2 = jnp.max(rest, axis=-1, keepdims=True)
    e2 = jnp.min(jnp.where(rest == v2, lane, LANES), axis=-1, keepdims=True)
    oh2 = lane == e2
    ex = jnp.exp(v2 - v1)
    g1 = 1.0 / (1.0 + ex)
    g2 = ex / (1.0 + ex)
    cnt = jnp.where(oh1 | oh2, 1.0, 0.0)
    t = lax.broadcasted_iota(jnp.int32, (rows, rows), 0)
    s = lax.broadcasted_iota(jnp.int32, (rows, rows), 1)
    before = _dot(jnp.where(s < t, 1.0, 0.0).astype(BF16), _bf(cnt)) + run_ref[0:1, :]
    rank1 = jnp.sum(jnp.where(oh1, before, 0.0), axis=-1, keepdims=True)
    rank2 = jnp.sum(jnp.where(oh2, before, 0.0), axis=-1, keepdims=True)
    fields = (e1.astype(F32), e2.astype(F32), g1, g2, rank1, rank2)
    meta = jnp.zeros((rows, LANES), F32)
    for idx, val in enumerate(fields):
        meta = jnp.where(lane == idx, val, meta)
    meta_ref[...] = meta
    run = run_ref[0:1, :] + jnp.sum(cnt, axis=0, keepdims=True)
    run_ref[...] = jnp.broadcast_to(run, run_ref.shape)
    cnt_ref[...] = jnp.broadcast_to(run, cnt_ref.shape)


def _router(x2, gain, w_router):
    n, d = x2.shape
    wr = jnp.concatenate([w_router, jnp.zeros((d, LANES - w_router.shape[1]), F32)], axis=1)
    return pl.pallas_call(
        _router_kernel,
        grid=(n // ROW_TILE,),
        in_specs=[pl.BlockSpec((ROW_TILE, d), lambda i: (i, 0)),
                  pl.BlockSpec((1, d), lambda i: (0, 0)),
                  pl.BlockSpec((d, LANES), lambda i: (0, 0))],
        out_specs=[pl.BlockSpec((ROW_TILE, d), lambda i: (i, 0)),
                   pl.BlockSpec((ROW_TILE, LANES), lambda i: (i, 0)),
                   pl.BlockSpec((8, LANES), lambda i: (0, 0))],
        out_shape=[jax.ShapeDtypeStruct((n, d), F32),
                   jax.ShapeDtypeStruct((n, LANES), F32),
                   jax.ShapeDtypeStruct((8, LANES), F32)],
        scratch_shapes=[pltpu.VMEM((8, LANES), F32)],
        compiler_params=_cparams("arbitrary"),
        name="moe_router",
    )(x2, gain.reshape(1, d), wr)


def _gather_rows(src, idx):
    info = plsc.get_sparse_core_info()
    nc, ns = info.num_cores, info.num_subcores
    workers = nc * ns
    m, d = idx.shape[0], src.shape[1]
    per_worker = m // workers
    nchunk = per_worker // SC_WINDOW
    assert m == workers * nchunk * SC_WINDOW and nchunk % 2 == 0, "row count must split evenly over subcores"
    mesh = plsc.VectorSubcoreMesh(core_axis_name="c", subcore_axis_name="s")

    @functools.partial(
        pl.kernel, mesh=mesh,
        out_type=jax.ShapeDtypeStruct((m, d), src.dtype),
        scratch_types=[pltpu.VMEM((nchunk, SC_WINDOW), jnp.int32),
                       pltpu.VMEM((SC_WINDOW, d), src.dtype), pltpu.VMEM((SC_WINDOW, d), src.dtype),
                       pltpu.SemaphoreType.DMA, pltpu.SemaphoreType.DMA,
                       pltpu.SemaphoreType.DMA, pltpu.SemaphoreType.DMA],
    )
    def gather_kernel(src_hbm, idx_hbm, out_hbm, idx_v, buf0, buf1, g0, g1, w0, w1):
        wid = lax.axis_index("s") * nc + lax.axis_index("c")
        base = wid * per_worker
        pltpu.sync_copy(idx_hbm.at[wid], idx_v)
        bufs, gsem, wsem = (buf0, buf1), (g0, g1), (w0, w1)

        def gather(c, slot):
            return pltpu.make_async_copy(src_hbm.at[idx_v.at[c]], bufs[slot], gsem[slot])

        def write(c, slot):
            return pltpu.make_async_copy(bufs[slot], out_hbm.at[pl.ds(base + c * SC_WINDOW, SC_WINDOW)], wsem[slot])

        gather(0, 0).start()
        gather(1, 1).start()

        @pl.loop(0, nchunk, step=2)
        def _(c):
            for slot in range(2):
                cc = c + slot
                gather(cc, slot).wait()
                write(cc, slot).start()
                write(cc, slot).wait()

                @pl.when(cc + 2 < nchunk)
                def _():
                    gather(cc + 2, slot).start()

    return gather_kernel(src, idx.reshape(workers, nchunk, SC_WINDOW))


def _expert_kernel(be_ref, nu_ref, x_ref, wg_ref, wu_ref, wd_ref, o_ref, h_ref):
    i, j = pl.program_id(0), pl.program_id(1)

    @pl.when(i < nu_ref[0])
    def _():
        @pl.when(j == 0)
        def _():
            h_ref[...] = _bf(x_ref[...])
            o_ref[...] = jnp.zeros_like(o_ref)

        h = h_ref[...]
        act = jax.nn.silu(_dot(h, wg_ref[0])) * _dot(h, wu_ref[0])
        o_ref[...] += _dot(_bf(act), wd_ref[0])

    @pl.when((i >= nu_ref[0]) & (j == 0))
    def _():
        o_ref[...] = jnp.zeros_like(o_ref)


def _expert_swiglu(xb, block_expert, n_used, wg, wu, wd):
    n_rows, d = xb.shape
    d_ff = wg.shape[2]
    tf = _ffn_tile(d_ff, 4)
    nf = d_ff // tf
    nblk = n_rows // MOE_BLOCK

    def row(i, j, be, nu):
        return jnp.minimum(i, nu[0] - 1)

    def col(i, j, be, nu):
        return jnp.where(i < nu[0], j, nf - 1)

    grid_spec = pltpu.PrefetchScalarGridSpec(
        num_scalar_prefetch=2,
        grid=(nblk, nf),
        in_specs=[pl.BlockSpec((MOE_BLOCK, d), lambda i, j, be, nu: (row(i, j, be, nu), 0)),
                  pl.BlockSpec((1, d, tf), lambda i, j, be, nu: (be[row(i, j, be, nu)], 0, col(i, j, be, nu))),
                  pl.BlockSpec((1, d, tf), lambda i, j, be, nu: (be[row(i, j, be, nu)], 0, col(i, j, be, nu))),
                  pl.BlockSpec((1, tf, d), lambda i, j, be, nu: (be[row(i, j, be, nu)], col(i, j, be, nu), 0))],
        out_specs=pl.BlockSpec((MOE_BLOCK, d), lambda i, j, be, nu: (i, 0)),
        scratch_shapes=[pltpu.VMEM((MOE_BLOCK, d), BF16)],
    )
    return pl.pallas_call(
        _expert_kernel,
        grid_spec=grid_spec,
        out_shape=jax.ShapeDtypeStruct((n_rows, d), F32),
        compiler_params=_cparams("arbitrary", "arbitrary"),
        name="expert_swiglu",
    )(block_expert, n_used, xb, _bf(wg), _bf(wu), _bf(wd))


def _combine_kernel(x_ref, yg_ref, meta_ref, g_ref, o_ref):
    d = x_ref.shape[1]
    meta = meta_ref[...]
    lane = lax.broadcasted_iota(jnp.int32, meta.shape, 1)
    g1 = jnp.sum(jnp.where(lane == META_G, meta, 0.0), axis=-1, keepdims=True)
    g2 = jnp.sum(jnp.where(lane == META_G + 1, meta, 0.0), axis=-1, keepdims=True)
    y = x_ref[...] + (g1 * yg_ref[:, :d] + g2 * yg_ref[:, d:])
    o_ref[...] = _rmsnorm(y, g_ref[...])


def _combine_final_norm(x2, yg, meta, gain):
    n, d = x2.shape
    return pl.pallas_call(
        _combine_kernel,
        grid=(n // ROW_TILE,),
        in_specs=[pl.BlockSpec((ROW_TILE, d), lambda i: (i, 0)),
                  pl.BlockSpec((ROW_TILE, 2 * d), lambda i: (i, 0)),
                  pl.BlockSpec((ROW_TILE, LANES), lambda i: (i, 0)),
                  pl.BlockSpec((1, d), lambda i: (0, 0))],
        out_specs=pl.BlockSpec((ROW_TILE, d), lambda i: (i, 0)),
        out_shape=jax.ShapeDtypeStruct((n, d), F32),
        compiler_params=_cparams("parallel"),
        name="moe_combine_norm",
    )(x2, yg, meta, gain.reshape(1, d))


def _moe_layer(x2, gain, w_router, wg, wu, wd, final_gain):
    n, d = x2.shape
    h, meta, counts = _router(x2, gain, w_router)
    expert = meta[:, META_E:META_E + 2].astype(jnp.int32)
    rank = meta[:, META_RANK:META_RANK + 2].astype(jnp.int32)
    count = counts[0, :N_EXPERTS].astype(jnp.int32)
    padded = (count + MOE_BLOCK - 1) // MOE_BLOCK * MOE_BLOCK
    pad_end = jnp.cumsum(padded)
    pad_start = pad_end - padded
    dest = (pad_start[expert] + rank).reshape(-1)
    n_rows = (2 * n // MOE_BLOCK + N_EXPERTS) * MOE_BLOCK
    row_tok = jnp.zeros((n_rows,), jnp.int32).at[dest].set(jnp.repeat(jnp.arange(n, dtype=jnp.int32), 2))
    block_start = jnp.arange(n_rows // MOE_BLOCK, dtype=jnp.int32) * MOE_BLOCK
    block_expert = jnp.minimum(jnp.searchsorted(pad_end, block_start, side='right'), N_EXPERTS - 1).astype(jnp.int32)
    n_used = (pad_end[-1:] // MOE_BLOCK).astype(jnp.int32)
    xb = _gather_rows(h, row_tok)
    ys = _expert_swiglu(xb, block_expert, n_used, wg, wu, wd)
    yg = _gather_rows(ys, dest).reshape(n, 2 * d)
    return _combine_final_norm(x2, yg, meta, final_gain)


def _mixer_layer(x2, batch, norm_mix, w_in, w_out, shift_mu, gm_ln_w, gm_ln_b, gm_ws, gm_bs,
                 rw_w_up, rw_w0, rw_a_up, rw_a0, rw_g_up, rw_k_k, rw_k_a, rw_r_k, rw_lnx_w, rw_lnx_b):
    n, d = x2.shape
    gm2 = 2 * gm_ln_w.shape[0]
    w_in = _bf(w_in)
    p_gm, p_rw = _norm_inproj(x2, norm_mix, w_in[:, :gm2], w_in[:, gm2:])
    y_gm = _gmlp(p_gm, gm_ln_w, gm_ln_b, gm_ws, gm_bs)
    r, c, lw, k, v, a, b, g, bonus = _rwkv_prep(p_rw, batch, shift_mu, rw_w_up, rw_w0, rw_a_up, rw_a0, rw_g_up,
                                             rw_k_k, rw_k_a, rw_r_k.reshape(-1))
    y_scan = _wkv(r, c, lw, k, v, a, b)
    flat = lambda z: z.reshape(n, -1)
    return _post_outproj(x2, y_gm, flat(y_scan), flat(bonus), flat(g), rw_lnx_w, rw_lnx_b, w_out)


def kernel(x, norm_mix, w_in, w_out, shift_mu, gm_ln_w, gm_ln_b, gm_ws, gm_bs, rw_w_up, rw_w0, rw_a_up, rw_a0,
           rw_g_up, rw_k_k, rw_k_a, rw_r_k, rw_lnx_w, rw_lnx_b, norm_ffn, ffn_w_gate, ffn_w_up, ffn_w_down,
           moe_router, moe_w_gate, moe_w_up, moe_w_down, norm_final):
    batch, t, d = x.shape
    depth = norm_mix.shape[0]
    assert depth == 2 and t % ROW_TILE == 0, "two layers (dense then MoE), sequence a multiple of the row tile"
    x2 = x.reshape(batch * t, d)
    for i in range(depth):
        x2 = _mixer_layer(x2, batch, norm_mix[i], w_in[i], w_out[i], shift_mu[i], gm_ln_w[i], gm_ln_b[i],
                          gm_ws[i], gm_bs[i], rw_w_up[i], rw_w0[i], rw_a_up[i], rw_a0[i], rw_g_up[i],
                          rw_k_k[i], rw_k_a[i], rw_r_k[i], rw_lnx_w[i], rw_lnx_b[i])
        if i % 2 == 0:
            x2 = _swiglu(x2, norm_ffn[i], ffn_w_gate[i // 2], ffn_w_up[i // 2], ffn_w_down[i // 2])
        else:
            x2 = _moe_layer(x2, norm_ffn[i], moe_router[i // 2], moe_w_gate[i // 2], moe_w_up[i // 2],
                            moe_w_down[i // 2], norm_final)
    return x2.reshape(batch, t, d)
```

```python
import functools

import jax
import jax.numpy as jnp
from jax import lax
from jax.experimental import pallas as pl
from jax.experimental.pallas import tpu as pltpu
from jax.experimental.pallas import tpu_sc as plsc

F32 = jnp.float32
BF16 = jnp.bfloat16

HEAD_DIM = 64
LANES = 128
GM_CHUNK = 128
WKV_CHUNK = 64
N_EXPERTS = 8
RMS_EPS = 1e-6
LN_EPS = 1e-5
GN_EPS = 64e-5
NEG_BIG = -1e30

ROW_TILE = 512
WKV_STEP = 256
MOE_BLOCK = 512
SC_WINDOW = 32
VMEM_LIMIT = 56 * 1024 * 1024


def _cparams(*sem):
    return pltpu.CompilerParams(dimension_semantics=sem, vmem_limit_bytes=VMEM_LIMIT)


def _bf(x):
    return x.astype(BF16)


def _dot(a, b):
    return jnp.dot(a, b, preferred_element_type=F32)


def _dot_nt(a, b):
    return lax.dot_general(a, b, (((1,), (1,)), ((), ())), preferred_element_type=F32)


def _dot_tn(a, b):
    return lax.dot_general(a, b, (((0,), (0,)), ((), ())), preferred_element_type=F32)


def _split_dot(x, m, parts=3):
    acc = None
    rem = x
    for p in range(parts):
        hi = _bf(rem)
        d = _dot(hi, m)
        acc = d if acc is None else acc + d
        if p + 1 < parts:
            rem = rem - hi.astype(F32)
    return acc


def _split_dot_left(m, x, parts=3):
    acc = None
    rem = x
    for p in range(parts):
        hi = _bf(rem)
        d = _dot(m, hi)
        acc = d if acc is None else acc + d
        if p + 1 < parts:
            rem = rem - hi.astype(F32)
    return acc


def _rmsnorm(x, g):
    return x * lax.rsqrt(jnp.mean(x * x, axis=-1, keepdims=True) + RMS_EPS) * g


def _gelu(x):
    return 0.5 * x * (1.0 + lax.erf(x * (2.0 ** -0.5)))


def _head_avg_matrix(width):
    r = lax.broadcasted_iota(jnp.int32, (width, width), 0) // HEAD_DIM
    c = lax.broadcasted_iota(jnp.int32, (width, width), 1) // HEAD_DIM
    return jnp.where(r == c, 1.0 / HEAD_DIM, 0.0).astype(BF16)


def _head_layernorm(x, avg, w, b, eps):
    mu = _split_dot(x, avg)
    xc = x - mu
    var = _split_dot(xc * xc, avg)
    return xc * lax.rsqrt(var + eps) * w + b


def _norm_inproj_kernel(x_ref, g_ref, wgm_ref, wrw_ref, pgm_ref, prw_ref):
    h = _bf(_rmsnorm(x_ref[...], g_ref[...]))
    pgm_ref[...] = _dot(h, wgm_ref[...])
    prw_ref[...] = _dot(h, wrw_ref[...])


def _norm_inproj(x2, gain, w_gm, w_rw):
    n, d = x2.shape
    wg, wr = w_gm.shape[1], w_rw.shape[1]
    return pl.pallas_call(
        _norm_inproj_kernel,
        grid=(n // ROW_TILE,),
        in_specs=[pl.BlockSpec((ROW_TILE, d), lambda i: (i, 0)),
                  pl.BlockSpec((1, d), lambda i: (0, 0)),
                  pl.BlockSpec((d, wg), lambda i: (0, 0)),
                  pl.BlockSpec((d, wr), lambda i: (0, 0))],
        out_specs=[pl.BlockSpec((ROW_TILE, wg), lambda i: (i, 0)),
                   pl.BlockSpec((ROW_TILE, wr), lambda i: (i, 0))],
        out_shape=[jax.ShapeDtypeStruct((n, wg), F32), jax.ShapeDtypeStruct((n, wr), F32)],
        compiler_params=_cparams("parallel"),
        name="norm_inproj",
    )(x2, gain.reshape(1, d), w_gm, w_rw)


def _gmlp_kernel(p_ref, lnw_ref, lnb_ref, ws_ref, bias_ref, o_ref):
    width = o_ref.shape[1]
    p = p_ref[...]
    u = _gelu(p[:, :width])
    v = _gelu(p[:, width:])
    vn = _head_layernorm(v, _head_avg_matrix(width), lnw_ref[...], lnb_ref[...], LN_EPS)
    t = lax.broadcasted_iota(jnp.int32, (GM_CHUNK, GM_CHUNK), 0)
    s = lax.broadcasted_iota(jnp.int32, (GM_CHUNK, GM_CHUNK), 1)
    causal = s <= t
    head0 = lax.broadcasted_iota(jnp.int32, (GM_CHUNK, LANES), 1) < HEAD_DIM
    mixed = []
    for pair in range(width // LANES):
        vp = vn[:, pair * LANES:(pair + 1) * LANES]
        v_st = _bf(jnp.concatenate([jnp.where(head0, vp, 0.0), jnp.where(head0, 0.0, vp)], axis=0))
        w2 = _bf(jnp.concatenate([jnp.where(causal, ws_ref[2 * pair], 0.0),
                                  jnp.where(causal, ws_ref[2 * pair + 1], 0.0)], axis=1))
        mixed.append(_dot(w2, v_st))
    o_ref[...] = u * (jnp.concatenate(mixed, axis=1) + bias_ref[...])


def _gmlp(p_gm, ln_w, ln_b, ws, bs):
    n, w2 = p_gm.shape
    width = w2 // 2
    bias = jnp.repeat(bs.T, HEAD_DIM, axis=1)
    return pl.pallas_call(
        _gmlp_kernel,
        grid=(n // GM_CHUNK,),
        in_specs=[pl.BlockSpec((GM_CHUNK, w2), lambda i: (i, 0)),
                  pl.BlockSpec((1, width), lambda i: (0, 0)),
                  pl.BlockSpec((1, width), lambda i: (0, 0)),
                  pl.BlockSpec(ws.shape, lambda i: (0, 0, 0)),
                  pl.BlockSpec((GM_CHUNK, width), lambda i: (0, 0))],
        out_specs=pl.BlockSpec((GM_CHUNK, width), lambda i: (i, 0)),
        out_shape=jax.ShapeDtypeStruct((n, width), F32),
        compiler_params=_cparams("parallel"),
        name="gmlp",
    )(p_gm, ln_w.reshape(1, width), ln_b.reshape(1, width), ws, bias)


def _rwkv_prep_kernel(p_ref, mu_ref, wup_ref, w0_ref, aup_ref, a0_ref, gup_ref, kk_ref, ka_ref, rk_ref,
                      r_ref, c_ref, lw_ref, k_ref, v_ref, a_ref, b_ref, g_ref, bonus_ref, carry_ref):
    width = r_ref.shape[2]

    @pl.when(pl.program_id(1) == 0)
    def _():
        carry_ref[...] = jnp.zeros_like(carry_ref)

    p = p_ref[0]
    rows = p.shape[0]
    first = lax.broadcasted_iota(jnp.int32, p.shape, 0) == 0
    prev = jnp.where(first, carry_ref[0:1, :], pltpu.roll(p, 1, axis=0))
    carry_ref[0:1, :] = p[rows - 1:rows, :]
    ps = p + (prev - p) * mu_ref[...]
    r = ps[:, :width]
    k = ps[:, width:2 * width]
    v = ps[:, 2 * width:3 * width]
    lora_in = ps[:, 3 * width:3 * width + LANES]
    gd = ps[:, 3 * width + LANES:]
    log_w = -jax.nn.softplus(-(w0_ref[...] + _dot(_bf(jnp.tanh(lora_in)), wup_ref[...]))) - 0.5
    lw = -jnp.exp(log_w)
    a_lr = jax.nn.sigmoid(a0_ref[...] + _dot(_bf(lora_in), aup_ref[...]))
    g = _dot(_bf(jax.nn.sigmoid(gd)), gup_ref[...])
    ones = _head_avg_matrix(width) * HEAD_DIM
    kk = k * kk_ref[...]
    kk = kk / jnp.maximum(jnp.sqrt(_split_dot(kk * kk, ones)), 1e-12)
    kmod = k * (1.0 + (a_lr - 1.0) * ka_ref[...])
    r_ref[0] = r
    lw_ref[0] = lw
    t = lax.broadcasted_iota(jnp.int32, (rows, rows), 0)
    s = lax.broadcasted_iota(jnp.int32, (rows, rows), 1)
    tri = ((t // WKV_CHUNK == s // WKV_CHUNK) & (s <= t)).astype(BF16)
    c_ref[0] = _split_dot_left(tri, lw)
    k_ref[0] = kmod
    v_ref[0] = v
    a_ref[0] = -kk
    b_ref[0] = kk * a_lr
    g_ref[0] = g
    bonus_ref[0] = _split_dot(r * kmod * rk_ref[...], ones) * v


def _rwkv_prep(p_rw, batch, mu, w_up, w0, a_up, a0, g_up, k_k, k_a, r_k):
    n, win = p_rw.shape
    t = n // batch
    width = w0.shape[0]
    rank = w_up.shape[0]
    zeros = jnp.zeros((LANES - rank, width), F32)
    wup_pad = _bf(jnp.concatenate([w_up, zeros], axis=0))
    aup_pad = _bf(jnp.concatenate([zeros, a_up], axis=0))
    row = lambda z: z.reshape(1, -1)
    vec = pl.BlockSpec((1, width), lambda b, i: (0, 0))
    out = pl.BlockSpec((1, ROW_TILE, width), lambda b, i: (b, i, 0))
    return pl.pallas_call(
        _rwkv_prep_kernel,
        grid=(batch, t // ROW_TILE),
        in_specs=[pl.BlockSpec((1, ROW_TILE, win), lambda b, i: (b, i, 0)),
                  pl.BlockSpec((1, win), lambda b, i: (0, 0)),
                  pl.BlockSpec((LANES, width), lambda b, i: (0, 0)), vec,
                  pl.BlockSpec((LANES, width), lambda b, i: (0, 0)), vec,
                  pl.BlockSpec(g_up.shape, lambda b, i: (0, 0)), vec, vec, vec],
        out_specs=[out] * 9,
        out_shape=[jax.ShapeDtypeStruct((batch, t, width), F32)] * 9,
        scratch_shapes=[pltpu.VMEM((8, win), F32)],
        compiler_params=_cparams("parallel", "arbitrary"),
        name="rwkv_prep",
    )(p_rw.reshape(batch, t, win), row(mu), wup_pad, row(w0), aup_pad, row(a0), _bf(g_up),
      row(k_k), row(k_a), row(r_k))


def _wkv_kernel(r_ref, c_ref, lw_ref, k_ref, v_ref, a_ref, b_ref, y_ref, h_ref,
                pw_s, akm_s, rbm_s, rkm_s, v_s, kc_s, rh_s, bc_s, sol_s, rkv_s, kv_s, dec_s):
    @pl.when(pl.program_id(1) == 0)
    def _():
        h_ref[...] = jnp.zeros_like(h_ref)

    n = WKV_CHUNK
    two = 2 * n
    head0 = lax.broadcasted_iota(jnp.int32, (n, LANES), 1) < HEAD_DIM
    row = lax.broadcasted_iota(jnp.int32, (two, two), 0)
    col = lax.broadcasted_iota(jnp.int32, (two, two), 1)
    same = (row // n) == (col // n)
    strict = same & ((col % n) < (row % n))
    incl = same & ((col % n) <= (row % n))
    eye = row == col
    npairs = y_ref.shape[2] // LANES
    nchunks = y_ref.shape[1] // n
    items = [(ch, pair) for ch in range(nchunks) for pair in range(npairs)]

    def window(ch, pair):
        return slice(ch * n, (ch + 1) * n), slice(pair * LANES, (pair + 1) * LANES)

    def stack(z):
        return jnp.concatenate([jnp.where(head0, z, 0.0), jnp.where(head0, 0.0, z)], axis=0)

    def dup(z):
        zb = _bf(z)
        return jnp.concatenate([zb, zb], axis=0)

    for it, (ch, pair) in enumerate(items):
        ts, ls = window(ch, pair)
        r, k, v, a, b = (ref[0, ts, ls] for ref in (r_ref, k_ref, v_ref, a_ref, b_ref))
        c = c_ref[0, ts, ls]
        cex = c - lw_ref[0, ts, ls]
        mid = c[n // 2 - 1:n // 2, :]
        last = c[n - 1:n, :]
        g_inv = jnp.exp(mid - c)
        g_end = jnp.exp(last - c)
        a_st = _bf(stack(a * jnp.exp(cex - mid)))
        r_st = _bf(stack(r * jnp.exp(c - mid)))
        b_dup, k_dup = dup(b * g_inv), dup(k * g_inv)
        pw_s[it] = _bf(jnp.where(strict, _dot_nt(a_st, b_dup), 0.0))
        akm_s[it] = _bf(jnp.where(strict, _dot_nt(a_st, k_dup), 0.0))
        rbm_s[it] = _bf(jnp.where(incl, _dot_nt(r_st, b_dup), 0.0))
        rkm_s[it] = _bf(jnp.where(incl, _dot_nt(r_st, k_dup), 0.0))
        v_s[it] = _bf(stack(v))
        kc_s[it] = _bf(stack(k * g_end))
        bc_s[it] = _bf(stack(b * g_end))
        rh_s[it] = _bf(stack(r * jnp.exp(c)))
        sol_s[it, :, :LANES] = stack(a * jnp.exp(cex))
        e_col = jnp.sum(jnp.where(eye, jnp.exp(last), 0.0), axis=1, keepdims=True)
        dec_s[it] = jnp.broadcast_to(e_col, (two, LANES))

    for it in range(len(items)):
        v_st = v_s[it]
        sol_s[it, :, LANES:] = _dot(akm_s[it], v_st)
        rkv_s[it] = _dot(rkm_s[it], v_st)
        kv_s[it] = _dot_tn(kc_s[it], v_st)

    levels = n.bit_length() - 1
    for level in range(levels):
        for it in range(len(items)):
            pw = pw_s[it]
            sol = sol_s[it]
            sol_s[it] = sol + _dot(pw, _bf(sol))
            if level + 1 < levels:
                pw_s[it] = _bf(_dot(pw, pw))

    for it, (ch, pair) in enumerate(items):
        ts, ls = window(ch, pair)
        h = h_ref[pair]
        sol = sol_s[it]
        gmat = _dot(jnp.concatenate([_bf(sol[:, :LANES]), rh_s[it]], axis=0), _bf(h))
        u = _bf(gmat[:two] + sol[:, LANES:])
        y_st = gmat[two:] + _dot(rbm_s[it], u) + rkv_s[it]
        y_ref[0, ts, ls] = y_st[:n] + y_st[n:]
        h_ref[pair] = dec_s[it] * h + _dot_tn(bc_s[it], u) + kv_s[it]


def _wkv(r, c, lw, k, v, a, b):
    batch, t, width = r.shape
    npairs = width // LANES
    items = (WKV_STEP // WKV_CHUNK) * npairs
    spec = pl.BlockSpec((1, WKV_STEP, width), lambda bi, i: (bi, i, 0))
    sq = lambda dtype: pltpu.VMEM((items, LANES, LANES), dtype)
    return pl.pallas_call(
        _wkv_kernel,
        grid=(batch, t // WKV_STEP),
        in_specs=[spec] * 7,
        out_specs=spec,
        out_shape=jax.ShapeDtypeStruct((batch, t, width), F32),
        scratch_shapes=[pltpu.VMEM((npairs, LANES, LANES), F32)] + [sq(BF16)] * 8
                       + [pltpu.VMEM((items, LANES, 2 * LANES), F32), sq(F32), sq(F32), sq(F32)],
        compiler_params=_cparams("parallel", "arbitrary"),
        name="wkv7",
    )(r, c, lw, k, v, a, b)


def _post_outproj_kernel(x_ref, ygm_ref, ys_ref, bonus_ref, g_ref, lnw_ref, lnb_ref, wtop_ref, wbot_ref, o_ref):
    width = ys_ref.shape[1]
    yn = _head_layernorm(ys_ref[...], _head_avg_matrix(width), lnw_ref[...], lnb_ref[...], GN_EPS)
    y_rw = (yn + bonus_ref[...]) * g_ref[...]
    o_ref[...] = x_ref[...] + _dot(_bf(ygm_ref[...]), wtop_ref[...]) + _dot(_bf(y_rw), wbot_ref[...])


def _post_outproj(x2, y_gm, y_scan, bonus, g, lnx_w, lnx_b, w_out):
    n, d = x2.shape
    width = y_gm.shape[1]
    rows = lambda w: pl.BlockSpec((ROW_TILE, w), lambda i: (i, 0))
    vec = pl.BlockSpec((1, width), lambda i: (0, 0))
    wspec = pl.BlockSpec((width, d), lambda i: (0, 0))
    w_out = _bf(w_out)
    return pl.pallas_call(
        _post_outproj_kernel,
        grid=(n // ROW_TILE,),
        in_specs=[rows(d), rows(width), rows(width), rows(width), rows(width), vec, vec, wspec, wspec],
        out_specs=rows(d),
        out_shape=jax.ShapeDtypeStruct((n, d), F32),
        compiler_params=_cparams("parallel"),
        name="post_outproj",
    )(x2, y_gm, y_scan, bonus, g, lnx_w.reshape(1, width), lnx_b.reshape(1, width),
      w_out[:width], w_out[width:])


def _swiglu_kernel(x_ref, g_ref, wg_ref, wu_ref, wd_ref, o_ref, h_ref):
    j = pl.program_id(1)

    @pl.when(j == 0)
    def _():
        x = x_ref[...]
        h_ref[...] = _bf(_rmsnorm(x, g_ref[...]))
        o_ref[...] = x

    h = h_ref[...]
    act = jax.nn.silu(_dot(h, wg_ref[...])) * _dot(h, wu_ref[...])
    o_ref[...] += _dot(_bf(act), wd_ref[...])


def _ffn_tile(d_ff, parts):
    assert d_ff % (parts * LANES) == 0, f"d_ff={d_ff} does not split into {parts} lane-aligned tiles"
    return d_ff // parts


def _swiglu(x2, gain, wg, wu, wd):
    n, d = x2.shape
    d_ff = wg.shape[1]
    tf = _ffn_tile(d_ff, 2)
    return pl.pallas_call(
        _swiglu_kernel,
        grid=(n // ROW_TILE, d_ff // tf),
        in_specs=[pl.BlockSpec((ROW_TILE, d), lambda i, j: (i, 0)),
                  pl.BlockSpec((1, d), lambda i, j: (0, 0)),
                  pl.BlockSpec((d, tf), lambda i, j: (0, j)),
                  pl.BlockSpec((d, tf), lambda i, j: (0, j)),
                  pl.BlockSpec((tf, d), lambda i, j: (j, 0))],
        out_specs=pl.BlockSpec((ROW_TILE, d), lambda i, j: (i, 0)),
        out_shape=jax.ShapeDtypeStruct((n, d), F32),
        scratch_shapes=[pltpu.VMEM((ROW_TILE, d), BF16)],
        compiler_params=_cparams("parallel", "arbitrary"),
        name="dense_swiglu",
    )(x2, gain.reshape(1, d), _bf(wg), _bf(wu), _bf(wd))


META_E, META_G, META_RANK = 0, 2, 4


def _router_kernel(x_ref, g_ref, wr_ref, h_ref, meta_ref, cnt_ref, run_ref):
    @pl.when(pl.program_id(0) == 0)
    def _():
        run_ref[...] = jnp.zeros_like(run_ref)

    h = _rmsnorm(x_ref[...], g_ref[...])
    h_ref[...] = h
    rows = h.shape[0]
    lane = lax.broadcasted_iota(jnp.int32, (rows, LANES), 1)
    logits = jnp.dot(h, wr_ref[...], preferred_element_type=F32, precision=lax.Precision.HIGHEST)
    logits = jnp.where(lane < N_EXPERTS, logits, NEG_BIG)
    v1 = jnp.max(logits, axis=-1, keepdims=True)
    e1 = jnp.min(jnp.where(logits == v1, lane, LANES), axis=-1, keepdims=True)
    oh1 = lane == e1
    rest = jnp.where(oh1, NEG_BIG, logits)
    v2 = jnp.max(rest, axis=-1, keepdims=True)
    e2 = jnp.min(jnp.where(rest == v2, lane, LANES), axis=-1, keepdims=True)
    oh2 = lane == e2
    ex = jnp.exp(v2 - v1)
    g1 = 1.0 / (1.0 + ex)
    g2 = ex / (1.0 + ex)
    cnt = jnp.where(oh1 | oh2, 1.0, 0.0)
    t = lax.broadcasted_iota(jnp.int32, (rows, rows), 0)
    s = lax.broadcasted_iota(jnp.int32, (rows, rows), 1)
    before = _dot(jnp.where(s < t, 1.0, 0.0).astype(BF16), _bf(cnt)) + run_ref[0:1, :]
    rank1 = jnp.sum(jnp.where(oh1, before, 0.0), axis=-1, keepdims=True)
    rank2 = jnp.sum(jnp.where(oh2, before, 0.0), axis=-1, keepdims=True)
    fields = (e1.astype(F32), e2.astype(F32), g1, g2, rank1, rank2)
    meta = jnp.zeros((rows, LANES), F32)
    for idx, val in enumerate(fields):
        meta = jnp.where(lane == idx, val, meta)
    meta_ref[...] = meta
    run = run_ref[0:1, :] + jnp.sum(cnt, axis=0, keepdims=True)
    run_ref[...] = jnp.broadcast_to(run, run_ref.shape)
    cnt_ref[...] = jnp.broadcast_to(run, cnt_ref.shape)


def _router(x2, gain, w_router):
    n, d = x2.shape
    wr = jnp.concatenate([w_router, jnp.zeros((d, LANES - w_router.shape[1]), F32)], axis=1)
    return pl.pallas_call(
        _router_kernel,
        grid=(n // ROW_TILE,),
        in_specs=[pl.BlockSpec((ROW_TILE, d), lambda i: (i, 0)),
                  pl.BlockSpec((1, d), lambda i: (0, 0)),
                  pl.BlockSpec((d, LANES), lambda i: (0, 0))],
        out_specs=[pl.BlockSpec((ROW_TILE, d), lambda i: (i, 0)),
                   pl.BlockSpec((ROW_TILE, LANES), lambda i: (i, 0)),
                   pl.BlockSpec((8, LANES), lambda i: (0, 0))],
        out_shape=[jax.ShapeDtypeStruct((n, d), F32),
                   jax.ShapeDtypeStruct((n, LANES), F32),
                   jax.ShapeDtypeStruct((8, LANES), F32)],
        scratch_shapes=[pltpu.VMEM((8, LANES), F32)],
        compiler_params=_cparams("arbitrary"),
        name="moe_router",
    )(x2, gain.reshape(1, d), wr)


def _sc_workers():
    info = plsc.get_sparse_core_info()
    return info.num_cores, info.num_cores * info.num_subcores


def _dispatch_rows(h, dest0, dest1, pad_rows):
    nc, workers = _sc_workers()
    n, d = h.shape
    n_pad = pad_rows.shape[0]
    per_worker = n // workers
    nchunk = per_worker // SC_WINDOW
    npad = n_pad // (workers * SC_WINDOW)
    assert n == workers * nchunk * SC_WINDOW and nchunk % 2 == 0 and n_pad == workers * npad * SC_WINDOW
    mesh = plsc.VectorSubcoreMesh(core_axis_name="c", subcore_axis_name="s")
    idx_t = lambda m: pltpu.VMEM((m, SC_WINDOW), jnp.int32)
    buf_t = pltpu.VMEM((SC_WINDOW, d), h.dtype)

    @functools.partial(
        pl.kernel, mesh=mesh,
        out_type=jax.ShapeDtypeStruct((2 * n + n_pad, d), h.dtype),
        scratch_types=[idx_t(nchunk), idx_t(nchunk), idx_t(npad), buf_t, buf_t, buf_t]
                      + [pltpu.SemaphoreType.DMA] * 6,
    )
    def dispatch_kernel(h_hbm, d0_hbm, d1_hbm, pad_hbm, zero_hbm, out_hbm, d0_v, d1_v, pad_v, buf0, buf1, zbuf,
                        r0, r1, w0, w1, x0, x1):
        wid = lax.axis_index("s") * nc + lax.axis_index("c")
        base = wid * per_worker
        pltpu.sync_copy(d0_hbm.at[wid], d0_v)
        pltpu.sync_copy(d1_hbm.at[wid], d1_v)
        pltpu.sync_copy(pad_hbm.at[wid], pad_v)
        pltpu.sync_copy(zero_hbm, zbuf)
        bufs, rsem, wsem, xsem = (buf0, buf1), (r0, r1), (w0, w1), (x0, x1)

        def read(c, slot):
            return pltpu.make_async_copy(h_hbm.at[pl.ds(base + c * SC_WINDOW, SC_WINDOW)], bufs[slot], rsem[slot])

        read(0, 0).start()
        read(1, 1).start()
        for pc in range(npad):
            pltpu.sync_copy(zbuf, out_hbm.at[pad_v.at[pc]])

        @pl.loop(0, nchunk, step=2)
        def _(c):
            for slot in range(2):
                cc = c + slot
                read(cc, slot).wait()
                first = pltpu.make_async_copy(bufs[slot], out_hbm.at[d0_v.at[cc]], wsem[slot])
                second = pltpu.make_async_copy(bufs[slot], out_hbm.at[d1_v.at[cc]], xsem[slot])
                first.start()
                second.start()
                first.wait()
                second.wait()

                @pl.when(cc + 2 < nchunk)
                def _():
                    read(cc + 2, slot).start()

    split = lambda z, m: z.reshape(workers, m, SC_WINDOW)
    return dispatch_kernel(h, split(dest0, nchunk), split(dest1, nchunk), split(pad_rows, npad),
                           jnp.zeros((SC_WINDOW, d), h.dtype))


def _gather_rows(src, idx):
    info = plsc.get_sparse_core_info()
    nc, ns = info.num_cores, info.num_subcores
    workers = nc * ns
    m, d = idx.shape[0], src.shape[1]
    per_worker = m // workers
    nchunk = per_worker // SC_WINDOW
    assert m == workers * nchunk * SC_WINDOW and nchunk % 2 == 0, "row count must split evenly over subcores"
    mesh = plsc.VectorSubcoreMesh(core_axis_name="c", subcore_axis_name="s")

    @functools.partial(
        pl.kernel, mesh=mesh,
        out_type=jax.ShapeDtypeStruct((m, d), src.dtype),
        scratch_types=[pltpu.VMEM((nchunk, SC_WINDOW), jnp.int32),
                       pltpu.VMEM((SC_WINDOW, d), src.dtype), pltpu.VMEM((SC_WINDOW, d), src.dtype),
                       pltpu.SemaphoreType.DMA, pltpu.SemaphoreType.DMA,
                       pltpu.SemaphoreType.DMA, pltpu.SemaphoreType.DMA],
    )
    def gather_kernel(src_hbm, idx_hbm, out_hbm, idx_v, buf0, buf1, g0, g1, w0, w1):
        wid = lax.axis_index("s") * nc + lax.axis_index("c")
        base = wid * per_worker
        pltpu.sync_copy(idx_hbm.at[wid], idx_v)
        bufs, gsem, wsem = (buf0, buf1), (g0, g1), (w0, w1)

        def gather(c, slot):
            return pltpu.make_async_copy(src_hbm.at[idx_v.at[c]], bufs[slot], gsem[slot])

        def write(c, slot):
            return pltpu.make_async_copy(bufs[slot], out_hbm.at[pl.ds(base + c * SC_WINDOW, SC_WINDOW)], wsem[slot])

        gather(0, 0).start()
        gather(1, 1).start()

        @pl.loop(0, nchunk, step=2)
        def _(c):
            for slot in range(2):
                cc = c + slot
                gather(cc, slot).wait()
                write(cc, slot).start()
                write(cc, slot).wait()

                @pl.when(cc + 2 < nchunk)
                def _():
                    gather(cc + 2, slot).start()

    return gather_kernel(src, idx.reshape(workers, nchunk, SC_WINDOW))


def _expert_kernel(be_ref, nu_ref, x_ref, wg_ref, wu_ref, wd_ref, o_ref, h_ref):
    i, j = pl.program_id(0), pl.program_id(1)

    @pl.when(i < nu_ref[0])
    def _():
        @pl.when(j == 0)
        def _():
            h_ref[...] = _bf(x_ref[...])
            o_ref[...] = jnp.zeros_like(o_ref)

        h = h_ref[...]
        act = jax.nn.silu(_dot(h, wg_ref[0])) * _dot(h, wu_ref[0])
        o_ref[...] += _dot(_bf(act), wd_ref[0])

    @pl.when((i >= nu_ref[0]) & (j == 0))
    def _():
        o_ref[...] = jnp.zeros_like(o_ref)


def _expert_swiglu(xb, block_expert, n_used, wg, wu, wd):
    n_rows, d = xb.shape
    d_ff = wg.shape[2]
    tf = _ffn_tile(d_ff, 4)
    nf = d_ff // tf
    nblk = n_rows // MOE_BLOCK

    def row(i, j, be, nu):
        return jnp.minimum(i, nu[0] - 1)

    def col(i, j, be, nu):
        return jnp.where(i < nu[0], j, nf - 1)

    grid_spec = pltpu.PrefetchScalarGridSpec(
        num_scalar_prefetch=2,
        grid=(nblk, nf),
        in_specs=[pl.BlockSpec((MOE_BLOCK, d), lambda i, j, be, nu: (row(i, j, be, nu), 0)),
                  pl.BlockSpec((1, d, tf), lambda i, j, be, nu: (be[row(i, j, be, nu)], 0, col(i, j, be, nu))),
                  pl.BlockSpec((1, d, tf), lambda i, j, be, nu: (be[row(i, j, be, nu)], 0, col(i, j, be, nu))),
                  pl.BlockSpec((1, tf, d), lambda i, j, be, nu: (be[row(i, j, be, nu)], col(i, j, be, nu), 0))],
        out_specs=pl.BlockSpec((MOE_BLOCK, d), lambda i, j, be, nu: (i, 0)),
        scratch_shapes=[pltpu.VMEM((MOE_BLOCK, d), BF16)],
    )
    return pl.pallas_call(
        _expert_kernel,
        grid_spec=grid_spec,
        out_shape=jax.ShapeDtypeStruct((n_rows, d), F32),
        compiler_params=_cparams("arbitrary", "arbitrary"),
        name="expert_swiglu",
    )(block_expert, n_used, xb, _bf(wg), _bf(wu), _bf(wd))


def _combine_kernel(x_ref, y1_ref, y2_ref, meta_ref, g_ref, o_ref):
    meta = meta_ref[...]
    lane = lax.broadcasted_iota(jnp.int32, meta.shape, 1)
    g1 = jnp.sum(jnp.where(lane == META_G, meta, 0.0), axis=-1, keepdims=True)
    g2 = jnp.sum(jnp.where(lane == META_G + 1, meta, 0.0), axis=-1, keepdims=True)
    y = x_ref[...] + (g1 * y1_ref[...] + g2 * y2_ref[...])
    o_ref[...] = _rmsnorm(y, g_ref[...])


def _combine_final_norm(x2, yg, meta, gain):
    n, d = x2.shape
    nblk = n // ROW_TILE
    return pl.pallas_call(
        _combine_kernel,
        grid=(nblk,),
        in_specs=[pl.BlockSpec((ROW_TILE, d), lambda i: (i, 0)),
                  pl.BlockSpec((ROW_TILE, d), lambda i: (i, 0)),
                  pl.BlockSpec((ROW_TILE, d), lambda i: (i + nblk, 0)),
                  pl.BlockSpec((ROW_TILE, LANES), lambda i: (i, 0)),
                  pl.BlockSpec((1, d), lambda i: (0, 0))],
        out_specs=pl.BlockSpec((ROW_TILE, d), lambda i: (i, 0)),
        out_shape=jax.ShapeDtypeStruct((n, d), F32),
        compiler_params=_cparams("parallel"),
        name="moe_combine_norm",
    )(x2, yg, yg, meta, gain.reshape(1, d))


def _moe_layer(x2, gain, w_router, wg, wu, wd, final_gain):
    n, d = x2.shape
    h, meta, counts = _router(x2, gain, w_router)
    expert = meta[:, META_E:META_E + 2].astype(jnp.int32)
    rank = meta[:, META_RANK:META_RANK + 2].astype(jnp.int32)
    count = counts[0, :N_EXPERTS].astype(jnp.int32)
    padded = (count + MOE_BLOCK - 1) // MOE_BLOCK * MOE_BLOCK
    pad_end = jnp.cumsum(padded)
    pad_start = pad_end - padded
    dest = pad_start[expert] + rank
    n_rows = (2 * n // MOE_BLOCK + N_EXPERTS) * MOE_BLOCK
    empties = padded - count
    e_end = jnp.cumsum(empties)
    i = jnp.arange(n_rows - 2 * n, dtype=jnp.int32)
    grp = jnp.minimum(jnp.searchsorted(e_end, i, side='right'), N_EXPERTS).astype(jnp.int32)
    first_empty = jnp.concatenate([pad_start + count, pad_end[-1:]])
    first_index = jnp.concatenate([e_end - empties, e_end[-1:]])
    pad_rows = (first_empty[grp] + i - first_index[grp]).astype(jnp.int32)
    block_start = jnp.arange(n_rows // MOE_BLOCK, dtype=jnp.int32) * MOE_BLOCK
    block_expert = jnp.minimum(jnp.searchsorted(pad_end, block_start, side='right'), N_EXPERTS - 1).astype(jnp.int32)
    n_used = (pad_end[-1:] // MOE_BLOCK).astype(jnp.int32)
    xb = _dispatch_rows(h, dest[:, 0], dest[:, 1], pad_rows)
    ys = _expert_swiglu(xb, block_expert, n_used, wg, wu, wd)
    yg = _gather_rows(ys, jnp.concatenate([dest[:, 0], dest[:, 1]]))
    return _combine_final_norm(x2, yg, meta, final_gain)


def _mixer_layer(x2, batch, norm_mix, w_in, w_out, shift_mu, gm_ln_w, gm_ln_b, gm_ws, gm_bs,
                 rw_w_up, rw_w0, rw_a_up, rw_a0, rw_g_up, rw_k_k, rw_k_a, rw_r_k, rw_lnx_w, rw_lnx_b):
    n, d = x2.shape
    gm2 = 2 * gm_ln_w.shape[0]
    w_in = _bf(w_in)
    p_gm, p_rw = _norm_inproj(x2, norm_mix, w_in[:, :gm2], w_in[:, gm2:])
    y_gm = _gmlp(p_gm, gm_ln_w, gm_ln_b, gm_ws, gm_bs)
    r, c, lw, k, v, a, b, g, bonus = _rwkv_prep(p_rw, batch, shift_mu, rw_w_up, rw_w0, rw_a_up, rw_a0, rw_g_up,
                                             rw_k_k, rw_k_a, rw_r_k.reshape(-1))
    y_scan = _wkv(r, c, lw, k, v, a, b)
    flat = lambda z: z.reshape(n, -1)
    return _post_outproj(x2, y_gm, flat(y_scan), flat(bonus), flat(g), rw_lnx_w, rw_lnx_b, w_out)


def kernel(x, norm_mix, w_in, w_out, shift_mu, gm_ln_w, gm_ln_b, gm_ws, gm_bs, rw_w_up, rw_w0, rw_a_up, rw_a0,
           rw_g_up, rw_k_k, rw_k_a, rw_r_k, rw_lnx_w, rw_lnx_b, norm_ffn, ffn_w_gate, ffn_w_up, ffn_w_down,
           moe_router, moe_w_gate, moe_w_up, moe_w_down, norm_final):
    batch, t, d = x.shape
    depth = norm_mix.shape[0]
    assert depth == 2 and t % ROW_TILE == 0, "two layers (dense then MoE), sequence a multiple of the row tile"
    x2 = x.reshape(batch * t, d)
    for i in range(depth):
        x2 = _mixer_layer(x2, batch, norm_mix[i], w_in[i], w_out[i], shift_mu[i], gm_ln_w[i], gm_ln_b[i],
                          gm_ws[i], gm_bs[i], rw_w_up[i], rw_w0[i], rw_a_up[i], rw_a0[i], rw_g_up[i],
                          rw_k_k[i], rw_k_a[i], rw_r_k[i], rw_lnx_w[i], rw_lnx_b[i])
        if i % 2 == 0:
            x2 = _swiglu(x2, norm_ffn[i], ffn_w_gate[i // 2], ffn_w_up[i // 2], ffn_w_down[i // 2])
        else:
            x2 = _moe_layer(x2, norm_ffn[i], moe_router[i // 2], moe_w_gate[i // 2], moe_w_up[i // 2],
                            moe_w_down[i // 2], norm_final)
    return x2.reshape(batch, t, d)
```

```python
import functools

import jax
import jax.numpy as jnp
from jax import lax
from jax.experimental import pallas as pl
from jax.experimental.pallas import tpu as pltpu
from jax.experimental.pallas import tpu_sc as plsc

F32 = jnp.float32
BF16 = jnp.bfloat16

HEAD_DIM = 64
LANES = 128
GM_CHUNK = 128
WKV_CHUNK = 64
N_EXPERTS = 8
RMS_EPS = 1e-6
LN_EPS = 1e-5
GN_EPS = 64e-5
NEG_BIG = -1e30

ROW_TILE = 512
WKV_STEP = 256
MOE_BLOCK = 512
SC_WINDOW = 32
VMEM_LIMIT = 56 * 1024 * 1024


def _cparams(*sem):
    return pltpu.CompilerParams(dimension_semantics=sem, vmem_limit_bytes=VMEM_LIMIT)


def _bf(x):
    return x.astype(BF16)


def _dot(a, b):
    return jnp.dot(a, b, preferred_element_type=F32)


def _dot_nt(a, b):
    return lax.dot_general(a, b, (((1,), (1,)), ((), ())), preferred_element_type=F32)


def _dot_tn(a, b):
    return lax.dot_general(a, b, (((0,), (0,)), ((), ())), preferred_element_type=F32)


def _split_dot(x, m, parts=3):
    acc = None
    rem = x
    for p in range(parts):
        hi = _bf(rem)
        d = _dot(hi, m)
        acc = d if acc is None else acc + d
        if p + 1 < parts:
            rem = rem - hi.astype(F32)
    return acc


def _split_dot_left(m, x, parts=3):
    acc = None
    rem = x
    for p in range(parts):
        hi = _bf(rem)
        d = _dot(m, hi)
        acc = d if acc is None else acc + d
        if p + 1 < parts:
            rem = rem - hi.astype(F32)
    return acc


def _rmsnorm(x, g):
    return x * lax.rsqrt(jnp.mean(x * x, axis=-1, keepdims=True) + RMS_EPS) * g


def _gelu(x):
    return 0.5 * x * (1.0 + lax.erf(x * (2.0 ** -0.5)))


def _head_avg_matrix(width):
    r = lax.broadcasted_iota(jnp.int32, (width, width), 0) // HEAD_DIM
    c = lax.broadcasted_iota(jnp.int32, (width, width), 1) // HEAD_DIM
    return jnp.where(r == c, 1.0 / HEAD_DIM, 0.0).astype(BF16)


def _head_layernorm(x, avg, w, b, eps):
    mu = _split_dot(x, avg)
    xc = x - mu
    var = _split_dot(xc * xc, avg)
    return xc * lax.rsqrt(var + eps) * w + b


def _norm_inproj_kernel(x_ref, g_ref, wgm_ref, wrw_ref, pgm_ref, prw_ref):
    h = _bf(_rmsnorm(x_ref[...], g_ref[...]))
    pgm_ref[...] = _dot(h, wgm_ref[...])
    prw_ref[...] = _dot(h, wrw_ref[...])


def _norm_inproj(x2, gain, w_gm, w_rw):
    n, d = x2.shape
    wg, wr = w_gm.shape[1], w_rw.shape[1]
    return pl.pallas_call(
        _norm_inproj_kernel,
        grid=(n // ROW_TILE,),
        in_specs=[pl.BlockSpec((ROW_TILE, d), lambda i: (i, 0)),
                  pl.BlockSpec((1, d), lambda i: (0, 0)),
                  pl.BlockSpec((d, wg), lambda i: (0, 0)),
                  pl.BlockSpec((d, wr), lambda i: (0, 0))],
        out_specs=[pl.BlockSpec((ROW_TILE, wg), lambda i: (i, 0)),
                   pl.BlockSpec((ROW_TILE, wr), lambda i: (i, 0))],
        out_shape=[jax.ShapeDtypeStruct((n, wg), F32), jax.ShapeDtypeStruct((n, wr), F32)],
        compiler_params=_cparams("parallel"),
        name="norm_inproj",
    )(x2, gain.reshape(1, d), w_gm, w_rw)


def _gmlp_kernel(p_ref, lnw_ref, lnb_ref, ws_ref, bias_ref, o_ref):
    width = o_ref.shape[1]
    p = p_ref[...]
    u = _gelu(p[:, :width])
    v = _gelu(p[:, width:])
    vn = _head_layernorm(v, _head_avg_matrix(width), lnw_ref[...], lnb_ref[...], LN_EPS)
    t = lax.broadcasted_iota(jnp.int32, (GM_CHUNK, GM_CHUNK), 0)
    s = lax.broadcasted_iota(jnp.int32, (GM_CHUNK, GM_CHUNK), 1)
    causal = s <= t
    head0 = lax.broadcasted_iota(jnp.int32, (GM_CHUNK, LANES), 1) < HEAD_DIM
    mixed = []
    for pair in range(width // LANES):
        vp = vn[:, pair * LANES:(pair + 1) * LANES]
        v_st = _bf(jnp.concatenate([jnp.where(head0, vp, 0.0), jnp.where(head0, 0.0, vp)], axis=0))
        w2 = _bf(jnp.concatenate([jnp.where(causal, ws_ref[2 * pair], 0.0),
                                  jnp.where(causal, ws_ref[2 * pair + 1], 0.0)], axis=1))
        mixed.append(_dot(w2, v_st))
    o_ref[...] = u * (jnp.concatenate(mixed, axis=1) + bias_ref[...])


def _gmlp(p_gm, ln_w, ln_b, ws, bs):
    n, w2 = p_gm.shape
    width = w2 // 2
    bias = jnp.repeat(bs.T, HEAD_DIM, axis=1)
    return pl.pallas_call(
        _gmlp_kernel,
        grid=(n // GM_CHUNK,),
        in_specs=[pl.BlockSpec((GM_CHUNK, w2), lambda i: (i, 0)),
                  pl.BlockSpec((1, width), lambda i: (0, 0)),
                  pl.BlockSpec((1, width), lambda i: (0, 0)),
                  pl.BlockSpec(ws.shape, lambda i: (0, 0, 0)),
                  pl.BlockSpec((GM_CHUNK, width), lambda i: (0, 0))],
        out_specs=pl.BlockSpec((GM_CHUNK, width), lambda i: (i, 0)),
        out_shape=jax.ShapeDtypeStruct((n, width), F32),
        compiler_params=_cparams("parallel"),
        name="gmlp",
    )(p_gm, ln_w.reshape(1, width), ln_b.reshape(1, width), ws, bias)


def _rwkv_prep_kernel(p_ref, mu_ref, wup_ref, w0_ref, aup_ref, a0_ref, gup_ref, kk_ref, ka_ref, rk_ref,
                      r_ref, c_ref, lw_ref, k_ref, v_ref, a_ref, b_ref, g_ref, bonus_ref, carry_ref):
    width = r_ref.shape[2]

    @pl.when(pl.program_id(1) == 0)
    def _():
        carry_ref[...] = jnp.zeros_like(carry_ref)

    p = p_ref[0]
    rows = p.shape[0]
    first = lax.broadcasted_iota(jnp.int32, p.shape, 0) == 0
    prev = jnp.where(first, carry_ref[0:1, :], pltpu.roll(p, 1, axis=0))
    carry_ref[0:1, :] = p[rows - 1:rows, :]
    ps = p + (prev - p) * mu_ref[...]
    r = ps[:, :width]
    k = ps[:, width:2 * width]
    v = ps[:, 2 * width:3 * width]
    lora_in = ps[:, 3 * width:3 * width + LANES]
    gd = ps[:, 3 * width + LANES:]
    log_w = -jax.nn.softplus(-(w0_ref[...] + _dot(_bf(jnp.tanh(lora_in)), wup_ref[...]))) - 0.5
    lw = -jnp.exp(log_w)
    a_lr = jax.nn.sigmoid(a0_ref[...] + _dot(_bf(lora_in), aup_ref[...]))
    g = _dot(_bf(jax.nn.sigmoid(gd)), gup_ref[...])
    ones = _head_avg_matrix(width) * HEAD_DIM
    kk = k * kk_ref[...]
    kk = kk / jnp.maximum(jnp.sqrt(_split_dot(kk * kk, ones)), 1e-12)
    kmod = k * (1.0 + (a_lr - 1.0) * ka_ref[...])
    r_ref[0] = r
    lw_ref[0] = lw
    t = lax.broadcasted_iota(jnp.int32, (rows, rows), 0)
    s = lax.broadcasted_iota(jnp.int32, (rows, rows), 1)
    tri = ((t // WKV_CHUNK == s // WKV_CHUNK) & (s <= t)).astype(BF16)
    c_ref[0] = _split_dot_left(tri, lw)
    k_ref[0] = kmod
    v_ref[0] = v
    a_ref[0] = -kk
    b_ref[0] = kk * a_lr
    g_ref[0] = g
    bonus_ref[0] = _split_dot(r * kmod * rk_ref[...], ones) * v


def _rwkv_prep(p_rw, batch, mu, w_up, w0, a_up, a0, g_up, k_k, k_a, r_k):
    n, win = p_rw.shape
    t = n // batch
    width = w0.shape[0]
    rank = w_up.shape[0]
    zeros = jnp.zeros((LANES - rank, width), F32)
    wup_pad = _bf(jnp.concatenate([w_up, zeros], axis=0))
    aup_pad = _bf(jnp.concatenate([zeros, a_up], axis=0))
    row = lambda z: z.reshape(1, -1)
    vec = pl.BlockSpec((1, width), lambda b, i: (0, 0))
    out = pl.BlockSpec((1, ROW_TILE, width), lambda b, i: (b, i, 0))
    return pl.pallas_call(
        _rwkv_prep_kernel,
        grid=(batch, t // ROW_TILE),
        in_specs=[pl.BlockSpec((1, ROW_TILE, win), lambda b, i: (b, i, 0)),
                  pl.BlockSpec((1, win), lambda b, i: (0, 0)),
                  pl.BlockSpec((LANES, width), lambda b, i: (0, 0)), vec,
                  pl.BlockSpec((LANES, width), lambda b, i: (0, 0)), vec,
                  pl.BlockSpec(g_up.shape, lambda b, i: (0, 0)), vec, vec, vec],
        out_specs=[out] * 9,
        out_shape=[jax.ShapeDtypeStruct((batch, t, width), F32)] * 9,
        scratch_shapes=[pltpu.VMEM((8, win), F32)],
        compiler_params=_cparams("parallel", "arbitrary"),
        name="rwkv_prep",
    )(p_rw.reshape(batch, t, win), row(mu), wup_pad, row(w0), aup_pad, row(a0), _bf(g_up),
      row(k_k), row(k_a), row(r_k))


def _wkv_kernel(r_ref, c_ref, lw_ref, k_ref, v_ref, a_ref, b_ref, y_ref, h_ref,
                pw_s, akm_s, rbm_s, rkm_s, v_s, kc_s, rh_s, bc_s, sol_s, rkv_s, kv_s, dec_s):
    @pl.when(pl.program_id(1) == 0)
    def _():
        h_ref[...] = jnp.zeros_like(h_ref)

    n = WKV_CHUNK
    two = 2 * n
    head0 = lax.broadcasted_iota(jnp.int32, (n, LANES), 1) < HEAD_DIM
    row = lax.broadcasted_iota(jnp.int32, (two, two), 0)
    col = lax.broadcasted_iota(jnp.int32, (two, two), 1)
    same = (row // n) == (col // n)
    strict = same & ((col % n) < (row % n))
    incl = same & ((col % n) <= (row % n))
    eye = row == col
    npairs = y_ref.shape[2] // LANES
    nchunks = y_ref.shape[1] // n
    items = [(ch, pair) for ch in range(nchunks) for pair in range(npairs)]

    def window(ch, pair):
        return slice(ch * n, (ch + 1) * n), slice(pair * LANES, (pair + 1) * LANES)

    def stack(z):
        return jnp.concatenate([jnp.where(head0, z, 0.0), jnp.where(head0, 0.0, z)], axis=0)

    def dup(z):
        zb = _bf(z)
        return jnp.concatenate([zb, zb], axis=0)

    for it, (ch, pair) in enumerate(items):
        ts, ls = window(ch, pair)
        r, k, v, a, b = (ref[0, ts, ls] for ref in (r_ref, k_ref, v_ref, a_ref, b_ref))
        c = c_ref[0, ts, ls]
        cex = c - lw_ref[0, ts, ls]
        mid = c[n // 2 - 1:n // 2, :]
        last = c[n - 1:n, :]
        g_inv = jnp.exp(mid - c)
        g_end = jnp.exp(last - c)
        a_st = _bf(stack(a * jnp.exp(cex - mid)))
        r_st = _bf(stack(r * jnp.exp(c - mid)))
        b_dup, k_dup = dup(b * g_inv), dup(k * g_inv)
        pw_s[it] = _bf(jnp.where(strict, _dot_nt(a_st, b_dup), 0.0))
        akm_s[it] = _bf(jnp.where(strict, _dot_nt(a_st, k_dup), 0.0))
        rbm_s[it] = _bf(jnp.where(incl, _dot_nt(r_st, b_dup), 0.0))
        rkm_s[it] = _bf(jnp.where(incl, _dot_nt(r_st, k_dup), 0.0))
        v_s[it] = _bf(stack(v))
        kc_s[it] = _bf(stack(k * g_end))
        bc_s[it] = _bf(stack(b * g_end))
        rh_s[it] = _bf(stack(r * jnp.exp(c)))
        sol_s[it, :, :LANES] = stack(a * jnp.exp(cex))
        e_col = jnp.sum(jnp.where(eye, jnp.exp(last), 0.0), axis=1, keepdims=True)
        dec_s[it] = jnp.broadcast_to(e_col, (two, LANES))

    for it in range(len(items)):
        v_st = v_s[it]
        sol_s[it, :, LANES:] = _dot(akm_s[it], v_st)
        rkv_s[it] = _dot(rkm_s[it], v_st)
        kv_s[it] = _dot_tn(kc_s[it], v_st)

    levels = n.bit_length() - 1
    for level in range(levels):
        for it in range(len(items)):
            pw = pw_s[it]
            sol = sol_s[it]
            sol_s[it] = sol + _dot(pw, _bf(sol))
            if level + 1 < levels:
                pw_s[it] = _bf(_dot(pw, pw))

    for it, (ch, pair) in enumerate(items):
        ts, ls = window(ch, pair)
        h = h_ref[pair]
        sol = sol_s[it]
        gmat = _dot(jnp.concatenate([_bf(sol[:, :LANES]), rh_s[it]], axis=0), _bf(h))
        u = _bf(gmat[:two] + sol[:, LANES:])
        y_st = gmat[two:] + _dot(rbm_s[it], u) + rkv_s[it]
        y_ref[0, ts, ls] = y_st[:n] + y_st[n:]
        h_ref[pair] = dec_s[it] * h + _dot_tn(bc_s[it], u) + kv_s[it]


def _wkv(r, c, lw, k, v, a, b):
    batch, t, width = r.shape
    npairs = width // LANES
    items = (WKV_STEP // WKV_CHUNK) * npairs
    spec = pl.BlockSpec((1, WKV_STEP, width), lambda bi, i: (bi, i, 0))
    sq = lambda dtype: pltpu.VMEM((items, LANES, LANES), dtype)
    return pl.pallas_call(
        _wkv_kernel,
        grid=(batch, t // WKV_STEP),
        in_specs=[spec] * 7,
        out_specs=spec,
        out_shape=jax.ShapeDtypeStruct((batch, t, width), F32),
        scratch_shapes=[pltpu.VMEM((npairs, LANES, LANES), F32)] + [sq(BF16)] * 8
                       + [pltpu.VMEM((items, LANES, 2 * LANES), F32), sq(F32), sq(F32), sq(F32)],
        compiler_params=_cparams("parallel", "arbitrary"),
        name="wkv7",
    )(r, c, lw, k, v, a, b)


def _post_outproj_kernel(x_ref, ygm_ref, ys_ref, bonus_ref, g_ref, lnw_ref, lnb_ref, wtop_ref, wbot_ref, o_ref):
    width = ys_ref.shape[1]
    yn = _head_layernorm(ys_ref[...], _head_avg_matrix(width), lnw_ref[...], lnb_ref[...], GN_EPS)
    y_rw = (yn + bonus_ref[...]) * g_ref[...]
    o_ref[...] = x_ref[...] + _dot(_bf(ygm_ref[...]), wtop_ref[...]) + _dot(_bf(y_rw), wbot_ref[...])


def _post_outproj(x2, y_gm, y_scan, bonus, g, lnx_w, lnx_b, w_out):
    n, d = x2.shape
    width = y_gm.shape[1]
    rows = lambda w: pl.BlockSpec((ROW_TILE, w), lambda i: (i, 0))
    vec = pl.BlockSpec((1, width), lambda i: (0, 0))
    wspec = pl.BlockSpec((width, d), lambda i: (0, 0))
    w_out = _bf(w_out)
    return pl.pallas_call(
        _post_outproj_kernel,
        grid=(n // ROW_TILE,),
        in_specs=[rows(d), rows(width), rows(width), rows(width), rows(width), vec, vec, wspec, wspec],
        out_specs=rows(d),
        out_shape=jax.ShapeDtypeStruct((n, d), F32),
        compiler_params=_cparams("parallel"),
        name="post_outproj",
    )(x2, y_gm, y_scan, bonus, g, lnx_w.reshape(1, width), lnx_b.reshape(1, width),
      w_out[:width], w_out[width:])


def _swiglu_kernel(x_ref, g_ref, wg_ref, wu_ref, wd_ref, o_ref):
    x = x_ref[...]
    h = _bf(_rmsnorm(x, g_ref[...]))
    act = jax.nn.silu(_dot(h, wg_ref[...])) * _dot(h, wu_ref[...])
    o_ref[...] = x + _dot(_bf(act), wd_ref[...])


def _ffn_tile(d_ff, parts):
    assert d_ff % (parts * LANES) == 0, f"d_ff={d_ff} does not split into {parts} lane-aligned tiles"
    return d_ff // parts


def _resident(shape):
    return pl.BlockSpec(shape, lambda *_: (0,) * len(shape), pipeline_mode=pl.Buffered(1))


def _swiglu(x2, gain, wg, wu, wd):
    n, d = x2.shape
    d_ff = wg.shape[1]
    return pl.pallas_call(
        _swiglu_kernel,
        grid=(n // ROW_TILE,),
        in_specs=[pl.BlockSpec((ROW_TILE, d), lambda i: (i, 0)),
                  _resident((1, d)), _resident((d, d_ff)), _resident((d, d_ff)), _resident((d_ff, d))],
        out_specs=pl.BlockSpec((ROW_TILE, d), lambda i: (i, 0)),
        out_shape=jax.ShapeDtypeStruct((n, d), F32),
        compiler_params=_cparams("parallel"),
        name="dense_swiglu",
    )(x2, gain.reshape(1, d), _bf(wg), _bf(wu), _bf(wd))


META_E, META_G, META_RANK = 0, 2, 4


def _router_kernel(x_ref, g_ref, wr_ref, h_ref, meta_ref, cnt_ref, run_ref):
    @pl.when(pl.program_id(0) == 0)
    def _():
        run_ref[...] = jnp.zeros_like(run_ref)

    h = _rmsnorm(x_ref[...], g_ref[...])
    h_ref[...] = h
    rows = h.shape[0]
    lane = lax.broadcasted_iota(jnp.int32, (rows, LANES), 1)
    logits = jnp.dot(h, wr_ref[...], preferred_element_type=F32, precision=lax.Precision.HIGHEST)
    logits = jnp.where(lane < N_EXPERTS, logits, NEG_BIG)
    v1 = jnp.max(logits, axis=-1, keepdims=True)
    e1 = jnp.min(jnp.where(logits == v1, lane, LANES), axis=-1, keepdims=True)
    oh1 = lane == e1
    rest = jnp.where(oh1, NEG_BIG, logits)
    v2 = jnp.max(rest, axis=-1, keepdims=True)
    e2 = jnp.min(jnp.where(rest == v2, lane, LANES), axis=-1, keepdims=True)
    oh2 = lane == e2
    ex = jnp.exp(v2 - v1)
    g1 = 1.0 / (1.0 + ex)
    g2 = ex / (1.0 + ex)
    cnt = jnp.where(oh1 | oh2, 1.0, 0.0)
    t = lax.broadcasted_iota(jnp.int32, (rows, rows), 0)
    s = lax.broadcasted_iota(jnp.int32, (rows, rows), 1)
    before = _dot(jnp.where(s < t, 1.0, 0.0).astype(BF16), _bf(cnt)) + run_ref[0:1, :]
    rank1 = jnp.sum(jnp.where(oh1, before, 0.0), axis=-1, keepdims=True)
    rank2 = jnp.sum(jnp.where(oh2, before, 0.0), axis=-1, keepdims=True)
    fields = (e1.astype(F32), e2.astype(F32), g1, g2, rank1, rank2)
    meta = jnp.zeros((rows, LANES), F32)
    for idx, val in enumerate(fields):
        meta = jnp.where(lane == idx, val, meta)
    meta_ref[...] = meta
    run = run_ref[0:1, :] + jnp.sum(cnt, axis=0, keepdims=True)
    run_ref[...] = jnp.broadcast_to(run, run_ref.shape)
    cnt_ref[...] = jnp.broadcast_to(run, cnt_ref.shape)


def _router(x2, gain, w_router):
    n, d = x2.shape
    wr = jnp.concatenate([w_router, jnp.zeros((d, LANES - w_router.shape[1]), F32)], axis=1)
    return pl.pallas_call(
        _router_kernel,
        grid=(n // ROW_TILE,),
        in_specs=[pl.BlockSpec((ROW_TILE, d), lambda i: (i, 0)),
                  pl.BlockSpec((1, d), lambda i: (0, 0)),
                  pl.BlockSpec((d, LANES), lambda i: (0, 0))],
        out_specs=[pl.BlockSpec((ROW_TILE, d), lambda i: (i, 0)),
                   pl.BlockSpec((ROW_TILE, LANES), lambda i: (i, 0)),
                   pl.BlockSpec((8, LANES), lambda i: (0, 0))],
        out_shape=[jax.ShapeDtypeStruct((n, d), F32),
                   jax.ShapeDtypeStruct((n, LANES), F32),
                   jax.ShapeDtypeStruct((8, LANES), F32)],
        scratch_shapes=[pltpu.VMEM((8, LANES), F32)],
        compiler_params=_cparams("arbitrary"),
        name="moe_router",
    )(x2, gain.reshape(1, d), wr)


def _sc_workers():
    info = plsc.get_sparse_core_info()
    return info.num_cores, info.num_cores * info.num_subcores


def _dispatch_rows(h, dest0, dest1, pad_rows):
    nc, workers = _sc_workers()
    n, d = h.shape
    n_pad = pad_rows.shape[0]
    per_worker = n // workers
    nchunk = per_worker // SC_WINDOW
    npad = n_pad // (workers * SC_WINDOW)
    assert n == workers * nchunk * SC_WINDOW and nchunk % 2 == 0 and n_pad == workers * npad * SC_WINDOW
    mesh = plsc.VectorSubcoreMesh(core_axis_name="c", subcore_axis_name="s")
    idx_t = lambda m: pltpu.VMEM((m, SC_WINDOW), jnp.int32)
    buf_t = pltpu.VMEM((SC_WINDOW, d), h.dtype)

    @functools.partial(
        pl.kernel, mesh=mesh,
        out_type=jax.ShapeDtypeStruct((2 * n + n_pad, d), h.dtype),
        scratch_types=[idx_t(nchunk), idx_t(nchunk), idx_t(npad), buf_t, buf_t, buf_t]
                      + [pltpu.SemaphoreType.DMA] * 6,
    )
    def dispatch_kernel(h_hbm, d0_hbm, d1_hbm, pad_hbm, zero_hbm, out_hbm, d0_v, d1_v, pad_v, buf0, buf1, zbuf,
                        r0, r1, w0, w1, x0, x1):
        wid = lax.axis_index("s") * nc + lax.axis_index("c")
        base = wid * per_worker
        pltpu.sync_copy(d0_hbm.at[wid], d0_v)
        pltpu.sync_copy(d1_hbm.at[wid], d1_v)
        pltpu.sync_copy(pad_hbm.at[wid], pad_v)
        pltpu.sync_copy(zero_hbm, zbuf)
        bufs, rsem, wsem, xsem = (buf0, buf1), (r0, r1), (w0, w1), (x0, x1)

        def read(c, slot):
            return pltpu.make_async_copy(h_hbm.at[pl.ds(base + c * SC_WINDOW, SC_WINDOW)], bufs[slot], rsem[slot])

        read(0, 0).start()
        read(1, 1).start()
        for pc in range(npad):
            pltpu.sync_copy(zbuf, out_hbm.at[pad_v.at[pc]])

        @pl.loop(0, nchunk, step=2)
        def _(c):
            for slot in range(2):
                cc = c + slot
                read(cc, slot).wait()
                first = pltpu.make_async_copy(bufs[slot], out_hbm.at[d0_v.at[cc]], wsem[slot])
                second = pltpu.make_async_copy(bufs[slot], out_hbm.at[d1_v.at[cc]], xsem[slot])
                first.start()
                second.start()
                first.wait()
                second.wait()

                @pl.when(cc + 2 < nchunk)
                def _():
                    read(cc + 2, slot).start()

    split = lambda z, m: z.reshape(workers, m, SC_WINDOW)
    return dispatch_kernel(h, split(dest0, nchunk), split(dest1, nchunk), split(pad_rows, npad),
                           jnp.zeros((SC_WINDOW, d), h.dtype))


def _gather_rows(src, idx):
    info = plsc.get_sparse_core_info()
    nc, ns = info.num_cores, info.num_subcores
    workers = nc * ns
    m, d = idx.shape[0], src.shape[1]
    per_worker = m // workers
    nchunk = per_worker // SC_WINDOW
    assert m == workers * nchunk * SC_WINDOW and nchunk % 2 == 0, "row count must split evenly over subcores"
    mesh = plsc.VectorSubcoreMesh(core_axis_name="c", subcore_axis_name="s")

    @functools.partial(
        pl.kernel, mesh=mesh,
        out_type=jax.ShapeDtypeStruct((m, d), src.dtype),
        scratch_types=[pltpu.VMEM((nchunk, SC_WINDOW), jnp.int32),
                       pltpu.VMEM((SC_WINDOW, d), src.dtype), pltpu.VMEM((SC_WINDOW, d), src.dtype),
                       pltpu.SemaphoreType.DMA, pltpu.SemaphoreType.DMA,
                       pltpu.SemaphoreType.DMA, pltpu.SemaphoreType.DMA],
    )
    def gather_kernel(src_hbm, idx_hbm, out_hbm, idx_v, buf0, buf1, g0, g1, w0, w1):
        wid = lax.axis_index("s") * nc + lax.axis_index("c")
        base = wid * per_worker
        pltpu.sync_copy(idx_hbm.at[wid], idx_v)
        bufs, gsem, wsem = (buf0, buf1), (g0, g1), (w0, w1)

        def gather(c, slot):
            return pltpu.make_async_copy(src_hbm.at[idx_v.at[c]], bufs[slot], gsem[slot])

        def write(c, slot):
            return pltpu.make_async_copy(bufs[slot], out_hbm.at[pl.ds(base + c * SC_WINDOW, SC_WINDOW)], wsem[slot])

        gather(0, 0).start()
        gather(1, 1).start()

        @pl.loop(0, nchunk, step=2)
        def _(c):
            for slot in range(2):
                cc = c + slot
                gather(cc, slot).wait()
                write(cc, slot).start()
                write(cc, slot).wait()

                @pl.when(cc + 2 < nchunk)
                def _():
                    gather(cc + 2, slot).start()

    return gather_kernel(src, idx.reshape(workers, nchunk, SC_WINDOW))


def _expert_kernel(be_ref, nu_ref, x_ref, wg_ref, wu_ref, wd_ref, o_ref, h_ref):
    i, j = pl.program_id(0), pl.program_id(1)

    @pl.when(i < nu_ref[0])
    def _():
        @pl.when(j == 0)
        def _():
            h_ref[...] = _bf(x_ref[...])
            o_ref[...] = jnp.zeros_like(o_ref)

        h = h_ref[...]
        act = jax.nn.silu(_dot(h, wg_ref[0])) * _dot(h, wu_ref[0])
        o_ref[...] += _dot(_bf(act), wd_ref[0])

    @pl.when((i >= nu_ref[0]) & (j == 0))
    def _():
        o_ref[...] = jnp.zeros_like(o_ref)


def _expert_swiglu(xb, block_expert, n_used, wg, wu, wd):
    n_rows, d = xb.shape
    d_ff = wg.shape[2]
    tf = _ffn_tile(d_ff, 2)
    nf = d_ff // tf
    nblk = n_rows // MOE_BLOCK

    def row(i, j, be, nu):
        return jnp.minimum(i, nu[0] - 1)

    def col(i, j, be, nu):
        return jnp.where(i < nu[0], j, nf - 1)

    grid_spec = pltpu.PrefetchScalarGridSpec(
        num_scalar_prefetch=2,
        grid=(nblk, nf),
        in_specs=[pl.BlockSpec((MOE_BLOCK, d), lambda i, j, be, nu: (row(i, j, be, nu), 0)),
                  pl.BlockSpec((1, d, tf), lambda i, j, be, nu: (be[row(i, j, be, nu)], 0, col(i, j, be, nu))),
                  pl.BlockSpec((1, d, tf), lambda i, j, be, nu: (be[row(i, j, be, nu)], 0, col(i, j, be, nu))),
                  pl.BlockSpec((1, tf, d), lambda i, j, be, nu: (be[row(i, j, be, nu)], col(i, j, be, nu), 0))],
        out_specs=pl.BlockSpec((MOE_BLOCK, d), lambda i, j, be, nu: (i, 0)),
        scratch_shapes=[pltpu.VMEM((MOE_BLOCK, d), BF16)],
    )
    return pl.pallas_call(
        _expert_kernel,
        grid_spec=grid_spec,
        out_shape=jax.ShapeDtypeStruct((n_rows, d), F32),
        compiler_params=_cparams("arbitrary", "arbitrary"),
        name="expert_swiglu",
    )(block_expert, n_used, xb, _bf(wg), _bf(wu), _bf(wd))


def _combine_kernel(x_ref, y1_ref, y2_ref, meta_ref, g_ref, o_ref):
    meta = meta_ref[...]
    lane = lax.broadcasted_iota(jnp.int32, meta.shape, 1)
    g1 = jnp.sum(jnp.where(lane == META_G, meta, 0.0), axis=-1, keepdims=True)
    g2 = jnp.sum(jnp.where(lane == META_G + 1, meta, 0.0), axis=-1, keepdims=True)
    y = x_ref[...] + (g1 * y1_ref[...] + g2 * y2_ref[...])
    o_ref[...] = _rmsnorm(y, g_ref[...])


def _combine_final_norm(x2, yg, meta, gain):
    n, d = x2.shape
    nblk = n // ROW_TILE
    return pl.pallas_call(
        _combine_kernel,
        grid=(nblk,),
        in_specs=[pl.BlockSpec((ROW_TILE, d), lambda i: (i, 0)),
                  pl.BlockSpec((ROW_TILE, d), lambda i: (i, 0)),
                  pl.BlockSpec((ROW_TILE, d), lambda i: (i + nblk, 0)),
                  pl.BlockSpec((ROW_TILE, LANES), lambda i: (i, 0)),
                  pl.BlockSpec((1, d), lambda i: (0, 0))],
        out_specs=pl.BlockSpec((ROW_TILE, d), lambda i: (i, 0)),
        out_shape=jax.ShapeDtypeStruct((n, d), F32),
        compiler_params=_cparams("parallel"),
        name="moe_combine_norm",
    )(x2, yg, yg, meta, gain.reshape(1, d))


def _moe_layer(x2, gain, w_router, wg, wu, wd, final_gain):
    n, d = x2.shape
    h, meta, counts = _router(x2, gain, w_router)
    expert = meta[:, META_E:META_E + 2].astype(jnp.int32)
    rank = meta[:, META_RANK:META_RANK + 2].astype(jnp.int32)
    count = counts[0, :N_EXPERTS].astype(jnp.int32)
    padded = (count + MOE_BLOCK - 1) // MOE_BLOCK * MOE_BLOCK
    pad_end = jnp.cumsum(padded)
    pad_start = pad_end - padded
    dest = pad_start[expert] + rank
    n_rows = (2 * n // MOE_BLOCK + N_EXPERTS) * MOE_BLOCK
    empties = padded - count
    e_end = jnp.cumsum(empties)
    i = jnp.arange(n_rows - 2 * n, dtype=jnp.int32)
    grp = jnp.minimum(jnp.searchsorted(e_end, i, side='right'), N_EXPERTS).astype(jnp.int32)
    first_empty = jnp.concatenate([pad_start + count, pad_end[-1:]])
    first_index = jnp.concatenate([e_end - empties, e_end[-1:]])
    pad_rows = (first_empty[grp] + i - first_index[grp]).astype(jnp.int32)
    block_start = jnp.arange(n_rows // MOE_BLOCK, dtype=jnp.int32) * MOE_BLOCK
    block_expert = jnp.minimum(jnp.searchsorted(pad_end, block_start, side='right'), N_EXPERTS - 1).astype(jnp.int32)
    n_used = (pad_end[-1:] // MOE_BLOCK).astype(jnp.int32)
    xb = _dispatch_rows(h, dest[:, 0], dest[:, 1], pad_rows)
    ys = _expert_swiglu(xb, block_expert, n_used, wg, wu, wd)
    yg = _gather_rows(ys, jnp.concatenate([dest[:, 0], dest[:, 1]]))
    return _combine_final_norm(x2, yg, meta, final_gain)


def _mixer_layer(x2, batch, norm_mix, w_in, w_out, shift_mu, gm_ln_w, gm_ln_b, gm_ws, gm_bs,
                 rw_w_up, rw_w0, rw_a_up, rw_a0, rw_g_up, rw_k_k, rw_k_a, rw_r_k, rw_lnx_w, rw_lnx_b):
    n, d = x2.shape
    gm2 = 2 * gm_ln_w.shape[0]
    w_in = _bf(w_in)
    p_gm, p_rw = _norm_inproj(x2, norm_mix, w_in[:, :gm2], w_in[:, gm2:])
    y_gm = _gmlp(p_gm, gm_ln_w, gm_ln_b, gm_ws, gm_bs)
    r, c, lw, k, v, a, b, g, bonus = _rwkv_prep(p_rw, batch, shift_mu, rw_w_up, rw_w0, rw_a_up, rw_a0, rw_g_up,
                                             rw_k_k, rw_k_a, rw_r_k.reshape(-1))
    y_scan = _wkv(r, c, lw, k, v, a, b)
    flat = lambda z: z.reshape(n, -1)
    return _post_outproj(x2, y_gm, flat(y_scan), flat(bonus), flat(g), rw_lnx_w, rw_lnx_b, w_out)


def kernel(x, norm_mix, w_in, w_out, shift_mu, gm_ln_w, gm_ln_b, gm_ws, gm_bs, rw_w_up, rw_w0, rw_a_up, rw_a0,
           rw_g_up, rw_k_k, rw_k_a, rw_r_k, rw_lnx_w, rw_lnx_b, norm_ffn, ffn_w_gate, ffn_w_up, ffn_w_down,
           moe_router, moe_w_gate, moe_w_up, moe_w_down, norm_final):
    batch, t, d = x.shape
    depth = norm_mix.shape[0]
    assert depth == 2 and t % ROW_TILE == 0, "two layers (dense then MoE), sequence a multiple of the row tile"
    x2 = x.reshape(batch * t, d)
    for i in range(depth):
        x2 = _mixer_layer(x2, batch, norm_mix[i], w_in[i], w_out[i], shift_mu[i], gm_ln_w[i], gm_ln_b[i],
                          gm_ws[i], gm_bs[i], rw_w_up[i], rw_w0[i], rw_a_up[i], rw_a0[i], rw_g_up[i],
                          rw_k_k[i], rw_k_a[i], rw_r_k[i], rw_lnx_w[i], rw_lnx_b[i])
        if i % 2 == 0:
            x2 = _swiglu(x2, norm_ffn[i], ffn_w_gate[i // 2], ffn_w_up[i // 2], ffn_w_down[i // 2])
        else:
            x2 = _moe_layer(x2, norm_ffn[i], moe_router[i // 2], moe_w_gate[i // 2], moe_w_up[i // 2],
                            moe_w_down[i // 2], norm_final)
    return x2.reshape(batch, t, d)
```

```python
import functools

import jax
import jax.numpy as jnp
from jax import lax
from jax.experimental import pallas as pl
from jax.experimental.pallas import tpu as pltpu
from jax.experimental.pallas import tpu_sc as plsc

F32 = jnp.float32
BF16 = jnp.bfloat16

HEAD_DIM = 64
LANES = 128
GM_CHUNK = 128
WKV_CHUNK = 64
N_EXPERTS = 8
RMS_EPS = 1e-6
LN_EPS = 1e-5
GN_EPS = 64e-5
NEG_BIG = -1e30

ROW_TILE = 512
WKV_STEP = 256
MOE_BLOCK = 512
SC_WINDOW = 32
VMEM_LIMIT = 56 * 1024 * 1024


def _cparams(*sem):
    return pltpu.CompilerParams(dimension_semantics=sem, vmem_limit_bytes=VMEM_LIMIT)


def _bf(x):
    return x.astype(BF16)


def _dot(a, b):
    return jnp.dot(a, b, preferred_element_type=F32)


def _dot_nt(a, b):
    return lax.dot_general(a, b, (((1,), (1,)), ((), ())), preferred_element_type=F32)


def _dot_tn(a, b):
    return lax.dot_general(a, b, (((0,), (0,)), ((), ())), preferred_element_type=F32)


def _split_dot_left(m, x, parts=2):
    acc = None
    rem = x
    for p in range(parts):
        hi = _bf(rem)
        d = _dot(m, hi)
        acc = d if acc is None else acc + d
        if p + 1 < parts:
            rem = rem - hi.astype(F32)
    return acc


def _rmsnorm(x, g):
    return x * lax.rsqrt(jnp.mean(x * x, axis=-1, keepdims=True) + RMS_EPS) * g


def _gelu(x):
    return 0.5 * x * (1.0 + lax.erf(x * (2.0 ** -0.5)))


def _head_avg_matrix(width):
    r = lax.broadcasted_iota(jnp.int32, (width, width), 0) // HEAD_DIM
    c = lax.broadcasted_iota(jnp.int32, (width, width), 1) // HEAD_DIM
    return jnp.where(r == c, 1.0 / HEAD_DIM, 0.0).astype(BF16)


def _head_layernorm(x, avg, w, b, eps):
    mu = _dot(_bf(x), avg)
    xc = x - mu
    var = _dot(_bf(xc * xc), avg)
    return xc * lax.rsqrt(var + eps) * w + b


def _resident(shape):
    return pl.BlockSpec(shape, lambda *_: (0,) * len(shape), pipeline_mode=pl.Buffered(1))


def _gmlp_chunk(p, avg, lnw, lnb, w2, bias, head0):
    width = p.shape[1] // 2
    u = _gelu(p[:, :width])
    vn = _head_layernorm(_gelu(p[:, width:]), avg, lnw, lnb, LN_EPS)
    mixed = []
    for pair, w in enumerate(w2):
        vp = vn[:, pair * LANES:(pair + 1) * LANES]
        v_st = _bf(jnp.concatenate([jnp.where(head0, vp, 0.0), jnp.where(head0, 0.0, vp)], axis=0))
        mixed.append(_dot(w, v_st))
    return u * (jnp.concatenate(mixed, axis=1) + bias)


def _mixer_in_kernel(x_ref, gain_ref, wgm_ref, wrw_ref, lnw_ref, lnb_ref, ws_ref, bias_ref,
                     mu_ref, wup_ref, w0_ref, aup_ref, a0_ref, gup_ref, kk_ref, ka_ref, rk_ref,
                     ygm_ref, r_ref, c_ref, lw_ref, k_ref, v_ref, a_ref, b_ref, g_ref, bonus_ref, carry_ref):
    width = r_ref.shape[2]
    rows = x_ref.shape[1]

    @pl.when(pl.program_id(1) == 0)
    def _():
        carry_ref[...] = jnp.zeros_like(carry_ref)

    h = _bf(_rmsnorm(x_ref[0], gain_ref[...]))
    avg = _head_avg_matrix(width)

    p_gm = _dot(h, wgm_ref[...])
    t = lax.broadcasted_iota(jnp.int32, (GM_CHUNK, GM_CHUNK), 0)
    s = lax.broadcasted_iota(jnp.int32, (GM_CHUNK, GM_CHUNK), 1)
    causal = s <= t
    head0 = lax.broadcasted_iota(jnp.int32, (GM_CHUNK, LANES), 1) < HEAD_DIM
    w2 = [_bf(jnp.concatenate([jnp.where(causal, ws_ref[2 * pair], 0.0),
                               jnp.where(causal, ws_ref[2 * pair + 1], 0.0)], axis=1))
          for pair in range(width // LANES)]
    for ch in range(rows // GM_CHUNK):
        ts = slice(ch * GM_CHUNK, (ch + 1) * GM_CHUNK)
        ygm_ref[0, ts, :] = _bf(_gmlp_chunk(p_gm[ts], avg, lnw_ref[...], lnb_ref[...], w2, bias_ref[...], head0))

    p = _dot(h, wrw_ref[...])
    first = lax.broadcasted_iota(jnp.int32, p.shape, 0) == 0
    prev = jnp.where(first, carry_ref[0:1, :], pltpu.roll(p, 1, axis=0))
    carry_ref[0:1, :] = p[rows - 1:rows, :]
    ps = p + (prev - p) * mu_ref[...]
    r = ps[:, :width]
    k = ps[:, width:2 * width]
    v = ps[:, 2 * width:3 * width]
    lora_in = ps[:, 3 * width:3 * width + LANES]
    gd = ps[:, 3 * width + LANES:]
    log_w = -jax.nn.softplus(-(w0_ref[...] + _dot(_bf(jnp.tanh(lora_in)), wup_ref[...]))) - 0.5
    lw = -jnp.exp(log_w)
    a_lr = jax.nn.sigmoid(a0_ref[...] + _dot(_bf(lora_in), aup_ref[...]))
    ones = avg * HEAD_DIM
    kk = k * kk_ref[...]
    kk = kk / jnp.maximum(jnp.sqrt(_dot(_bf(kk * kk), ones)), 1e-12)
    kmod = k * (1.0 + (a_lr - 1.0) * ka_ref[...])
    t = lax.broadcasted_iota(jnp.int32, (rows, rows), 0)
    s = lax.broadcasted_iota(jnp.int32, (rows, rows), 1)
    tri = ((t // WKV_CHUNK == s // WKV_CHUNK) & (s <= t)).astype(BF16)
    lw_ref[0] = lw
    c_ref[0] = _split_dot_left(tri, lw)
    r_ref[0] = _bf(r)
    k_ref[0] = _bf(kmod)
    v_ref[0] = _bf(v)
    a_ref[0] = _bf(-kk)
    b_ref[0] = _bf(kk * a_lr)
    g_ref[0] = _bf(_dot(_bf(jax.nn.sigmoid(gd)), gup_ref[...]))
    bonus_ref[0] = _bf(_dot(_bf(r * kmod * rk_ref[...]), ones) * v)


def _mixer_in(x3, gain, w_gm, w_rw, ln_w, ln_b, ws, bs, mu, w_up, w0, a_up, a0, g_up, k_k, k_a, r_k):
    batch, t, d = x3.shape
    width = w0.shape[0]
    rank = w_up.shape[0]
    zeros = jnp.zeros((LANES - rank, width), F32)
    wup_pad = _bf(jnp.concatenate([w_up, zeros], axis=0))
    aup_pad = _bf(jnp.concatenate([zeros, a_up], axis=0))
    bias = jnp.repeat(bs.T, HEAD_DIM, axis=1)
    row = lambda z: z.reshape(1, -1)
    params = (row(gain), w_gm, w_rw, row(ln_w), row(ln_b), ws, bias, row(mu), wup_pad, row(w0), aup_pad, row(a0),
              _bf(g_up), row(k_k), row(k_a), row(r_k))
    out = pl.BlockSpec((1, ROW_TILE, width), lambda b, i: (b, i, 0))
    sds = lambda dtype: jax.ShapeDtypeStruct((batch, t, width), dtype)
    return pl.pallas_call(
        _mixer_in_kernel,
        grid=(batch, t // ROW_TILE),
        in_specs=[pl.BlockSpec((1, ROW_TILE, d), lambda b, i: (b, i, 0))] + [_resident(z.shape) for z in params],
        out_specs=[out] * 10,
        out_shape=[sds(BF16), sds(BF16), sds(F32), sds(F32)] + [sds(BF16)] * 6,
        scratch_shapes=[pltpu.VMEM((8, w_rw.shape[1]), F32)],
        compiler_params=_cparams("parallel", "arbitrary"),
        name="mixer_in",
    )(x3, *params)


def _wkv_kernel(r_ref, c_ref, lw_ref, k_ref, v_ref, a_ref, b_ref, y_ref, h_ref,
                pw_s, akm_s, rbm_s, rkm_s, v_s, kc_s, rh_s, bc_s, sol_s, rkv_s, kv_s, dec_s):
    @pl.when(pl.program_id(1) == 0)
    def _():
        h_ref[...] = jnp.zeros_like(h_ref)

    n = WKV_CHUNK
    two = 2 * n
    head0 = lax.broadcasted_iota(jnp.int32, (n, LANES), 1) < HEAD_DIM
    row = lax.broadcasted_iota(jnp.int32, (two, two), 0)
    col = lax.broadcasted_iota(jnp.int32, (two, two), 1)
    same = (row // n) == (col // n)
    strict = same & ((col % n) < (row % n))
    incl = same & ((col % n) <= (row % n))
    eye = row == col
    npairs = y_ref.shape[2] // LANES
    nchunks = y_ref.shape[1] // n
    items = [(ch, pair) for ch in range(nchunks) for pair in range(npairs)]

    def window(ch, pair):
        return slice(ch * n, (ch + 1) * n), slice(pair * LANES, (pair + 1) * LANES)

    def stack(z):
        return jnp.concatenate([jnp.where(head0, z, 0.0), jnp.where(head0, 0.0, z)], axis=0)

    def dup(z):
        zb = _bf(z)
        return jnp.concatenate([zb, zb], axis=0)

    for it, (ch, pair) in enumerate(items):
        ts, ls = window(ch, pair)
        r, k, v, a, b = (ref[0, ts, ls].astype(F32) for ref in (r_ref, k_ref, v_ref, a_ref, b_ref))
        c = c_ref[0, ts, ls]
        cex = c - lw_ref[0, ts, ls]
        mid = c[n // 2 - 1:n // 2, :]
        last = c[n - 1:n, :]
        g_inv = jnp.exp(mid - c)
        g_end = jnp.exp(last - c)
        a_st = _bf(stack(a * jnp.exp(cex - mid)))
        r_st = _bf(stack(r * jnp.exp(c - mid)))
        b_dup, k_dup = dup(b * g_inv), dup(k * g_inv)
        pw_s[it] = _bf(jnp.where(strict, _dot_nt(a_st, b_dup), 0.0))
        akm_s[it] = _bf(jnp.where(strict, _dot_nt(a_st, k_dup), 0.0))
        rbm_s[it] = _bf(jnp.where(incl, _dot_nt(r_st, b_dup), 0.0))
        rkm_s[it] = _bf(jnp.where(incl, _dot_nt(r_st, k_dup), 0.0))
        v_s[it] = _bf(stack(v))
        kc_s[it] = _bf(stack(k * g_end))
        bc_s[it] = _bf(stack(b * g_end))
        rh_s[it] = _bf(stack(r * jnp.exp(c)))
        sol_s[it, :, :LANES] = stack(a * jnp.exp(cex))
        e_col = jnp.sum(jnp.where(eye, jnp.exp(last), 0.0), axis=1, keepdims=True)
        dec_s[it] = jnp.broadcast_to(e_col, (two, LANES))

    for it in range(len(items)):
        v_st = v_s[it]
        sol_s[it, :, LANES:] = _dot(akm_s[it], v_st)
        rkv_s[it] = _dot(rkm_s[it], v_st)
        kv_s[it] = _dot_tn(kc_s[it], v_st)

    levels = n.bit_length() - 1
    for level in range(levels):
        for it in range(len(items)):
            pw = pw_s[it]
            sol = sol_s[it]
            sol_s[it] = sol + _dot(pw, _bf(sol))
            if level + 1 < levels:
                pw_s[it] = _bf(_dot(pw, pw))

    for it, (ch, pair) in enumerate(items):
        ts, ls = window(ch, pair)
        h = h_ref[pair]
        sol = sol_s[it]
        gmat = _dot(jnp.concatenate([_bf(sol[:, :LANES]), rh_s[it]], axis=0), _bf(h))
        u = _bf(gmat[:two] + sol[:, LANES:])
        y_st = gmat[two:] + _dot(rbm_s[it], u) + rkv_s[it]
        y_ref[0, ts, ls] = y_st[:n] + y_st[n:]
        h_ref[pair] = dec_s[it] * h + _dot_tn(bc_s[it], u) + kv_s[it]


def _wkv(r, c, lw, k, v, a, b):
    batch, t, width = r.shape
    npairs = width // LANES
    items = (WKV_STEP // WKV_CHUNK) * npairs
    spec = pl.BlockSpec((1, WKV_STEP, width), lambda bi, i: (bi, i, 0))
    sq = lambda dtype: pltpu.VMEM((items, LANES, LANES), dtype)
    return pl.pallas_call(
        _wkv_kernel,
        grid=(batch, t // WKV_STEP),
        in_specs=[spec] * 7,
        out_specs=spec,
        out_shape=jax.ShapeDtypeStruct((batch, t, width), F32),
        scratch_shapes=[pltpu.VMEM((npairs, LANES, LANES), F32)] + [sq(BF16)] * 8
                       + [pltpu.VMEM((items, LANES, 2 * LANES), F32), sq(F32), sq(F32), sq(F32)],
        compiler_params=_cparams("parallel", "arbitrary"),
        name="wkv7",
    )(r, c, lw, k, v, a, b)


def _post_outproj_kernel(x_ref, ygm_ref, ys_ref, bonus_ref, g_ref, lnw_ref, lnb_ref, wtop_ref, wbot_ref, o_ref):
    width = ys_ref.shape[1]
    yn = _head_layernorm(ys_ref[...], _head_avg_matrix(width), lnw_ref[...], lnb_ref[...], GN_EPS)
    y_rw = (yn + bonus_ref[...]) * g_ref[...]
    o_ref[...] = x_ref[...] + _dot(ygm_ref[...], wtop_ref[...]) + _dot(_bf(y_rw), wbot_ref[...])


def _post_outproj(x2, y_gm, y_scan, bonus, g, lnx_w, lnx_b, w_out):
    n, d = x2.shape
    width = y_gm.shape[1]
    rows = lambda w: pl.BlockSpec((ROW_TILE, w), lambda i: (i, 0))
    vec = pl.BlockSpec((1, width), lambda i: (0, 0))
    wspec = pl.BlockSpec((width, d), lambda i: (0, 0))
    w_out = _bf(w_out)
    return pl.pallas_call(
        _post_outproj_kernel,
        grid=(n // ROW_TILE,),
        in_specs=[rows(d), rows(width), rows(width), rows(width), rows(width), vec, vec, wspec, wspec],
        out_specs=rows(d),
        out_shape=jax.ShapeDtypeStruct((n, d), F32),
        compiler_params=_cparams("parallel"),
        name="post_outproj",
    )(x2, y_gm, y_scan, bonus, g, lnx_w.reshape(1, width), lnx_b.reshape(1, width),
      w_out[:width], w_out[width:])


def _swiglu_kernel(x_ref, g_ref, wg_ref, wu_ref, wd_ref, o_ref):
    x = x_ref[...]
    h = _bf(_rmsnorm(x, g_ref[...]))
    act = jax.nn.silu(_dot(h, wg_ref[...])) * _dot(h, wu_ref[...])
    o_ref[...] = x + _dot(_bf(act), wd_ref[...])


def _ffn_tile(d_ff, parts):
    assert d_ff % (parts * LANES) == 0, f"d_ff={d_ff} does not split into {parts} lane-aligned tiles"
    return d_ff // parts


def _swiglu(x2, gain, wg, wu, wd):
    n, d = x2.shape
    d_ff = wg.shape[1]
    return pl.pallas_call(
        _swiglu_kernel,
        grid=(n // ROW_TILE,),
        in_specs=[pl.BlockSpec((ROW_TILE, d), lambda i: (i, 0)),
                  _resident((1, d)), _resident((d, d_ff)), _resident((d, d_ff)), _resident((d_ff, d))],
        out_specs=pl.BlockSpec((ROW_TILE, d), lambda i: (i, 0)),
        out_shape=jax.ShapeDtypeStruct((n, d), F32),
        compiler_params=_cparams("parallel"),
        name="dense_swiglu",
    )(x2, gain.reshape(1, d), _bf(wg), _bf(wu), _bf(wd))


META_E, META_G, META_RANK = 0, 2, 4


def _router_kernel(x_ref, g_ref, wr_ref, h_ref, meta_ref, cnt_ref, run_ref):
    @pl.when(pl.program_id(0) == 0)
    def _():
        run_ref[...] = jnp.zeros_like(run_ref)

    h = _rmsnorm(x_ref[...], g_ref[...])
    h_ref[...] = h
    rows = h.shape[0]
    lane = lax.broadcasted_iota(jnp.int32, (rows, LANES), 1)
    logits = jnp.dot(h, wr_ref[...], preferred_element_type=F32, precision=lax.Precision.HIGHEST)
    logits = jnp.where(lane < N_EXPERTS, logits, NEG_BIG)
    v1 = jnp.max(logits, axis=-1, keepdims=True)
    e1 = jnp.min(jnp.where(logits == v1, lane, LANES), axis=-1, keepdims=True)
    oh1 = lane == e1
    rest = jnp.where(oh1, NEG_BIG, logits)
    v2 = jnp.max(rest, axis=-1, keepdims=True)
    e2 = jnp.min(jnp.where(rest == v2, lane, LANES), axis=-1, keepdims=True)
    oh2 = lane == e2
    ex = jnp.exp(v2 - v1)
    g1 = 1.0 / (1.0 + ex)
    g2 = ex / (1.0 + ex)
    cnt = jnp.where(oh1 | oh2, 1.0, 0.0)
    t = lax.broadcasted_iota(jnp.int32, (rows, rows), 0)
    s = lax.broadcasted_iota(jnp.int32, (rows, rows), 1)
    before = _dot(jnp.where(s < t, 1.0, 0.0).astype(BF16), _bf(cnt)) + run_ref[0:1, :]
    rank1 = jnp.sum(jnp.where(oh1, before, 0.0), axis=-1, keepdims=True)
    rank2 = jnp.sum(jnp.where(oh2, before, 0.0), axis=-1, keepdims=True)
    fields = (e1.astype(F32), e2.astype(F32), g1, g2, rank1, rank2)
    meta = jnp.zeros((rows, LANES), F32)
    for idx, val in enumerate(fields):
        meta = jnp.where(lane == idx, val, meta)
    meta_ref[...] = meta
    run = run_ref[0:1, :] + jnp.sum(cnt, axis=0, keepdims=True)
    run_ref[...] = jnp.broadcast_to(run, run_ref.shape)
    cnt_ref[...] = jnp.broadcast_to(run, cnt_ref.shape)


def _router(x2, gain, w_router):
    n, d = x2.shape
    wr = jnp.concatenate([w_router, jnp.zeros((d, LANES - w_router.shape[1]), F32)], axis=1)
    return pl.pallas_call(
        _router_kernel,
        grid=(n // ROW_TILE,),
        in_specs=[pl.BlockSpec((ROW_TILE, d), lambda i: (i, 0)),
                  pl.BlockSpec((1, d), lambda i: (0, 0)),
                  pl.BlockSpec((d, LANES), lambda i: (0, 0))],
        out_specs=[pl.BlockSpec((ROW_TILE, d), lambda i: (i, 0)),
                   pl.BlockSpec((ROW_TILE, LANES), lambda i: (i, 0)),
                   pl.BlockSpec((8, LANES), lambda i: (0, 0))],
        out_shape=[jax.ShapeDtypeStruct((n, d), F32),
                   jax.ShapeDtypeStruct((n, LANES), F32),
                   jax.ShapeDtypeStruct((8, LANES), F32)],
        scratch_shapes=[pltpu.VMEM((8, LANES), F32)],
        compiler_params=_cparams("arbitrary"),
        name="moe_router",
    )(x2, gain.reshape(1, d), wr)


def _sc_workers():
    info = plsc.get_sparse_core_info()
    return info.num_cores, info.num_cores * info.num_subcores


def _dispatch_rows(h, dest0, dest1, pad_rows):
    nc, workers = _sc_workers()
    n, d = h.shape
    n_pad = pad_rows.shape[0]
    per_worker = n // workers
    nchunk = per_worker // SC_WINDOW
    npad = n_pad // (workers * SC_WINDOW)
    assert n == workers * nchunk * SC_WINDOW and nchunk % 2 == 0 and n_pad == workers * npad * SC_WINDOW
    mesh = plsc.VectorSubcoreMesh(core_axis_name="c", subcore_axis_name="s")
    idx_t = lambda m: pltpu.VMEM((m, SC_WINDOW), jnp.int32)
    buf_t = pltpu.VMEM((SC_WINDOW, d), h.dtype)

    @functools.partial(
        pl.kernel, mesh=mesh,
        out_type=jax.ShapeDtypeStruct((2 * n + n_pad, d), h.dtype),
        scratch_types=[idx_t(nchunk), idx_t(nchunk), idx_t(npad), buf_t, buf_t, buf_t]
                      + [pltpu.SemaphoreType.DMA] * 6,
    )
    def dispatch_kernel(h_hbm, d0_hbm, d1_hbm, pad_hbm, zero_hbm, out_hbm, d0_v, d1_v, pad_v, buf0, buf1, zbuf,
                        r0, r1, w0, w1, x0, x1):
        wid = lax.axis_index("s") * nc + lax.axis_index("c")
        base = wid * per_worker
        pltpu.sync_copy(d0_hbm.at[wid], d0_v)
        pltpu.sync_copy(d1_hbm.at[wid], d1_v)
        pltpu.sync_copy(pad_hbm.at[wid], pad_v)
        pltpu.sync_copy(zero_hbm, zbuf)
        bufs, rsem, wsem, xsem = (buf0, buf1), (r0, r1), (w0, w1), (x0, x1)

        def read(c, slot):
            return pltpu.make_async_copy(h_hbm.at[pl.ds(base + c * SC_WINDOW, SC_WINDOW)], bufs[slot], rsem[slot])

        read(0, 0).start()
        read(1, 1).start()
        for pc in range(npad):
            pltpu.sync_copy(zbuf, out_hbm.at[pad_v.at[pc]])

        @pl.loop(0, nchunk, step=2)
        def _(c):
            for slot in range(2):
                cc = c + slot
                read(cc, slot).wait()
                first = pltpu.make_async_copy(bufs[slot], out_hbm.at[d0_v.at[cc]], wsem[slot])
                second = pltpu.make_async_copy(bufs[slot], out_hbm.at[d1_v.at[cc]], xsem[slot])
                first.start()
                second.start()
                first.wait()
                second.wait()

                @pl.when(cc + 2 < nchunk)
                def _():
                    read(cc + 2, slot).start()

    split = lambda z, m: z.reshape(workers, m, SC_WINDOW)
    return dispatch_kernel(h, split(dest0, nchunk), split(dest1, nchunk), split(pad_rows, npad),
                           jnp.zeros((SC_WINDOW, d), h.dtype))


def _gather_rows(src, idx):
    info = plsc.get_sparse_core_info()
    nc, ns = info.num_cores, info.num_subcores
    workers = nc * ns
    m, d = idx.shape[0], src.shape[1]
    per_worker = m // workers
    nchunk = per_worker // SC_WINDOW
    assert m == workers * nchunk * SC_WINDOW and nchunk % 2 == 0, "row count must split evenly over subcores"
    mesh = plsc.VectorSubcoreMesh(core_axis_name="c", subcore_axis_name="s")

    @functools.partial(
        pl.kernel, mesh=mesh,
        out_type=jax.ShapeDtypeStruct((m, d), src.dtype),
        scratch_types=[pltpu.VMEM((nchunk, SC_WINDOW), jnp.int32),
                       pltpu.VMEM((SC_WINDOW, d), src.dtype), pltpu.VMEM((SC_WINDOW, d), src.dtype),
                       pltpu.SemaphoreType.DMA, pltpu.SemaphoreType.DMA,
                       pltpu.SemaphoreType.DMA, pltpu.SemaphoreType.DMA],
    )
    def gather_kernel(src_hbm, idx_hbm, out_hbm, idx_v, buf0, buf1, g0, g1, w0, w1):
        wid = lax.axis_index("s") * nc + lax.axis_index("c")
        base = wid * per_worker
        pltpu.sync_copy(idx_hbm.at[wid], idx_v)
        bufs, gsem, wsem = (buf0, buf1), (g0, g1), (w0, w1)

        def gather(c, slot):
            return pltpu.make_async_copy(src_hbm.at[idx_v.at[c]], bufs[slot], gsem[slot])

        def write(c, slot):
            return pltpu.make_async_copy(bufs[slot], out_hbm.at[pl.ds(base + c * SC_WINDOW, SC_WINDOW)], wsem[slot])

        gather(0, 0).start()
        gather(1, 1).start()

        @pl.loop(0, nchunk, step=2)
        def _(c):
            for slot in range(2):
                cc = c + slot
                gather(cc, slot).wait()
                write(cc, slot).start()
                write(cc, slot).wait()

                @pl.when(cc + 2 < nchunk)
                def _():
                    gather(cc + 2, slot).start()

    return gather_kernel(src, idx.reshape(workers, nchunk, SC_WINDOW))


def _expert_kernel(be_ref, nu_ref, x_ref, wg_ref, wu_ref, wd_ref, o_ref, h_ref):
    i, j = pl.program_id(0), pl.program_id(1)

    @pl.when(i < nu_ref[0])
    def _():
        @pl.when(j == 0)
        def _():
            h_ref[...] = _bf(x_ref[...])
            o_ref[...] = jnp.zeros_like(o_ref)

        h = h_ref[...]
        act = jax.nn.silu(_dot(h, wg_ref[0])) * _dot(h, wu_ref[0])
        o_ref[...] += _dot(_bf(act), wd_ref[0])

    @pl.when((i >= nu_ref[0]) & (j == 0))
    def _():
        o_ref[...] = jnp.zeros_like(o_ref)


def _expert_swiglu(xb, block_expert, n_used, wg, wu, wd):
    n_rows, d = xb.shape
    d_ff = wg.shape[2]
    tf = _ffn_tile(d_ff, 2)
    nf = d_ff // tf
    nblk = n_rows // MOE_BLOCK

    def row(i, j, be, nu):
        return jnp.minimum(i, nu[0] - 1)

    def col(i, j, be, nu):
        return jnp.where(i < nu[0], j, nf - 1)

    grid_spec = pltpu.PrefetchScalarGridSpec(
        num_scalar_prefetch=2,
        grid=(nblk, nf),
        in_specs=[pl.BlockSpec((MOE_BLOCK, d), lambda i, j, be, nu: (row(i, j, be, nu), 0)),
                  pl.BlockSpec((1, d, tf), lambda i, j, be, nu: (be[row(i, j, be, nu)], 0, col(i, j, be, nu))),
                  pl.BlockSpec((1, d, tf), lambda i, j, be, nu: (be[row(i, j, be, nu)], 0, col(i, j, be, nu))),
                  pl.BlockSpec((1, tf, d), lambda i, j, be, nu: (be[row(i, j, be, nu)], col(i, j, be, nu), 0))],
        out_specs=pl.BlockSpec((MOE_BLOCK, d), lambda i, j, be, nu: (i, 0)),
        scratch_shapes=[pltpu.VMEM((MOE_BLOCK, d), BF16)],
    )
    return pl.pallas_call(
        _expert_kernel,
        grid_spec=grid_spec,
        out_shape=jax.ShapeDtypeStruct((n_rows, d), F32),
        compiler_params=_cparams("arbitrary", "arbitrary"),
        name="expert_swiglu",
    )(block_expert, n_used, xb, _bf(wg), _bf(wu), _bf(wd))


def _combine_kernel(x_ref, y1_ref, y2_ref, meta_ref, g_ref, o_ref):
    meta = meta_ref[...]
    lane = lax.broadcasted_iota(jnp.int32, meta.shape, 1)
    g1 = jnp.sum(jnp.where(lane == META_G, meta, 0.0), axis=-1, keepdims=True)
    g2 = jnp.sum(jnp.where(lane == META_G + 1, meta, 0.0), axis=-1, keepdims=True)
    y = x_ref[...] + (g1 * y1_ref[...] + g2 * y2_ref[...])
    o_ref[...] = _rmsnorm(y, g_ref[...])


def _combine_final_norm(x2, yg, meta, gain):
    n, d = x2.shape
    nblk = n // ROW_TILE
    return pl.pallas_call(
        _combine_kernel,
        grid=(nblk,),
        in_specs=[pl.BlockSpec((ROW_TILE, d), lambda i: (i, 0)),
                  pl.BlockSpec((ROW_TILE, d), lambda i: (i, 0)),
                  pl.BlockSpec((ROW_TILE, d), lambda i: (i + nblk, 0)),
                  pl.BlockSpec((ROW_TILE, LANES), lambda i: (i, 0)),
                  pl.BlockSpec((1, d), lambda i: (0, 0))],
        out_specs=pl.BlockSpec((ROW_TILE, d), lambda i: (i, 0)),
        out_shape=jax.ShapeDtypeStruct((n, d), F32),
        compiler_params=_cparams("parallel"),
        name="moe_combine_norm",
    )(x2, yg, yg, meta, gain.reshape(1, d))


def _moe_layer(x2, gain, w_router, wg, wu, wd, final_gain):
    n, d = x2.shape
    h, meta, counts = _router(x2, gain, w_router)
    expert = meta[:, META_E:META_E + 2].astype(jnp.int32)
    rank = meta[:, META_RANK:META_RANK + 2].astype(jnp.int32)
    count = counts[0, :N_EXPERTS].astype(jnp.int32)
    padded = (count + MOE_BLOCK - 1) // MOE_BLOCK * MOE_BLOCK
    pad_end = jnp.cumsum(padded)
    pad_start = pad_end - padded
    dest = pad_start[expert] + rank
    n_rows = (2 * n // MOE_BLOCK + N_EXPERTS) * MOE_BLOCK
    empties = padded - count
    e_end = jnp.cumsum(empties)
    i = jnp.arange(n_rows - 2 * n, dtype=jnp.int32)
    grp = jnp.minimum(jnp.searchsorted(e_end, i, side='right'), N_EXPERTS).astype(jnp.int32)
    first_empty = jnp.concatenate([pad_start + count, pad_end[-1:]])
    first_index = jnp.concatenate([e_end - empties, e_end[-1:]])
    pad_rows = (first_empty[grp] + i - first_index[grp]).astype(jnp.int32)
    block_start = jnp.arange(n_rows // MOE_BLOCK, dtype=jnp.int32) * MOE_BLOCK
    block_expert = jnp.minimum(jnp.searchsorted(pad_end, block_start, side='right'), N_EXPERTS - 1).astype(jnp.int32)
    n_used = (pad_end[-1:] // MOE_BLOCK).astype(jnp.int32)
    xb = _dispatch_rows(h, dest[:, 0], dest[:, 1], pad_rows)
    ys = _expert_swiglu(xb, block_expert, n_used, wg, wu, wd)
    yg = _gather_rows(ys, jnp.concatenate([dest[:, 0], dest[:, 1]]))
    return _combine_final_norm(x2, yg, meta, final_gain)


def _mixer_layer(x2, batch, norm_mix, w_in, w_out, shift_mu, gm_ln_w, gm_ln_b, gm_ws, gm_bs,
                 rw_w_up, rw_w0, rw_a_up, rw_a0, rw_g_up, rw_k_k, rw_k_a, rw_r_k, rw_lnx_w, rw_lnx_b):
    n, d = x2.shape
    gm2 = 2 * gm_ln_w.shape[0]
    w_in = _bf(w_in)
    y_gm, r, c, lw, k, v, a, b, g, bonus = _mixer_in(
        x2.reshape(batch, n // batch, d), norm_mix, w_in[:, :gm2], w_in[:, gm2:], gm_ln_w, gm_ln_b, gm_ws, gm_bs,
        shift_mu, rw_w_up, rw_w0, rw_a_up, rw_a0, rw_g_up, rw_k_k, rw_k_a, rw_r_k.reshape(-1))
    y_scan = _wkv(r, c, lw, k, v, a, b)
    flat = lambda z: z.reshape(n, -1)
    return _post_outproj(x2, flat(y_gm), flat(y_scan), flat(bonus), flat(g), rw_lnx_w, rw_lnx_b, w_out)


def kernel(x, norm_mix, w_in, w_out, shift_mu, gm_ln_w, gm_ln_b, gm_ws, gm_bs, rw_w_up, rw_w0, rw_a_up, rw_a0,
           rw_g_up, rw_k_k, rw_k_a, rw_r_k, rw_lnx_w, rw_lnx_b, norm_ffn, ffn_w_gate, ffn_w_up, ffn_w_down,
           moe_router, moe_w_gate, moe_w_up, moe_w_down, norm_final):
    batch, t, d = x.shape
    depth = norm_mix.shape[0]
    assert depth == 2 and t % ROW_TILE == 0, "two layers (dense then MoE), sequence a multiple of the row tile"
    x2 = x.reshape(batch * t, d)
    for i in range(depth):
        x2 = _mixer_layer(x2, batch, norm_mix[i], w_in[i], w_out[i], shift_mu[i], gm_ln_w[i], gm_ln_b[i],
                          gm_ws[i], gm_bs[i], rw_w_up[i], rw_w0[i], rw_a_up[i], rw_a0[i], rw_g_up[i],
                          rw_k_k[i], rw_k_a[i], rw_r_k[i], rw_lnx_w[i], rw_lnx_b[i])
        if i % 2 == 0:
            x2 = _swiglu(x2, norm_ffn[i], ffn_w_gate[i // 2], ffn_w_up[i // 2], ffn_w_down[i // 2])
        else:
            x2 = _moe_layer(x2, norm_ffn[i], moe_router[i // 2], moe_w_gate[i // 2], moe_w_up[i // 2],
                            moe_w_down[i // 2], norm_final)
    return x2.reshape(batch, t, d)
```

```python
import functools

import jax
import jax.numpy as jnp
from jax import lax
from jax.experimental import pallas as pl
from jax.experimental.pallas import tpu as pltpu
from jax.experimental.pallas import tpu_sc as plsc

F32 = jnp.float32
BF16 = jnp.bfloat16

HEAD_DIM = 64
LANES = 128
GM_CHUNK = 128
WKV_CHUNK = 64
N_EXPERTS = 8
RMS_EPS = 1e-6
LN_EPS = 1e-5
GN_EPS = 64e-5
NEG_BIG = -1e30

ROW_TILE = 512
WKV_STEP = 256
MOE_BLOCK = 512
SC_WINDOW = 32
VMEM_LIMIT = 56 * 1024 * 1024


def _cparams(*sem):
    return pltpu.CompilerParams(dimension_semantics=sem, vmem_limit_bytes=VMEM_LIMIT)


def _bf(x):
    return x.astype(BF16)


def _dot(a, b):
    return jnp.dot(a, b, preferred_element_type=F32)


def _dot_nt(a, b):
    return lax.dot_general(a, b, (((1,), (1,)), ((), ())), preferred_element_type=F32)


def _dot_tn(a, b):
    return lax.dot_general(a, b, (((0,), (0,)), ((), ())), preferred_element_type=F32)


def _split_dot_left(m, x, parts=2):
    acc = None
    rem = x
    for p in range(parts):
        hi = _bf(rem)
        d = _dot(m, hi)
        acc = d if acc is None else acc + d
        if p + 1 < parts:
            rem = rem - hi.astype(F32)
    return acc


def _rmsnorm(x, g):
    return x * lax.rsqrt(jnp.mean(x * x, axis=-1, keepdims=True) + RMS_EPS) * g


def _gelu(x):
    return 0.5 * x * (1.0 + lax.erf(x * (2.0 ** -0.5)))


def _head_avg_matrix(width):
    r = lax.broadcasted_iota(jnp.int32, (width, width), 0) // HEAD_DIM
    c = lax.broadcasted_iota(jnp.int32, (width, width), 1) // HEAD_DIM
    return jnp.where(r == c, 1.0 / HEAD_DIM, 0.0).astype(BF16)


def _head_layernorm(x, avg, w, b, eps):
    mu = _dot(_bf(x), avg)
    xc = x - mu
    var = _dot(_bf(xc * xc), avg)
    return xc * lax.rsqrt(var + eps) * w + b


def _resident(shape):
    return pl.BlockSpec(shape, lambda *_: (0,) * len(shape), pipeline_mode=pl.Buffered(1))


def _gmlp_chunk(p, avg, lnw, lnb, w2, bias, head0):
    width = p.shape[1] // 2
    u = _gelu(p[:, :width])
    vn = _head_layernorm(_gelu(p[:, width:]), avg, lnw, lnb, LN_EPS)
    mixed = []
    for pair, w in enumerate(w2):
        vp = vn[:, pair * LANES:(pair + 1) * LANES]
        v_st = _bf(jnp.concatenate([jnp.where(head0, vp, 0.0), jnp.where(head0, 0.0, vp)], axis=0))
        mixed.append(_dot(w, v_st))
    return u * (jnp.concatenate(mixed, axis=1) + bias)


def _mixer_in_kernel(x_ref, gain_ref, wgm_ref, wrw_ref, lnw_ref, lnb_ref, ws_ref, bias_ref,
                     mu_ref, wup_ref, w0_ref, aup_ref, a0_ref, gup_ref, kk_ref, ka_ref, rk_ref,
                     ygm_ref, r_ref, c_ref, lw_ref, k_ref, v_ref, a_ref, b_ref, g_ref, bonus_ref, carry_ref):
    width = r_ref.shape[2]
    rows = x_ref.shape[1]

    @pl.when(pl.program_id(1) == 0)
    def _():
        carry_ref[...] = jnp.zeros_like(carry_ref)

    h = _bf(_rmsnorm(x_ref[0], gain_ref[...]))
    avg = _head_avg_matrix(width)

    p_gm = _dot(h, wgm_ref[...])
    t = lax.broadcasted_iota(jnp.int32, (GM_CHUNK, GM_CHUNK), 0)
    s = lax.broadcasted_iota(jnp.int32, (GM_CHUNK, GM_CHUNK), 1)
    causal = s <= t
    head0 = lax.broadcasted_iota(jnp.int32, (GM_CHUNK, LANES), 1) < HEAD_DIM
    w2 = [_bf(jnp.concatenate([jnp.where(causal, ws_ref[2 * pair], 0.0),
                               jnp.where(causal, ws_ref[2 * pair + 1], 0.0)], axis=1))
          for pair in range(width // LANES)]
    for ch in range(rows // GM_CHUNK):
        ts = slice(ch * GM_CHUNK, (ch + 1) * GM_CHUNK)
        ygm_ref[0, ts, :] = _bf(_gmlp_chunk(p_gm[ts], avg, lnw_ref[...], lnb_ref[...], w2, bias_ref[...], head0))

    p = _dot(h, wrw_ref[...])
    first = lax.broadcasted_iota(jnp.int32, p.shape, 0) == 0
    prev = jnp.where(first, carry_ref[0:1, :], pltpu.roll(p, 1, axis=0))
    carry_ref[0:1, :] = p[rows - 1:rows, :]
    ps = p + (prev - p) * mu_ref[...]
    r = ps[:, :width]
    k = ps[:, width:2 * width]
    v = ps[:, 2 * width:3 * width]
    lora_in = ps[:, 3 * width:3 * width + LANES]
    gd = ps[:, 3 * width + LANES:]
    log_w = -jax.nn.softplus(-(w0_ref[...] + _dot(_bf(jnp.tanh(lora_in)), wup_ref[...]))) - 0.5
    lw = -jnp.exp(log_w)
    a_lr = jax.nn.sigmoid(a0_ref[...] + _dot(_bf(lora_in), aup_ref[...]))
    ones = avg * HEAD_DIM
    kk = k * kk_ref[...]
    kk = kk / jnp.maximum(jnp.sqrt(_dot(_bf(kk * kk), ones)), 1e-12)
    kmod = k * (1.0 + (a_lr - 1.0) * ka_ref[...])
    t = lax.broadcasted_iota(jnp.int32, (rows, rows), 0)
    s = lax.broadcasted_iota(jnp.int32, (rows, rows), 1)
    tri = ((t // WKV_CHUNK == s // WKV_CHUNK) & (s <= t)).astype(BF16)
    lw_ref[0] = lw
    c_ref[0] = _split_dot_left(tri, lw)
    r_ref[0] = _bf(r)
    k_ref[0] = _bf(kmod)
    v_ref[0] = _bf(v)
    a_ref[0] = _bf(-kk)
    b_ref[0] = _bf(kk * a_lr)
    g_ref[0] = _bf(_dot(_bf(jax.nn.sigmoid(gd)), gup_ref[...]))
    bonus_ref[0] = _bf(_dot(_bf(r * kmod * rk_ref[...]), ones) * v)


def _mixer_in(x3, gain, w_gm, w_rw, ln_w, ln_b, ws, bs, mu, w_up, w0, a_up, a0, g_up, k_k, k_a, r_k):
    batch, t, d = x3.shape
    width = w0.shape[0]
    rank = w_up.shape[0]
    zeros = jnp.zeros((LANES - rank, width), F32)
    wup_pad = _bf(jnp.concatenate([w_up, zeros], axis=0))
    aup_pad = _bf(jnp.concatenate([zeros, a_up], axis=0))
    bias = jnp.repeat(bs.T, HEAD_DIM, axis=1)
    row = lambda z: z.reshape(1, -1)
    params = (row(gain), w_gm, w_rw, row(ln_w), row(ln_b), ws, bias, row(mu), wup_pad, row(w0), aup_pad, row(a0),
              _bf(g_up), row(k_k), row(k_a), row(r_k))
    out = pl.BlockSpec((1, ROW_TILE, width), lambda b, i: (b, i, 0))
    sds = lambda dtype: jax.ShapeDtypeStruct((batch, t, width), dtype)
    return pl.pallas_call(
        _mixer_in_kernel,
        grid=(batch, t // ROW_TILE),
        in_specs=[pl.BlockSpec((1, ROW_TILE, d), lambda b, i: (b, i, 0))] + [_resident(z.shape) for z in params],
        out_specs=[out] * 10,
        out_shape=[sds(BF16), sds(BF16), sds(F32), sds(F32)] + [sds(BF16)] * 6,
        scratch_shapes=[pltpu.VMEM((8, w_rw.shape[1]), F32)],
        compiler_params=_cparams("parallel", "arbitrary"),
        name="mixer_in",
    )(x3, *params)


def _wkv_kernel(r_ref, c_ref, lw_ref, k_ref, v_ref, a_ref, b_ref, y_ref, h_ref,
                q_s, v_s, kc_s, ark_s, bcrb_s, mz_s, rhs_s, sol_s, t_s, rh_s, rkv_s, kv_s, dec_s, hc_s, yc_s):
    @pl.when(pl.program_id(1) == 0)
    def _():
        h_ref[...] = jnp.zeros_like(h_ref)

    n = WKV_CHUNK
    two = 2 * n
    head0 = lax.broadcasted_iota(jnp.int32, (n, LANES), 1) < HEAD_DIM
    row = lax.broadcasted_iota(jnp.int32, (two, two), 0)
    col = lax.broadcasted_iota(jnp.int32, (two, two), 1)
    same = (row // n) == (col // n)
    strict = same & ((col % n) < (row % n))
    incl = same & ((col % n) <= (row % n))
    eye = row == col
    row2 = lax.broadcasted_iota(jnp.int32, (two, 2 * two), 0)
    col2 = lax.broadcasted_iota(jnp.int32, (two, 2 * two), 1) % two
    same2 = (row2 // n) == (col2 // n)
    strict2 = same2 & ((col2 % n) < (row2 % n))
    incl2 = same2 & ((col2 % n) <= (row2 % n))
    npairs = y_ref.shape[2] // LANES
    nchunks = y_ref.shape[1] // n
    items = [(ch, pair) for ch in range(nchunks) for pair in range(npairs)]

    def window(ch, pair):
        return slice(ch * n, (ch + 1) * n), slice(pair * LANES, (pair + 1) * LANES)

    def stack(z):
        return jnp.concatenate([jnp.where(head0, z, 0.0), jnp.where(head0, 0.0, z)], axis=0)

    def dup(z):
        zb = _bf(z)
        return jnp.concatenate([zb, zb], axis=0)

    for it, (ch, pair) in enumerate(items):
        ts, ls = window(ch, pair)
        r, k, v, a, b = (ref[0, ts, ls].astype(F32) for ref in (r_ref, k_ref, v_ref, a_ref, b_ref))
        c = c_ref[0, ts, ls]
        cex = c - lw_ref[0, ts, ls]
        mid = c[n // 2 - 1:n // 2, :]
        last = c[n - 1:n, :]
        g_inv = jnp.exp(mid - c)
        g_end = jnp.exp(last - c)
        ar_st = _bf(jnp.concatenate([stack(a * jnp.exp(cex - mid)), stack(r * jnp.exp(c - mid))], axis=0))
        bk_dup = jnp.concatenate([dup(b * g_inv), dup(k * g_inv)], axis=0)
        sc = _dot_nt(ar_st, bk_dup)
        top = jnp.where(strict2, sc[:two], 0.0)
        bot = jnp.where(incl2, sc[two:], 0.0)
        t_s[it] = jnp.where(eye, 1.0, top[:, :two])
        q_s[it] = _bf(top[:, :two])
        bcrb_s[it] = _bf(jnp.concatenate([stack(b * g_end).T, bot[:, :two]], axis=0))
        ark_s[it] = _bf(jnp.concatenate([top[:, two:], bot[:, two:]], axis=0))
        v_s[it] = _bf(stack(v))
        kc_s[it] = _bf(stack(k * g_end))
        rh_s[it] = stack(r * jnp.exp(c))
        rhs_s[it, :, :LANES] = _bf(stack(a * jnp.exp(cex)))
        e_col = jnp.sum(jnp.where(eye, jnp.exp(last), 0.0), axis=1, keepdims=True)
        dec_s[it] = jnp.broadcast_to(e_col, (two, LANES))

    for it in range(len(items)):
        v_st = v_s[it]
        both = _dot(ark_s[it], v_st)
        rhs_s[it, :, LANES:] = _bf(both[:two])
        rkv_s[it] = both[two:]
        kv_s[it] = _dot_tn(kc_s[it], v_st)

    levels = n.bit_length() - 1
    for level in range(1, levels):
        for it in range(len(items)):
            q = q_s[it]
            t_acc = t_s[it]
            if level == 1:
                q = _bf(_dot(q, q))
            if level + 1 < levels:
                prod = _dot(q, jnp.concatenate([_bf(t_acc), q], axis=1))
                q_s[it] = _bf(prod[:, two:])
                t_s[it] = t_acc + prod[:, :two]
            else:
                t_s[it] = t_acc + _dot(q, _bf(t_acc))

    for it in range(len(items)):
        sol_s[it] = _bf(_dot(_bf(t_s[it]), rhs_s[it]))

    for it in range(len(items)):
        prod = _dot(bcrb_s[it], sol_s[it])
        mz_s[it] = _bf(jnp.concatenate([prod[:two, :LANES], rh_s[it] + prod[two:, :LANES]], axis=0))
        hc_s[it] = prod[:two, LANES:] + kv_s[it]
        yc_s[it] = prod[two:, LANES:] + rkv_s[it]

    for it, (ch, pair) in enumerate(items):
        ts, ls = window(ch, pair)
        h = h_ref[pair]
        gmat = _dot(mz_s[it], _bf(h))
        h_ref[pair] = dec_s[it] * h + gmat[:two] + hc_s[it]
        y_st = gmat[two:] + yc_s[it]
        y_ref[0, ts, ls] = y_st[:n] + y_st[n:]


def _wkv(r, c, lw, k, v, a, b):
    batch, t, width = r.shape
    npairs = width // LANES
    items = (WKV_STEP // WKV_CHUNK) * npairs
    spec = pl.BlockSpec((1, WKV_STEP, width), lambda bi, i: (bi, i, 0))
    sq = lambda dtype: pltpu.VMEM((items, LANES, LANES), dtype)
    return pl.pallas_call(
        _wkv_kernel,
        grid=(batch, t // WKV_STEP),
        in_specs=[spec] * 7,
        out_specs=spec,
        out_shape=jax.ShapeDtypeStruct((batch, t, width), F32),
        scratch_shapes=[pltpu.VMEM((npairs, LANES, LANES), F32)] + [sq(BF16)] * 3
                       + [pltpu.VMEM((items, 2 * LANES, LANES), BF16)] * 3
                       + [pltpu.VMEM((items, LANES, 2 * LANES), BF16)] * 2 + [sq(F32)] * 7,
        compiler_params=_cparams("parallel", "arbitrary"),
        name="wkv7",
    )(r, c, lw, k, v, a, b)


def _post_outproj_kernel(x_ref, ygm_ref, ys_ref, bonus_ref, g_ref, lnw_ref, lnb_ref, wtop_ref, wbot_ref, o_ref):
    width = ys_ref.shape[1]
    yn = _head_layernorm(ys_ref[...], _head_avg_matrix(width), lnw_ref[...], lnb_ref[...], GN_EPS)
    y_rw = (yn + bonus_ref[...]) * g_ref[...]
    o_ref[...] = x_ref[...] + _dot(ygm_ref[...], wtop_ref[...]) + _dot(_bf(y_rw), wbot_ref[...])


def _post_outproj(x2, y_gm, y_scan, bonus, g, lnx_w, lnx_b, w_out):
    n, d = x2.shape
    width = y_gm.shape[1]
    rows = lambda w: pl.BlockSpec((ROW_TILE, w), lambda i: (i, 0))
    vec = pl.BlockSpec((1, width), lambda i: (0, 0))
    wspec = pl.BlockSpec((width, d), lambda i: (0, 0))
    w_out = _bf(w_out)
    return pl.pallas_call(
        _post_outproj_kernel,
        grid=(n // ROW_TILE,),
        in_specs=[rows(d), rows(width), rows(width), rows(width), rows(width), vec, vec, wspec, wspec],
        out_specs=rows(d),
        out_shape=jax.ShapeDtypeStruct((n, d), F32),
        compiler_params=_cparams("parallel"),
        name="post_outproj",
    )(x2, y_gm, y_scan, bonus, g, lnx_w.reshape(1, width), lnx_b.reshape(1, width),
      w_out[:width], w_out[width:])


def _swiglu_kernel(x_ref, g_ref, wg_ref, wu_ref, wd_ref, o_ref):
    x = x_ref[...]
    h = _bf(_rmsnorm(x, g_ref[...]))
    act = jax.nn.silu(_dot(h, wg_ref[...])) * _dot(h, wu_ref[...])
    o_ref[...] = x + _dot(_bf(act), wd_ref[...])


def _ffn_tile(d_ff, parts):
    assert d_ff % (parts * LANES) == 0, f"d_ff={d_ff} does not split into {parts} lane-aligned tiles"
    return d_ff // parts


def _swiglu(x2, gain, wg, wu, wd):
    n, d = x2.shape
    d_ff = wg.shape[1]
    return pl.pallas_call(
        _swiglu_kernel,
        grid=(n // ROW_TILE,),
        in_specs=[pl.BlockSpec((ROW_TILE, d), lambda i: (i, 0)),
                  _resident((1, d)), _resident((d, d_ff)), _resident((d, d_ff)), _resident((d_ff, d))],
        out_specs=pl.BlockSpec((ROW_TILE, d), lambda i: (i, 0)),
        out_shape=jax.ShapeDtypeStruct((n, d), F32),
        compiler_params=_cparams("parallel"),
        name="dense_swiglu",
    )(x2, gain.reshape(1, d), _bf(wg), _bf(wu), _bf(wd))


META_E, META_G, META_RANK = 0, 2, 4


def _router_kernel(x_ref, g_ref, wr_ref, h_ref, meta_ref, cnt_ref, run_ref):
    @pl.when(pl.program_id(0) == 0)
    def _():
        run_ref[...] = jnp.zeros_like(run_ref)

    h = _rmsnorm(x_ref[...], g_ref[...])
    h_ref[...] = h
    rows = h.shape[0]
    lane = lax.broadcasted_iota(jnp.int32, (rows, LANES), 1)
    logits = jnp.dot(h, wr_ref[...], preferred_element_type=F32, precision=lax.Precision.HIGHEST)
    logits = jnp.where(lane < N_EXPERTS, logits, NEG_BIG)
    v1 = jnp.max(logits, axis=-1, keepdims=True)
    e1 = jnp.min(jnp.where(logits == v1, lane, LANES), axis=-1, keepdims=True)
    oh1 = lane == e1
    rest = jnp.where(oh1, NEG_BIG, logits)
    v2 = jnp.max(rest, axis=-1, keepdims=True)
    e2 = jnp.min(jnp.where(rest == v2, lane, LANES), axis=-1, keepdims=True)
    oh2 = lane == e2
    ex = jnp.exp(v2 - v1)
    g1 = 1.0 / (1.0 + ex)
    g2 = ex / (1.0 + ex)
    cnt = jnp.where(oh1 | oh2, 1.0, 0.0)
    t = lax.broadcasted_iota(jnp.int32, (rows, rows), 0)
    s = lax.broadcasted_iota(jnp.int32, (rows, rows), 1)
    before = _dot(jnp.where(s < t, 1.0, 0.0).astype(BF16), _bf(cnt)) + run_ref[0:1, :]
    rank1 = jnp.sum(jnp.where(oh1, before, 0.0), axis=-1, keepdims=True)
    rank2 = jnp.sum(jnp.where(oh2, before, 0.0), axis=-1, keepdims=True)
    fields = (e1.astype(F32), e2.astype(F32), g1, g2, rank1, rank2)
    meta = jnp.zeros((rows, LANES), F32)
    for idx, val in enumerate(fields):
        meta = jnp.where(lane == idx, val, meta)
    meta_ref[...] = meta
    run = run_ref[0:1, :] + jnp.sum(cnt, axis=0, keepdims=True)
    run_ref[...] = jnp.broadcast_to(run, run_ref.shape)
    cnt_ref[...] = jnp.broadcast_to(run, cnt_ref.shape)


def _router(x2, gain, w_router):
    n, d = x2.shape
    wr = jnp.concatenate([w_router, jnp.zeros((d, LANES - w_router.shape[1]), F32)], axis=1)
    return pl.pallas_call(
        _router_kernel,
        grid=(n // ROW_TILE,),
        in_specs=[pl.BlockSpec((ROW_TILE, d), lambda i: (i, 0)),
                  pl.BlockSpec((1, d), lambda i: (0, 0)),
                  pl.BlockSpec((d, LANES), lambda i: (0, 0))],
        out_specs=[pl.BlockSpec((ROW_TILE, d), lambda i: (i, 0)),
                   pl.BlockSpec((ROW_TILE, LANES), lambda i: (i, 0)),
                   pl.BlockSpec((8, LANES), lambda i: (0, 0))],
        out_shape=[jax.ShapeDtypeStruct((n, d), F32),
                   jax.ShapeDtypeStruct((n, LANES), F32),
                   jax.ShapeDtypeStruct((8, LANES), F32)],
        scratch_shapes=[pltpu.VMEM((8, LANES), F32)],
        compiler_params=_cparams("arbitrary"),
        name="moe_router",
    )(x2, gain.reshape(1, d), wr)


def _sc_workers():
    info = plsc.get_sparse_core_info()
    return info.num_cores, info.num_cores * info.num_subcores


def _dispatch_rows(h, dest0, dest1, pad_rows):
    nc, workers = _sc_workers()
    n, d = h.shape
    n_pad = pad_rows.shape[0]
    per_worker = n // workers
    nchunk = per_worker // SC_WINDOW
    npad = n_pad // (workers * SC_WINDOW)
    assert n == workers * nchunk * SC_WINDOW and nchunk % 2 == 0 and n_pad == workers * npad * SC_WINDOW
    mesh = plsc.VectorSubcoreMesh(core_axis_name="c", subcore_axis_name="s")
    idx_t = lambda m: pltpu.VMEM((m, SC_WINDOW), jnp.int32)
    buf_t = pltpu.VMEM((SC_WINDOW, d), h.dtype)

    @functools.partial(
        pl.kernel, mesh=mesh,
        out_type=jax.ShapeDtypeStruct((2 * n + n_pad, d), h.dtype),
        scratch_types=[idx_t(nchunk), idx_t(nchunk), idx_t(npad), buf_t, buf_t, buf_t]
                      + [pltpu.SemaphoreType.DMA] * 6,
    )
    def dispatch_kernel(h_hbm, d0_hbm, d1_hbm, pad_hbm, zero_hbm, out_hbm, d0_v, d1_v, pad_v, buf0, buf1, zbuf,
                        r0, r1, w0, w1, x0, x1):
        wid = lax.axis_index("s") * nc + lax.axis_index("c")
        base = wid * per_worker
        pltpu.sync_copy(d0_hbm.at[wid], d0_v)
        pltpu.sync_copy(d1_hbm.at[wid], d1_v)
        pltpu.sync_copy(pad_hbm.at[wid], pad_v)
        pltpu.sync_copy(zero_hbm, zbuf)
        bufs, rsem, wsem, xsem = (buf0, buf1), (r0, r1), (w0, w1), (x0, x1)

        def read(c, slot):
            return pltpu.make_async_copy(h_hbm.at[pl.ds(base + c * SC_WINDOW, SC_WINDOW)], bufs[slot], rsem[slot])

        read(0, 0).start()
        read(1, 1).start()
        for pc in range(npad):
            pltpu.sync_copy(zbuf, out_hbm.at[pad_v.at[pc]])

        @pl.loop(0, nchunk, step=2)
        def _(c):
            for slot in range(2):
                cc = c + slot
                read(cc, slot).wait()
                first = pltpu.make_async_copy(bufs[slot], out_hbm.at[d0_v.at[cc]], wsem[slot])
                second = pltpu.make_async_copy(bufs[slot], out_hbm.at[d1_v.at[cc]], xsem[slot])
                first.start()
                second.start()
                first.wait()
                second.wait()

                @pl.when(cc + 2 < nchunk)
                def _():
                    read(cc + 2, slot).start()

    split = lambda z, m: z.reshape(workers, m, SC_WINDOW)
    return dispatch_kernel(h, split(dest0, nchunk), split(dest1, nchunk), split(pad_rows, npad),
                           jnp.zeros((SC_WINDOW, d), h.dtype))


def _gather_rows(src, idx):
    info = plsc.get_sparse_core_info()
    nc, ns = info.num_cores, info.num_subcores
    workers = nc * ns
    m, d = idx.shape[0], src.shape[1]
    per_worker = m // workers
    nchunk = per_worker // SC_WINDOW
    assert m == workers * nchunk * SC_WINDOW and nchunk % 2 == 0, "row count must split evenly over subcores"
    mesh = plsc.VectorSubcoreMesh(core_axis_name="c", subcore_axis_name="s")

    @functools.partial(
        pl.kernel, mesh=mesh,
        out_type=jax.ShapeDtypeStruct((m, d), src.dtype),
        scratch_types=[pltpu.VMEM((nchunk, SC_WINDOW), jnp.int32),
                       pltpu.VMEM((SC_WINDOW, d), src.dtype), pltpu.VMEM((SC_WINDOW, d), src.dtype),
                       pltpu.SemaphoreType.DMA, pltpu.SemaphoreType.DMA,
                       pltpu.SemaphoreType.DMA, pltpu.SemaphoreType.DMA],
    )
    def gather_kernel(src_hbm, idx_hbm, out_hbm, idx_v, buf0, buf1, g0, g1, w0, w1):
        wid = lax.axis_index("s") * nc + lax.axis_index("c")
        base = wid * per_worker
        pltpu.sync_copy(idx_hbm.at[wid], idx_v)
        bufs, gsem, wsem = (buf0, buf1), (g0, g1), (w0, w1)

        def gather(c, slot):
            return pltpu.make_async_copy(src_hbm.at[idx_v.at[c]], bufs[slot], gsem[slot])

        def write(c, slot):
            return pltpu.make_async_copy(bufs[slot], out_hbm.at[pl.ds(base + c * SC_WINDOW, SC_WINDOW)], wsem[slot])

        gather(0, 0).start()
        gather(1, 1).start()

        @pl.loop(0, nchunk, step=2)
        def _(c):
            for slot in range(2):
                cc = c + slot
                gather(cc, slot).wait()
                write(cc, slot).start()
                write(cc, slot).wait()

                @pl.when(cc + 2 < nchunk)
                def _():
                    gather(cc + 2, slot).start()

    return gather_kernel(src, idx.reshape(workers, nchunk, SC_WINDOW))


def _expert_kernel(be_ref, nu_ref, x_ref, wg_ref, wu_ref, wd_ref, o_ref, h_ref):
    i, j = pl.program_id(0), pl.program_id(1)

    @pl.when(i < nu_ref[0])
    def _():
        @pl.when(j == 0)
        def _():
            h_ref[...] = _bf(x_ref[...])
            o_ref[...] = jnp.zeros_like(o_ref)

        h = h_ref[...]
        act = jax.nn.silu(_dot(h, wg_ref[0])) * _dot(h, wu_ref[0])
        o_ref[...] += _dot(_bf(act), wd_ref[0])

    @pl.when((i >= nu_ref[0]) & (j == 0))
    def _():
        o_ref[...] = jnp.zeros_like(o_ref)


def _expert_swiglu(xb, block_expert, n_used, wg, wu, wd):
    n_rows, d = xb.shape
    d_ff = wg.shape[2]
    tf = _ffn_tile(d_ff, 2)
    nf = d_ff // tf
    nblk = n_rows // MOE_BLOCK

    def row(i, j, be, nu):
        return jnp.minimum(i, nu[0] - 1)

    def col(i, j, be, nu):
        return jnp.where(i < nu[0], j, nf - 1)

    grid_spec = pltpu.PrefetchScalarGridSpec(
        num_scalar_prefetch=2,
        grid=(nblk, nf),
        in_specs=[pl.BlockSpec((MOE_BLOCK, d), lambda i, j, be, nu: (row(i, j, be, nu), 0)),
                  pl.BlockSpec((1, d, tf), lambda i, j, be, nu: (be[row(i, j, be, nu)], 0, col(i, j, be, nu))),
                  pl.BlockSpec((1, d, tf), lambda i, j, be, nu: (be[row(i, j, be, nu)], 0, col(i, j, be, nu))),
                  pl.BlockSpec((1, tf, d), lambda i, j, be, nu: (be[row(i, j, be, nu)], col(i, j, be, nu), 0))],
        out_specs=pl.BlockSpec((MOE_BLOCK, d), lambda i, j, be, nu: (i, 0)),
        scratch_shapes=[pltpu.VMEM((MOE_BLOCK, d), BF16)],
    )
    return pl.pallas_call(
        _expert_kernel,
        grid_spec=grid_spec,
        out_shape=jax.ShapeDtypeStruct((n_rows, d), F32),
        compiler_params=_cparams("arbitrary", "arbitrary"),
        name="expert_swiglu",
    )(block_expert, n_used, xb, _bf(wg), _bf(wu), _bf(wd))


def _combine_kernel(x_ref, y1_ref, y2_ref, meta_ref, g_ref, o_ref):
    meta = meta_ref[...]
    lane = lax.broadcasted_iota(jnp.int32, meta.shape, 1)
    g1 = jnp.sum(jnp.where(lane == META_G, meta, 0.0), axis=-1, keepdims=True)
    g2 = jnp.sum(jnp.where(lane == META_G + 1, meta, 0.0), axis=-1, keepdims=True)
    y = x_ref[...] + (g1 * y1_ref[...] + g2 * y2_ref[...])
    o_ref[...] = _rmsnorm(y, g_ref[...])


def _combine_final_norm(x2, yg, meta, gain):
    n, d = x2.shape
    nblk = n // ROW_TILE
    return pl.pallas_call(
        _combine_kernel,
        grid=(nblk,),
        in_specs=[pl.BlockSpec((ROW_TILE, d), lambda i: (i, 0)),
                  pl.BlockSpec((ROW_TILE, d), lambda i: (i, 0)),
                  pl.BlockSpec((ROW_TILE, d), lambda i: (i + nblk, 0)),
                  pl.BlockSpec((ROW_TILE, LANES), lambda i: (i, 0)),
                  pl.BlockSpec((1, d), lambda i: (0, 0))],
        out_specs=pl.BlockSpec((ROW_TILE, d), lambda i: (i, 0)),
        out_shape=jax.ShapeDtypeStruct((n, d), F32),
        compiler_params=_cparams("parallel"),
        name="moe_combine_norm",
    )(x2, yg, yg, meta, gain.reshape(1, d))


def _moe_layer(x2, gain, w_router, wg, wu, wd, final_gain):
    n, d = x2.shape
    h, meta, counts = _router(x2, gain, w_router)
    expert = meta[:, META_E:META_E + 2].astype(jnp.int32)
    rank = meta[:, META_RANK:META_RANK + 2].astype(jnp.int32)
    count = counts[0, :N_EXPERTS].astype(jnp.int32)
    padded = (count + MOE_BLOCK - 1) // MOE_BLOCK * MOE_BLOCK
    pad_end = jnp.cumsum(padded)
    pad_start = pad_end - padded
    dest = pad_start[expert] + rank
    n_rows = (2 * n // MOE_BLOCK + N_EXPERTS) * MOE_BLOCK
    empties = padded - count
    e_end = jnp.cumsum(empties)
    i = jnp.arange(n_rows - 2 * n, dtype=jnp.int32)
    grp = jnp.minimum(jnp.searchsorted(e_end, i, side='right'), N_EXPERTS).astype(jnp.int32)
    first_empty = jnp.concatenate([pad_start + count, pad_end[-1:]])
    first_index = jnp.concatenate([e_end - empties, e_end[-1:]])
    pad_rows = (first_empty[grp] + i - first_index[grp]).astype(jnp.int32)
    block_start = jnp.arange(n_rows // MOE_BLOCK, dtype=jnp.int32) * MOE_BLOCK
    block_expert = jnp.minimum(jnp.searchsorted(pad_end, block_start, side='right'), N_EXPERTS - 1).astype(jnp.int32)
    n_used = (pad_end[-1:] // MOE_BLOCK).astype(jnp.int32)
    xb = _dispatch_rows(h, dest[:, 0], dest[:, 1], pad_rows)
    ys = _expert_swiglu(xb, block_expert, n_used, wg, wu, wd)
    yg = _gather_rows(ys, jnp.concatenate([dest[:, 0], dest[:, 1]]))
    return _combine_final_norm(x2, yg, meta, final_gain)


def _mixer_layer(x2, batch, norm_mix, w_in, w_out, shift_mu, gm_ln_w, gm_ln_b, gm_ws, gm_bs,
                 rw_w_up, rw_w0, rw_a_up, rw_a0, rw_g_up, rw_k_k, rw_k_a, rw_r_k, rw_lnx_w, rw_lnx_b):
    n, d = x2.shape
    gm2 = 2 * gm_ln_w.shape[0]
    w_in = _bf(w_in)
    y_gm, r, c, lw, k, v, a, b, g, bonus = _mixer_in(
        x2.reshape(batch, n // batch, d), norm_mix, w_in[:, :gm2], w_in[:, gm2:], gm_ln_w, gm_ln_b, gm_ws, gm_bs,
        shift_mu, rw_w_up, rw_w0, rw_a_up, rw_a0, rw_g_up, rw_k_k, rw_k_a, rw_r_k.reshape(-1))
    y_scan = _wkv(r, c, lw, k, v, a, b)
    flat = lambda z: z.reshape(n, -1)
    return _post_outproj(x2, flat(y_gm), flat(y_scan), flat(bonus), flat(g), rw_lnx_w, rw_lnx_b, w_out)


def kernel(x, norm_mix, w_in, w_out, shift_mu, gm_ln_w, gm_ln_b, gm_ws, gm_bs, rw_w_up, rw_w0, rw_a_up, rw_a0,
           rw_g_up, rw_k_k, rw_k_a, rw_r_k, rw_lnx_w, rw_lnx_b, norm_ffn, ffn_w_gate, ffn_w_up, ffn_w_down,
           moe_router, moe_w_gate, moe_w_up, moe_w_down, norm_final):
    batch, t, d = x.shape
    depth = norm_mix.shape[0]
    assert depth == 2 and t % ROW_TILE == 0, "two layers (dense then MoE), sequence a multiple of the row tile"
    x2 = x.reshape(batch * t, d)
    for i in range(depth):
        x2 = _mixer_layer(x2, batch, norm_mix[i], w_in[i], w_out[i], shift_mu[i], gm_ln_w[i], gm_ln_b[i],
                          gm_ws[i], gm_bs[i], rw_w_up[i], rw_w0[i], rw_a_up[i], rw_a0[i], rw_g_up[i],
                          rw_k_k[i], rw_k_a[i], rw_r_k[i], rw_lnx_w[i], rw_lnx_b[i])
        if i % 2 == 0:
            x2 = _swiglu(x2, norm_ffn[i], ffn_w_gate[i // 2], ffn_w_up[i // 2], ffn_w_down[i // 2])
        else:
            x2 = _moe_layer(x2, norm_ffn[i], moe_router[i // 2], moe_w_gate[i // 2], moe_w_up[i // 2],
                            moe_w_down[i // 2], norm_final)
    return x2.reshape(batch, t, d)
```

```python
import functools

import jax
import jax.numpy as jnp
from jax import lax
from jax.experimental import pallas as pl
from jax.experimental.pallas import tpu as pltpu
from jax.experimental.pallas import tpu_sc as plsc

F32 = jnp.float32
BF16 = jnp.bfloat16

HEAD_DIM = 64
LANES = 128
GM_CHUNK = 128
WKV_CHUNK = 64
N_EXPERTS = 8
RMS_EPS = 1e-6
LN_EPS = 1e-5
GN_EPS = 64e-5
NEG_BIG = -1e30

ROW_TILE = 512
WKV_STEP = 256
MOE_BLOCK = 512
SC_WINDOW = 32
VMEM_LIMIT = 56 * 1024 * 1024


def _cparams(*sem):
    return pltpu.CompilerParams(dimension_semantics=sem, vmem_limit_bytes=VMEM_LIMIT)


def _bf(x):
    return x.astype(BF16)


def _dot(a, b):
    return jnp.dot(a, b, preferred_element_type=F32)


def _dot_nt(a, b):
    return lax.dot_general(a, b, (((1,), (1,)), ((), ())), preferred_element_type=F32)


def _dot_tn(a, b):
    return lax.dot_general(a, b, (((0,), (0,)), ((), ())), preferred_element_type=F32)


def _split_dot_left(m, x, parts=2):
    acc = None
    rem = x
    for p in range(parts):
        hi = _bf(rem)
        d = _dot(m, hi)
        acc = d if acc is None else acc + d
        if p + 1 < parts:
            rem = rem - hi.astype(F32)
    return acc


def _rmsnorm(x, g):
    return x * lax.rsqrt(jnp.mean(x * x, axis=-1, keepdims=True) + RMS_EPS) * g


def _gelu(x):
    return 0.5 * x * (1.0 + lax.erf(x * (2.0 ** -0.5)))


def _head_avg_matrix(width):
    r = lax.broadcasted_iota(jnp.int32, (width, width), 0) // HEAD_DIM
    c = lax.broadcasted_iota(jnp.int32, (width, width), 1) // HEAD_DIM
    return jnp.where(r == c, 1.0 / HEAD_DIM, 0.0).astype(BF16)


def _head_layernorm(x, avg, w, b, eps):
    mu = _dot(_bf(x), avg)
    xc = x - mu
    var = _dot(_bf(xc * xc), avg)
    return xc * lax.rsqrt(var + eps) * w + b


def _resident(shape):
    return pl.BlockSpec(shape, lambda *_: (0,) * len(shape), pipeline_mode=pl.Buffered(1))


def _gmlp_chunk(p, avg, lnw, lnb, w2, bias, head0):
    width = p.shape[1] // 2
    u = _gelu(p[:, :width])
    vn = _head_layernorm(_gelu(p[:, width:]), avg, lnw, lnb, LN_EPS)
    mixed = []
    for pair, w in enumerate(w2):
        vp = vn[:, pair * LANES:(pair + 1) * LANES]
        v_st = _bf(jnp.concatenate([jnp.where(head0, vp, 0.0), jnp.where(head0, 0.0, vp)], axis=0))
        mixed.append(_dot(w, v_st))
    return u * (jnp.concatenate(mixed, axis=1) + bias)


def _mixer_in_kernel(x_ref, gain_ref, wgm_ref, wrw_ref, lnw_ref, lnb_ref, ws_ref, bias_ref,
                     mu_ref, wup_ref, w0_ref, aup_ref, a0_ref, gup_ref, kk_ref, ka_ref, rk_ref,
                     ygm_ref, r_ref, c_ref, lw_ref, k_ref, v_ref, a_ref, b_ref, g_ref, bonus_ref, carry_ref):
    width = r_ref.shape[2]
    rows = x_ref.shape[1]

    @pl.when(pl.program_id(1) == 0)
    def _():
        carry_ref[...] = jnp.zeros_like(carry_ref)

    h = _bf(_rmsnorm(x_ref[0], gain_ref[...]))
    avg = _head_avg_matrix(width)

    p_gm = _dot(h, wgm_ref[...])
    t = lax.broadcasted_iota(jnp.int32, (GM_CHUNK, GM_CHUNK), 0)
    s = lax.broadcasted_iota(jnp.int32, (GM_CHUNK, GM_CHUNK), 1)
    causal = s <= t
    head0 = lax.broadcasted_iota(jnp.int32, (GM_CHUNK, LANES), 1) < HEAD_DIM
    w2 = [_bf(jnp.concatenate([jnp.where(causal, ws_ref[2 * pair], 0.0),
                               jnp.where(causal, ws_ref[2 * pair + 1], 0.0)], axis=1))
          for pair in range(width // LANES)]
    for ch in range(rows // GM_CHUNK):
        ts = slice(ch * GM_CHUNK, (ch + 1) * GM_CHUNK)
        ygm_ref[0, ts, :] = _bf(_gmlp_chunk(p_gm[ts], avg, lnw_ref[...], lnb_ref[...], w2, bias_ref[...], head0))

    p = _dot(h, wrw_ref[...])
    first = lax.broadcasted_iota(jnp.int32, p.shape, 0) == 0
    prev = jnp.where(first, carry_ref[0:1, :], pltpu.roll(p, 1, axis=0))
    carry_ref[0:1, :] = p[rows - 1:rows, :]
    ps = p + (prev - p) * mu_ref[...]
    r = ps[:, :width]
    k = ps[:, width:2 * width]
    v = ps[:, 2 * width:3 * width]
    lora_in = ps[:, 3 * width:3 * width + LANES]
    gd = ps[:, 3 * width + LANES:]
    log_w = -jax.nn.softplus(-(w0_ref[...] + _dot(_bf(jnp.tanh(lora_in)), wup_ref[...]))) - 0.5
    lw = -jnp.exp(log_w)
    a_lr = jax.nn.sigmoid(a0_ref[...] + _dot(_bf(lora_in), aup_ref[...]))
    ones = avg * HEAD_DIM
    kk = k * kk_ref[...]
    kk = kk / jnp.maximum(jnp.sqrt(_dot(_bf(kk * kk), ones)), 1e-12)
    kmod = k * (1.0 + (a_lr - 1.0) * ka_ref[...])
    t = lax.broadcasted_iota(jnp.int32, (rows, rows), 0)
    s = lax.broadcasted_iota(jnp.int32, (rows, rows), 1)
    tri = ((t // WKV_CHUNK == s // WKV_CHUNK) & (s <= t)).astype(BF16)
    lw_ref[0] = lw
    c_ref[0] = _split_dot_left(tri, lw)
    r_ref[0] = _bf(r)
    k_ref[0] = _bf(kmod)
    v_ref[0] = _bf(v)
    a_ref[0] = _bf(-kk)
    b_ref[0] = _bf(kk * a_lr)
    g_ref[0] = _bf(_dot(_bf(jax.nn.sigmoid(gd)), gup_ref[...]))
    bonus_ref[0] = _bf(_dot(_bf(r * kmod * rk_ref[...]), ones) * v)


def _mixer_in(x3, gain, w_gm, w_rw, ln_w, ln_b, ws, bs, mu, w_up, w0, a_up, a0, g_up, k_k, k_a, r_k):
    batch, t, d = x3.shape
    width = w0.shape[0]
    rank = w_up.shape[0]
    zeros = jnp.zeros((LANES - rank, width), F32)
    wup_pad = _bf(jnp.concatenate([w_up, zeros], axis=0))
    aup_pad = _bf(jnp.concatenate([zeros, a_up], axis=0))
    bias = jnp.repeat(bs.T, HEAD_DIM, axis=1)
    row = lambda z: z.reshape(1, -1)
    params = (row(gain), w_gm, w_rw, row(ln_w), row(ln_b), ws, bias, row(mu), wup_pad, row(w0), aup_pad, row(a0),
              _bf(g_up), row(k_k), row(k_a), row(r_k))
    out = pl.BlockSpec((1, ROW_TILE, width), lambda b, i: (b, i, 0))
    sds = lambda dtype: jax.ShapeDtypeStruct((batch, t, width), dtype)
    return pl.pallas_call(
        _mixer_in_kernel,
        grid=(batch, t // ROW_TILE),
        in_specs=[pl.BlockSpec((1, ROW_TILE, d), lambda b, i: (b, i, 0))] + [_resident(z.shape) for z in params],
        out_specs=[out] * 10,
        out_shape=[sds(BF16), sds(BF16), sds(F32), sds(F32)] + [sds(BF16)] * 6,
        scratch_shapes=[pltpu.VMEM((8, w_rw.shape[1]), F32)],
        compiler_params=_cparams("parallel", "arbitrary"),
        name="mixer_in",
    )(x3, *params)


def _wkv_kernel(r_ref, c_ref, lw_ref, k_ref, v_ref, a_ref, b_ref, y_ref, h_ref,
                q_s, v_s, kc_s, ark_s, bcrb_s, mz_s, rhs_s, sol_s, t_s, rh_s, rkv_s, kv_s, dec_s, hc_s, yc_s):
    @pl.when(pl.program_id(1) == 0)
    def _():
        h_ref[...] = jnp.zeros_like(h_ref)

    n = WKV_CHUNK
    two = 2 * n
    head0 = lax.broadcasted_iota(jnp.int32, (n, LANES), 1) < HEAD_DIM
    row = lax.broadcasted_iota(jnp.int32, (two, two), 0)
    col = lax.broadcasted_iota(jnp.int32, (two, two), 1)
    eye = row == col
    row2 = lax.broadcasted_iota(jnp.int32, (two, 2 * two), 0)
    col2 = lax.broadcasted_iota(jnp.int32, (two, 2 * two), 1) % two
    same2 = (row2 // n) == (col2 // n)
    strict2 = same2 & ((col2 % n) < (row2 % n))
    incl2 = same2 & ((col2 % n) <= (row2 % n))
    npairs = y_ref.shape[2] // LANES
    nchunks = y_ref.shape[1] // n
    items = [(ch, pair) for ch in range(nchunks) for pair in range(npairs)]

    def window(ch, pair):
        return slice(ch * n, (ch + 1) * n), slice(pair * LANES, (pair + 1) * LANES)

    def stack(z):
        return jnp.concatenate([jnp.where(head0, z, 0.0), jnp.where(head0, 0.0, z)], axis=0)

    def dup(z):
        zb = _bf(z)
        return jnp.concatenate([zb, zb], axis=0)

    def scores(it):
        ts, ls = window(*items[it])
        r, k, v, a, b = (ref[0, ts, ls].astype(F32) for ref in (r_ref, k_ref, v_ref, a_ref, b_ref))
        c = c_ref[0, ts, ls]
        cex = c - lw_ref[0, ts, ls]
        mid = c[n // 2 - 1:n // 2, :]
        last = c[n - 1:n, :]
        g_inv = jnp.exp(mid - c)
        g_end = jnp.exp(last - c)
        ar_st = _bf(jnp.concatenate([stack(a * jnp.exp(cex - mid)), stack(r * jnp.exp(c - mid))], axis=0))
        bk_dup = jnp.concatenate([dup(b * g_inv), dup(k * g_inv)], axis=0)
        sc = _dot_nt(ar_st, bk_dup)
        top = jnp.where(strict2, sc[:two], 0.0)
        bot = jnp.where(incl2, sc[two:], 0.0)
        t_s[it] = jnp.where(eye, 1.0, top[:, :two])
        q_s[it] = _bf(top[:, :two])
        bcrb_s[it] = _bf(jnp.concatenate([stack(b * g_end).T, bot[:, :two]], axis=0))
        ark_s[it] = _bf(jnp.concatenate([top[:, two:], bot[:, two:]], axis=0))
        v_s[it] = _bf(stack(v))
        kc_s[it] = _bf(stack(k * g_end))
        rh_s[it] = stack(r * jnp.exp(c))
        rhs_s[it, :, :LANES] = _bf(stack(a * jnp.exp(cex)))
        e_col = jnp.sum(jnp.where(eye, jnp.exp(last), 0.0), axis=1, keepdims=True)
        dec_s[it] = jnp.broadcast_to(e_col, (two, LANES))

    def values(it):
        v_st = v_s[it]
        both = _dot(ark_s[it], v_st)
        rhs_s[it, :, LANES:] = _bf(both[:two])
        rkv_s[it] = both[two:]
        kv_s[it] = _dot_tn(kc_s[it], v_st)

    levels = n.bit_length() - 1

    def inverse_level(level, it):
        q = q_s[it]
        t_acc = t_s[it]
        if level == 1:
            q = _bf(_dot(q, q))
        if level + 1 < levels:
            prod = _dot(q, jnp.concatenate([_bf(t_acc), q], axis=1))
            q_s[it] = _bf(prod[:, two:])
            t_s[it] = t_acc + prod[:, :two]
        else:
            t_s[it] = t_acc + _dot(q, _bf(t_acc))

    def solve(it):
        sol_s[it] = _bf(_dot(_bf(t_s[it]), rhs_s[it]))

    def fold(it):
        prod = _dot(bcrb_s[it], sol_s[it])
        mz_s[it] = _bf(jnp.concatenate([prod[:two, :LANES], rh_s[it] + prod[two:, :LANES]], axis=0))
        hc_s[it] = prod[:two, LANES:] + kv_s[it]
        yc_s[it] = prod[two:, LANES:] + rkv_s[it]

    def advance(it):
        ts, ls = window(*items[it])
        pair = items[it][1]
        h = h_ref[pair]
        gmat = _dot(mz_s[it], _bf(h))
        h_ref[pair] = dec_s[it] * h + gmat[:two] + hc_s[it]
        y_st = gmat[two:] + yc_s[it]
        y_ref[0, ts, ls] = y_st[:n] + y_st[n:]

    stages = ([scores, values] + [functools.partial(inverse_level, level) for level in range(1, levels)]
              + [solve, fold, advance])
    for stage in stages:
        for it in range(len(items)):
            stage(it)


def _wkv(r, c, lw, k, v, a, b):
    batch, t, width = r.shape
    npairs = width // LANES
    items = (WKV_STEP // WKV_CHUNK) * npairs
    spec = pl.BlockSpec((1, WKV_STEP, width), lambda bi, i: (bi, i, 0))
    sq = lambda dtype: pltpu.VMEM((items, LANES, LANES), dtype)
    return pl.pallas_call(
        _wkv_kernel,
        grid=(batch, t // WKV_STEP),
        in_specs=[spec] * 7,
        out_specs=spec,
        out_shape=jax.ShapeDtypeStruct((batch, t, width), F32),
        scratch_shapes=[pltpu.VMEM((npairs, LANES, LANES), F32)] + [sq(BF16)] * 3
                       + [pltpu.VMEM((items, 2 * LANES, LANES), BF16)] * 3
                       + [pltpu.VMEM((items, LANES, 2 * LANES), BF16)] * 2 + [sq(F32)] * 7,
        compiler_params=_cparams("parallel", "arbitrary"),
        name="wkv7",
    )(r, c, lw, k, v, a, b)


N_MIX_OUT_REFS = 9


def _mixer_out(x_ref, ygm_ref, ys_ref, bonus_ref, g_ref, lnw_ref, lnb_ref, wtop_ref, wbot_ref):
    width = ys_ref.shape[1]
    yn = _head_layernorm(ys_ref[...], _head_avg_matrix(width), lnw_ref[...], lnb_ref[...], GN_EPS)
    y_rw = (yn + bonus_ref[...]) * g_ref[...]
    return x_ref[...] + _dot(ygm_ref[...], wtop_ref[...]) + _dot(_bf(y_rw), wbot_ref[...])


def _mixer_out_operands(x2, y_gm, y_scan, bonus, g, lnx_w, lnx_b, w_out):
    d = x2.shape[1]
    width = y_gm.shape[1]
    rows = lambda w: pl.BlockSpec((ROW_TILE, w), lambda i: (i, 0))
    w_out = _bf(w_out)
    specs = [rows(d), rows(width), rows(width), rows(width), rows(width),
             _resident((1, width)), _resident((1, width)), _resident((width, d)), _resident((width, d))]
    args = (x2, y_gm, y_scan, bonus, g, lnx_w.reshape(1, width), lnx_b.reshape(1, width), w_out[:width], w_out[width:])
    assert len(specs) == len(args) == N_MIX_OUT_REFS
    return specs, args


def _out_swiglu_kernel(*refs):
    mix_refs, (gain_ref, wg_ref, wu_ref, wd_ref, o_ref) = refs[:N_MIX_OUT_REFS], refs[N_MIX_OUT_REFS:]
    x = _mixer_out(*mix_refs)
    h = _bf(_rmsnorm(x, gain_ref[...]))
    act = jax.nn.silu(_dot(h, wg_ref[...])) * _dot(h, wu_ref[...])
    o_ref[...] = x + _dot(_bf(act), wd_ref[...])


def _ffn_tile(d_ff, parts):
    assert d_ff % (parts * LANES) == 0, f"d_ff={d_ff} does not split into {parts} lane-aligned tiles"
    return d_ff // parts


def _out_swiglu(mix_args, gain, wg, wu, wd):
    specs, args = _mixer_out_operands(*mix_args)
    n, d = args[0].shape
    d_ff = wg.shape[1]
    return pl.pallas_call(
        _out_swiglu_kernel,
        grid=(n // ROW_TILE,),
        in_specs=specs + [_resident((1, d)), _resident((d, d_ff)), _resident((d, d_ff)), _resident((d_ff, d))],
        out_specs=pl.BlockSpec((ROW_TILE, d), lambda i: (i, 0)),
        out_shape=jax.ShapeDtypeStruct((n, d), F32),
        compiler_params=_cparams("parallel"),
        name="out_dense_swiglu",
    )(*args, gain.reshape(1, d), _bf(wg), _bf(wu), _bf(wd))


META_E, META_G, META_RANK = 0, 2, 4


def _out_router_kernel(*refs):
    mix_refs = refs[:N_MIX_OUT_REFS]
    g_ref, whi_ref, wlo_ref, x_ref, h_ref, meta_ref, cnt_ref, run_ref = refs[N_MIX_OUT_REFS:]

    @pl.when(pl.program_id(0) == 0)
    def _():
        run_ref[...] = jnp.zeros_like(run_ref)

    x = _mixer_out(*mix_refs)
    x_ref[...] = x
    h = _rmsnorm(x, g_ref[...])
    h_ref[...] = h
    rows = h.shape[0]
    lane = lax.broadcasted_iota(jnp.int32, (rows, LANES), 1)
    h_hi = _bf(h)
    h_lo = _bf(h - h_hi.astype(F32))
    logits = _dot(h_hi, whi_ref[...]) + _dot(h_lo, whi_ref[...]) + _dot(h_hi, wlo_ref[...])
    logits = jnp.where(lane < N_EXPERTS, logits, NEG_BIG)
    v1 = jnp.max(logits, axis=-1, keepdims=True)
    e1 = jnp.min(jnp.where(logits == v1, lane, LANES), axis=-1, keepdims=True)
    oh1 = lane == e1
    rest = jnp.where(oh1, NEG_BIG, logits)
    v2 = jnp.max(rest, axis=-1, keepdims=True)
    e2 = jnp.min(jnp.where(rest == v2, lane, LANES), axis=-1, keepdims=True)
    oh2 = lane == e2
    ex = jnp.exp(v2 - v1)
    g1 = 1.0 / (1.0 + ex)
    g2 = ex / (1.0 + ex)
    cnt = jnp.where(oh1 | oh2, 1.0, 0.0)
    t = lax.broadcasted_iota(jnp.int32, (rows, rows), 0)
    s = lax.broadcasted_iota(jnp.int32, (rows, rows), 1)
    before = _dot(jnp.where(s < t, 1.0, 0.0).astype(BF16), _bf(cnt)) + run_ref[0:1, :]
    rank1 = jnp.sum(jnp.where(oh1, before, 0.0), axis=-1, keepdims=True)
    rank2 = jnp.sum(jnp.where(oh2, before, 0.0), axis=-1, keepdims=True)
    fields = (e1.astype(F32), e2.astype(F32), g1, g2, rank1, rank2)
    meta = jnp.zeros((rows, LANES), F32)
    for idx, val in enumerate(fields):
        meta = jnp.where(lane == idx, val, meta)
    meta_ref[...] = meta
    run = run_ref[0:1, :] + jnp.sum(cnt, axis=0, keepdims=True)
    run_ref[...] = jnp.broadcast_to(run, run_ref.shape)
    cnt_ref[...] = jnp.broadcast_to(run, cnt_ref.shape)


def _out_router(mix_args, gain, w_router):
    specs, args = _mixer_out_operands(*mix_args)
    n, d = args[0].shape
    wr = jnp.concatenate([w_router, jnp.zeros((d, LANES - w_router.shape[1]), F32)], axis=1)
    wr_hi = _bf(wr)
    wr_lo = _bf(wr - wr_hi.astype(F32))
    rows = lambda w: pl.BlockSpec((ROW_TILE, w), lambda i: (i, 0))
    return pl.pallas_call(
        _out_router_kernel,
        grid=(n // ROW_TILE,),
        in_specs=specs + [_resident((1, d)), _resident((d, LANES)), _resident((d, LANES))],
        out_specs=[rows(d), rows(d), rows(LANES), pl.BlockSpec((8, LANES), lambda i: (0, 0))],
        out_shape=[jax.ShapeDtypeStruct((n, d), F32), jax.ShapeDtypeStruct((n, d), F32),
                   jax.ShapeDtypeStruct((n, LANES), F32), jax.ShapeDtypeStruct((8, LANES), F32)],
        scratch_shapes=[pltpu.VMEM((8, LANES), F32)],
        compiler_params=_cparams("arbitrary"),
        name="out_moe_router",
    )(*args, gain.reshape(1, d), wr_hi, wr_lo)


def _sc_workers():
    info = plsc.get_sparse_core_info()
    return info.num_cores, info.num_cores * info.num_subcores


def _dispatch_rows(h, dest0, dest1, pad_rows):
    nc, workers = _sc_workers()
    n, d = h.shape
    n_pad = pad_rows.shape[0]
    per_worker = n // workers
    nchunk = per_worker // SC_WINDOW
    npad = n_pad // (workers * SC_WINDOW)
    assert n == workers * nchunk * SC_WINDOW and nchunk % 2 == 0 and n_pad == workers * npad * SC_WINDOW
    mesh = plsc.VectorSubcoreMesh(core_axis_name="c", subcore_axis_name="s")
    idx_t = lambda m: pltpu.VMEM((m, SC_WINDOW), jnp.int32)
    buf_t = pltpu.VMEM((SC_WINDOW, d), h.dtype)

    @functools.partial(
        pl.kernel, mesh=mesh,
        out_type=jax.ShapeDtypeStruct((2 * n + n_pad, d), h.dtype),
        scratch_types=[idx_t(nchunk), idx_t(nchunk), idx_t(npad), buf_t, buf_t, buf_t]
                      + [pltpu.SemaphoreType.DMA] * 6,
    )
    def dispatch_kernel(h_hbm, d0_hbm, d1_hbm, pad_hbm, zero_hbm, out_hbm, d0_v, d1_v, pad_v, buf0, buf1, zbuf,
                        r0, r1, w0, w1, x0, x1):
        wid = lax.axis_index("s") * nc + lax.axis_index("c")
        base = wid * per_worker
        pltpu.sync_copy(d0_hbm.at[wid], d0_v)
        pltpu.sync_copy(d1_hbm.at[wid], d1_v)
        pltpu.sync_copy(pad_hbm.at[wid], pad_v)
        pltpu.sync_copy(zero_hbm, zbuf)
        bufs, rsem, wsem, xsem = (buf0, buf1), (r0, r1), (w0, w1), (x0, x1)

        def read(c, slot):
            return pltpu.make_async_copy(h_hbm.at[pl.ds(base + c * SC_WINDOW, SC_WINDOW)], bufs[slot], rsem[slot])

        read(0, 0).start()
        read(1, 1).start()
        for pc in range(npad):
            pltpu.sync_copy(zbuf, out_hbm.at[pad_v.at[pc]])

        @pl.loop(0, nchunk, step=2)
        def _(c):
            for slot in range(2):
                cc = c + slot
                read(cc, slot).wait()
                first = pltpu.make_async_copy(bufs[slot], out_hbm.at[d0_v.at[cc]], wsem[slot])
                second = pltpu.make_async_copy(bufs[slot], out_hbm.at[d1_v.at[cc]], xsem[slot])
                first.start()
                second.start()
                first.wait()
                second.wait()

                @pl.when(cc + 2 < nchunk)
                def _():
                    read(cc + 2, slot).start()

    split = lambda z, m: z.reshape(workers, m, SC_WINDOW)
    return dispatch_kernel(h, split(dest0, nchunk), split(dest1, nchunk), split(pad_rows, npad),
                           jnp.zeros((SC_WINDOW, d), h.dtype))


def _gather_rows(src, idx):
    info = plsc.get_sparse_core_info()
    nc, ns = info.num_cores, info.num_subcores
    workers = nc * ns
    m, d = idx.shape[0], src.shape[1]
    per_worker = m // workers
    nchunk = per_worker // SC_WINDOW
    assert m == workers * nchunk * SC_WINDOW and nchunk % 2 == 0, "row count must split evenly over subcores"
    mesh = plsc.VectorSubcoreMesh(core_axis_name="c", subcore_axis_name="s")

    @functools.partial(
        pl.kernel, mesh=mesh,
        out_type=jax.ShapeDtypeStruct((m, d), src.dtype),
        scratch_types=[pltpu.VMEM((nchunk, SC_WINDOW), jnp.int32),
                       pltpu.VMEM((SC_WINDOW, d), src.dtype), pltpu.VMEM((SC_WINDOW, d), src.dtype),
                       pltpu.SemaphoreType.DMA, pltpu.SemaphoreType.DMA,
                       pltpu.SemaphoreType.DMA, pltpu.SemaphoreType.DMA],
    )
    def gather_kernel(src_hbm, idx_hbm, out_hbm, idx_v, buf0, buf1, g0, g1, w0, w1):
        wid = lax.axis_index("s") * nc + lax.axis_index("c")
        base = wid * per_worker
        pltpu.sync_copy(idx_hbm.at[wid], idx_v)
        bufs, gsem, wsem = (buf0, buf1), (g0, g1), (w0, w1)

        def gather(c, slot):
            return pltpu.make_async_copy(src_hbm.at[idx_v.at[c]], bufs[slot], gsem[slot])

        def write(c, slot):
            return pltpu.make_async_copy(bufs[slot], out_hbm.at[pl.ds(base + c * SC_WINDOW, SC_WINDOW)], wsem[slot])

        gather(0, 0).start()
        gather(1, 1).start()

        @pl.loop(0, nchunk, step=2)
        def _(c):
            for slot in range(2):
                cc = c + slot
                gather(cc, slot).wait()
                write(cc, slot).start()
                write(cc, slot).wait()

                @pl.when(cc + 2 < nchunk)
                def _():
                    gather(cc + 2, slot).start()

    return gather_kernel(src, idx.reshape(workers, nchunk, SC_WINDOW))


def _expert_kernel(be_ref, nu_ref, x_ref, wg_ref, wu_ref, wd_ref, o_ref, h_ref):
    i, j = pl.program_id(0), pl.program_id(1)

    @pl.when(i < nu_ref[0])
    def _():
        @pl.when(j == 0)
        def _():
            h_ref[...] = _bf(x_ref[...])
            o_ref[...] = jnp.zeros_like(o_ref)

        h = h_ref[...]
        act = jax.nn.silu(_dot(h, wg_ref[0])) * _dot(h, wu_ref[0])
        o_ref[...] += _dot(_bf(act), wd_ref[0])

    @pl.when((i >= nu_ref[0]) & (j == 0))
    def _():
        o_ref[...] = jnp.zeros_like(o_ref)


def _expert_swiglu(xb, block_expert, n_used, wg, wu, wd):
    n_rows, d = xb.shape
    d_ff = wg.shape[2]
    tf = _ffn_tile(d_ff, 2)
    nf = d_ff // tf
    nblk = n_rows // MOE_BLOCK

    def row(i, j, be, nu):
        return jnp.minimum(i, nu[0] - 1)

    def col(i, j, be, nu):
        return jnp.where(i < nu[0], j, nf - 1)

    grid_spec = pltpu.PrefetchScalarGridSpec(
        num_scalar_prefetch=2,
        grid=(nblk, nf),
        in_specs=[pl.BlockSpec((MOE_BLOCK, d), lambda i, j, be, nu: (row(i, j, be, nu), 0)),
                  pl.BlockSpec((1, d, tf), lambda i, j, be, nu: (be[row(i, j, be, nu)], 0, col(i, j, be, nu))),
                  pl.BlockSpec((1, d, tf), lambda i, j, be, nu: (be[row(i, j, be, nu)], 0, col(i, j, be, nu))),
                  pl.BlockSpec((1, tf, d), lambda i, j, be, nu: (be[row(i, j, be, nu)], col(i, j, be, nu), 0))],
        out_specs=pl.BlockSpec((MOE_BLOCK, d), lambda i, j, be, nu: (i, 0)),
        scratch_shapes=[pltpu.VMEM((MOE_BLOCK, d), BF16)],
    )
    return pl.pallas_call(
        _expert_kernel,
        grid_spec=grid_spec,
        out_shape=jax.ShapeDtypeStruct((n_rows, d), F32),
        compiler_params=_cparams("arbitrary", "arbitrary"),
        name="expert_swiglu",
    )(block_expert, n_used, xb, _bf(wg), _bf(wu), _bf(wd))


def _combine_kernel(x_ref, y1_ref, y2_ref, meta_ref, g_ref, o_ref):
    meta = meta_ref[...]
    lane = lax.broadcasted_iota(jnp.int32, meta.shape, 1)
    g1 = jnp.sum(jnp.where(lane == META_G, meta, 0.0), axis=-1, keepdims=True)
    g2 = jnp.sum(jnp.where(lane == META_G + 1, meta, 0.0), axis=-1, keepdims=True)
    y = x_ref[...] + (g1 * y1_ref[...] + g2 * y2_ref[...])
    o_ref[...] = _rmsnorm(y, g_ref[...])


def _combine_final_norm(x2, yg, meta, gain):
    n, d = x2.shape
    nblk = n // ROW_TILE
    return pl.pallas_call(
        _combine_kernel,
        grid=(nblk,),
        in_specs=[pl.BlockSpec((ROW_TILE, d), lambda i: (i, 0)),
                  pl.BlockSpec((ROW_TILE, d), lambda i: (i, 0)),
                  pl.BlockSpec((ROW_TILE, d), lambda i: (i + nblk, 0)),
                  pl.BlockSpec((ROW_TILE, LANES), lambda i: (i, 0)),
                  pl.BlockSpec((1, d), lambda i: (0, 0))],
        out_specs=pl.BlockSpec((ROW_TILE, d), lambda i: (i, 0)),
        out_shape=jax.ShapeDtypeStruct((n, d), F32),
        compiler_params=_cparams("parallel"),
        name="moe_combine_norm",
    )(x2, yg, yg, meta, gain.reshape(1, d))


def _moe_layer(mix_args, gain, w_router, wg, wu, wd, final_gain):
    x2, h, meta, counts = _out_router(mix_args, gain, w_router)
    n, d = x2.shape
    expert = meta[:, META_E:META_E + 2].astype(jnp.int32)
    rank = meta[:, META_RANK:META_RANK + 2].astype(jnp.int32)
    count = counts[0, :N_EXPERTS].astype(jnp.int32)
    padded = (count + MOE_BLOCK - 1) // MOE_BLOCK * MOE_BLOCK
    pad_end = jnp.cumsum(padded)
    pad_start = pad_end - padded
    dest = pad_start[expert] + rank
    n_rows = (2 * n // MOE_BLOCK + N_EXPERTS) * MOE_BLOCK
    empties = padded - count
    e_end = jnp.cumsum(empties)
    i = jnp.arange(n_rows - 2 * n, dtype=jnp.int32)
    grp = jnp.minimum(jnp.searchsorted(e_end, i, side='right'), N_EXPERTS).astype(jnp.int32)
    first_empty = jnp.concatenate([pad_start + count, pad_end[-1:]])
    first_index = jnp.concatenate([e_end - empties, e_end[-1:]])
    pad_rows = (first_empty[grp] + i - first_index[grp]).astype(jnp.int32)
    block_start = jnp.arange(n_rows // MOE_BLOCK, dtype=jnp.int32) * MOE_BLOCK
    block_expert = jnp.minimum(jnp.searchsorted(pad_end, block_start, side='right'), N_EXPERTS - 1).astype(jnp.int32)
    n_used = (pad_end[-1:] // MOE_BLOCK).astype(jnp.int32)
    xb = _dispatch_rows(h, dest[:, 0], dest[:, 1], pad_rows)
    ys = _expert_swiglu(xb, block_expert, n_used, wg, wu, wd)
    yg = _gather_rows(ys, jnp.concatenate([dest[:, 0], dest[:, 1]]))
    return _combine_final_norm(x2, yg, meta, final_gain)


def _mixer_layer(x2, batch, norm_mix, w_in, w_out, shift_mu, gm_ln_w, gm_ln_b, gm_ws, gm_bs,
                 rw_w_up, rw_w0, rw_a_up, rw_a0, rw_g_up, rw_k_k, rw_k_a, rw_r_k, rw_lnx_w, rw_lnx_b):
    n, d = x2.shape
    gm2 = 2 * gm_ln_w.shape[0]
    w_in = _bf(w_in)
    y_gm, r, c, lw, k, v, a, b, g, bonus = _mixer_in(
        x2.reshape(batch, n // batch, d), norm_mix, w_in[:, :gm2], w_in[:, gm2:], gm_ln_w, gm_ln_b, gm_ws, gm_bs,
        shift_mu, rw_w_up, rw_w0, rw_a_up, rw_a0, rw_g_up, rw_k_k, rw_k_a, rw_r_k.reshape(-1))
    y_scan = _wkv(r, c, lw, k, v, a, b)
    flat = lambda z: z.reshape(n, -1)
    return x2, flat(y_gm), flat(y_scan), flat(bonus), flat(g), rw_lnx_w, rw_lnx_b, w_out


def kernel(x, norm_mix, w_in, w_out, shift_mu, gm_ln_w, gm_ln_b, gm_ws, gm_bs, rw_w_up, rw_w0, rw_a_up, rw_a0,
           rw_g_up, rw_k_k, rw_k_a, rw_r_k, rw_lnx_w, rw_lnx_b, norm_ffn, ffn_w_gate, ffn_w_up, ffn_w_down,
           moe_router, moe_w_gate, moe_w_up, moe_w_down, norm_final):
    batch, t, d = x.shape
    depth = norm_mix.shape[0]
    assert depth == 2 and t % ROW_TILE == 0, "two layers (dense then MoE), sequence a multiple of the row tile"
    x2 = x.reshape(batch * t, d)
    for i in range(depth):
        mixed = _mixer_layer(x2, batch, norm_mix[i], w_in[i], w_out[i], shift_mu[i], gm_ln_w[i], gm_ln_b[i],
                             gm_ws[i], gm_bs[i], rw_w_up[i], rw_w0[i], rw_a_up[i], rw_a0[i], rw_g_up[i],
                             rw_k_k[i], rw_k_a[i], rw_r_k[i], rw_lnx_w[i], rw_lnx_b[i])
        if i % 2 == 0:
            x2 = _out_swiglu(mixed, norm_ffn[i], ffn_w_gate[i // 2], ffn_w_up[i // 2], ffn_w_down[i // 2])
        else:
            x2 = _moe_layer(mixed, norm_ffn[i], moe_router[i // 2], moe_w_gate[i // 2], moe_w_up[i // 2],
                            moe_w_down[i // 2], norm_final)
    return x2.reshape(batch, t, d)
```

```python
import functools

import jax
import jax.numpy as jnp
from jax import lax
from jax.experimental import pallas as pl
from jax.experimental.pallas import tpu as pltpu
from jax.experimental.pallas import tpu_sc as plsc

F32 = jnp.float32
BF16 = jnp.bfloat16

HEAD_DIM = 64
LANES = 128
GM_CHUNK = 128
WKV_CHUNK = 64
N_EXPERTS = 8
RMS_EPS = 1e-6
LN_EPS = 1e-5
GN_EPS = 64e-5
NEG_BIG = -1e30

ROW_TILE = 512
MIX_TILE = 1024
WKV_STEP = 256
MOE_BLOCK = 512
SC_WINDOW = 32
VMEM_LIMIT = 56 * 1024 * 1024


def _cparams(*sem):
    return pltpu.CompilerParams(dimension_semantics=sem, vmem_limit_bytes=VMEM_LIMIT)


def _bf(x):
    return x.astype(BF16)


def _dot(a, b):
    return jnp.dot(a, b, preferred_element_type=F32)


def _dot_nt(a, b):
    return lax.dot_general(a, b, (((1,), (1,)), ((), ())), preferred_element_type=F32)


def _dot_tn(a, b):
    return lax.dot_general(a, b, (((0,), (0,)), ((), ())), preferred_element_type=F32)


def _split_dot_left(m, x, parts=2):
    acc = None
    rem = x
    for p in range(parts):
        hi = _bf(rem)
        d = _dot(m, hi)
        acc = d if acc is None else acc + d
        if p + 1 < parts:
            rem = rem - hi.astype(F32)
    return acc


def _rmsnorm(x, g):
    return x * lax.rsqrt(jnp.mean(x * x, axis=-1, keepdims=True) + RMS_EPS) * g


def _gelu(x):
    return 0.5 * x * (1.0 + lax.erf(x * (2.0 ** -0.5)))


def _head_avg_matrix(width):
    r = lax.broadcasted_iota(jnp.int32, (width, width), 0) // HEAD_DIM
    c = lax.broadcasted_iota(jnp.int32, (width, width), 1) // HEAD_DIM
    return jnp.where(r == c, 1.0 / HEAD_DIM, 0.0).astype(BF16)


def _head_layernorm(x, avg, w, b, eps):
    mu = _dot(_bf(x), avg)
    xc = x - mu
    var = _dot(_bf(xc * xc), avg)
    return xc * lax.rsqrt(var + eps) * w + b


def _resident(shape):
    return pl.BlockSpec(shape, lambda *_: (0,) * len(shape), pipeline_mode=pl.Buffered(1))


def _gmlp_chunk(p, avg, lnw, lnb, w2, bias, head0):
    width = p.shape[1] // 2
    u = _gelu(p[:, :width])
    vn = _head_layernorm(_gelu(p[:, width:]), avg, lnw, lnb, LN_EPS)
    mixed = []
    for pair, w in enumerate(w2):
        vp = vn[:, pair * LANES:(pair + 1) * LANES]
        v_st = _bf(jnp.concatenate([jnp.where(head0, vp, 0.0), jnp.where(head0, 0.0, vp)], axis=0))
        mixed.append(_dot(w, v_st))
    return u * (jnp.concatenate(mixed, axis=1) + bias)


def _mixer_in_kernel(x_ref, gain_ref, wgm_ref, wrw_ref, lnw_ref, lnb_ref, ws_ref, bias_ref,
                     mu_ref, wup_ref, w0_ref, aup_ref, a0_ref, gup_ref, kk_ref, ka_ref, rk_ref,
                     ygm_ref, r_ref, c_ref, lw_ref, k_ref, v_ref, a_ref, b_ref, g_ref, bonus_ref, carry_ref):
    width = r_ref.shape[2]
    rows = x_ref.shape[1]

    @pl.when(pl.program_id(1) == 0)
    def _():
        carry_ref[...] = jnp.zeros_like(carry_ref)

    avg = _head_avg_matrix(width)
    ones = avg * HEAD_DIM
    t = lax.broadcasted_iota(jnp.int32, (GM_CHUNK, GM_CHUNK), 0)
    s = lax.broadcasted_iota(jnp.int32, (GM_CHUNK, GM_CHUNK), 1)
    causal = s <= t
    head0 = lax.broadcasted_iota(jnp.int32, (GM_CHUNK, LANES), 1) < HEAD_DIM
    w2 = [_bf(jnp.concatenate([jnp.where(causal, ws_ref[2 * pair], 0.0),
                               jnp.where(causal, ws_ref[2 * pair + 1], 0.0)], axis=1))
          for pair in range(width // LANES)]
    t = lax.broadcasted_iota(jnp.int32, (ROW_TILE, ROW_TILE), 0)
    s = lax.broadcasted_iota(jnp.int32, (ROW_TILE, ROW_TILE), 1)
    tri = ((t // WKV_CHUNK == s // WKV_CHUNK) & (s <= t)).astype(BF16)
    first = lax.broadcasted_iota(jnp.int32, (ROW_TILE, wrw_ref.shape[1]), 0) == 0

    subs = [slice(i * ROW_TILE, (i + 1) * ROW_TILE) for i in range(rows // ROW_TILE)]
    proj = []
    for rs in subs:
        h = _bf(_rmsnorm(x_ref[0, rs, :], gain_ref[...]))
        proj.append((_dot(h, wgm_ref[...]), _dot(h, wrw_ref[...])))

    last_row = carry_ref[0:1, :]
    for rs, (p_gm, p) in zip(subs, proj):
        for ch in range(ROW_TILE // GM_CHUNK):
            ts = slice(ch * GM_CHUNK, (ch + 1) * GM_CHUNK)
            out_rows = slice(rs.start + ts.start, rs.start + ts.stop)
            ygm_ref[0, out_rows, :] = _bf(_gmlp_chunk(p_gm[ts], avg, lnw_ref[...], lnb_ref[...], w2, bias_ref[...],
                                                      head0))
        prev = jnp.where(first, last_row, pltpu.roll(p, 1, axis=0))
        last_row = p[ROW_TILE - 1:ROW_TILE, :]
        ps = p + (prev - p) * mu_ref[...]
        r = ps[:, :width]
        k = ps[:, width:2 * width]
        v = ps[:, 2 * width:3 * width]
        lora_in = ps[:, 3 * width:3 * width + LANES]
        gd = ps[:, 3 * width + LANES:]
        lw = -(jnp.exp(-0.5)) * jax.nn.sigmoid(w0_ref[...] + _dot(_bf(jnp.tanh(lora_in)), wup_ref[...]))
        a_lr = jax.nn.sigmoid(a0_ref[...] + _dot(_bf(lora_in), aup_ref[...]))
        kk = k * kk_ref[...]
        kk = kk * lax.rsqrt(jnp.maximum(_dot(_bf(kk * kk), ones), 1e-24))
        kmod = k * (1.0 + (a_lr - 1.0) * ka_ref[...])
        lw_ref[0, rs, :] = lw
        c_ref[0, rs, :] = _split_dot_left(tri, lw)
        r_ref[0, rs, :] = _bf(r)
        k_ref[0, rs, :] = _bf(kmod)
        v_ref[0, rs, :] = _bf(v)
        a_ref[0, rs, :] = _bf(-kk)
        b_ref[0, rs, :] = _bf(kk * a_lr)
        g_ref[0, rs, :] = _bf(_dot(_bf(jax.nn.sigmoid(gd)), gup_ref[...]))
        bonus_ref[0, rs, :] = _bf(_dot(_bf(r * kmod * rk_ref[...]), ones) * v)
    carry_ref[0:1, :] = last_row


def _mixer_in(x3, gain, w_gm, w_rw, ln_w, ln_b, ws, bs, mu, w_up, w0, a_up, a0, g_up, k_k, k_a, r_k):
    batch, t, d = x3.shape
    width = w0.shape[0]
    rank = w_up.shape[0]
    zeros = jnp.zeros((LANES - rank, width), F32)
    wup_pad = _bf(jnp.concatenate([w_up, zeros], axis=0))
    aup_pad = _bf(jnp.concatenate([zeros, a_up], axis=0))
    bias = jnp.repeat(bs.T, HEAD_DIM, axis=1)
    row = lambda z: z.reshape(1, -1)
    params = (row(gain), w_gm, w_rw, row(ln_w), row(ln_b), ws, bias, row(mu), wup_pad, row(w0), aup_pad, row(a0),
              _bf(g_up), row(k_k), row(k_a), row(r_k))
    tile = min(MIX_TILE, t)
    out = pl.BlockSpec((1, tile, width), lambda b, i: (b, i, 0))
    sds = lambda dtype: jax.ShapeDtypeStruct((batch, t, width), dtype)
    return pl.pallas_call(
        _mixer_in_kernel,
        grid=(batch, t // tile),
        in_specs=[pl.BlockSpec((1, tile, d), lambda b, i: (b, i, 0))] + [_resident(z.shape) for z in params],
        out_specs=[out] * 10,
        out_shape=[sds(BF16), sds(BF16), sds(F32), sds(F32)] + [sds(BF16)] * 6,
        scratch_shapes=[pltpu.VMEM((8, w_rw.shape[1]), F32)],
        compiler_params=_cparams("parallel", "arbitrary"),
        name="mixer_in",
    )(x3, *params)


def _wkv_kernel(r_ref, c_ref, lw_ref, k_ref, v_ref, a_ref, b_ref, y_ref, h_ref,
                q_s, v_s, kc_s, ark_s, bcrb_s, mz_s, rhs_s, sol_s, t_s, rh_s, rkv_s, kv_s, dec_s, hc_s, yc_s):
    @pl.when(pl.program_id(1) == 0)
    def _():
        h_ref[...] = jnp.zeros_like(h_ref)

    n = WKV_CHUNK
    two = 2 * n
    head0 = lax.broadcasted_iota(jnp.int32, (n, LANES), 1) < HEAD_DIM
    row = lax.broadcasted_iota(jnp.int32, (two, two), 0)
    col = lax.broadcasted_iota(jnp.int32, (two, two), 1)
    eye = row == col
    row2 = lax.broadcasted_iota(jnp.int32, (two, 2 * two), 0)
    col2 = lax.broadcasted_iota(jnp.int32, (two, 2 * two), 1) % two
    same2 = (row2 // n) == (col2 // n)
    strict2 = same2 & ((col2 % n) < (row2 % n))
    incl2 = same2 & ((col2 % n) <= (row2 % n))
    npairs = y_ref.shape[2] // LANES
    nchunks = y_ref.shape[1] // n
    items = [(ch, pair) for ch in range(nchunks) for pair in range(npairs)]

    def window(ch, pair):
        return slice(ch * n, (ch + 1) * n), slice(pair * LANES, (pair + 1) * LANES)

    def stack(z):
        return jnp.concatenate([jnp.where(head0, z, 0.0), jnp.where(head0, 0.0, z)], axis=0)

    def dup(z):
        zb = _bf(z)
        return jnp.concatenate([zb, zb], axis=0)

    def scores(it):
        ts, ls = window(*items[it])
        r, k, v, a, b = (ref[0, ts, ls].astype(F32) for ref in (r_ref, k_ref, v_ref, a_ref, b_ref))
        c = c_ref[0, ts, ls]
        cex = c - lw_ref[0, ts, ls]
        mid = c[n // 2 - 1:n // 2, :]
        last = c[n - 1:n, :]
        g_inv = jnp.exp(mid - c)
        g_end = jnp.exp(last - c)
        ar_st = _bf(jnp.concatenate([stack(a * jnp.exp(cex - mid)), stack(r * jnp.exp(c - mid))], axis=0))
        bk_dup = jnp.concatenate([dup(b * g_inv), dup(k * g_inv)], axis=0)
        sc = _dot_nt(ar_st, bk_dup)
        top = jnp.where(strict2, sc[:two], 0.0)
        bot = jnp.where(incl2, sc[two:], 0.0)
        t_s[it] = jnp.where(eye, 1.0, top[:, :two])
        q_s[it] = _bf(top[:, :two])
        bcrb_s[it] = _bf(jnp.concatenate([stack(b * g_end).T, bot[:, :two]], axis=0))
        ark_s[it] = _bf(jnp.concatenate([top[:, two:], bot[:, two:]], axis=0))
        v_s[it] = _bf(stack(v))
        kc_s[it] = _bf(stack(k * g_end))
        rh_s[it] = stack(r * jnp.exp(c))
        rhs_s[it, :, :LANES] = _bf(stack(a * jnp.exp(cex)))
        e_col = jnp.sum(jnp.where(eye, jnp.exp(last), 0.0), axis=1, keepdims=True)
        dec_s[it] = jnp.broadcast_to(e_col, (two, LANES))

    def values(it):
        v_st = v_s[it]
        both = _dot(ark_s[it], v_st)
        rhs_s[it, :, LANES:] = _bf(both[:two])
        rkv_s[it] = both[two:]
        kv_s[it] = _dot_tn(kc_s[it], v_st)

    levels = n.bit_length() - 1

    def inverse_level(level, it):
        q = q_s[it]
        t_acc = t_s[it]
        if level == 1:
            q = _bf(_dot(q, q))
        if level + 1 < levels:
            prod = _dot(q, jnp.concatenate([_bf(t_acc), q], axis=1))
            q_s[it] = _bf(prod[:, two:])
            t_s[it] = t_acc + prod[:, :two]
        else:
            t_s[it] = t_acc + _dot(q, _bf(t_acc))

    def solve(it):
        sol_s[it] = _bf(_dot(_bf(t_s[it]), rhs_s[it]))

    def fold(it):
        prod = _dot(bcrb_s[it], sol_s[it])
        mz_s[it] = _bf(jnp.concatenate([prod[:two, :LANES], rh_s[it] + prod[two:, :LANES]], axis=0))
        hc_s[it] = prod[:two, LANES:] + kv_s[it]
        yc_s[it] = prod[two:, LANES:] + rkv_s[it]

    def advance(it):
        ts, ls = window(*items[it])
        pair = items[it][1]
        h = h_ref[pair]
        gmat = _dot(mz_s[it], _bf(h))
        h_ref[pair] = dec_s[it] * h + gmat[:two] + hc_s[it]
        y_st = gmat[two:] + yc_s[it]
        y_ref[0, ts, ls] = y_st[:n] + y_st[n:]

    stages = ([scores, values] + [functools.partial(inverse_level, level) for level in range(1, levels)]
              + [solve, fold, advance])
    for stage in stages:
        for it in range(len(items)):
            stage(it)


def _wkv(r, c, lw, k, v, a, b):
    batch, t, width = r.shape
    npairs = width // LANES
    items = (WKV_STEP // WKV_CHUNK) * npairs
    spec = pl.BlockSpec((1, WKV_STEP, width), lambda bi, i: (bi, i, 0))
    sq = lambda dtype: pltpu.VMEM((items, LANES, LANES), dtype)
    return pl.pallas_call(
        _wkv_kernel,
        grid=(batch, t // WKV_STEP),
        in_specs=[spec] * 7,
        out_specs=spec,
        out_shape=jax.ShapeDtypeStruct((batch, t, width), F32),
        scratch_shapes=[pltpu.VMEM((npairs, LANES, LANES), F32)] + [sq(BF16)] * 3
                       + [pltpu.VMEM((items, 2 * LANES, LANES), BF16)] * 3
                       + [pltpu.VMEM((items, LANES, 2 * LANES), BF16)] * 2 + [sq(F32)] * 7,
        compiler_params=_cparams("parallel", "arbitrary"),
        name="wkv7",
    )(r, c, lw, k, v, a, b)


N_MIX_OUT_REFS = 9


def _mixer_out(x_ref, ygm_ref, ys_ref, bonus_ref, g_ref, lnw_ref, lnb_ref, wtop_ref, wbot_ref):
    width = ys_ref.shape[1]
    yn = _head_layernorm(ys_ref[...], _head_avg_matrix(width), lnw_ref[...], lnb_ref[...], GN_EPS)
    y_rw = (yn + bonus_ref[...]) * g_ref[...]
    return x_ref[...] + _dot(ygm_ref[...], wtop_ref[...]) + _dot(_bf(y_rw), wbot_ref[...])


def _mixer_out_operands(x2, y_gm, y_scan, bonus, g, lnx_w, lnx_b, w_out):
    d = x2.shape[1]
    width = y_gm.shape[1]
    rows = lambda w: pl.BlockSpec((ROW_TILE, w), lambda i: (i, 0))
    w_out = _bf(w_out)
    specs = [rows(d), rows(width), rows(width), rows(width), rows(width),
             _resident((1, width)), _resident((1, width)), _resident((width, d)), _resident((width, d))]
    args = (x2, y_gm, y_scan, bonus, g, lnx_w.reshape(1, width), lnx_b.reshape(1, width), w_out[:width], w_out[width:])
    assert len(specs) == len(args) == N_MIX_OUT_REFS
    return specs, args


def _out_swiglu_kernel(*refs):
    mix_refs, (gain_ref, wg_ref, wu_ref, wd_ref, o_ref) = refs[:N_MIX_OUT_REFS], refs[N_MIX_OUT_REFS:]
    x = _mixer_out(*mix_refs)
    h = _bf(_rmsnorm(x, gain_ref[...]))
    act = jax.nn.silu(_dot(h, wg_ref[...])) * _dot(h, wu_ref[...])
    o_ref[...] = x + _dot(_bf(act), wd_ref[...])


def _ffn_tile(d_ff, parts):
    assert d_ff % (parts * LANES) == 0, f"d_ff={d_ff} does not split into {parts} lane-aligned tiles"
    return d_ff // parts


def _out_swiglu(mix_args, gain, wg, wu, wd):
    specs, args = _mixer_out_operands(*mix_args)
    n, d = args[0].shape
    d_ff = wg.shape[1]
    return pl.pallas_call(
        _out_swiglu_kernel,
        grid=(n // ROW_TILE,),
        in_specs=specs + [_resident((1, d)), _resident((d, d_ff)), _resident((d, d_ff)), _resident((d_ff, d))],
        out_specs=pl.BlockSpec((ROW_TILE, d), lambda i: (i, 0)),
        out_shape=jax.ShapeDtypeStruct((n, d), F32),
        compiler_params=_cparams("parallel"),
        name="out_dense_swiglu",
    )(*args, gain.reshape(1, d), _bf(wg), _bf(wu), _bf(wd))


META_E, META_G, META_RANK = 0, 2, 4


def _out_router_kernel(*refs):
    mix_refs = refs[:N_MIX_OUT_REFS]
    g_ref, whi_ref, wlo_ref, x_ref, h_ref, meta_ref, cnt_ref, run_ref = refs[N_MIX_OUT_REFS:]

    @pl.when(pl.program_id(0) == 0)
    def _():
        run_ref[...] = jnp.zeros_like(run_ref)

    x = _mixer_out(*mix_refs)
    x_ref[...] = x
    h = _rmsnorm(x, g_ref[...])
    h_ref[...] = h
    rows = h.shape[0]
    lane = lax.broadcasted_iota(jnp.int32, (rows, LANES), 1)
    h_hi = _bf(h)
    h_lo = _bf(h - h_hi.astype(F32))
    logits = _dot(h_hi, whi_ref[...]) + _dot(h_lo, whi_ref[...]) + _dot(h_hi, wlo_ref[...])
    logits = jnp.where(lane < N_EXPERTS, logits, NEG_BIG)
    v1 = jnp.max(logits, axis=-1, keepdims=True)
    e1 = jnp.min(jnp.where(logits == v1, lane, LANES), axis=-1, keepdims=True)
    oh1 = lane == e1
    rest = jnp.where(oh1, NEG_BIG, logits)
    v2 = jnp.max(rest, axis=-1, keepdims=True)
    e2 = jnp.min(jnp.where(rest == v2, lane, LANES), axis=-1, keepdims=True)
    oh2 = lane == e2
    ex = jnp.exp(v2 - v1)
    g1 = 1.0 / (1.0 + ex)
    g2 = ex / (1.0 + ex)
    cnt = jnp.where(oh1 | oh2, 1.0, 0.0)
    t = lax.broadcasted_iota(jnp.int32, (rows, rows), 0)
    s = lax.broadcasted_iota(jnp.int32, (rows, rows), 1)
    before = _dot(jnp.where(s < t, 1.0, 0.0).astype(BF16), _bf(cnt)) + run_ref[0:1, :]
    rank1 = jnp.sum(jnp.where(oh1, before, 0.0), axis=-1, keepdims=True)
    rank2 = jnp.sum(jnp.where(oh2, before, 0.0), axis=-1, keepdims=True)
    fields = (e1.astype(F32), e2.astype(F32), g1, g2, rank1, rank2)
    meta = jnp.zeros((rows, LANES), F32)
    for idx, val in enumerate(fields):
        meta = jnp.where(lane == idx, val, meta)
    meta_ref[...] = meta
    run = run_ref[0:1, :] + jnp.sum(cnt, axis=0, keepdims=True)
    run_ref[...] = jnp.broadcast_to(run, run_ref.shape)
    cnt_ref[...] = jnp.broadcast_to(run, cnt_ref.shape)


def _out_router(mix_args, gain, w_router):
    specs, args = _mixer_out_operands(*mix_args)
    n, d = args[0].shape
    wr = jnp.concatenate([w_router, jnp.zeros((d, LANES - w_router.shape[1]), F32)], axis=1)
    wr_hi = _bf(wr)
    wr_lo = _bf(wr - wr_hi.astype(F32))
    rows = lambda w: pl.BlockSpec((ROW_TILE, w), lambda i: (i, 0))
    return pl.pallas_call(
        _out_router_kernel,
        grid=(n // ROW_TILE,),
        in_specs=specs + [_resident((1, d)), _resident((d, LANES)), _resident((d, LANES))],
        out_specs=[rows(d), rows(d), rows(LANES), pl.BlockSpec((8, LANES), lambda i: (0, 0))],
        out_shape=[jax.ShapeDtypeStruct((n, d), F32), jax.ShapeDtypeStruct((n, d), F32),
                   jax.ShapeDtypeStruct((n, LANES), F32), jax.ShapeDtypeStruct((8, LANES), F32)],
        scratch_shapes=[pltpu.VMEM((8, LANES), F32)],
        compiler_params=_cparams("arbitrary"),
        name="out_moe_router",
    )(*args, gain.reshape(1, d), wr_hi, wr_lo)


def _sc_workers():
    info = plsc.get_sparse_core_info()
    return info.num_cores, info.num_cores * info.num_subcores


def _dispatch_rows(h, dest0, dest1, pad_rows):
    nc, workers = _sc_workers()
    n, d = h.shape
    n_pad = pad_rows.shape[0]
    per_worker = n // workers
    nchunk = per_worker // SC_WINDOW
    npad = n_pad // (workers * SC_WINDOW)
    assert n == workers * nchunk * SC_WINDOW and nchunk % 2 == 0 and n_pad == workers * npad * SC_WINDOW
    mesh = plsc.VectorSubcoreMesh(core_axis_name="c", subcore_axis_name="s")
    idx_t = lambda m: pltpu.VMEM((m, SC_WINDOW), jnp.int32)
    buf_t = pltpu.VMEM((SC_WINDOW, d), h.dtype)

    @functools.partial(
        pl.kernel, mesh=mesh,
        out_type=jax.ShapeDtypeStruct((2 * n + n_pad, d), h.dtype),
        scratch_types=[idx_t(nchunk), idx_t(nchunk), idx_t(npad), buf_t, buf_t, buf_t]
                      + [pltpu.SemaphoreType.DMA] * 6,
    )
    def dispatch_kernel(h_hbm, d0_hbm, d1_hbm, pad_hbm, zero_hbm, out_hbm, d0_v, d1_v, pad_v, buf0, buf1, zbuf,
                        r0, r1, w0, w1, x0, x1):
        wid = lax.axis_index("s") * nc + lax.axis_index("c")
        base = wid * per_worker
        pltpu.sync_copy(d0_hbm.at[wid], d0_v)
        pltpu.sync_copy(d1_hbm.at[wid], d1_v)
        pltpu.sync_copy(pad_hbm.at[wid], pad_v)
        pltpu.sync_copy(zero_hbm, zbuf)
        bufs, rsem, wsem, xsem = (buf0, buf1), (r0, r1), (w0, w1), (x0, x1)

        def read(c, slot):
            return pltpu.make_async_copy(h_hbm.at[pl.ds(base + c * SC_WINDOW, SC_WINDOW)], bufs[slot], rsem[slot])

        read(0, 0).start()
        read(1, 1).start()
        for pc in range(npad):
            pltpu.sync_copy(zbuf, out_hbm.at[pad_v.at[pc]])

        @pl.loop(0, nchunk, step=2)
        def _(c):
            for slot in range(2):
                cc = c + slot
                read(cc, slot).wait()
                first = pltpu.make_async_copy(bufs[slot], out_hbm.at[d0_v.at[cc]], wsem[slot])
                second = pltpu.make_async_copy(bufs[slot], out_hbm.at[d1_v.at[cc]], xsem[slot])
                first.start()
                second.start()
                first.wait()
                second.wait()

                @pl.when(cc + 2 < nchunk)
                def _():
                    read(cc + 2, slot).start()

    split = lambda z, m: z.reshape(workers, m, SC_WINDOW)
    return dispatch_kernel(h, split(dest0, nchunk), split(dest1, nchunk), split(pad_rows, npad),
                           jnp.zeros((SC_WINDOW, d), h.dtype))


def _gather_rows(src, idx):
    info = plsc.get_sparse_core_info()
    nc, ns = info.num_cores, info.num_subcores
    workers = nc * ns
    m, d = idx.shape[0], src.shape[1]
    per_worker = m // workers
    nchunk = per_worker // SC_WINDOW
    assert m == workers * nchunk * SC_WINDOW and nchunk % 2 == 0, "row count must split evenly over subcores"
    mesh = plsc.VectorSubcoreMesh(core_axis_name="c", subcore_axis_name="s")

    @functools.partial(
        pl.kernel, mesh=mesh,
        out_type=jax.ShapeDtypeStruct((m, d), src.dtype),
        scratch_types=[pltpu.VMEM((nchunk, SC_WINDOW), jnp.int32),
                       pltpu.VMEM((SC_WINDOW, d), src.dtype), pltpu.VMEM((SC_WINDOW, d), src.dtype),
                       pltpu.SemaphoreType.DMA, pltpu.SemaphoreType.DMA,
                       pltpu.SemaphoreType.DMA, pltpu.SemaphoreType.DMA],
    )
    def gather_kernel(src_hbm, idx_hbm, out_hbm, idx_v, buf0, buf1, g0, g1, w0, w1):
        wid = lax.axis_index("s") * nc + lax.axis_index("c")
        base = wid * per_worker
        pltpu.sync_copy(idx_hbm.at[wid], idx_v)
        bufs, gsem, wsem = (buf0, buf1), (g0, g1), (w0, w1)

        def gather(c, slot):
            return pltpu.make_async_copy(src_hbm.at[idx_v.at[c]], bufs[slot], gsem[slot])

        def write(c, slot):
            return pltpu.make_async_copy(bufs[slot], out_hbm.at[pl.ds(base + c * SC_WINDOW, SC_WINDOW)], wsem[slot])

        gather(0, 0).start()
        gather(1, 1).start()

        @pl.loop(0, nchunk, step=2)
        def _(c):
            for slot in range(2):
                cc = c + slot
                gather(cc, slot).wait()
                write(cc, slot).start()
                write(cc, slot).wait()

                @pl.when(cc + 2 < nchunk)
                def _():
                    gather(cc + 2, slot).start()

    return gather_kernel(src, idx.reshape(workers, nchunk, SC_WINDOW))


def _expert_kernel(be_ref, nu_ref, x_ref, wg_ref, wu_ref, wd_ref, o_ref, h_ref):
    i, j = pl.program_id(0), pl.program_id(1)

    @pl.when(i < nu_ref[0])
    def _():
        @pl.when(j == 0)
        def _():
            h_ref[...] = _bf(x_ref[...])
            o_ref[...] = jnp.zeros_like(o_ref)

        h = h_ref[...]
        act = jax.nn.silu(_dot(h, wg_ref[0])) * _dot(h, wu_ref[0])
        o_ref[...] += _dot(_bf(act), wd_ref[0])

    @pl.when((i >= nu_ref[0]) & (j == 0))
    def _():
        o_ref[...] = jnp.zeros_like(o_ref)


def _expert_swiglu(xb, block_expert, n_used, wg, wu, wd):
    n_rows, d = xb.shape
    d_ff = wg.shape[2]
    tf = _ffn_tile(d_ff, 2)
    nf = d_ff // tf
    nblk = n_rows // MOE_BLOCK

    def row(i, j, be, nu):
        return jnp.minimum(i, nu[0] - 1)

    def col(i, j, be, nu):
        return jnp.where(i < nu[0], j, nf - 1)

    grid_spec = pltpu.PrefetchScalarGridSpec(
        num_scalar_prefetch=2,
        grid=(nblk, nf),
        in_specs=[pl.BlockSpec((MOE_BLOCK, d), lambda i, j, be, nu: (row(i, j, be, nu), 0)),
                  pl.BlockSpec((1, d, tf), lambda i, j, be, nu: (be[row(i, j, be, nu)], 0, col(i, j, be, nu))),
                  pl.BlockSpec((1, d, tf), lambda i, j, be, nu: (be[row(i, j, be, nu)], 0, col(i, j, be, nu))),
                  pl.BlockSpec((1, tf, d), lambda i, j, be, nu: (be[row(i, j, be, nu)], col(i, j, be, nu), 0))],
        out_specs=pl.BlockSpec((MOE_BLOCK, d), lambda i, j, be, nu: (i, 0)),
        scratch_shapes=[pltpu.VMEM((MOE_BLOCK, d), BF16)],
    )
    return pl.pallas_call(
        _expert_kernel,
        grid_spec=grid_spec,
        out_shape=jax.ShapeDtypeStruct((n_rows, d), F32),
        compiler_params=_cparams("arbitrary", "arbitrary"),
        name="expert_swiglu",
    )(block_expert, n_used, xb, _bf(wg), _bf(wu), _bf(wd))


def _combine_kernel(x_ref, y1_ref, y2_ref, meta_ref, g_ref, o_ref):
    meta = meta_ref[...]
    lane = lax.broadcasted_iota(jnp.int32, meta.shape, 1)
    g1 = jnp.sum(jnp.where(lane == META_G, meta, 0.0), axis=-1, keepdims=True)
    g2 = jnp.sum(jnp.where(lane == META_G + 1, meta, 0.0), axis=-1, keepdims=True)
    y = x_ref[...] + (g1 * y1_ref[...] + g2 * y2_ref[...])
    o_ref[...] = _rmsnorm(y, g_ref[...])


def _combine_final_norm(x2, yg, meta, gain):
    n, d = x2.shape
    nblk = n // ROW_TILE
    return pl.pallas_call(
        _combine_kernel,
        grid=(nblk,),
        in_specs=[pl.BlockSpec((ROW_TILE, d), lambda i: (i, 0)),
                  pl.BlockSpec((ROW_TILE, d), lambda i: (i, 0)),
                  pl.BlockSpec((ROW_TILE, d), lambda i: (i + nblk, 0)),
                  pl.BlockSpec((ROW_TILE, LANES), lambda i: (i, 0)),
                  pl.BlockSpec((1, d), lambda i: (0, 0))],
        out_specs=pl.BlockSpec((ROW_TILE, d), lambda i: (i, 0)),
        out_shape=jax.ShapeDtypeStruct((n, d), F32),
        compiler_params=_cparams("parallel"),
        name="moe_combine_norm",
    )(x2, yg, yg, meta, gain.reshape(1, d))


def _moe_layer(mix_args, gain, w_router, wg, wu, wd, final_gain):
    x2, h, meta, counts = _out_router(mix_args, gain, w_router)
    n, d = x2.shape
    expert = meta[:, META_E:META_E + 2].astype(jnp.int32)
    rank = meta[:, META_RANK:META_RANK + 2].astype(jnp.int32)
    count = counts[0, :N_EXPERTS].astype(jnp.int32)
    padded = (count + MOE_BLOCK - 1) // MOE_BLOCK * MOE_BLOCK
    pad_end = jnp.cumsum(padded)
    pad_start = pad_end - padded
    dest = pad_start[expert] + rank
    n_rows = (2 * n // MOE_BLOCK + N_EXPERTS) * MOE_BLOCK
    empties = padded - count
    e_end = jnp.cumsum(empties)
    i = jnp.arange(n_rows - 2 * n, dtype=jnp.int32)
    grp = jnp.minimum(jnp.searchsorted(e_end, i, side='right'), N_EXPERTS).astype(jnp.int32)
    first_empty = jnp.concatenate([pad_start + count, pad_end[-1:]])
    first_index = jnp.concatenate([e_end - empties, e_end[-1:]])
    pad_rows = (first_empty[grp] + i - first_index[grp]).astype(jnp.int32)
    block_start = jnp.arange(n_rows // MOE_BLOCK, dtype=jnp.int32) * MOE_BLOCK
    block_expert = jnp.minimum(jnp.searchsorted(pad_end, block_start, side='right'), N_EXPERTS - 1).astype(jnp.int32)
    n_used = (pad_end[-1:] // MOE_BLOCK).astype(jnp.int32)
    xb = _dispatch_rows(h, dest[:, 0], dest[:, 1], pad_rows)
    ys = _expert_swiglu(xb, block_expert, n_used, wg, wu, wd)
    yg = _gather_rows(ys, jnp.concatenate([dest[:, 0], dest[:, 1]]))
    return _combine_final_norm(x2, yg, meta, final_gain)


def _mixer_layer(x2, batch, norm_mix, w_in, w_out, shift_mu, gm_ln_w, gm_ln_b, gm_ws, gm_bs,
                 rw_w_up, rw_w0, rw_a_up, rw_a0, rw_g_up, rw_k_k, rw_k_a, rw_r_k, rw_lnx_w, rw_lnx_b):
    n, d = x2.shape
    gm2 = 2 * gm_ln_w.shape[0]
    w_in = _bf(w_in)
    y_gm, r, c, lw, k, v, a, b, g, bonus = _mixer_in(
        x2.reshape(batch, n // batch, d), norm_mix, w_in[:, :gm2], w_in[:, gm2:], gm_ln_w, gm_ln_b, gm_ws, gm_bs,
        shift_mu, rw_w_up, rw_w0, rw_a_up, rw_a0, rw_g_up, rw_k_k, rw_k_a, rw_r_k.reshape(-1))
    y_scan = _wkv(r, c, lw, k, v, a, b)
    flat = lambda z: z.reshape(n, -1)
    return x2, flat(y_gm), flat(y_scan), flat(bonus), flat(g), rw_lnx_w, rw_lnx_b, w_out


def kernel(x, norm_mix, w_in, w_out, shift_mu, gm_ln_w, gm_ln_b, gm_ws, gm_bs, rw_w_up, rw_w0, rw_a_up, rw_a0,
           rw_g_up, rw_k_k, rw_k_a, rw_r_k, rw_lnx_w, rw_lnx_b, norm_ffn, ffn_w_gate, ffn_w_up, ffn_w_down,
           moe_router, moe_w_gate, moe_w_up, moe_w_down, norm_final):
    batch, t, d = x.shape
    depth = norm_mix.shape[0]
    assert depth == 2 and t % ROW_TILE == 0, "two layers (dense then MoE), sequence a multiple of the row tile"
    x2 = x.reshape(batch * t, d)
    for i in range(depth):
        mixed = _mixer_layer(x2, batch, norm_mix[i], w_in[i], w_out[i], shift_mu[i], gm_ln_w[i], gm_ln_b[i],
                             gm_ws[i], gm_bs[i], rw_w_up[i], rw_w0[i], rw_a_up[i], rw_a0[i], rw_g_up[i],
                             rw_k_k[i], rw_k_a[i], rw_r_k[i], rw_lnx_w[i], rw_lnx_b[i])
        if i % 2 == 0:
            x2 = _out_swiglu(mixed, norm_ffn[i], ffn_w_gate[i // 2], ffn_w_up[i // 2], ffn_w_down[i // 2])
        else:
            x2 = _moe_layer(mixed, norm_ffn[i], moe_router[i // 2], moe_w_gate[i // 2], moe_w_up[i // 2],
                            moe_w_down[i // 2], norm_final)
    return x2.reshape(batch, t, d)
```

```python
import functools

import jax
import jax.numpy as jnp
from jax import lax
from jax.experimental import pallas as pl
from jax.experimental.pallas import tpu as pltpu
from jax.experimental.pallas import tpu_sc as plsc

F32 = jnp.float32
BF16 = jnp.bfloat16

HEAD_DIM = 64
LANES = 128
GM_CHUNK = 128
WKV_CHUNK = 64
N_EXPERTS = 8
RMS_EPS = 1e-6
LN_EPS = 1e-5
GN_EPS = 64e-5
NEG_BIG = -1e30

ROW_TILE = 512
MIX_TILE = 1024
WKV_STEP = 256
MOE_BLOCK = 512
SC_WINDOW = 32
VMEM_LIMIT = 56 * 1024 * 1024


def _cparams(*sem):
    return pltpu.CompilerParams(dimension_semantics=sem, vmem_limit_bytes=VMEM_LIMIT)


def _bf(x):
    return x.astype(BF16)


def _dot(a, b):
    return jnp.dot(a, b, preferred_element_type=F32)


def _dot_nt(a, b):
    return lax.dot_general(a, b, (((1,), (1,)), ((), ())), preferred_element_type=F32)


def _dot_tn(a, b):
    return lax.dot_general(a, b, (((0,), (0,)), ((), ())), preferred_element_type=F32)


def _split_dot_left(m, x, parts=2):
    acc = None
    rem = x
    for p in range(parts):
        hi = _bf(rem)
        d = _dot(m, hi)
        acc = d if acc is None else acc + d
        if p + 1 < parts:
            rem = rem - hi.astype(F32)
    return acc


def _rmsnorm(x, g):
    return x * lax.rsqrt(jnp.mean(x * x, axis=-1, keepdims=True) + RMS_EPS) * g


def _gelu(x):
    return 0.5 * x * (1.0 + lax.erf(x * (2.0 ** -0.5)))


def _head_avg_matrix(width):
    r = lax.broadcasted_iota(jnp.int32, (width, width), 0) // HEAD_DIM
    c = lax.broadcasted_iota(jnp.int32, (width, width), 1) // HEAD_DIM
    return jnp.where(r == c, 1.0 / HEAD_DIM, 0.0).astype(BF16)


def _head_layernorm(x, avg, w, b, eps):
    mu = _dot(_bf(x), avg)
    xc = x - mu
    var = _dot(_bf(xc * xc), avg)
    return xc * lax.rsqrt(var + eps) * w + b


def _resident(shape):
    return pl.BlockSpec(shape, lambda *_: (0,) * len(shape), pipeline_mode=pl.Buffered(1))


def _gmlp_chunk(p, avg, lnw, lnb, w2, bias, head0):
    width = p.shape[1] // 2
    u = _gelu(p[:, :width])
    vn = _head_layernorm(_gelu(p[:, width:]), avg, lnw, lnb, LN_EPS)
    mixed = []
    for pair, w in enumerate(w2):
        vp = vn[:, pair * LANES:(pair + 1) * LANES]
        v_st = _bf(jnp.concatenate([jnp.where(head0, vp, 0.0), jnp.where(head0, 0.0, vp)], axis=0))
        mixed.append(_dot(w, v_st))
    return u * (jnp.concatenate(mixed, axis=1) + bias)


def _mixer_in_kernel(x_ref, gain_ref, wgm_ref, wrw_ref, lnw_ref, lnb_ref, ws_ref, bias_ref,
                     mu_ref, wup_ref, w0_ref, aup_ref, a0_ref, gup_ref, kk_ref, ka_ref, rk_ref,
                     ygm_ref, r_ref, c_ref, lw_ref, k_ref, v_ref, a_ref, b_ref, g_ref, bonus_ref, carry_ref):
    width = r_ref.shape[2]
    rows = x_ref.shape[1]

    @pl.when(pl.program_id(1) == 0)
    def _():
        carry_ref[...] = jnp.zeros_like(carry_ref)

    avg = _head_avg_matrix(width)
    ones = avg * HEAD_DIM
    t = lax.broadcasted_iota(jnp.int32, (GM_CHUNK, GM_CHUNK), 0)
    s = lax.broadcasted_iota(jnp.int32, (GM_CHUNK, GM_CHUNK), 1)
    causal = s <= t
    head0 = lax.broadcasted_iota(jnp.int32, (GM_CHUNK, LANES), 1) < HEAD_DIM
    w2 = [_bf(jnp.concatenate([jnp.where(causal, ws_ref[2 * pair], 0.0),
                               jnp.where(causal, ws_ref[2 * pair + 1], 0.0)], axis=1))
          for pair in range(width // LANES)]
    t = lax.broadcasted_iota(jnp.int32, (ROW_TILE, ROW_TILE), 0)
    s = lax.broadcasted_iota(jnp.int32, (ROW_TILE, ROW_TILE), 1)
    tri = ((t // WKV_CHUNK == s // WKV_CHUNK) & (s <= t)).astype(BF16)
    first = lax.broadcasted_iota(jnp.int32, (ROW_TILE, wrw_ref.shape[1]), 0) == 0

    subs = [slice(i * ROW_TILE, (i + 1) * ROW_TILE) for i in range(rows // ROW_TILE)]
    proj = []
    for rs in subs:
        h = _bf(_rmsnorm(x_ref[0, rs, :], gain_ref[...]))
        proj.append((_dot(h, wgm_ref[...]), _dot(h, wrw_ref[...])))

    last_row = carry_ref[0:1, :]
    for rs, (p_gm, p) in zip(subs, proj):
        for ch in range(ROW_TILE // GM_CHUNK):
            ts = slice(ch * GM_CHUNK, (ch + 1) * GM_CHUNK)
            out_rows = slice(rs.start + ts.start, rs.start + ts.stop)
            ygm_ref[0, out_rows, :] = _bf(_gmlp_chunk(p_gm[ts], avg, lnw_ref[...], lnb_ref[...], w2, bias_ref[...],
                                                      head0))
        prev = jnp.where(first, last_row, pltpu.roll(p, 1, axis=0))
        last_row = p[ROW_TILE - 1:ROW_TILE, :]
        ps = p + (prev - p) * mu_ref[...]
        r = ps[:, :width]
        k = ps[:, width:2 * width]
        v = ps[:, 2 * width:3 * width]
        lora_in = ps[:, 3 * width:3 * width + LANES]
        gd = ps[:, 3 * width + LANES:]
        lw = -(jnp.exp(-0.5)) * jax.nn.sigmoid(w0_ref[...] + _dot(_bf(jnp.tanh(lora_in)), wup_ref[...]))
        a_lr = jax.nn.sigmoid(a0_ref[...] + _dot(_bf(lora_in), aup_ref[...]))
        kk = k * kk_ref[...]
        kk = kk * lax.rsqrt(jnp.maximum(_dot(_bf(kk * kk), ones), 1e-24))
        kmod = k * (1.0 + (a_lr - 1.0) * ka_ref[...])
        lw_ref[0, rs, :] = lw
        c_ref[0, rs, :] = _split_dot_left(tri, lw)
        r_ref[0, rs, :] = _bf(r)
        k_ref[0, rs, :] = _bf(kmod)
        v_ref[0, rs, :] = _bf(v)
        a_ref[0, rs, :] = _bf(-kk)
        b_ref[0, rs, :] = _bf(kk * a_lr)
        g_ref[0, rs, :] = _bf(_dot(_bf(jax.nn.sigmoid(gd)), gup_ref[...]))
        bonus_ref[0, rs, :] = _bf(_dot(_bf(r * kmod * rk_ref[...]), ones) * v)
    carry_ref[0:1, :] = last_row


def _mixer_in(x3, gain, w_gm, w_rw, ln_w, ln_b, ws, bs, mu, w_up, w0, a_up, a0, g_up, k_k, k_a, r_k):
    batch, t, d = x3.shape
    width = w0.shape[0]
    rank = w_up.shape[0]
    zeros = jnp.zeros((LANES - rank, width), F32)
    wup_pad = _bf(jnp.concatenate([w_up, zeros], axis=0))
    aup_pad = _bf(jnp.concatenate([zeros, a_up], axis=0))
    bias = jnp.repeat(bs.T, HEAD_DIM, axis=1)
    row = lambda z: z.reshape(1, -1)
    params = (row(gain), w_gm, w_rw, row(ln_w), row(ln_b), ws, bias, row(mu), wup_pad, row(w0), aup_pad, row(a0),
              _bf(g_up), row(k_k), row(k_a), row(r_k))
    tile = min(MIX_TILE, t)
    out = pl.BlockSpec((1, tile, width), lambda b, i: (b, i, 0))
    sds = lambda dtype: jax.ShapeDtypeStruct((batch, t, width), dtype)
    return pl.pallas_call(
        _mixer_in_kernel,
        grid=(batch, t // tile),
        in_specs=[pl.BlockSpec((1, tile, d), lambda b, i: (b, i, 0))] + [_resident(z.shape) for z in params],
        out_specs=[out] * 10,
        out_shape=[sds(BF16), sds(BF16), sds(F32), sds(F32)] + [sds(BF16)] * 6,
        scratch_shapes=[pltpu.VMEM((8, w_rw.shape[1]), F32)],
        compiler_params=_cparams("parallel", "arbitrary"),
        name="mixer_in",
    )(x3, *params)


def _wkv_kernel(r_ref, c_ref, lw_ref, k_ref, v_ref, a_ref, b_ref, y_ref, h_ref,
                q_s, v_s, kc_s, ark_s, bcrb_s, mz_s, rhs_s, sol_s, t_s, rh_s, rkv_s, kv_s, dec_s, hc_s, yc_s):
    @pl.when(pl.program_id(1) == 0)
    def _():
        h_ref[...] = jnp.zeros_like(h_ref)

    n = WKV_CHUNK
    two = 2 * n
    head0 = lax.broadcasted_iota(jnp.int32, (n, LANES), 1) < HEAD_DIM
    row = lax.broadcasted_iota(jnp.int32, (two, two), 0)
    col = lax.broadcasted_iota(jnp.int32, (two, two), 1)
    eye = row == col
    row2 = lax.broadcasted_iota(jnp.int32, (two, 2 * two), 0)
    col2 = lax.broadcasted_iota(jnp.int32, (two, 2 * two), 1) % two
    same2 = (row2 // n) == (col2 // n)
    strict2 = same2 & ((col2 % n) < (row2 % n))
    incl2 = same2 & ((col2 % n) <= (row2 % n))
    npairs = y_ref.shape[2] // LANES
    nchunks = y_ref.shape[1] // n
    items = [(ch, pair) for ch in range(nchunks) for pair in range(npairs)]

    def window(ch, pair):
        return slice(ch * n, (ch + 1) * n), slice(pair * LANES, (pair + 1) * LANES)

    def stack(z):
        return jnp.concatenate([jnp.where(head0, z, 0.0), jnp.where(head0, 0.0, z)], axis=0)

    def dup(z):
        zb = _bf(z)
        return jnp.concatenate([zb, zb], axis=0)

    def scores(it):
        ts, ls = window(*items[it])
        r, k, v, a, b = (ref[0, ts, ls].astype(F32) for ref in (r_ref, k_ref, v_ref, a_ref, b_ref))
        c = c_ref[0, ts, ls]
        cex = c - lw_ref[0, ts, ls]
        mid = c[n // 2 - 1:n // 2, :]
        last = c[n - 1:n, :]
        g_inv = jnp.exp(mid - c)
        g_end = jnp.exp(last - c)
        ar_st = _bf(jnp.concatenate([stack(a * jnp.exp(cex - mid)), stack(r * jnp.exp(c - mid))], axis=0))
        bk_dup = jnp.concatenate([dup(b * g_inv), dup(k * g_inv)], axis=0)
        sc = _dot_nt(ar_st, bk_dup)
        top = jnp.where(strict2, sc[:two], 0.0)
        bot = jnp.where(incl2, sc[two:], 0.0)
        t_s[it] = jnp.where(eye, 1.0, top[:, :two])
        q_s[it] = _bf(top[:, :two])
        bcrb_s[it] = _bf(jnp.concatenate([stack(b * g_end).T, bot[:, :two]], axis=0))
        ark_s[it] = _bf(jnp.concatenate([top[:, two:], bot[:, two:]], axis=0))
        v_s[it] = _bf(stack(v))
        kc_s[it] = _bf(stack(k * g_end))
        rh_s[it] = stack(r * jnp.exp(c))
        rhs_s[it, :, :LANES] = _bf(stack(a * jnp.exp(cex)))
        e_col = jnp.sum(jnp.where(eye, jnp.exp(last), 0.0), axis=1, keepdims=True)
        dec_s[it] = jnp.broadcast_to(e_col, (two, LANES))

    def values(it):
        v_st = v_s[it]
        both = _dot(ark_s[it], v_st)
        rhs_s[it, :, LANES:] = _bf(both[:two])
        rkv_s[it] = both[two:]
        kv_s[it] = _dot_tn(kc_s[it], v_st)

    levels = n.bit_length() - 1

    def inverse_level(level, it):
        q = q_s[it]
        t_acc = t_s[it]
        if level == 1:
            q = _bf(_dot(q, q))
        if level + 1 < levels:
            prod = _dot(q, jnp.concatenate([_bf(t_acc), q], axis=1))
            q_s[it] = _bf(prod[:, two:])
            t_s[it] = t_acc + prod[:, :two]
        else:
            t_s[it] = t_acc + _dot(q, _bf(t_acc))

    def solve(it):
        sol_s[it] = _bf(_dot(_bf(t_s[it]), rhs_s[it]))

    def fold(it):
        prod = _dot(bcrb_s[it], sol_s[it])
        mz_s[it] = _bf(jnp.concatenate([prod[:two, :LANES], rh_s[it] + prod[two:, :LANES]], axis=0))
        hc_s[it] = prod[:two, LANES:] + kv_s[it]
        yc_s[it] = prod[two:, LANES:] + rkv_s[it]

    def advance(it):
        ts, ls = window(*items[it])
        pair = items[it][1]
        h = h_ref[pair]
        gmat = _dot(mz_s[it], _bf(h))
        h_ref[pair] = dec_s[it] * h + gmat[:two] + hc_s[it]
        y_st = gmat[two:] + yc_s[it]
        y_ref[0, ts, ls] = y_st[:n] + y_st[n:]

    stages = ([scores, values] + [functools.partial(inverse_level, level) for level in range(1, levels)]
              + [solve, fold, advance])
    for stage in stages:
        for it in range(len(items)):
            stage(it)


def _wkv(r, c, lw, k, v, a, b):
    batch, t, width = r.shape
    npairs = width // LANES
    items = (WKV_STEP // WKV_CHUNK) * npairs
    spec = pl.BlockSpec((1, WKV_STEP, width), lambda bi, i: (bi, i, 0))
    sq = lambda dtype: pltpu.VMEM((items, LANES, LANES), dtype)
    return pl.pallas_call(
        _wkv_kernel,
        grid=(batch, t // WKV_STEP),
        in_specs=[spec] * 7,
        out_specs=spec,
        out_shape=jax.ShapeDtypeStruct((batch, t, width), F32),
        scratch_shapes=[pltpu.VMEM((npairs, LANES, LANES), F32)] + [sq(BF16)] * 3
                       + [pltpu.VMEM((items, 2 * LANES, LANES), BF16)] * 3
                       + [pltpu.VMEM((items, LANES, 2 * LANES), BF16)] * 2 + [sq(F32)] * 7,
        compiler_params=_cparams("parallel", "arbitrary"),
        name="wkv7",
    )(r, c, lw, k, v, a, b)


N_MIX_OUT_REFS = 9


def _mixer_out(x_ref, ygm_ref, ys_ref, bonus_ref, g_ref, lnw_ref, lnb_ref, wtop_ref, wbot_ref):
    width = ys_ref.shape[1]
    yn = _head_layernorm(ys_ref[...], _head_avg_matrix(width), lnw_ref[...], lnb_ref[...], GN_EPS)
    y_rw = (yn + bonus_ref[...]) * g_ref[...]
    return x_ref[...] + _dot(ygm_ref[...], wtop_ref[...]) + _dot(_bf(y_rw), wbot_ref[...])


def _mixer_out_operands(x2, y_gm, y_scan, bonus, g, lnx_w, lnx_b, w_out):
    d = x2.shape[1]
    width = y_gm.shape[1]
    rows = lambda w: pl.BlockSpec((ROW_TILE, w), lambda i: (i, 0))
    w_out = _bf(w_out)
    specs = [rows(d), rows(width), rows(width), rows(width), rows(width),
             _resident((1, width)), _resident((1, width)), _resident((width, d)), _resident((width, d))]
    args = (x2, y_gm, y_scan, bonus, g, lnx_w.reshape(1, width), lnx_b.reshape(1, width), w_out[:width], w_out[width:])
    assert len(specs) == len(args) == N_MIX_OUT_REFS
    return specs, args


def _out_swiglu_kernel(*refs):
    mix_refs, (gain_ref, wg_ref, wu_ref, wd_ref, o_ref) = refs[:N_MIX_OUT_REFS], refs[N_MIX_OUT_REFS:]
    x = _mixer_out(*mix_refs)
    h = _bf(_rmsnorm(x, gain_ref[...]))
    act = jax.nn.silu(_dot(h, wg_ref[...])) * _dot(h, wu_ref[...])
    o_ref[...] = x + _dot(_bf(act), wd_ref[...])


def _ffn_tile(d_ff, parts):
    assert d_ff % (parts * LANES) == 0, f"d_ff={d_ff} does not split into {parts} lane-aligned tiles"
    return d_ff // parts


def _out_swiglu(mix_args, gain, wg, wu, wd):
    specs, args = _mixer_out_operands(*mix_args)
    n, d = args[0].shape
    d_ff = wg.shape[1]
    return pl.pallas_call(
        _out_swiglu_kernel,
        grid=(n // ROW_TILE,),
        in_specs=specs + [_resident((1, d)), _resident((d, d_ff)), _resident((d, d_ff)), _resident((d_ff, d))],
        out_specs=pl.BlockSpec((ROW_TILE, d), lambda i: (i, 0)),
        out_shape=jax.ShapeDtypeStruct((n, d), F32),
        compiler_params=_cparams("parallel"),
        name="out_dense_swiglu",
    )(*args, gain.reshape(1, d), _bf(wg), _bf(wu), _bf(wd))


META_E, META_G, META_RANK = 0, 2, 4


def _out_router_kernel(*refs):
    mix_refs = refs[:N_MIX_OUT_REFS]
    g_ref, whi_ref, wlo_ref, x_ref, h_ref, meta_ref, cnt_ref, run_ref = refs[N_MIX_OUT_REFS:]

    @pl.when(pl.program_id(0) == 0)
    def _():
        run_ref[...] = jnp.zeros_like(run_ref)

    x = _mixer_out(*mix_refs)
    x_ref[...] = x
    h = _rmsnorm(x, g_ref[...])
    h_ref[...] = h
    rows = h.shape[0]
    lane = lax.broadcasted_iota(jnp.int32, (rows, LANES), 1)
    h_hi = _bf(h)
    h_lo = _bf(h - h_hi.astype(F32))
    logits = _dot(h_hi, whi_ref[...]) + _dot(h_lo, whi_ref[...]) + _dot(h_hi, wlo_ref[...])
    logits = jnp.where(lane < N_EXPERTS, logits, NEG_BIG)
    v1 = jnp.max(logits, axis=-1, keepdims=True)
    e1 = jnp.min(jnp.where(logits == v1, lane, LANES), axis=-1, keepdims=True)
    oh1 = lane == e1
    rest = jnp.where(oh1, NEG_BIG, logits)
    v2 = jnp.max(rest, axis=-1, keepdims=True)
    e2 = jnp.min(jnp.where(rest == v2, lane, LANES), axis=-1, keepdims=True)
    oh2 = lane == e2
    ex = jnp.exp(v2 - v1)
    g1 = 1.0 / (1.0 + ex)
    g2 = ex / (1.0 + ex)
    cnt = jnp.where(oh1 | oh2, 1.0, 0.0)
    t = lax.broadcasted_iota(jnp.int32, (rows, rows), 0)
    s = lax.broadcasted_iota(jnp.int32, (rows, rows), 1)
    before = _dot(jnp.where(s < t, 1.0, 0.0).astype(BF16), _bf(cnt)) + run_ref[0:1, :]
    rank1 = jnp.sum(jnp.where(oh1, before, 0.0), axis=-1, keepdims=True)
    rank2 = jnp.sum(jnp.where(oh2, before, 0.0), axis=-1, keepdims=True)
    fields = (e1.astype(F32), e2.astype(F32), g1, g2, rank1, rank2)
    meta = jnp.zeros((rows, LANES), F32)
    for idx, val in enumerate(fields):
        meta = jnp.where(lane == idx, val, meta)
    meta_ref[...] = meta
    run = run_ref[0:1, :] + jnp.sum(cnt, axis=0, keepdims=True)
    run_ref[...] = jnp.broadcast_to(run, run_ref.shape)
    cnt_ref[...] = jnp.broadcast_to(run, cnt_ref.shape)


def _out_router(mix_args, gain, w_router):
    specs, args = _mixer_out_operands(*mix_args)
    n, d = args[0].shape
    wr = jnp.concatenate([w_router, jnp.zeros((d, LANES - w_router.shape[1]), F32)], axis=1)
    wr_hi = _bf(wr)
    wr_lo = _bf(wr - wr_hi.astype(F32))
    rows = lambda w: pl.BlockSpec((ROW_TILE, w), lambda i: (i, 0))
    return pl.pallas_call(
        _out_router_kernel,
        grid=(n // ROW_TILE,),
        in_specs=specs + [_resident((1, d)), _resident((d, LANES)), _resident((d, LANES))],
        out_specs=[rows(d), rows(d), rows(LANES), pl.BlockSpec((8, LANES), lambda i: (0, 0))],
        out_shape=[jax.ShapeDtypeStruct((n, d), F32), jax.ShapeDtypeStruct((n, d), F32),
                   jax.ShapeDtypeStruct((n, LANES), F32), jax.ShapeDtypeStruct((8, LANES), F32)],
        scratch_shapes=[pltpu.VMEM((8, LANES), F32)],
        compiler_params=_cparams("arbitrary"),
        name="out_moe_router",
    )(*args, gain.reshape(1, d), wr_hi, wr_lo)


def _sc_workers():
    info = plsc.get_sparse_core_info()
    return info.num_cores, info.num_cores * info.num_subcores


def _dispatch_rows(h, dest0, dest1, pad_rows):
    nc, workers = _sc_workers()
    n, d = h.shape
    n_pad = pad_rows.shape[0]
    per_worker = n // workers
    nchunk = per_worker // SC_WINDOW
    npad = n_pad // (workers * SC_WINDOW)
    assert n == workers * nchunk * SC_WINDOW and nchunk % 2 == 0 and n_pad == workers * npad * SC_WINDOW
    mesh = plsc.VectorSubcoreMesh(core_axis_name="c", subcore_axis_name="s")
    idx_t = lambda m: pltpu.VMEM((m, SC_WINDOW), jnp.int32)
    buf_t = pltpu.VMEM((SC_WINDOW, d), h.dtype)

    @functools.partial(
        pl.kernel, mesh=mesh,
        out_type=jax.ShapeDtypeStruct((2 * n + n_pad, d), h.dtype),
        scratch_types=[idx_t(nchunk), idx_t(nchunk), idx_t(npad), buf_t, buf_t, buf_t]
                      + [pltpu.SemaphoreType.DMA] * 6,
    )
    def dispatch_kernel(h_hbm, d0_hbm, d1_hbm, pad_hbm, zero_hbm, out_hbm, d0_v, d1_v, pad_v, buf0, buf1, zbuf,
                        r0, r1, w0, w1, x0, x1):
        wid = lax.axis_index("s") * nc + lax.axis_index("c")
        base = wid * per_worker
        pltpu.sync_copy(d0_hbm.at[wid], d0_v)
        pltpu.sync_copy(d1_hbm.at[wid], d1_v)
        pltpu.sync_copy(pad_hbm.at[wid], pad_v)
        pltpu.sync_copy(zero_hbm, zbuf)
        bufs, rsem, wsem, xsem = (buf0, buf1), (r0, r1), (w0, w1), (x0, x1)

        def read(c, slot):
            return pltpu.make_async_copy(h_hbm.at[pl.ds(base + c * SC_WINDOW, SC_WINDOW)], bufs[slot], rsem[slot])

        read(0, 0).start()
        read(1, 1).start()
        for pc in range(npad):
            pltpu.sync_copy(zbuf, out_hbm.at[pad_v.at[pc]])

        @pl.loop(0, nchunk, step=2)
        def _(c):
            for slot in range(2):
                cc = c + slot
                read(cc, slot).wait()
                first = pltpu.make_async_copy(bufs[slot], out_hbm.at[d0_v.at[cc]], wsem[slot])
                second = pltpu.make_async_copy(bufs[slot], out_hbm.at[d1_v.at[cc]], xsem[slot])
                first.start()
                second.start()
                first.wait()
                second.wait()

                @pl.when(cc + 2 < nchunk)
                def _():
                    read(cc + 2, slot).start()

    split = lambda z, m: z.reshape(workers, m, SC_WINDOW)
    return dispatch_kernel(h, split(dest0, nchunk), split(dest1, nchunk), split(pad_rows, npad),
                           jnp.zeros((SC_WINDOW, d), h.dtype))


def _gather_rows(src, idx):
    info = plsc.get_sparse_core_info()
    nc, ns = info.num_cores, info.num_subcores
    workers = nc * ns
    m, d = idx.shape[0], src.shape[1]
    per_worker = m // workers
    nchunk = per_worker // SC_WINDOW
    assert m == workers * nchunk * SC_WINDOW and nchunk % 2 == 0, "row count must split evenly over subcores"
    mesh = plsc.VectorSubcoreMesh(core_axis_name="c", subcore_axis_name="s")

    @functools.partial(
        pl.kernel, mesh=mesh,
        out_type=jax.ShapeDtypeStruct((m, d), src.dtype),
        scratch_types=[pltpu.VMEM((nchunk, SC_WINDOW), jnp.int32),
                       pltpu.VMEM((SC_WINDOW, d), src.dtype), pltpu.VMEM((SC_WINDOW, d), src.dtype),
                       pltpu.SemaphoreType.DMA, pltpu.SemaphoreType.DMA,
                       pltpu.SemaphoreType.DMA, pltpu.SemaphoreType.DMA],
    )
    def gather_kernel(src_hbm, idx_hbm, out_hbm, idx_v, buf0, buf1, g0, g1, w0, w1):
        wid = lax.axis_index("s") * nc + lax.axis_index("c")
        base = wid * per_worker
        pltpu.sync_copy(idx_hbm.at[wid], idx_v)
        bufs, gsem, wsem = (buf0, buf1), (g0, g1), (w0, w1)

        def gather(c, slot):
            return pltpu.make_async_copy(src_hbm.at[idx_v.at[c]], bufs[slot], gsem[slot])

        def write(c, slot):
            return pltpu.make_async_copy(bufs[slot], out_hbm.at[pl.ds(base + c * SC_WINDOW, SC_WINDOW)], wsem[slot])

        gather(0, 0).start()
        gather(1, 1).start()

        @pl.loop(0, nchunk, step=2)
        def _(c):
            for slot in range(2):
                cc = c + slot
                gather(cc, slot).wait()
                write(cc, slot).start()
                write(cc, slot).wait()

                @pl.when(cc + 2 < nchunk)
                def _():
                    gather(cc + 2, slot).start()

    return gather_kernel(src, idx.reshape(workers, nchunk, SC_WINDOW))


def _pack_bf16_pairs(y):
    w = y.shape[1] // 2
    lo = pltpu.bitcast(y[:, :w].astype(BF16).astype(F32), jnp.uint32)
    hi = pltpu.bitcast(y[:, w:].astype(BF16).astype(F32), jnp.uint32)
    return (hi & jnp.uint32(0xFFFF0000)) | (lo >> 16)


def _unpack_bf16_pairs(words):
    lo = pltpu.bitcast(words << 16, F32)
    hi = pltpu.bitcast(words & jnp.uint32(0xFFFF0000), F32)
    return jnp.concatenate([lo, hi], axis=1)


def _expert_kernel(be_ref, nu_ref, x_ref, wg_ref, wu_ref, wd_ref, o_ref, h_ref, acc_ref):
    i, j = pl.program_id(0), pl.program_id(1)
    last = pl.num_programs(1) - 1

    @pl.when(i < nu_ref[0])
    def _():
        @pl.when(j == 0)
        def _():
            h_ref[...] = _bf(x_ref[...])

        h = h_ref[...]
        act = jax.nn.silu(_dot(h, wg_ref[0])) * _dot(h, wu_ref[0])
        part = _dot(_bf(act), wd_ref[0])

        @pl.when(j == 0)
        def _():
            acc_ref[...] = part

        @pl.when((j > 0) & (j < last))
        def _():
            acc_ref[...] += part

        @pl.when(j == last)
        def _():
            o_ref[...] = _pack_bf16_pairs(acc_ref[...] + part)

    @pl.when((i >= nu_ref[0]) & (j == 0))
    def _():
        o_ref[...] = jnp.zeros_like(o_ref)


def _expert_swiglu(xb, block_expert, n_used, wg, wu, wd):
    n_rows, d = xb.shape
    d_ff = wg.shape[2]
    tf = _ffn_tile(d_ff, 2)
    nf = d_ff // tf
    assert nf >= 2, "the packed store happens on a d_ff step after the first"
    nblk = n_rows // MOE_BLOCK

    def row(i, j, be, nu):
        return jnp.minimum(i, nu[0] - 1)

    def col(i, j, be, nu):
        return jnp.where(i < nu[0], j, nf - 1)

    grid_spec = pltpu.PrefetchScalarGridSpec(
        num_scalar_prefetch=2,
        grid=(nblk, nf),
        in_specs=[pl.BlockSpec((MOE_BLOCK, d), lambda i, j, be, nu: (row(i, j, be, nu), 0)),
                  pl.BlockSpec((1, d, tf), lambda i, j, be, nu: (be[row(i, j, be, nu)], 0, col(i, j, be, nu))),
                  pl.BlockSpec((1, d, tf), lambda i, j, be, nu: (be[row(i, j, be, nu)], 0, col(i, j, be, nu))),
                  pl.BlockSpec((1, tf, d), lambda i, j, be, nu: (be[row(i, j, be, nu)], col(i, j, be, nu), 0))],
        out_specs=pl.BlockSpec((MOE_BLOCK, d // 2), lambda i, j, be, nu: (i, 0)),
        scratch_shapes=[pltpu.VMEM((MOE_BLOCK, d), BF16), pltpu.VMEM((MOE_BLOCK, d), F32)],
    )
    return pl.pallas_call(
        _expert_kernel,
        grid_spec=grid_spec,
        out_shape=jax.ShapeDtypeStruct((n_rows, d // 2), jnp.uint32),
        compiler_params=_cparams("arbitrary", "arbitrary"),
        name="expert_swiglu",
    )(block_expert, n_used, xb, _bf(wg), _bf(wu), _bf(wd))


def _combine_kernel(x_ref, y1_ref, y2_ref, meta_ref, g_ref, o_ref):
    meta = meta_ref[...]
    lane = lax.broadcasted_iota(jnp.int32, meta.shape, 1)
    g1 = jnp.sum(jnp.where(lane == META_G, meta, 0.0), axis=-1, keepdims=True)
    g2 = jnp.sum(jnp.where(lane == META_G + 1, meta, 0.0), axis=-1, keepdims=True)
    y = x_ref[...] + (g1 * _unpack_bf16_pairs(y1_ref[...]) + g2 * _unpack_bf16_pairs(y2_ref[...]))
    o_ref[...] = _rmsnorm(y, g_ref[...])


def _combine_final_norm(x2, yg, meta, gain):
    n, d = x2.shape
    nblk = n // ROW_TILE
    return pl.pallas_call(
        _combine_kernel,
        grid=(nblk,),
        in_specs=[pl.BlockSpec((ROW_TILE, d), lambda i: (i, 0)),
                  pl.BlockSpec((ROW_TILE, d // 2), lambda i: (i, 0)),
                  pl.BlockSpec((ROW_TILE, d // 2), lambda i: (i + nblk, 0)),
                  pl.BlockSpec((ROW_TILE, LANES), lambda i: (i, 0)),
                  pl.BlockSpec((1, d), lambda i: (0, 0))],
        out_specs=pl.BlockSpec((ROW_TILE, d), lambda i: (i, 0)),
        out_shape=jax.ShapeDtypeStruct((n, d), F32),
        compiler_params=_cparams("parallel"),
        name="moe_combine_norm",
    )(x2, yg, yg, meta, gain.reshape(1, d))


def _moe_layer(mix_args, gain, w_router, wg, wu, wd, final_gain):
    x2, h, meta, counts = _out_router(mix_args, gain, w_router)
    n, d = x2.shape
    expert = meta[:, META_E:META_E + 2].astype(jnp.int32)
    rank = meta[:, META_RANK:META_RANK + 2].astype(jnp.int32)
    count = counts[0, :N_EXPERTS].astype(jnp.int32)
    padded = (count + MOE_BLOCK - 1) // MOE_BLOCK * MOE_BLOCK
    pad_end = jnp.cumsum(padded)
    pad_start = pad_end - padded
    dest = pad_start[expert] + rank
    n_rows = (2 * n // MOE_BLOCK + N_EXPERTS) * MOE_BLOCK
    empties = padded - count
    e_end = jnp.cumsum(empties)
    i = jnp.arange(n_rows - 2 * n, dtype=jnp.int32)
    grp = jnp.sum(i[:, None] >= e_end[None, :], axis=1).astype(jnp.int32)
    first_empty = jnp.concatenate([pad_start + count, pad_end[-1:]])
    first_index = jnp.concatenate([e_end - empties, e_end[-1:]])
    pad_rows = (first_empty[grp] + i - first_index[grp]).astype(jnp.int32)
    block_start = jnp.arange(n_rows // MOE_BLOCK, dtype=jnp.int32) * MOE_BLOCK
    block_expert = jnp.minimum(jnp.sum(block_start[:, None] >= pad_end[None, :], axis=1), N_EXPERTS - 1)
    block_expert = block_expert.astype(jnp.int32)
    n_used = (pad_end[-1:] // MOE_BLOCK).astype(jnp.int32)
    xb = _dispatch_rows(h, dest[:, 0], dest[:, 1], pad_rows)
    ys = _expert_swiglu(xb, block_expert, n_used, wg, wu, wd)
    yg = _gather_rows(ys, jnp.concatenate([dest[:, 0], dest[:, 1]]))
    return _combine_final_norm(x2, yg, meta, final_gain)


def _mixer_layer(x2, batch, norm_mix, w_in, w_out, shift_mu, gm_ln_w, gm_ln_b, gm_ws, gm_bs,
                 rw_w_up, rw_w0, rw_a_up, rw_a0, rw_g_up, rw_k_k, rw_k_a, rw_r_k, rw_lnx_w, rw_lnx_b):
    n, d = x2.shape
    gm2 = 2 * gm_ln_w.shape[0]
    w_in = _bf(w_in)
    y_gm, r, c, lw, k, v, a, b, g, bonus = _mixer_in(
        x2.reshape(batch, n // batch, d), norm_mix, w_in[:, :gm2], w_in[:, gm2:], gm_ln_w, gm_ln_b, gm_ws, gm_bs,
        shift_mu, rw_w_up, rw_w0, rw_a_up, rw_a0, rw_g_up, rw_k_k, rw_k_a, rw_r_k.reshape(-1))
    y_scan = _wkv(r, c, lw, k, v, a, b)
    flat = lambda z: z.reshape(n, -1)
    return x2, flat(y_gm), flat(y_scan), flat(bonus), flat(g), rw_lnx_w, rw_lnx_b, w_out


def kernel(x, norm_mix, w_in, w_out, shift_mu, gm_ln_w, gm_ln_b, gm_ws, gm_bs, rw_w_up, rw_w0, rw_a_up, rw_a0,
           rw_g_up, rw_k_k, rw_k_a, rw_r_k, rw_lnx_w, rw_lnx_b, norm_ffn, ffn_w_gate, ffn_w_up, ffn_w_down,
           moe_router, moe_w_gate, moe_w_up, moe_w_down, norm_final):
    batch, t, d = x.shape
    depth = norm_mix.shape[0]
    assert depth == 2 and t % ROW_TILE == 0, "two layers (dense then MoE), sequence a multiple of the row tile"
    x2 = x.reshape(batch * t, d)
    for i in range(depth):
        mixed = _mixer_layer(x2, batch, norm_mix[i], w_in[i], w_out[i], shift_mu[i], gm_ln_w[i], gm_ln_b[i],
                             gm_ws[i], gm_bs[i], rw_w_up[i], rw_w0[i], rw_a_up[i], rw_a0[i], rw_g_up[i],
                             rw_k_k[i], rw_k_a[i], rw_r_k[i], rw_lnx_w[i], rw_lnx_b[i])
        if i % 2 == 0:
            x2 = _out_swiglu(mixed, norm_ffn[i], ffn_w_gate[i // 2], ffn_w_up[i // 2], ffn_w_down[i // 2])
        else:
            x2 = _moe_layer(mixed, norm_ffn[i], moe_router[i // 2], moe_w_gate[i // 2], moe_w_up[i // 2],
                            moe_w_down[i // 2], norm_final)
    return x2.reshape(batch, t, d)
```

```python
import functools

import jax
import jax.numpy as jnp
from jax import lax
from jax.experimental import pallas as pl
from jax.experimental.pallas import tpu as pltpu
from jax.experimental.pallas import tpu_sc as plsc

F32 = jnp.float32
BF16 = jnp.bfloat16

HEAD_DIM = 64
LANES = 128
GM_CHUNK = 128
WKV_CHUNK = 64
N_EXPERTS = 8
RMS_EPS = 1e-6
LN_EPS = 1e-5
GN_EPS = 64e-5
NEG_BIG = -1e30

ROW_TILE = 512
MIX_TILE = 1024
WKV_STEP = 256
MOE_BLOCK = 512
SC_WINDOW = 32
VMEM_LIMIT = 56 * 1024 * 1024


def _cparams(*sem):
    return pltpu.CompilerParams(dimension_semantics=sem, vmem_limit_bytes=VMEM_LIMIT)


def _bf(x):
    return x.astype(BF16)


def _dot(a, b):
    return jnp.dot(a, b, preferred_element_type=F32)


def _dot_nt(a, b):
    return lax.dot_general(a, b, (((1,), (1,)), ((), ())), preferred_element_type=F32)


def _dot_tn(a, b):
    return lax.dot_general(a, b, (((0,), (0,)), ((), ())), preferred_element_type=F32)


def _split_dot_left(m, x, parts=2):
    acc = None
    rem = x
    for p in range(parts):
        hi = _bf(rem)
        d = _dot(m, hi)
        acc = d if acc is None else acc + d
        if p + 1 < parts:
            rem = rem - hi.astype(F32)
    return acc


def _rmsnorm(x, g):
    return x * lax.rsqrt(jnp.mean(x * x, axis=-1, keepdims=True) + RMS_EPS) * g


def _gelu(x):
    return 0.5 * x * (1.0 + lax.erf(x * (2.0 ** -0.5)))


def _head_avg_matrix(width):
    r = lax.broadcasted_iota(jnp.int32, (width, width), 0) // HEAD_DIM
    c = lax.broadcasted_iota(jnp.int32, (width, width), 1) // HEAD_DIM
    return jnp.where(r == c, 1.0 / HEAD_DIM, 0.0).astype(BF16)


def _head_layernorm(x, avg, w, b, eps):
    mu = _dot(_bf(x), avg)
    xc = x - mu
    var = _dot(_bf(xc * xc), avg)
    return xc * lax.rsqrt(var + eps) * w + b


def _resident(shape):
    return pl.BlockSpec(shape, lambda *_: (0,) * len(shape), pipeline_mode=pl.Buffered(1))


def _gmlp_chunk(p, avg, lnw, lnb, w2, bias, head0):
    width = p.shape[1] // 2
    u = _gelu(p[:, :width])
    vn = _head_layernorm(_gelu(p[:, width:]), avg, lnw, lnb, LN_EPS)
    mixed = []
    for pair, w in enumerate(w2):
        vp = vn[:, pair * LANES:(pair + 1) * LANES]
        v_st = _bf(jnp.concatenate([jnp.where(head0, vp, 0.0), jnp.where(head0, 0.0, vp)], axis=0))
        mixed.append(_dot(w, v_st))
    return u * (jnp.concatenate(mixed, axis=1) + bias)


def _mixer_in_kernel(x_ref, gain_ref, wgm_ref, wrw_ref, lnw_ref, lnb_ref, ws_ref, bias_ref,
                     mu_ref, wup_ref, w0_ref, aup_ref, a0_ref, gup_ref, kk_ref, ka_ref, rk_ref,
                     ygm_ref, r_ref, c_ref, lw_ref, k_ref, v_ref, a_ref, b_ref, g_ref, bonus_ref, carry_ref):
    width = r_ref.shape[2]
    rows = x_ref.shape[1]

    @pl.when(pl.program_id(1) == 0)
    def _():
        carry_ref[...] = jnp.zeros_like(carry_ref)

    avg = _head_avg_matrix(width)
    ones = avg * HEAD_DIM
    t = lax.broadcasted_iota(jnp.int32, (GM_CHUNK, GM_CHUNK), 0)
    s = lax.broadcasted_iota(jnp.int32, (GM_CHUNK, GM_CHUNK), 1)
    causal = s <= t
    head0 = lax.broadcasted_iota(jnp.int32, (GM_CHUNK, LANES), 1) < HEAD_DIM
    w2 = [_bf(jnp.concatenate([jnp.where(causal, ws_ref[2 * pair], 0.0),
                               jnp.where(causal, ws_ref[2 * pair + 1], 0.0)], axis=1))
          for pair in range(width // LANES)]
    t = lax.broadcasted_iota(jnp.int32, (ROW_TILE, ROW_TILE), 0)
    s = lax.broadcasted_iota(jnp.int32, (ROW_TILE, ROW_TILE), 1)
    tri = ((t // WKV_CHUNK == s // WKV_CHUNK) & (s <= t)).astype(BF16)
    first = lax.broadcasted_iota(jnp.int32, (ROW_TILE, wrw_ref.shape[1]), 0) == 0

    subs = [slice(i * ROW_TILE, (i + 1) * ROW_TILE) for i in range(rows // ROW_TILE)]
    proj = []
    for rs in subs:
        h = _bf(_rmsnorm(x_ref[0, rs, :], gain_ref[...]))
        proj.append((_dot(h, wgm_ref[...]), _dot(h, wrw_ref[...])))

    last_row = carry_ref[0:1, :]
    for rs, (p_gm, p) in zip(subs, proj):
        for ch in range(ROW_TILE // GM_CHUNK):
            ts = slice(ch * GM_CHUNK, (ch + 1) * GM_CHUNK)
            out_rows = slice(rs.start + ts.start, rs.start + ts.stop)
            ygm_ref[0, out_rows, :] = _bf(_gmlp_chunk(p_gm[ts], avg, lnw_ref[...], lnb_ref[...], w2, bias_ref[...],
                                                      head0))
        prev = jnp.where(first, last_row, pltpu.roll(p, 1, axis=0))
        last_row = p[ROW_TILE - 1:ROW_TILE, :]
        ps = p + (prev - p) * mu_ref[...]
        r = ps[:, :width]
        k = ps[:, width:2 * width]
        v = ps[:, 2 * width:3 * width]
        lora_in = ps[:, 3 * width:3 * width + LANES]
        gd = ps[:, 3 * width + LANES:]
        lw = -(jnp.exp(-0.5)) * jax.nn.sigmoid(w0_ref[...] + _dot(_bf(jnp.tanh(lora_in)), wup_ref[...]))
        a_lr = jax.nn.sigmoid(a0_ref[...] + _dot(_bf(lora_in), aup_ref[...]))
        kk = k * kk_ref[...]
        kk = kk * lax.rsqrt(jnp.maximum(_dot(_bf(kk * kk), ones), 1e-24))
        kmod = k * (1.0 + (a_lr - 1.0) * ka_ref[...])
        lw_ref[0, rs, :] = lw
        c_ref[0, rs, :] = _split_dot_left(tri, lw)
        r_ref[0, rs, :] = _bf(r)
        k_ref[0, rs, :] = _bf(kmod)
        v_ref[0, rs, :] = _bf(v)
        a_ref[0, rs, :] = _bf(-kk)
        b_ref[0, rs, :] = _bf(kk * a_lr)
        g_ref[0, rs, :] = _bf(_dot(_bf(jax.nn.sigmoid(gd)), gup_ref[...]))
        bonus_ref[0, rs, :] = _bf(_dot(_bf(r * kmod * rk_ref[...]), ones) * v)
    carry_ref[0:1, :] = last_row


def _mixer_in(x3, gain, w_gm, w_rw, ln_w, ln_b, ws, bs, mu, w_up, w0, a_up, a0, g_up, k_k, k_a, r_k):
    batch, t, d = x3.shape
    width = w0.shape[0]
    rank = w_up.shape[0]
    zeros = jnp.zeros((LANES - rank, width), F32)
    wup_pad = _bf(jnp.concatenate([w_up, zeros], axis=0))
    aup_pad = _bf(jnp.concatenate([zeros, a_up], axis=0))
    bias = jnp.repeat(bs.T, HEAD_DIM, axis=1)
    row = lambda z: z.reshape(1, -1)
    params = (row(gain), w_gm, w_rw, row(ln_w), row(ln_b), ws, bias, row(mu), wup_pad, row(w0), aup_pad, row(a0),
              _bf(g_up), row(k_k), row(k_a), row(r_k))
    tile = min(MIX_TILE, t)
    out = pl.BlockSpec((1, tile, width), lambda b, i: (b, i, 0))
    sds = lambda dtype: jax.ShapeDtypeStruct((batch, t, width), dtype)
    return pl.pallas_call(
        _mixer_in_kernel,
        grid=(batch, t // tile),
        in_specs=[pl.BlockSpec((1, tile, d), lambda b, i: (b, i, 0))] + [_resident(z.shape) for z in params],
        out_specs=[out] * 10,
        out_shape=[sds(BF16), sds(BF16), sds(F32), sds(F32)] + [sds(BF16)] * 6,
        scratch_shapes=[pltpu.VMEM((8, w_rw.shape[1]), F32)],
        compiler_params=_cparams("parallel", "arbitrary"),
        name="mixer_in",
    )(x3, *params)


def _wkv_kernel(r_ref, c_ref, lw_ref, k_ref, v_ref, a_ref, b_ref, y_ref, h_ref,
                q_s, v_s, kc_s, ark_s, bcrb_s, mz_s, rhs_s, sol_s, t_s, rh_s, rkv_s, kv_s, dec_s, hc_s, yc_s):
    @pl.when(pl.program_id(1) == 0)
    def _():
        h_ref[...] = jnp.zeros_like(h_ref)

    n = WKV_CHUNK
    two = 2 * n
    head0 = lax.broadcasted_iota(jnp.int32, (n, LANES), 1) < HEAD_DIM
    row = lax.broadcasted_iota(jnp.int32, (two, two), 0)
    col = lax.broadcasted_iota(jnp.int32, (two, two), 1)
    eye = row == col
    row2 = lax.broadcasted_iota(jnp.int32, (two, 2 * two), 0)
    col2 = lax.broadcasted_iota(jnp.int32, (two, 2 * two), 1) % two
    same2 = (row2 // n) == (col2 // n)
    strict2 = same2 & ((col2 % n) < (row2 % n))
    incl2 = same2 & ((col2 % n) <= (row2 % n))
    npairs = y_ref.shape[2] // LANES
    nchunks = y_ref.shape[1] // n
    items = [(ch, pair) for ch in range(nchunks) for pair in range(npairs)]

    def window(ch, pair):
        return slice(ch * n, (ch + 1) * n), slice(pair * LANES, (pair + 1) * LANES)

    def stack(z):
        return jnp.concatenate([jnp.where(head0, z, 0.0), jnp.where(head0, 0.0, z)], axis=0)

    def dup(z):
        zb = _bf(z)
        return jnp.concatenate([zb, zb], axis=0)

    def scores(it):
        ts, ls = window(*items[it])
        r, k, v, a, b = (ref[0, ts, ls].astype(F32) for ref in (r_ref, k_ref, v_ref, a_ref, b_ref))
        c = c_ref[0, ts, ls]
        cex = c - lw_ref[0, ts, ls]
        mid = c[n // 2 - 1:n // 2, :]
        last = c[n - 1:n, :]
        g_inv = jnp.exp(mid - c)
        g_end = jnp.exp(last - c)
        ar_st = _bf(jnp.concatenate([stack(a * jnp.exp(cex - mid)), stack(r * jnp.exp(c - mid))], axis=0))
        bk_dup = jnp.concatenate([dup(b * g_inv), dup(k * g_inv)], axis=0)
        sc = _dot_nt(ar_st, bk_dup)
        top = jnp.where(strict2, sc[:two], 0.0)
        bot = jnp.where(incl2, sc[two:], 0.0)
        t_s[it] = jnp.where(eye, 1.0, top[:, :two])
        q_s[it] = _bf(top[:, :two])
        bcrb_s[it] = _bf(jnp.concatenate([stack(b * g_end).T, bot[:, :two]], axis=0))
        ark_s[it] = _bf(jnp.concatenate([top[:, two:], bot[:, two:]], axis=0))
        v_s[it] = _bf(stack(v))
        kc_s[it] = _bf(stack(k * g_end))
        rh_s[it] = stack(r * jnp.exp(c))
        rhs_s[it, :, :LANES] = _bf(stack(a * jnp.exp(cex)))
        e_col = jnp.sum(jnp.where(eye, jnp.exp(last), 0.0), axis=1, keepdims=True)
        dec_s[it] = jnp.broadcast_to(e_col, (two, LANES))

    def values(it):
        v_st = v_s[it]
        both = _dot(ark_s[it], v_st)
        rhs_s[it, :, LANES:] = _bf(both[:two])
        rkv_s[it] = both[two:]
        kv_s[it] = _dot_tn(kc_s[it], v_st)

    levels = n.bit_length() - 1

    def inverse_level(level, it):
        q = q_s[it]
        t_acc = t_s[it]
        if level == 1:
            q = _bf(_dot(q, q))
        if level + 1 < levels:
            prod = _dot(q, jnp.concatenate([_bf(t_acc), q], axis=1))
            q_s[it] = _bf(prod[:, two:])
            t_s[it] = t_acc + prod[:, :two]
        else:
            t_s[it] = t_acc + _dot(q, _bf(t_acc))

    def solve(it):
        sol_s[it] = _bf(_dot(_bf(t_s[it]), rhs_s[it]))

    def fold(it):
        prod = _dot(bcrb_s[it], sol_s[it])
        mz_s[it] = _bf(jnp.concatenate([prod[:two, :LANES], rh_s[it] + prod[two:, :LANES]], axis=0))
        hc_s[it] = prod[:two, LANES:] + kv_s[it]
        yc_s[it] = prod[two:, LANES:] + rkv_s[it]

    def advance(it):
        ts, ls = window(*items[it])
        pair = items[it][1]
        h = h_ref[pair]
        gmat = _dot(mz_s[it], _bf(h))
        h_ref[pair] = dec_s[it] * h + gmat[:two] + hc_s[it]
        y_st = gmat[two:] + yc_s[it]
        y_ref[0, ts, ls] = y_st[:n] + y_st[n:]

    stages = ([scores, values] + [functools.partial(inverse_level, level) for level in range(1, levels)]
              + [solve, fold, advance])
    for stage in stages:
        for it in range(len(items)):
            stage(it)


def _wkv(r, c, lw, k, v, a, b):
    batch, t, width = r.shape
    npairs = width // LANES
    items = (WKV_STEP // WKV_CHUNK) * npairs
    spec = pl.BlockSpec((1, WKV_STEP, width), lambda bi, i: (bi, i, 0))
    sq = lambda dtype: pltpu.VMEM((items, LANES, LANES), dtype)
    return pl.pallas_call(
        _wkv_kernel,
        grid=(batch, t // WKV_STEP),
        in_specs=[spec] * 7,
        out_specs=spec,
        out_shape=jax.ShapeDtypeStruct((batch, t, width), F32),
        scratch_shapes=[pltpu.VMEM((npairs, LANES, LANES), F32)] + [sq(BF16)] * 3
                       + [pltpu.VMEM((items, 2 * LANES, LANES), BF16)] * 3
                       + [pltpu.VMEM((items, LANES, 2 * LANES), BF16)] * 2 + [sq(F32)] * 7,
        compiler_params=_cparams("parallel", "arbitrary"),
        name="wkv7",
    )(r, c, lw, k, v, a, b)


N_MIX_OUT_REFS = 9


def _mixer_out(x_ref, ygm_ref, ys_ref, bonus_ref, g_ref, lnw_ref, lnb_ref, wtop_ref, wbot_ref):
    width = ys_ref.shape[1]
    yn = _head_layernorm(ys_ref[...], _head_avg_matrix(width), lnw_ref[...], lnb_ref[...], GN_EPS)
    y_rw = (yn + bonus_ref[...]) * g_ref[...]
    return x_ref[...] + _dot(ygm_ref[...], wtop_ref[...]) + _dot(_bf(y_rw), wbot_ref[...])


def _mixer_out_operands(x2, y_gm, y_scan, bonus, g, lnx_w, lnx_b, w_out):
    d = x2.shape[1]
    width = y_gm.shape[1]
    rows = lambda w: pl.BlockSpec((ROW_TILE, w), lambda i: (i, 0))
    w_out = _bf(w_out)
    specs = [rows(d), rows(width), rows(width), rows(width), rows(width),
             _resident((1, width)), _resident((1, width)), _resident((width, d)), _resident((width, d))]
    args = (x2, y_gm, y_scan, bonus, g, lnx_w.reshape(1, width), lnx_b.reshape(1, width), w_out[:width], w_out[width:])
    assert len(specs) == len(args) == N_MIX_OUT_REFS
    return specs, args


def _out_swiglu_kernel(*refs):
    mix_refs, (gain_ref, wg_ref, wu_ref, wd_ref, o_ref) = refs[:N_MIX_OUT_REFS], refs[N_MIX_OUT_REFS:]
    x = _mixer_out(*mix_refs)
    h = _bf(_rmsnorm(x, gain_ref[...]))
    act = jax.nn.silu(_dot(h, wg_ref[...])) * _dot(h, wu_ref[...])
    o_ref[...] = x + _dot(_bf(act), wd_ref[...])


def _ffn_tile(d_ff, parts):
    assert d_ff % (parts * LANES) == 0, f"d_ff={d_ff} does not split into {parts} lane-aligned tiles"
    return d_ff // parts


def _out_swiglu(mix_args, gain, wg, wu, wd):
    specs, args = _mixer_out_operands(*mix_args)
    n, d = args[0].shape
    d_ff = wg.shape[1]
    return pl.pallas_call(
        _out_swiglu_kernel,
        grid=(n // ROW_TILE,),
        in_specs=specs + [_resident((1, d)), _resident((d, d_ff)), _resident((d, d_ff)), _resident((d_ff, d))],
        out_specs=pl.BlockSpec((ROW_TILE, d), lambda i: (i, 0)),
        out_shape=jax.ShapeDtypeStruct((n, d), F32),
        compiler_params=_cparams("parallel"),
        name="out_dense_swiglu",
    )(*args, gain.reshape(1, d), _bf(wg), _bf(wu), _bf(wd))


META_E, META_G, META_RANK = 0, 2, 4


def _out_router_kernel(*refs):
    mix_refs = refs[:N_MIX_OUT_REFS]
    g_ref, whi_ref, wlo_ref, x_ref, h_ref, meta_ref, cnt_ref, run_ref = refs[N_MIX_OUT_REFS:]

    @pl.when(pl.program_id(0) == 0)
    def _():
        run_ref[...] = jnp.zeros_like(run_ref)

    x = _mixer_out(*mix_refs)
    x_ref[...] = x
    h = _rmsnorm(x, g_ref[...])
    h_ref[...] = _pack_bf16_pairs(h)
    rows = h.shape[0]
    lane = lax.broadcasted_iota(jnp.int32, (rows, LANES), 1)
    h_hi = _bf(h)
    h_lo = _bf(h - h_hi.astype(F32))
    logits = _dot(h_hi, whi_ref[...]) + _dot(h_lo, whi_ref[...]) + _dot(h_hi, wlo_ref[...])
    logits = jnp.where(lane < N_EXPERTS, logits, NEG_BIG)
    v1 = jnp.max(logits, axis=-1, keepdims=True)
    e1 = jnp.min(jnp.where(logits == v1, lane, LANES), axis=-1, keepdims=True)
    oh1 = lane == e1
    rest = jnp.where(oh1, NEG_BIG, logits)
    v2 = jnp.max(rest, axis=-1, keepdims=True)
    e2 = jnp.min(jnp.where(rest == v2, lane, LANES), axis=-1, keepdims=True)
    oh2 = lane == e2
    ex = jnp.exp(v2 - v1)
    g1 = 1.0 / (1.0 + ex)
    g2 = ex / (1.0 + ex)
    cnt = jnp.where(oh1 | oh2, 1.0, 0.0)
    t = lax.broadcasted_iota(jnp.int32, (rows, rows), 0)
    s = lax.broadcasted_iota(jnp.int32, (rows, rows), 1)
    before = _dot(jnp.where(s < t, 1.0, 0.0).astype(BF16), _bf(cnt)) + run_ref[0:1, :]
    rank1 = jnp.sum(jnp.where(oh1, before, 0.0), axis=-1, keepdims=True)
    rank2 = jnp.sum(jnp.where(oh2, before, 0.0), axis=-1, keepdims=True)
    fields = (e1.astype(F32), e2.astype(F32), g1, g2, rank1, rank2)
    meta = jnp.zeros((rows, LANES), F32)
    for idx, val in enumerate(fields):
        meta = jnp.where(lane == idx, val, meta)
    meta_ref[...] = meta
    run = run_ref[0:1, :] + jnp.sum(cnt, axis=0, keepdims=True)
    run_ref[...] = jnp.broadcast_to(run, run_ref.shape)
    cnt_ref[...] = jnp.broadcast_to(run, cnt_ref.shape)


def _out_router(mix_args, gain, w_router):
    specs, args = _mixer_out_operands(*mix_args)
    n, d = args[0].shape
    wr = jnp.concatenate([w_router, jnp.zeros((d, LANES - w_router.shape[1]), F32)], axis=1)
    wr_hi = _bf(wr)
    wr_lo = _bf(wr - wr_hi.astype(F32))
    rows = lambda w: pl.BlockSpec((ROW_TILE, w), lambda i: (i, 0))
    return pl.pallas_call(
        _out_router_kernel,
        grid=(n // ROW_TILE,),
        in_specs=specs + [_resident((1, d)), _resident((d, LANES)), _resident((d, LANES))],
        out_specs=[rows(d), rows(d // 2), rows(LANES), pl.BlockSpec((8, LANES), lambda i: (0, 0))],
        out_shape=[jax.ShapeDtypeStruct((n, d), F32), jax.ShapeDtypeStruct((n, d // 2), jnp.uint32),
                   jax.ShapeDtypeStruct((n, LANES), F32), jax.ShapeDtypeStruct((8, LANES), F32)],
        scratch_shapes=[pltpu.VMEM((8, LANES), F32)],
        compiler_params=_cparams("arbitrary"),
        name="out_moe_router",
    )(*args, gain.reshape(1, d), wr_hi, wr_lo)


def _sc_workers():
    info = plsc.get_sparse_core_info()
    return info.num_cores, info.num_cores * info.num_subcores


def _dispatch_rows(h, dest0, dest1, pad_rows):
    nc, workers = _sc_workers()
    n, d = h.shape
    n_pad = pad_rows.shape[0]
    per_worker = n // workers
    nchunk = per_worker // SC_WINDOW
    npad = n_pad // (workers * SC_WINDOW)
    assert n == workers * nchunk * SC_WINDOW and nchunk % 2 == 0 and n_pad == workers * npad * SC_WINDOW
    mesh = plsc.VectorSubcoreMesh(core_axis_name="c", subcore_axis_name="s")
    idx_t = lambda m: pltpu.VMEM((m, SC_WINDOW), jnp.int32)
    buf_t = pltpu.VMEM((SC_WINDOW, d), h.dtype)

    @functools.partial(
        pl.kernel, mesh=mesh,
        out_type=jax.ShapeDtypeStruct((2 * n + n_pad, d), h.dtype),
        scratch_types=[idx_t(nchunk), idx_t(nchunk), idx_t(npad), buf_t, buf_t, buf_t]
                      + [pltpu.SemaphoreType.DMA] * 6,
    )
    def dispatch_kernel(h_hbm, d0_hbm, d1_hbm, pad_hbm, zero_hbm, out_hbm, d0_v, d1_v, pad_v, buf0, buf1, zbuf,
                        r0, r1, w0, w1, x0, x1):
        wid = lax.axis_index("s") * nc + lax.axis_index("c")
        base = wid * per_worker
        pltpu.sync_copy(d0_hbm.at[wid], d0_v)
        pltpu.sync_copy(d1_hbm.at[wid], d1_v)
        pltpu.sync_copy(pad_hbm.at[wid], pad_v)
        pltpu.sync_copy(zero_hbm, zbuf)
        bufs, rsem, wsem, xsem = (buf0, buf1), (r0, r1), (w0, w1), (x0, x1)

        def read(c, slot):
            return pltpu.make_async_copy(h_hbm.at[pl.ds(base + c * SC_WINDOW, SC_WINDOW)], bufs[slot], rsem[slot])

        read(0, 0).start()
        read(1, 1).start()
        for pc in range(npad):
            pltpu.sync_copy(zbuf, out_hbm.at[pad_v.at[pc]])

        @pl.loop(0, nchunk, step=2)
        def _(c):
            for slot in range(2):
                cc = c + slot
                read(cc, slot).wait()
                first = pltpu.make_async_copy(bufs[slot], out_hbm.at[d0_v.at[cc]], wsem[slot])
                second = pltpu.make_async_copy(bufs[slot], out_hbm.at[d1_v.at[cc]], xsem[slot])
                first.start()
                second.start()
                first.wait()
                second.wait()

                @pl.when(cc + 2 < nchunk)
                def _():
                    read(cc + 2, slot).start()

    split = lambda z, m: z.reshape(workers, m, SC_WINDOW)
    return dispatch_kernel(h, split(dest0, nchunk), split(dest1, nchunk), split(pad_rows, npad),
                           jnp.zeros((SC_WINDOW, d), h.dtype))


def _gather_rows(src, idx):
    info = plsc.get_sparse_core_info()
    nc, ns = info.num_cores, info.num_subcores
    workers = nc * ns
    m, d = idx.shape[0], src.shape[1]
    per_worker = m // workers
    nchunk = per_worker // SC_WINDOW
    assert m == workers * nchunk * SC_WINDOW and nchunk % 2 == 0, "row count must split evenly over subcores"
    mesh = plsc.VectorSubcoreMesh(core_axis_name="c", subcore_axis_name="s")

    @functools.partial(
        pl.kernel, mesh=mesh,
        out_type=jax.ShapeDtypeStruct((m, d), src.dtype),
        scratch_types=[pltpu.VMEM((nchunk, SC_WINDOW), jnp.int32),
                       pltpu.VMEM((SC_WINDOW, d), src.dtype), pltpu.VMEM((SC_WINDOW, d), src.dtype),
                       pltpu.SemaphoreType.DMA, pltpu.SemaphoreType.DMA,
                       pltpu.SemaphoreType.DMA, pltpu.SemaphoreType.DMA],
    )
    def gather_kernel(src_hbm, idx_hbm, out_hbm, idx_v, buf0, buf1, g0, g1, w0, w1):
        wid = lax.axis_index("s") * nc + lax.axis_index("c")
        base = wid * per_worker
        pltpu.sync_copy(idx_hbm.at[wid], idx_v)
        bufs, gsem, wsem = (buf0, buf1), (g0, g1), (w0, w1)

        def gather(c, slot):
            return pltpu.make_async_copy(src_hbm.at[idx_v.at[c]], bufs[slot], gsem[slot])

        def write(c, slot):
            return pltpu.make_async_copy(bufs[slot], out_hbm.at[pl.ds(base + c * SC_WINDOW, SC_WINDOW)], wsem[slot])

        gather(0, 0).start()
        gather(1, 1).start()

        @pl.loop(0, nchunk, step=2)
        def _(c):
            for slot in range(2):
                cc = c + slot
                gather(cc, slot).wait()
                write(cc, slot).start()
                write(cc, slot).wait()

                @pl.when(cc + 2 < nchunk)
                def _():
                    gather(cc + 2, slot).start()

    return gather_kernel(src, idx.reshape(workers, nchunk, SC_WINDOW))


def _pack_bf16_pairs(y):
    w = y.shape[1] // 2
    lo = pltpu.bitcast(y[:, :w].astype(BF16).astype(F32), jnp.uint32)
    hi = pltpu.bitcast(y[:, w:].astype(BF16).astype(F32), jnp.uint32)
    return (hi & jnp.uint32(0xFFFF0000)) | (lo >> 16)


def _unpack_bf16_pairs(words):
    lo = pltpu.bitcast(words << 16, F32)
    hi = pltpu.bitcast(words & jnp.uint32(0xFFFF0000), F32)
    return jnp.concatenate([lo, hi], axis=1)


def _expert_kernel(be_ref, nu_ref, x_ref, wg_ref, wu_ref, wd_ref, o_ref, h_ref, acc_ref):
    i, j = pl.program_id(0), pl.program_id(1)
    last = pl.num_programs(1) - 1

    @pl.when(i < nu_ref[0])
    def _():
        @pl.when(j == 0)
        def _():
            h_ref[...] = _bf(_unpack_bf16_pairs(x_ref[...]))

        h = h_ref[...]
        act = jax.nn.silu(_dot(h, wg_ref[0])) * _dot(h, wu_ref[0])
        part = _dot(_bf(act), wd_ref[0])

        @pl.when(j == 0)
        def _():
            acc_ref[...] = part

        @pl.when((j > 0) & (j < last))
        def _():
            acc_ref[...] += part

        @pl.when(j == last)
        def _():
            o_ref[...] = _pack_bf16_pairs(acc_ref[...] + part)

    @pl.when((i >= nu_ref[0]) & (j == 0))
    def _():
        o_ref[...] = jnp.zeros_like(o_ref)


def _expert_swiglu(xb, block_expert, n_used, wg, wu, wd):
    n_rows, d = xb.shape[0], 2 * xb.shape[1]
    d_ff = wg.shape[2]
    tf = _ffn_tile(d_ff, 2)
    nf = d_ff // tf
    assert nf >= 2, "the packed store happens on a d_ff step after the first"
    nblk = n_rows // MOE_BLOCK

    def row(i, j, be, nu):
        return jnp.minimum(i, nu[0] - 1)

    def col(i, j, be, nu):
        return jnp.where(i < nu[0], j, nf - 1)

    grid_spec = pltpu.PrefetchScalarGridSpec(
        num_scalar_prefetch=2,
        grid=(nblk, nf),
        in_specs=[pl.BlockSpec((MOE_BLOCK, d // 2), lambda i, j, be, nu: (row(i, j, be, nu), 0)),
                  pl.BlockSpec((1, d, tf), lambda i, j, be, nu: (be[row(i, j, be, nu)], 0, col(i, j, be, nu))),
                  pl.BlockSpec((1, d, tf), lambda i, j, be, nu: (be[row(i, j, be, nu)], 0, col(i, j, be, nu))),
                  pl.BlockSpec((1, tf, d), lambda i, j, be, nu: (be[row(i, j, be, nu)], col(i, j, be, nu), 0))],
        out_specs=pl.BlockSpec((MOE_BLOCK, d // 2), lambda i, j, be, nu: (i, 0)),
        scratch_shapes=[pltpu.VMEM((MOE_BLOCK, d), BF16), pltpu.VMEM((MOE_BLOCK, d), F32)],
    )
    return pl.pallas_call(
        _expert_kernel,
        grid_spec=grid_spec,
        out_shape=jax.ShapeDtypeStruct((n_rows, d // 2), jnp.uint32),
        compiler_params=_cparams("arbitrary", "arbitrary"),
        name="expert_swiglu",
    )(block_expert, n_used, xb, _bf(wg), _bf(wu), _bf(wd))


def _combine_kernel(x_ref, y1_ref, y2_ref, meta_ref, g_ref, o_ref):
    meta = meta_ref[...]
    lane = lax.broadcasted_iota(jnp.int32, meta.shape, 1)
    g1 = jnp.sum(jnp.where(lane == META_G, meta, 0.0), axis=-1, keepdims=True)
    g2 = jnp.sum(jnp.where(lane == META_G + 1, meta, 0.0), axis=-1, keepdims=True)
    y = x_ref[...] + (g1 * _unpack_bf16_pairs(y1_ref[...]) + g2 * _unpack_bf16_pairs(y2_ref[...]))
    o_ref[...] = _rmsnorm(y, g_ref[...])


def _combine_final_norm(x2, yg, meta, gain):
    n, d = x2.shape
    nblk = n // ROW_TILE
    return pl.pallas_call(
        _combine_kernel,
        grid=(nblk,),
        in_specs=[pl.BlockSpec((ROW_TILE, d), lambda i: (i, 0)),
                  pl.BlockSpec((ROW_TILE, d // 2), lambda i: (i, 0)),
                  pl.BlockSpec((ROW_TILE, d // 2), lambda i: (i + nblk, 0)),
                  pl.BlockSpec((ROW_TILE, LANES), lambda i: (i, 0)),
                  pl.BlockSpec((1, d), lambda i: (0, 0))],
        out_specs=pl.BlockSpec((ROW_TILE, d), lambda i: (i, 0)),
        out_shape=jax.ShapeDtypeStruct((n, d), F32),
        compiler_params=_cparams("parallel"),
        name="moe_combine_norm",
    )(x2, yg, yg, meta, gain.reshape(1, d))


def _moe_layer(mix_args, gain, w_router, wg, wu, wd, final_gain):
    x2, h, meta, counts = _out_router(mix_args, gain, w_router)
    n, d = x2.shape
    expert = meta[:, META_E:META_E + 2].astype(jnp.int32)
    rank = meta[:, META_RANK:META_RANK + 2].astype(jnp.int32)
    count = counts[0, :N_EXPERTS].astype(jnp.int32)
    padded = (count + MOE_BLOCK - 1) // MOE_BLOCK * MOE_BLOCK
    pad_end = jnp.cumsum(padded)
    pad_start = pad_end - padded
    dest = pad_start[expert] + rank
    n_rows = (2 * n // MOE_BLOCK + N_EXPERTS) * MOE_BLOCK
    empties = padded - count
    e_end = jnp.cumsum(empties)
    i = jnp.arange(n_rows - 2 * n, dtype=jnp.int32)
    grp = jnp.sum(i[:, None] >= e_end[None, :], axis=1).astype(jnp.int32)
    first_empty = jnp.concatenate([pad_start + count, pad_end[-1:]])
    first_index = jnp.concatenate([e_end - empties, e_end[-1:]])
    pad_rows = (first_empty[grp] + i - first_index[grp]).astype(jnp.int32)
    block_start = jnp.arange(n_rows // MOE_BLOCK, dtype=jnp.int32) * MOE_BLOCK
    block_expert = jnp.minimum(jnp.sum(block_start[:, None] >= pad_end[None, :], axis=1), N_EXPERTS - 1)
    block_expert = block_expert.astype(jnp.int32)
    n_used = (pad_end[-1:] // MOE_BLOCK).astype(jnp.int32)
    xb = _dispatch_rows(h, dest[:, 0], dest[:, 1], pad_rows)
    ys = _expert_swiglu(xb, block_expert, n_used, wg, wu, wd)
    yg = _gather_rows(ys, jnp.concatenate([dest[:, 0], dest[:, 1]]))
    return _combine_final_norm(x2, yg, meta, final_gain)


def _mixer_layer(x2, batch, norm_mix, w_in, w_out, shift_mu, gm_ln_w, gm_ln_b, gm_ws, gm_bs,
                 rw_w_up, rw_w0, rw_a_up, rw_a0, rw_g_up, rw_k_k, rw_k_a, rw_r_k, rw_lnx_w, rw_lnx_b):
    n, d = x2.shape
    gm2 = 2 * gm_ln_w.shape[0]
    w_in = _bf(w_in)
    y_gm, r, c, lw, k, v, a, b, g, bonus = _mixer_in(
        x2.reshape(batch, n // batch, d), norm_mix, w_in[:, :gm2], w_in[:, gm2:], gm_ln_w, gm_ln_b, gm_ws, gm_bs,
        shift_mu, rw_w_up, rw_w0, rw_a_up, rw_a0, rw_g_up, rw_k_k, rw_k_a, rw_r_k.reshape(-1))
    y_scan = _wkv(r, c, lw, k, v, a, b)
    flat = lambda z: z.reshape(n, -1)
    return x2, flat(y_gm), flat(y_scan), flat(bonus), flat(g), rw_lnx_w, rw_lnx_b, w_out


def kernel(x, norm_mix, w_in, w_out, shift_mu, gm_ln_w, gm_ln_b, gm_ws, gm_bs, rw_w_up, rw_w0, rw_a_up, rw_a0,
           rw_g_up, rw_k_k, rw_k_a, rw_r_k, rw_lnx_w, rw_lnx_b, norm_ffn, ffn_w_gate, ffn_w_up, ffn_w_down,
           moe_router, moe_w_gate, moe_w_up, moe_w_down, norm_final):
    batch, t, d = x.shape
    depth = norm_mix.shape[0]
    assert depth == 2 and t % ROW_TILE == 0, "two layers (dense then MoE), sequence a multiple of the row tile"
    x2 = x.reshape(batch * t, d)
    for i in range(depth):
        mixed = _mixer_layer(x2, batch, norm_mix[i], w_in[i], w_out[i], shift_mu[i], gm_ln_w[i], gm_ln_b[i],
                             gm_ws[i], gm_bs[i], rw_w_up[i], rw_w0[i], rw_a_up[i], rw_a0[i], rw_g_up[i],
                             rw_k_k[i], rw_k_a[i], rw_r_k[i], rw_lnx_w[i], rw_lnx_b[i])
        if i % 2 == 0:
            x2 = _out_swiglu(mixed, norm_ffn[i], ffn_w_gate[i // 2], ffn_w_up[i // 2], ffn_w_down[i // 2])
        else:
            x2 = _moe_layer(mixed, norm_ffn[i], moe_router[i // 2], moe_w_gate[i // 2], moe_w_up[i // 2],
                            moe_w_down[i // 2], norm_final)
    return x2.reshape(batch, t, d)
```

```python
import functools

import jax
import jax.numpy as jnp
from jax import lax
from jax.experimental import pallas as pl
from jax.experimental.pallas import tpu as pltpu
from jax.experimental.pallas import tpu_sc as plsc

F32 = jnp.float32
BF16 = jnp.bfloat16

HEAD_DIM = 64
LANES = 128
GM_CHUNK = 128
WKV_CHUNK = 64
N_EXPERTS = 8
RMS_EPS = 1e-6
LN_EPS = 1e-5
GN_EPS = 64e-5
NEG_BIG = -1e30

ROW_TILE = 512
MIX_TILE = 1024
WKV_STEP = 512
MOE_BLOCK = 512
SC_WINDOW = 32
VMEM_LIMIT = 56 * 1024 * 1024


def _cparams(*sem):
    return pltpu.CompilerParams(dimension_semantics=sem, vmem_limit_bytes=VMEM_LIMIT)


def _bf(x):
    return x.astype(BF16)


def _dot(a, b):
    return jnp.dot(a, b, preferred_element_type=F32)


def _dot_nt(a, b):
    return lax.dot_general(a, b, (((1,), (1,)), ((), ())), preferred_element_type=F32)


def _dot_tn(a, b):
    return lax.dot_general(a, b, (((0,), (0,)), ((), ())), preferred_element_type=F32)


def _split_dot_left(m, x, parts=2):
    acc = None
    rem = x
    for p in range(parts):
        hi = _bf(rem)
        d = _dot(m, hi)
        acc = d if acc is None else acc + d
        if p + 1 < parts:
            rem = rem - hi.astype(F32)
    return acc


def _rmsnorm(x, g):
    return x * lax.rsqrt(jnp.mean(x * x, axis=-1, keepdims=True) + RMS_EPS) * g


def _gelu(x):
    return 0.5 * x * (1.0 + lax.erf(x * (2.0 ** -0.5)))


def _head_avg_matrix(width):
    r = lax.broadcasted_iota(jnp.int32, (width, width), 0) // HEAD_DIM
    c = lax.broadcasted_iota(jnp.int32, (width, width), 1) // HEAD_DIM
    return jnp.where(r == c, 1.0 / HEAD_DIM, 0.0).astype(BF16)


def _head_layernorm(x, avg, w, b, eps):
    mu = _dot(_bf(x), avg)
    xc = x - mu
    var = _dot(_bf(xc * xc), avg)
    return xc * lax.rsqrt(var + eps) * w + b


def _resident(shape):
    return pl.BlockSpec(shape, lambda *_: (0,) * len(shape), pipeline_mode=pl.Buffered(1))


def _gmlp_chunk(p, avg, lnw, lnb, w2, bias, head0):
    width = p.shape[1] // 2
    u = _gelu(p[:, :width])
    vn = _head_layernorm(_gelu(p[:, width:]), avg, lnw, lnb, LN_EPS)
    mixed = []
    for pair, w in enumerate(w2):
        vp = vn[:, pair * LANES:(pair + 1) * LANES]
        v_st = _bf(jnp.concatenate([jnp.where(head0, vp, 0.0), jnp.where(head0, 0.0, vp)], axis=0))
        mixed.append(_dot(w, v_st))
    return u * (jnp.concatenate(mixed, axis=1) + bias)


def _mixer_in_kernel(x_ref, gain_ref, wgm_ref, wrw_ref, lnw_ref, lnb_ref, ws_ref, bias_ref,
                     mu_ref, wup_ref, w0_ref, aup_ref, a0_ref, gup_ref, kk_ref, ka_ref, rk_ref,
                     ygm_ref, r_ref, c_ref, lw_ref, k_ref, v_ref, a_ref, b_ref, g_ref, bonus_ref, carry_ref):
    width = r_ref.shape[2]
    rows = x_ref.shape[1]

    @pl.when(pl.program_id(1) == 0)
    def _():
        carry_ref[...] = jnp.zeros_like(carry_ref)

    avg = _head_avg_matrix(width)
    ones = avg * HEAD_DIM
    t = lax.broadcasted_iota(jnp.int32, (GM_CHUNK, GM_CHUNK), 0)
    s = lax.broadcasted_iota(jnp.int32, (GM_CHUNK, GM_CHUNK), 1)
    causal = s <= t
    head0 = lax.broadcasted_iota(jnp.int32, (GM_CHUNK, LANES), 1) < HEAD_DIM
    w2 = [_bf(jnp.concatenate([jnp.where(causal, ws_ref[2 * pair], 0.0),
                               jnp.where(causal, ws_ref[2 * pair + 1], 0.0)], axis=1))
          for pair in range(width // LANES)]
    t = lax.broadcasted_iota(jnp.int32, (ROW_TILE, ROW_TILE), 0)
    s = lax.broadcasted_iota(jnp.int32, (ROW_TILE, ROW_TILE), 1)
    tri = ((t // WKV_CHUNK == s // WKV_CHUNK) & (s <= t)).astype(BF16)
    first = lax.broadcasted_iota(jnp.int32, (ROW_TILE, wrw_ref.shape[1]), 0) == 0

    subs = [slice(i * ROW_TILE, (i + 1) * ROW_TILE) for i in range(rows // ROW_TILE)]
    proj = []
    for rs in subs:
        h = _bf(_rmsnorm(x_ref[0, rs, :], gain_ref[...]))
        proj.append((_dot(h, wgm_ref[...]), _dot(h, wrw_ref[...])))

    last_row = carry_ref[0:1, :]
    for rs, (p_gm, p) in zip(subs, proj):
        for ch in range(ROW_TILE // GM_CHUNK):
            ts = slice(ch * GM_CHUNK, (ch + 1) * GM_CHUNK)
            out_rows = slice(rs.start + ts.start, rs.start + ts.stop)
            ygm_ref[0, out_rows, :] = _bf(_gmlp_chunk(p_gm[ts], avg, lnw_ref[...], lnb_ref[...], w2, bias_ref[...],
                                                      head0))
        prev = jnp.where(first, last_row, pltpu.roll(p, 1, axis=0))
        last_row = p[ROW_TILE - 1:ROW_TILE, :]
        ps = p + (prev - p) * mu_ref[...]
        r = ps[:, :width]
        k = ps[:, width:2 * width]
        v = ps[:, 2 * width:3 * width]
        lora_in = ps[:, 3 * width:3 * width + LANES]
        gd = ps[:, 3 * width + LANES:]
        lw = -(jnp.exp(-0.5)) * jax.nn.sigmoid(w0_ref[...] + _dot(_bf(jnp.tanh(lora_in)), wup_ref[...]))
        a_lr = jax.nn.sigmoid(a0_ref[...] + _dot(_bf(lora_in), aup_ref[...]))
        kk = k * kk_ref[...]
        kk = kk * lax.rsqrt(jnp.maximum(_dot(_bf(kk * kk), ones), 1e-24))
        kmod = k * (1.0 + (a_lr - 1.0) * ka_ref[...])
        lw_ref[0, rs, :] = lw
        c_ref[0, rs, :] = _split_dot_left(tri, lw)
        r_ref[0, rs, :] = _bf(r)
        k_ref[0, rs, :] = _bf(kmod)
        v_ref[0, rs, :] = _bf(v)
        a_ref[0, rs, :] = _bf(-kk)
        b_ref[0, rs, :] = _bf(kk * a_lr)
        g_ref[0, rs, :] = _bf(_dot(_bf(jax.nn.sigmoid(gd)), gup_ref[...]))
        bonus_ref[0, rs, :] = _bf(_dot(_bf(r * kmod * rk_ref[...]), ones) * v)
    carry_ref[0:1, :] = last_row


def _mixer_in(x3, gain, w_gm, w_rw, ln_w, ln_b, ws, bs, mu, w_up, w0, a_up, a0, g_up, k_k, k_a, r_k):
    batch, t, d = x3.shape
    width = w0.shape[0]
    rank = w_up.shape[0]
    zeros = jnp.zeros((LANES - rank, width), F32)
    wup_pad = _bf(jnp.concatenate([w_up, zeros], axis=0))
    aup_pad = _bf(jnp.concatenate([zeros, a_up], axis=0))
    bias = jnp.repeat(bs.T, HEAD_DIM, axis=1)
    row = lambda z: z.reshape(1, -1)
    params = (row(gain), w_gm, w_rw, row(ln_w), row(ln_b), ws, bias, row(mu), wup_pad, row(w0), aup_pad, row(a0),
              _bf(g_up), row(k_k), row(k_a), row(r_k))
    tile = min(MIX_TILE, t)
    out = pl.BlockSpec((1, tile, width), lambda b, i: (b, i, 0))
    sds = lambda dtype: jax.ShapeDtypeStruct((batch, t, width), dtype)
    return pl.pallas_call(
        _mixer_in_kernel,
        grid=(batch, t // tile),
        in_specs=[pl.BlockSpec((1, tile, d), lambda b, i: (b, i, 0))] + [_resident(z.shape) for z in params],
        out_specs=[out] * 10,
        out_shape=[sds(BF16), sds(BF16), sds(F32), sds(F32)] + [sds(BF16)] * 6,
        scratch_shapes=[pltpu.VMEM((8, w_rw.shape[1]), F32)],
        compiler_params=_cparams("parallel", "arbitrary"),
        name="mixer_in",
    )(x3, *params)


def _wkv_kernel(r_ref, c_ref, lw_ref, k_ref, v_ref, a_ref, b_ref, y_ref, h_ref,
                q_s, v_s, kc_s, ark_s, bcrb_s, mz_s, rhs_s, sol_s, t_s, rh_s, rkv_s, kv_s, dec_s, hc_s, yc_s):
    @pl.when(pl.program_id(1) == 0)
    def _():
        h_ref[...] = jnp.zeros_like(h_ref)

    n = WKV_CHUNK
    two = 2 * n
    head0 = lax.broadcasted_iota(jnp.int32, (n, LANES), 1) < HEAD_DIM
    row = lax.broadcasted_iota(jnp.int32, (two, two), 0)
    col = lax.broadcasted_iota(jnp.int32, (two, two), 1)
    eye = row == col
    row2 = lax.broadcasted_iota(jnp.int32, (two, 2 * two), 0)
    col2 = lax.broadcasted_iota(jnp.int32, (two, 2 * two), 1) % two
    same2 = (row2 // n) == (col2 // n)
    strict2 = same2 & ((col2 % n) < (row2 % n))
    incl2 = same2 & ((col2 % n) <= (row2 % n))
    npairs = y_ref.shape[2] // LANES
    nchunks = y_ref.shape[1] // n
    items = [(ch, pair) for ch in range(nchunks) for pair in range(npairs)]

    def window(ch, pair):
        return slice(ch * n, (ch + 1) * n), slice(pair * LANES, (pair + 1) * LANES)

    def stack(z):
        return jnp.concatenate([jnp.where(head0, z, 0.0), jnp.where(head0, 0.0, z)], axis=0)

    def dup(z):
        zb = _bf(z)
        return jnp.concatenate([zb, zb], axis=0)

    def scores(it):
        ts, ls = window(*items[it])
        r, k, v, a, b = (ref[0, ts, ls].astype(F32) for ref in (r_ref, k_ref, v_ref, a_ref, b_ref))
        c = c_ref[0, ts, ls]
        cex = c - lw_ref[0, ts, ls]
        mid = c[n // 2 - 1:n // 2, :]
        last = c[n - 1:n, :]
        g_inv = jnp.exp(mid - c)
        g_end = jnp.exp(last - c)
        ar_st = _bf(jnp.concatenate([stack(a * jnp.exp(cex - mid)), stack(r * jnp.exp(c - mid))], axis=0))
        bk_dup = jnp.concatenate([dup(b * g_inv), dup(k * g_inv)], axis=0)
        sc = _dot_nt(ar_st, bk_dup)
        top = jnp.where(strict2, sc[:two], 0.0)
        bot = jnp.where(incl2, sc[two:], 0.0)
        t_s[it] = jnp.where(eye, 1.0, top[:, :two])
        q_s[it] = _bf(top[:, :two])
        bcrb_s[it] = _bf(jnp.concatenate([stack(b * g_end).T, bot[:, :two]], axis=0))
        ark_s[it] = _bf(jnp.concatenate([top[:, two:], bot[:, two:]], axis=0))
        v_s[it] = _bf(stack(v))
        kc_s[it] = _bf(stack(k * g_end))
        rh_s[it] = stack(r * jnp.exp(c))
        rhs_s[it, :, :LANES] = _bf(stack(a * jnp.exp(cex)))
        e_col = jnp.sum(jnp.where(eye, jnp.exp(last), 0.0), axis=1, keepdims=True)
        dec_s[it] = jnp.broadcast_to(e_col, (two, LANES))

    def values(it):
        v_st = v_s[it]
        both = _dot(ark_s[it], v_st)
        rhs_s[it, :, LANES:] = _bf(both[:two])
        rkv_s[it] = both[two:]
        kv_s[it] = _dot_tn(kc_s[it], v_st)

    levels = n.bit_length() - 1

    def inverse_level(level, it):
        q = q_s[it]
        t_acc = t_s[it]
        if level == 1:
            q = _bf(_dot(q, q))
        if level + 1 < levels:
            prod = _dot(q, jnp.concatenate([_bf(t_acc), q], axis=1))
            q_s[it] = _bf(prod[:, two:])
            t_s[it] = t_acc + prod[:, :two]
        else:
            t_s[it] = t_acc + _dot(q, _bf(t_acc))

    def solve(it):
        sol_s[it] = _bf(_dot(_bf(t_s[it]), rhs_s[it]))

    def fold(it):
        prod = _dot(bcrb_s[it], sol_s[it])
        mz_s[it] = _bf(jnp.concatenate([prod[:two, :LANES], rh_s[it] + prod[two:, :LANES]], axis=0))
        hc_s[it] = prod[:two, LANES:] + kv_s[it]
        yc_s[it] = prod[two:, LANES:] + rkv_s[it]

    def advance(it):
        ts, ls = window(*items[it])
        pair = items[it][1]
        h = h_ref[pair]
        gmat = _dot(mz_s[it], _bf(h))
        h_ref[pair] = dec_s[it] * h + gmat[:two] + hc_s[it]
        y_st = gmat[two:] + yc_s[it]
        y_ref[0, ts, ls] = y_st[:n] + y_st[n:]

    stages = ([scores, values] + [functools.partial(inverse_level, level) for level in range(1, levels)]
              + [solve, fold, advance])
    for stage in stages:
        for it in range(len(items)):
            stage(it)


def _wkv(r, c, lw, k, v, a, b):
    batch, t, width = r.shape
    npairs = width // LANES
    items = (WKV_STEP // WKV_CHUNK) * npairs
    spec = pl.BlockSpec((1, WKV_STEP, width), lambda bi, i: (bi, i, 0))
    sq = lambda dtype: pltpu.VMEM((items, LANES, LANES), dtype)
    return pl.pallas_call(
        _wkv_kernel,
        grid=(batch, t // WKV_STEP),
        in_specs=[spec] * 7,
        out_specs=spec,
        out_shape=jax.ShapeDtypeStruct((batch, t, width), F32),
        scratch_shapes=[pltpu.VMEM((npairs, LANES, LANES), F32)] + [sq(BF16)] * 3
                       + [pltpu.VMEM((items, 2 * LANES, LANES), BF16)] * 3
                       + [pltpu.VMEM((items, LANES, 2 * LANES), BF16)] * 2 + [sq(F32)] * 7,
        compiler_params=_cparams("parallel", "arbitrary"),
        name="wkv7",
    )(r, c, lw, k, v, a, b)


N_MIX_OUT_REFS = 9


def _mixer_out(x_ref, ygm_ref, ys_ref, bonus_ref, g_ref, lnw_ref, lnb_ref, wtop_ref, wbot_ref):
    width = ys_ref.shape[1]
    yn = _head_layernorm(ys_ref[...], _head_avg_matrix(width), lnw_ref[...], lnb_ref[...], GN_EPS)
    y_rw = (yn + bonus_ref[...]) * g_ref[...]
    return x_ref[...] + _dot(ygm_ref[...], wtop_ref[...]) + _dot(_bf(y_rw), wbot_ref[...])


def _mixer_out_operands(x2, y_gm, y_scan, bonus, g, lnx_w, lnx_b, w_out):
    d = x2.shape[1]
    width = y_gm.shape[1]
    rows = lambda w: pl.BlockSpec((ROW_TILE, w), lambda i: (i, 0))
    w_out = _bf(w_out)
    specs = [rows(d), rows(width), rows(width), rows(width), rows(width),
             _resident((1, width)), _resident((1, width)), _resident((width, d)), _resident((width, d))]
    args = (x2, y_gm, y_scan, bonus, g, lnx_w.reshape(1, width), lnx_b.reshape(1, width), w_out[:width], w_out[width:])
    assert len(specs) == len(args) == N_MIX_OUT_REFS
    return specs, args


def _out_swiglu_kernel(*refs):
    mix_refs, (gain_ref, wg_ref, wu_ref, wd_ref, o_ref) = refs[:N_MIX_OUT_REFS], refs[N_MIX_OUT_REFS:]
    x = _mixer_out(*mix_refs)
    h = _bf(_rmsnorm(x, gain_ref[...]))
    act = jax.nn.silu(_dot(h, wg_ref[...])) * _dot(h, wu_ref[...])
    o_ref[...] = x + _dot(_bf(act), wd_ref[...])


def _ffn_tile(d_ff, parts):
    assert d_ff % (parts * LANES) == 0, f"d_ff={d_ff} does not split into {parts} lane-aligned tiles"
    return d_ff // parts


def _out_swiglu(mix_args, gain, wg, wu, wd):
    specs, args = _mixer_out_operands(*mix_args)
    n, d = args[0].shape
    d_ff = wg.shape[1]
    return pl.pallas_call(
        _out_swiglu_kernel,
        grid=(n // ROW_TILE,),
        in_specs=specs + [_resident((1, d)), _resident((d, d_ff)), _resident((d, d_ff)), _resident((d_ff, d))],
        out_specs=pl.BlockSpec((ROW_TILE, d), lambda i: (i, 0)),
        out_shape=jax.ShapeDtypeStruct((n, d), F32),
        compiler_params=_cparams("parallel"),
        name="out_dense_swiglu",
    )(*args, gain.reshape(1, d), _bf(wg), _bf(wu), _bf(wd))


META_E, META_G, META_RANK = 0, 2, 4


def _out_router_kernel(*refs):
    mix_refs = refs[:N_MIX_OUT_REFS]
    g_ref, whi_ref, wlo_ref, x_ref, h_ref, meta_ref, cnt_ref, run_ref = refs[N_MIX_OUT_REFS:]

    @pl.when(pl.program_id(0) == 0)
    def _():
        run_ref[...] = jnp.zeros_like(run_ref)

    x = _mixer_out(*mix_refs)
    x_ref[...] = x
    h = _rmsnorm(x, g_ref[...])
    h_ref[...] = _pack_bf16_pairs(h)
    rows = h.shape[0]
    lane = lax.broadcasted_iota(jnp.int32, (rows, LANES), 1)
    h_hi = _bf(h)
    h_lo = _bf(h - h_hi.astype(F32))
    logits = _dot(h_hi, whi_ref[...]) + _dot(h_lo, whi_ref[...]) + _dot(h_hi, wlo_ref[...])
    logits = jnp.where(lane < N_EXPERTS, logits, NEG_BIG)
    v1 = jnp.max(logits, axis=-1, keepdims=True)
    e1 = jnp.min(jnp.where(logits == v1, lane, LANES), axis=-1, keepdims=True)
    oh1 = lane == e1
    rest = jnp.where(oh1, NEG_BIG, logits)
    v2 = jnp.max(rest, axis=-1, keepdims=True)
    e2 = jnp.min(jnp.where(rest == v2, lane, LANES), axis=-1, keepdims=True)
    oh2 = lane == e2
    ex = jnp.exp(v2 - v1)
    g1 = 1.0 / (1.0 + ex)
    g2 = ex / (1.0 + ex)
    cnt = jnp.where(oh1 | oh2, 1.0, 0.0)
    t = lax.broadcasted_iota(jnp.int32, (rows, rows), 0)
    s = lax.broadcasted_iota(jnp.int32, (rows, rows), 1)
    before = _dot(jnp.where(s < t, 1.0, 0.0).astype(BF16), _bf(cnt)) + run_ref[0:1, :]
    rank1 = jnp.sum(jnp.where(oh1, before, 0.0), axis=-1, keepdims=True)
    rank2 = jnp.sum(jnp.where(oh2, before, 0.0), axis=-1, keepdims=True)
    fields = (e1.astype(F32), e2.astype(F32), g1, g2, rank1, rank2)
    meta = jnp.zeros((rows, LANES), F32)
    for idx, val in enumerate(fields):
        meta = jnp.where(lane == idx, val, meta)
    meta_ref[...] = meta
    run = run_ref[0:1, :] + jnp.sum(cnt, axis=0, keepdims=True)
    run_ref[...] = jnp.broadcast_to(run, run_ref.shape)
    cnt_ref[...] = jnp.broadcast_to(run, cnt_ref.shape)


def _out_router(mix_args, gain, w_router):
    specs, args = _mixer_out_operands(*mix_args)
    n, d = args[0].shape
    wr = jnp.concatenate([w_router, jnp.zeros((d, LANES - w_router.shape[1]), F32)], axis=1)
    wr_hi = _bf(wr)
    wr_lo = _bf(wr - wr_hi.astype(F32))
    rows = lambda w: pl.BlockSpec((ROW_TILE, w), lambda i: (i, 0))
    return pl.pallas_call(
        _out_router_kernel,
        grid=(n // ROW_TILE,),
        in_specs=specs + [_resident((1, d)), _resident((d, LANES)), _resident((d, LANES))],
        out_specs=[rows(d), rows(d // 2), rows(LANES), pl.BlockSpec((8, LANES), lambda i: (0, 0))],
        out_shape=[jax.ShapeDtypeStruct((n, d), F32), jax.ShapeDtypeStruct((n, d // 2), jnp.uint32),
                   jax.ShapeDtypeStruct((n, LANES), F32), jax.ShapeDtypeStruct((8, LANES), F32)],
        scratch_shapes=[pltpu.VMEM((8, LANES), F32)],
        compiler_params=_cparams("arbitrary"),
        name="out_moe_router",
    )(*args, gain.reshape(1, d), wr_hi, wr_lo)


def _sc_workers():
    info = plsc.get_sparse_core_info()
    return info.num_cores, info.num_cores * info.num_subcores


def _dispatch_rows(h, dest0, dest1, pad_rows):
    nc, workers = _sc_workers()
    n, d = h.shape
    n_pad = pad_rows.shape[0]
    per_worker = n // workers
    nchunk = per_worker // SC_WINDOW
    npad = n_pad // (workers * SC_WINDOW)
    assert n == workers * nchunk * SC_WINDOW and nchunk % 2 == 0 and n_pad == workers * npad * SC_WINDOW
    mesh = plsc.VectorSubcoreMesh(core_axis_name="c", subcore_axis_name="s")
    idx_t = lambda m: pltpu.VMEM((m, SC_WINDOW), jnp.int32)
    buf_t = pltpu.VMEM((SC_WINDOW, d), h.dtype)

    @functools.partial(
        pl.kernel, mesh=mesh,
        out_type=jax.ShapeDtypeStruct((2 * n + n_pad, d), h.dtype),
        scratch_types=[idx_t(nchunk), idx_t(nchunk), idx_t(npad), buf_t, buf_t, buf_t]
                      + [pltpu.SemaphoreType.DMA] * 6,
    )
    def dispatch_kernel(h_hbm, d0_hbm, d1_hbm, pad_hbm, zero_hbm, out_hbm, d0_v, d1_v, pad_v, buf0, buf1, zbuf,
                        r0, r1, w0, w1, x0, x1):
        wid = lax.axis_index("s") * nc + lax.axis_index("c")
        base = wid * per_worker
        pltpu.sync_copy(d0_hbm.at[wid], d0_v)
        pltpu.sync_copy(d1_hbm.at[wid], d1_v)
        pltpu.sync_copy(pad_hbm.at[wid], pad_v)
        pltpu.sync_copy(zero_hbm, zbuf)
        bufs, rsem, wsem, xsem = (buf0, buf1), (r0, r1), (w0, w1), (x0, x1)

        def read(c, slot):
            return pltpu.make_async_copy(h_hbm.at[pl.ds(base + c * SC_WINDOW, SC_WINDOW)], bufs[slot], rsem[slot])

        read(0, 0).start()
        read(1, 1).start()
        for pc in range(npad):
            pltpu.sync_copy(zbuf, out_hbm.at[pad_v.at[pc]])

        @pl.loop(0, nchunk, step=2)
        def _(c):
            for slot in range(2):
                cc = c + slot
                read(cc, slot).wait()
                first = pltpu.make_async_copy(bufs[slot], out_hbm.at[d0_v.at[cc]], wsem[slot])
                second = pltpu.make_async_copy(bufs[slot], out_hbm.at[d1_v.at[cc]], xsem[slot])
                first.start()
                second.start()
                first.wait()
                second.wait()

                @pl.when(cc + 2 < nchunk)
                def _():
                    read(cc + 2, slot).start()

    split = lambda z, m: z.reshape(workers, m, SC_WINDOW)
    return dispatch_kernel(h, split(dest0, nchunk), split(dest1, nchunk), split(pad_rows, npad),
                           jnp.zeros((SC_WINDOW, d), h.dtype))


def _gather_rows(src, idx):
    info = plsc.get_sparse_core_info()
    nc, ns = info.num_cores, info.num_subcores
    workers = nc * ns
    m, d = idx.shape[0], src.shape[1]
    per_worker = m // workers
    nchunk = per_worker // SC_WINDOW
    assert m == workers * nchunk * SC_WINDOW and nchunk % 2 == 0, "row count must split evenly over subcores"
    mesh = plsc.VectorSubcoreMesh(core_axis_name="c", subcore_axis_name="s")

    @functools.partial(
        pl.kernel, mesh=mesh,
        out_type=jax.ShapeDtypeStruct((m, d), src.dtype),
        scratch_types=[pltpu.VMEM((nchunk, SC_WINDOW), jnp.int32),
                       pltpu.VMEM((SC_WINDOW, d), src.dtype), pltpu.VMEM((SC_WINDOW, d), src.dtype),
                       pltpu.SemaphoreType.DMA, pltpu.SemaphoreType.DMA,
                       pltpu.SemaphoreType.DMA, pltpu.SemaphoreType.DMA],
    )
    def gather_kernel(src_hbm, idx_hbm, out_hbm, idx_v, buf0, buf1, g0, g1, w0, w1):
        wid = lax.axis_index("s") * nc + lax.axis_index("c")
        base = wid * per_worker
        pltpu.sync_copy(idx_hbm.at[wid], idx_v)
        bufs, gsem, wsem = (buf0, buf1), (g0, g1), (w0, w1)

        def gather(c, slot):
            return pltpu.make_async_copy(src_hbm.at[idx_v.at[c]], bufs[slot], gsem[slot])

        def write(c, slot):
            return pltpu.make_async_copy(bufs[slot], out_hbm.at[pl.ds(base + c * SC_WINDOW, SC_WINDOW)], wsem[slot])

        gather(0, 0).start()
        gather(1, 1).start()

        @pl.loop(0, nchunk, step=2)
        def _(c):
            for slot in range(2):
                cc = c + slot
                gather(cc, slot).wait()
                write(cc, slot).start()
                write(cc, slot).wait()

                @pl.when(cc + 2 < nchunk)
                def _():
                    gather(cc + 2, slot).start()

    return gather_kernel(src, idx.reshape(workers, nchunk, SC_WINDOW))


def _pack_bf16_pairs(y):
    bits = lambda z: pltpu.bitcast(z.astype(BF16).astype(F32), jnp.uint32)
    words = []
    for t in range(0, y.shape[1] // LANES, 2):
        lo = bits(y[:, t * LANES:(t + 1) * LANES])
        hi = bits(y[:, (t + 1) * LANES:(t + 2) * LANES])
        words.append((hi & jnp.uint32(0xFFFF0000)) | (lo >> 16))
    return jnp.concatenate(words, axis=1)


def _unpack_bf16_pairs(words):
    tiles = []
    for t in range(words.shape[1] // LANES):
        w = words[:, t * LANES:(t + 1) * LANES]
        tiles += [pltpu.bitcast(w << 16, F32), pltpu.bitcast(w & jnp.uint32(0xFFFF0000), F32)]
    return jnp.concatenate(tiles, axis=1)


def _expert_kernel(be_ref, nu_ref, x_ref, wg_ref, wu_ref, wd_ref, o_ref, h_ref, acc_ref):
    i, j = pl.program_id(0), pl.program_id(1)
    last = pl.num_programs(1) - 1

    @pl.when(i < nu_ref[0])
    def _():
        @pl.when(j == 0)
        def _():
            h_ref[...] = _bf(_unpack_bf16_pairs(x_ref[...]))

        h = h_ref[...]
        act = jax.nn.silu(_dot(h, wg_ref[0])) * _dot(h, wu_ref[0])
        part = _dot(_bf(act), wd_ref[0])

        @pl.when(j == 0)
        def _():
            acc_ref[...] = part

        @pl.when((j > 0) & (j < last))
        def _():
            acc_ref[...] += part

        @pl.when(j == last)
        def _():
            o_ref[...] = _pack_bf16_pairs(acc_ref[...] + part)

    @pl.when((i >= nu_ref[0]) & (j == 0))
    def _():
        o_ref[...] = jnp.zeros_like(o_ref)


def _expert_swiglu(xb, block_expert, n_used, wg, wu, wd):
    n_rows, d = xb.shape[0], 2 * xb.shape[1]
    d_ff = wg.shape[2]
    tf = _ffn_tile(d_ff, 2)
    nf = d_ff // tf
    assert nf >= 2, "the packed store happens on a d_ff step after the first"
    nblk = n_rows // MOE_BLOCK

    def row(i, j, be, nu):
        return jnp.minimum(i, nu[0] - 1)

    def col(i, j, be, nu):
        return jnp.where(i < nu[0], j, nf - 1)

    grid_spec = pltpu.PrefetchScalarGridSpec(
        num_scalar_prefetch=2,
        grid=(nblk, nf),
        in_specs=[pl.BlockSpec((MOE_BLOCK, d // 2), lambda i, j, be, nu: (row(i, j, be, nu), 0)),
                  pl.BlockSpec((1, d, tf), lambda i, j, be, nu: (be[row(i, j, be, nu)], 0, col(i, j, be, nu))),
                  pl.BlockSpec((1, d, tf), lambda i, j, be, nu: (be[row(i, j, be, nu)], 0, col(i, j, be, nu))),
                  pl.BlockSpec((1, tf, d), lambda i, j, be, nu: (be[row(i, j, be, nu)], col(i, j, be, nu), 0))],
        out_specs=pl.BlockSpec((MOE_BLOCK, d // 2), lambda i, j, be, nu: (i, 0)),
        scratch_shapes=[pltpu.VMEM((MOE_BLOCK, d), BF16), pltpu.VMEM((MOE_BLOCK, d), F32)],
    )
    return pl.pallas_call(
        _expert_kernel,
        grid_spec=grid_spec,
        out_shape=jax.ShapeDtypeStruct((n_rows, d // 2), jnp.uint32),
        compiler_params=_cparams("arbitrary", "arbitrary"),
        name="expert_swiglu",
    )(block_expert, n_used, xb, _bf(wg), _bf(wu), _bf(wd))


def _combine_kernel(x_ref, y1_ref, y2_ref, meta_ref, g_ref, o_ref):
    meta = meta_ref[...]
    lane = lax.broadcasted_iota(jnp.int32, meta.shape, 1)
    g1 = jnp.sum(jnp.where(lane == META_G, meta, 0.0), axis=-1, keepdims=True)
    g2 = jnp.sum(jnp.where(lane == META_G + 1, meta, 0.0), axis=-1, keepdims=True)
    y = x_ref[...] + (g1 * _unpack_bf16_pairs(y1_ref[...]) + g2 * _unpack_bf16_pairs(y2_ref[...]))
    o_ref[...] = _rmsnorm(y, g_ref[...])


def _combine_final_norm(x2, yg, meta, gain):
    n, d = x2.shape
    nblk = n // ROW_TILE
    return pl.pallas_call(
        _combine_kernel,
        grid=(nblk,),
        in_specs=[pl.BlockSpec((ROW_TILE, d), lambda i: (i, 0)),
                  pl.BlockSpec((ROW_TILE, d // 2), lambda i: (i, 0)),
                  pl.BlockSpec((ROW_TILE, d // 2), lambda i: (i + nblk, 0)),
                  pl.BlockSpec((ROW_TILE, LANES), lambda i: (i, 0)),
                  pl.BlockSpec((1, d), lambda i: (0, 0))],
        out_specs=pl.BlockSpec((ROW_TILE, d), lambda i: (i, 0)),
        out_shape=jax.ShapeDtypeStruct((n, d), F32),
        compiler_params=_cparams("parallel"),
        name="moe_combine_norm",
    )(x2, yg, yg, meta, gain.reshape(1, d))


def _moe_layer(mix_args, gain, w_router, wg, wu, wd, final_gain):
    x2, h, meta, counts = _out_router(mix_args, gain, w_router)
    n, d = x2.shape
    expert = meta[:, META_E:META_E + 2].astype(jnp.int32)
    rank = meta[:, META_RANK:META_RANK + 2].astype(jnp.int32)
    count = counts[0, :N_EXPERTS].astype(jnp.int32)
    padded = (count + MOE_BLOCK - 1) // MOE_BLOCK * MOE_BLOCK
    pad_end = jnp.cumsum(padded)
    pad_start = pad_end - padded
    dest = pad_start[expert] + rank
    n_rows = (2 * n // MOE_BLOCK + N_EXPERTS) * MOE_BLOCK
    empties = padded - count
    e_end = jnp.cumsum(empties)
    i = jnp.arange(n_rows - 2 * n, dtype=jnp.int32)
    grp = jnp.sum(i[:, None] >= e_end[None, :], axis=1).astype(jnp.int32)
    first_empty = jnp.concatenate([pad_start + count, pad_end[-1:]])
    first_index = jnp.concatenate([e_end - empties, e_end[-1:]])
    pad_rows = (first_empty[grp] + i - first_index[grp]).astype(jnp.int32)
    block_start = jnp.arange(n_rows // MOE_BLOCK, dtype=jnp.int32) * MOE_BLOCK
    block_expert = jnp.minimum(jnp.sum(block_start[:, None] >= pad_end[None, :], axis=1), N_EXPERTS - 1)
    block_expert = block_expert.astype(jnp.int32)
    n_used = (pad_end[-1:] // MOE_BLOCK).astype(jnp.int32)
    xb = _dispatch_rows(h, dest[:, 0], dest[:, 1], pad_rows)
    ys = _expert_swiglu(xb, block_expert, n_used, wg, wu, wd)
    yg = _gather_rows(ys, jnp.concatenate([dest[:, 0], dest[:, 1]]))
    return _combine_final_norm(x2, yg, meta, final_gain)


def _mixer_layer(x2, batch, norm_mix, w_in, w_out, shift_mu, gm_ln_w, gm_ln_b, gm_ws, gm_bs,
                 rw_w_up, rw_w0, rw_a_up, rw_a0, rw_g_up, rw_k_k, rw_k_a, rw_r_k, rw_lnx_w, rw_lnx_b):
    n, d = x2.shape
    gm2 = 2 * gm_ln_w.shape[0]
    w_in = _bf(w_in)
    y_gm, r, c, lw, k, v, a, b, g, bonus = _mixer_in(
        x2.reshape(batch, n // batch, d), norm_mix, w_in[:, :gm2], w_in[:, gm2:], gm_ln_w, gm_ln_b, gm_ws, gm_bs,
        shift_mu, rw_w_up, rw_w0, rw_a_up, rw_a0, rw_g_up, rw_k_k, rw_k_a, rw_r_k.reshape(-1))
    y_scan = _wkv(r, c, lw, k, v, a, b)
    flat = lambda z: z.reshape(n, -1)
    return x2, flat(y_gm), flat(y_scan), flat(bonus), flat(g), rw_lnx_w, rw_lnx_b, w_out


def kernel(x, norm_mix, w_in, w_out, shift_mu, gm_ln_w, gm_ln_b, gm_ws, gm_bs, rw_w_up, rw_w0, rw_a_up, rw_a0,
           rw_g_up, rw_k_k, rw_k_a, rw_r_k, rw_lnx_w, rw_lnx_b, norm_ffn, ffn_w_gate, ffn_w_up, ffn_w_down,
           moe_router, moe_w_gate, moe_w_up, moe_w_down, norm_final):
    batch, t, d = x.shape
    depth = norm_mix.shape[0]
    assert depth == 2 and t % ROW_TILE == 0, "two layers (dense then MoE), sequence a multiple of the row tile"
    x2 = x.reshape(batch * t, d)
    for i in range(depth):
        mixed = _mixer_layer(x2, batch, norm_mix[i], w_in[i], w_out[i], shift_mu[i], gm_ln_w[i], gm_ln_b[i],
                             gm_ws[i], gm_bs[i], rw_w_up[i], rw_w0[i], rw_a_up[i], rw_a0[i], rw_g_up[i],
                             rw_k_k[i], rw_k_a[i], rw_r_k[i], rw_lnx_w[i], rw_lnx_b[i])
        if i % 2 == 0:
            x2 = _out_swiglu(mixed, norm_ffn[i], ffn_w_gate[i // 2], ffn_w_up[i // 2], ffn_w_down[i // 2])
        else:
            x2 = _moe_layer(mixed, norm_ffn[i], moe_router[i // 2], moe_w_gate[i // 2], moe_w_up[i // 2],
                            moe_w_down[i // 2], norm_final)
    return x2.reshape(batch, t, d)
```

```python
import functools

import jax
import jax.numpy as jnp
from jax import lax
from jax.experimental import pallas as pl
from jax.experimental.pallas import tpu as pltpu
from jax.experimental.pallas import tpu_sc as plsc

F32 = jnp.float32
BF16 = jnp.bfloat16

HEAD_DIM = 64
LANES = 128
MXU_TILE = 256
GM_CHUNK = 128
WKV_CHUNK = 64
N_EXPERTS = 8
RMS_EPS = 1e-6
LN_EPS = 1e-5
GN_EPS = 64e-5
NEG_BIG = -1e30

ROW_TILE = 512
MIX_TILE = 1024
WKV_STEP = 512
MOE_BLOCK = 512
SC_WINDOW = 32
VMEM_LIMIT = 56 * 1024 * 1024


def _cparams(*sem):
    return pltpu.CompilerParams(dimension_semantics=sem, vmem_limit_bytes=VMEM_LIMIT)


def _bf(x):
    return x.astype(BF16)


def _dot(a, b):
    return jnp.dot(a, b, preferred_element_type=F32)


def _dot_nt(a, b):
    return lax.dot_general(a, b, (((1,), (1,)), ((), ())), preferred_element_type=F32)


def _dot_tn(a, b):
    return lax.dot_general(a, b, (((0,), (0,)), ((), ())), preferred_element_type=F32)


def _split_dot_left(m, x, parts=2):
    acc = None
    rem = x
    for p in range(parts):
        hi = _bf(rem)
        d = _dot(m, hi)
        acc = d if acc is None else acc + d
        if p + 1 < parts:
            rem = rem - hi.astype(F32)
    return acc


def _rmsnorm(x, g):
    return x * lax.rsqrt(jnp.mean(x * x, axis=-1, keepdims=True) + RMS_EPS) * g


def _gelu(x):
    return 0.5 * x * (1.0 + lax.erf(x * (2.0 ** -0.5)))


def _head_avg_matrix():
    r = lax.broadcasted_iota(jnp.int32, (MXU_TILE, MXU_TILE), 0) // HEAD_DIM
    c = lax.broadcasted_iota(jnp.int32, (MXU_TILE, MXU_TILE), 1) // HEAD_DIM
    return jnp.where(r == c, 1.0 / HEAD_DIM, 0.0).astype(BF16)


def _head_dot(x, m):
    return jnp.concatenate([_dot(x[:, i:i + MXU_TILE], m) for i in range(0, x.shape[1], MXU_TILE)], axis=1)


def _head_layernorm(x, avg, w, b, eps):
    mu = _head_dot(_bf(x), avg)
    xc = x - mu
    var = _head_dot(_bf(xc * xc), avg)
    return xc * lax.rsqrt(var + eps) * w + b


def _resident(shape):
    return pl.BlockSpec(shape, lambda *_: (0,) * len(shape), pipeline_mode=pl.Buffered(1))


def _gmlp_chunk(p, avg, lnw, lnb, w2, bias, head0):
    width = p.shape[1] // 2
    u = _gelu(p[:, :width])
    vn = _head_layernorm(_gelu(p[:, width:]), avg, lnw, lnb, LN_EPS)
    mixed = []
    for pair, w in enumerate(w2):
        vp = vn[:, pair * LANES:(pair + 1) * LANES]
        v_st = _bf(jnp.concatenate([jnp.where(head0, vp, 0.0), jnp.where(head0, 0.0, vp)], axis=0))
        mixed.append(_dot(w, v_st))
    return u * (jnp.concatenate(mixed, axis=1) + bias)


def _mixer_in_kernel(x_ref, gain_ref, wgm_ref, wrw_ref, lnw_ref, lnb_ref, ws_ref, bias_ref,
                     mu_ref, wup_ref, w0_ref, aup_ref, a0_ref, gup_ref, kk_ref, ka_ref, rk_ref,
                     ygm_ref, r_ref, c_ref, lw_ref, k_ref, v_ref, a_ref, b_ref, g_ref, bonus_ref, carry_ref):
    width = r_ref.shape[2]
    rows = x_ref.shape[1]

    @pl.when(pl.program_id(1) == 0)
    def _():
        carry_ref[...] = jnp.zeros_like(carry_ref)

    avg = _head_avg_matrix()
    ones = avg * HEAD_DIM
    t = lax.broadcasted_iota(jnp.int32, (GM_CHUNK, GM_CHUNK), 0)
    s = lax.broadcasted_iota(jnp.int32, (GM_CHUNK, GM_CHUNK), 1)
    causal = s <= t
    head0 = lax.broadcasted_iota(jnp.int32, (GM_CHUNK, LANES), 1) < HEAD_DIM
    w2 = [_bf(jnp.concatenate([jnp.where(causal, ws_ref[2 * pair], 0.0),
                               jnp.where(causal, ws_ref[2 * pair + 1], 0.0)], axis=1))
          for pair in range(width // LANES)]
    t = lax.broadcasted_iota(jnp.int32, (MXU_TILE, MXU_TILE), 0)
    s = lax.broadcasted_iota(jnp.int32, (MXU_TILE, MXU_TILE), 1)
    tri = ((t // WKV_CHUNK == s // WKV_CHUNK) & (s <= t)).astype(BF16)
    first = lax.broadcasted_iota(jnp.int32, (ROW_TILE, wrw_ref.shape[1]), 0) == 0

    subs = [slice(i * ROW_TILE, (i + 1) * ROW_TILE) for i in range(rows // ROW_TILE)]
    proj = []
    for rs in subs:
        h = _bf(_rmsnorm(x_ref[0, rs, :], gain_ref[...]))
        proj.append((_dot(h, wgm_ref[...]), _dot(h, wrw_ref[...])))

    last_row = carry_ref[0:1, :]
    for rs, (p_gm, p) in zip(subs, proj):
        for ch in range(ROW_TILE // GM_CHUNK):
            ts = slice(ch * GM_CHUNK, (ch + 1) * GM_CHUNK)
            out_rows = slice(rs.start + ts.start, rs.start + ts.stop)
            ygm_ref[0, out_rows, :] = _bf(_gmlp_chunk(p_gm[ts], avg, lnw_ref[...], lnb_ref[...], w2, bias_ref[...],
                                                      head0))
        prev = jnp.where(first, last_row, pltpu.roll(p, 1, axis=0))
        last_row = p[ROW_TILE - 1:ROW_TILE, :]
        ps = p + (prev - p) * mu_ref[...]
        r = ps[:, :width]
        k = ps[:, width:2 * width]
        v = ps[:, 2 * width:3 * width]
        lora_in = ps[:, 3 * width:3 * width + LANES]
        gd = ps[:, 3 * width + LANES:]
        lw = -(jnp.exp(-0.5)) * jax.nn.sigmoid(w0_ref[...] + _dot(_bf(jnp.tanh(lora_in)), wup_ref[...]))
        a_lr = jax.nn.sigmoid(a0_ref[...] + _dot(_bf(lora_in), aup_ref[...]))
        kk = k * kk_ref[...]
        kk = kk * lax.rsqrt(jnp.maximum(_head_dot(_bf(kk * kk), ones), 1e-24))
        kmod = k * (1.0 + (a_lr - 1.0) * ka_ref[...])
        lw_ref[0, rs, :] = lw
        c_ref[0, rs, :] = jnp.concatenate([_split_dot_left(tri, lw[i:i + MXU_TILE])
                                           for i in range(0, ROW_TILE, MXU_TILE)], axis=0)
        r_ref[0, rs, :] = _bf(r)
        k_ref[0, rs, :] = _bf(kmod)
        v_ref[0, rs, :] = _bf(v)
        a_ref[0, rs, :] = _bf(-kk)
        b_ref[0, rs, :] = _bf(kk * a_lr)
        g_ref[0, rs, :] = _bf(_dot(_bf(jax.nn.sigmoid(gd)), gup_ref[...]))
        bonus_ref[0, rs, :] = _bf(_head_dot(_bf(r * kmod * rk_ref[...]), ones) * v)
    carry_ref[0:1, :] = last_row


def _mixer_in(x3, gain, w_gm, w_rw, ln_w, ln_b, ws, bs, mu, w_up, w0, a_up, a0, g_up, k_k, k_a, r_k):
    batch, t, d = x3.shape
    width = w0.shape[0]
    rank = w_up.shape[0]
    zeros = jnp.zeros((LANES - rank, width), F32)
    wup_pad = _bf(jnp.concatenate([w_up, zeros], axis=0))
    aup_pad = _bf(jnp.concatenate([zeros, a_up], axis=0))
    bias = jnp.repeat(bs.T, HEAD_DIM, axis=1)
    row = lambda z: z.reshape(1, -1)
    params = (row(gain), w_gm, w_rw, row(ln_w), row(ln_b), ws, bias, row(mu), wup_pad, row(w0), aup_pad, row(a0),
              _bf(g_up), row(k_k), row(k_a), row(r_k))
    tile = min(MIX_TILE, t)
    out = pl.BlockSpec((1, tile, width), lambda b, i: (b, i, 0))
    sds = lambda dtype: jax.ShapeDtypeStruct((batch, t, width), dtype)
    return pl.pallas_call(
        _mixer_in_kernel,
        grid=(batch, t // tile),
        in_specs=[pl.BlockSpec((1, tile, d), lambda b, i: (b, i, 0))] + [_resident(z.shape) for z in params],
        out_specs=[out] * 10,
        out_shape=[sds(BF16), sds(BF16), sds(F32), sds(F32)] + [sds(BF16)] * 6,
        scratch_shapes=[pltpu.VMEM((8, w_rw.shape[1]), F32)],
        compiler_params=_cparams("parallel", "arbitrary"),
        name="mixer_in",
    )(x3, *params)


def _wkv_kernel(r_ref, c_ref, lw_ref, k_ref, v_ref, a_ref, b_ref, y_ref, h_ref,
                q_s, v_s, kc_s, ark_s, bcrb_s, mz_s, rhs_s, sol_s, t_s, rh_s, rkv_s, kv_s, dec_s, hc_s, yc_s):
    @pl.when(pl.program_id(1) == 0)
    def _():
        h_ref[...] = jnp.zeros_like(h_ref)

    n = WKV_CHUNK
    two = 2 * n
    head0 = lax.broadcasted_iota(jnp.int32, (n, LANES), 1) < HEAD_DIM
    row = lax.broadcasted_iota(jnp.int32, (two, two), 0)
    col = lax.broadcasted_iota(jnp.int32, (two, two), 1)
    eye = row == col
    row2 = lax.broadcasted_iota(jnp.int32, (two, 2 * two), 0)
    col2 = lax.broadcasted_iota(jnp.int32, (two, 2 * two), 1) % two
    same2 = (row2 // n) == (col2 // n)
    strict2 = same2 & ((col2 % n) < (row2 % n))
    incl2 = same2 & ((col2 % n) <= (row2 % n))
    npairs = y_ref.shape[2] // LANES
    nchunks = y_ref.shape[1] // n
    items = [(ch, pair) for ch in range(nchunks) for pair in range(npairs)]

    def window(ch, pair):
        return slice(ch * n, (ch + 1) * n), slice(pair * LANES, (pair + 1) * LANES)

    def stack(z):
        return jnp.concatenate([jnp.where(head0, z, 0.0), jnp.where(head0, 0.0, z)], axis=0)

    def dup(z):
        zb = _bf(z)
        return jnp.concatenate([zb, zb], axis=0)

    def scores(it):
        ts, ls = window(*items[it])
        r, k, v, a, b = (ref[0, ts, ls].astype(F32) for ref in (r_ref, k_ref, v_ref, a_ref, b_ref))
        c = c_ref[0, ts, ls]
        cex = c - lw_ref[0, ts, ls]
        mid = c[n // 2 - 1:n // 2, :]
        last = c[n - 1:n, :]
        g_inv = jnp.exp(mid - c)
        g_end = jnp.exp(last - c)
        ar_st = _bf(jnp.concatenate([stack(a * jnp.exp(cex - mid)), stack(r * jnp.exp(c - mid))], axis=0))
        bk_dup = jnp.concatenate([dup(b * g_inv), dup(k * g_inv)], axis=0)
        sc = _dot_nt(ar_st, bk_dup)
        top = jnp.where(strict2, sc[:two], 0.0)
        bot = jnp.where(incl2, sc[two:], 0.0)
        t_s[it] = jnp.where(eye, 1.0, top[:, :two])
        q_s[it] = _bf(top[:, :two])
        bcrb_s[it] = _bf(jnp.concatenate([stack(b * g_end).T, bot[:, :two]], axis=0))
        ark_s[it] = _bf(jnp.concatenate([top[:, two:], bot[:, two:]], axis=0))
        v_s[it] = _bf(stack(v))
        kc_s[it] = _bf(stack(k * g_end))
        rh_s[it] = stack(r * jnp.exp(c))
        rhs_s[it, :, :LANES] = _bf(stack(a * jnp.exp(cex)))
        e_col = jnp.sum(jnp.where(eye, jnp.exp(last), 0.0), axis=1, keepdims=True)
        dec_s[it] = jnp.broadcast_to(e_col, (two, LANES))

    def values(it):
        v_st = v_s[it]
        both = _dot(ark_s[it], v_st)
        rhs_s[it, :, LANES:] = _bf(both[:two])
        rkv_s[it] = both[two:]
        kv_s[it] = _dot_tn(kc_s[it], v_st)

    levels = n.bit_length() - 1

    def inverse_level(level, it):
        q = q_s[it]
        t_acc = t_s[it]
        if level == 1:
            q = _bf(_dot(q, q))
        if level + 1 < levels:
            prod = _dot(q, jnp.concatenate([_bf(t_acc), q], axis=1))
            q_s[it] = _bf(prod[:, two:])
            t_s[it] = t_acc + prod[:, :two]
        else:
            t_s[it] = t_acc + _dot(q, _bf(t_acc))

    def solve(it):
        sol_s[it] = _bf(_dot(_bf(t_s[it]), rhs_s[it]))

    def fold(it):
        prod = _dot(bcrb_s[it], sol_s[it])
        mz_s[it] = _bf(jnp.concatenate([prod[:two, :LANES], rh_s[it] + prod[two:, :LANES]], axis=0))
        hc_s[it] = prod[:two, LANES:] + kv_s[it]
        yc_s[it] = prod[two:, LANES:] + rkv_s[it]

    def advance(it):
        ts, ls = window(*items[it])
        pair = items[it][1]
        h = h_ref[pair]
        gmat = _dot(mz_s[it], _bf(h))
        h_ref[pair] = dec_s[it] * h + gmat[:two] + hc_s[it]
        y_st = gmat[two:] + yc_s[it]
        y_ref[0, ts, ls] = y_st[:n] + y_st[n:]

    stages = ([scores, values] + [functools.partial(inverse_level, level) for level in range(1, levels)]
              + [solve, fold, advance])
    for stage in stages:
        for it in range(len(items)):
            stage(it)


def _wkv(r, c, lw, k, v, a, b):
    batch, t, width = r.shape
    npairs = width // LANES
    items = (WKV_STEP // WKV_CHUNK) * npairs
    spec = pl.BlockSpec((1, WKV_STEP, width), lambda bi, i: (bi, i, 0))
    sq = lambda dtype: pltpu.VMEM((items, LANES, LANES), dtype)
    return pl.pallas_call(
        _wkv_kernel,
        grid=(batch, t // WKV_STEP),
        in_specs=[spec] * 7,
        out_specs=spec,
        out_shape=jax.ShapeDtypeStruct((batch, t, width), F32),
        scratch_shapes=[pltpu.VMEM((npairs, LANES, LANES), F32)] + [sq(BF16)] * 3
                       + [pltpu.VMEM((items, 2 * LANES, LANES), BF16)] * 3
                       + [pltpu.VMEM((items, LANES, 2 * LANES), BF16)] * 2 + [sq(F32)] * 7,
        compiler_params=_cparams("parallel", "arbitrary"),
        name="wkv7",
    )(r, c, lw, k, v, a, b)


N_MIX_OUT_REFS = 9


def _mixer_out(x_ref, ygm_ref, ys_ref, bonus_ref, g_ref, lnw_ref, lnb_ref, wtop_ref, wbot_ref):
    width = ys_ref.shape[1]
    yn = _head_layernorm(ys_ref[...], _head_avg_matrix(), lnw_ref[...], lnb_ref[...], GN_EPS)
    y_rw = (yn + bonus_ref[...]) * g_ref[...]
    return x_ref[...] + _dot(ygm_ref[...], wtop_ref[...]) + _dot(_bf(y_rw), wbot_ref[...])


def _mixer_out_operands(x2, y_gm, y_scan, bonus, g, lnx_w, lnx_b, w_out):
    d = x2.shape[1]
    width = y_gm.shape[1]
    rows = lambda w: pl.BlockSpec((ROW_TILE, w), lambda i: (i, 0))
    w_out = _bf(w_out)
    specs = [rows(d), rows(width), rows(width), rows(width), rows(width),
             _resident((1, width)), _resident((1, width)), _resident((width, d)), _resident((width, d))]
    args = (x2, y_gm, y_scan, bonus, g, lnx_w.reshape(1, width), lnx_b.reshape(1, width), w_out[:width], w_out[width:])
    assert len(specs) == len(args) == N_MIX_OUT_REFS
    return specs, args


def _out_swiglu_kernel(*refs):
    mix_refs, (gain_ref, wg_ref, wu_ref, wd_ref, o_ref) = refs[:N_MIX_OUT_REFS], refs[N_MIX_OUT_REFS:]
    x = _mixer_out(*mix_refs)
    h = _bf(_rmsnorm(x, gain_ref[...]))
    act = jax.nn.silu(_dot(h, wg_ref[...])) * _dot(h, wu_ref[...])
    o_ref[...] = x + _dot(_bf(act), wd_ref[...])


def _ffn_tile(d_ff, parts):
    assert d_ff % (parts * LANES) == 0, f"d_ff={d_ff} does not split into {parts} lane-aligned tiles"
    return d_ff // parts


def _out_swiglu(mix_args, gain, wg, wu, wd):
    specs, args = _mixer_out_operands(*mix_args)
    n, d = args[0].shape
    d_ff = wg.shape[1]
    return pl.pallas_call(
        _out_swiglu_kernel,
        grid=(n // ROW_TILE,),
        in_specs=specs + [_resident((1, d)), _resident((d, d_ff)), _resident((d, d_ff)), _resident((d_ff, d))],
        out_specs=pl.BlockSpec((ROW_TILE, d), lambda i: (i, 0)),
        out_shape=jax.ShapeDtypeStruct((n, d), F32),
        compiler_params=_cparams("parallel"),
        name="out_dense_swiglu",
    )(*args, gain.reshape(1, d), _bf(wg), _bf(wu), _bf(wd))


META_E, META_G, META_RANK = 0, 2, 4


def _out_router_kernel(*refs):
    mix_refs = refs[:N_MIX_OUT_REFS]
    g_ref, whi_ref, wlo_ref, x_ref, h_ref, meta_ref, cnt_ref, run_ref = refs[N_MIX_OUT_REFS:]

    @pl.when(pl.program_id(0) == 0)
    def _():
        run_ref[...] = jnp.zeros_like(run_ref)

    x = _mixer_out(*mix_refs)
    x_ref[...] = x
    h = _rmsnorm(x, g_ref[...])
    h_ref[...] = _pack_bf16_pairs(h)
    rows = h.shape[0]
    lane = lax.broadcasted_iota(jnp.int32, (rows, LANES), 1)
    h_hi = _bf(h)
    h_lo = _bf(h - h_hi.astype(F32))
    logits = _dot(h_hi, whi_ref[...]) + _dot(h_lo, whi_ref[...]) + _dot(h_hi, wlo_ref[...])
    logits = jnp.where(lane < N_EXPERTS, logits, NEG_BIG)
    v1 = jnp.max(logits, axis=-1, keepdims=True)
    e1 = jnp.min(jnp.where(logits == v1, lane, LANES), axis=-1, keepdims=True)
    oh1 = lane == e1
    rest = jnp.where(oh1, NEG_BIG, logits)
    v2 = jnp.max(rest, axis=-1, keepdims=True)
    e2 = jnp.min(jnp.where(rest == v2, lane, LANES), axis=-1, keepdims=True)
    oh2 = lane == e2
    ex = jnp.exp(v2 - v1)
    g1 = 1.0 / (1.0 + ex)
    g2 = ex / (1.0 + ex)
    cnt = jnp.where(oh1 | oh2, 1.0, 0.0)
    t = lax.broadcasted_iota(jnp.int32, (rows, rows), 0)
    s = lax.broadcasted_iota(jnp.int32, (rows, rows), 1)
    before = _dot(jnp.where(s < t, 1.0, 0.0).astype(BF16), _bf(cnt)) + run_ref[0:1, :]
    rank1 = jnp.sum(jnp.where(oh1, before, 0.0), axis=-1, keepdims=True)
    rank2 = jnp.sum(jnp.where(oh2, before, 0.0), axis=-1, keepdims=True)
    fields = (e1.astype(F32), e2.astype(F32), g1, g2, rank1, rank2)
    meta = jnp.zeros((rows, LANES), F32)
    for idx, val in enumerate(fields):
        meta = jnp.where(lane == idx, val, meta)
    meta_ref[...] = meta
    run = run_ref[0:1, :] + jnp.sum(cnt, axis=0, keepdims=True)
    run_ref[...] = jnp.broadcast_to(run, run_ref.shape)
    cnt_ref[...] = jnp.broadcast_to(run, cnt_ref.shape)


def _out_router(mix_args, gain, w_router):
    specs, args = _mixer_out_operands(*mix_args)
    n, d = args[0].shape
    wr = jnp.concatenate([w_router, jnp.zeros((d, LANES - w_router.shape[1]), F32)], axis=1)
    wr_hi = _bf(wr)
    wr_lo = _bf(wr - wr_hi.astype(F32))
    rows = lambda w: pl.BlockSpec((ROW_TILE, w), lambda i: (i, 0))
    return pl.pallas_call(
        _out_router_kernel,
        grid=(n // ROW_TILE,),
        in_specs=specs + [_resident((1, d)), _resident((d, LANES)), _resident((d, LANES))],
        out_specs=[rows(d), rows(d // 2), rows(LANES), pl.BlockSpec((8, LANES), lambda i: (0, 0))],
        out_shape=[jax.ShapeDtypeStruct((n, d), F32), jax.ShapeDtypeStruct((n, d // 2), jnp.uint32),
                   jax.ShapeDtypeStruct((n, LANES), F32), jax.ShapeDtypeStruct((8, LANES), F32)],
        scratch_shapes=[pltpu.VMEM((8, LANES), F32)],
        compiler_params=_cparams("arbitrary"),
        name="out_moe_router",
    )(*args, gain.reshape(1, d), wr_hi, wr_lo)


def _sc_workers():
    info = plsc.get_sparse_core_info()
    return info.num_cores, info.num_cores * info.num_subcores


def _dispatch_rows(h, dest0, dest1, pad_rows):
    nc, workers = _sc_workers()
    n, d = h.shape
    n_pad = pad_rows.shape[0]
    per_worker = n // workers
    nchunk = per_worker // SC_WINDOW
    npad = n_pad // (workers * SC_WINDOW)
    assert n == workers * nchunk * SC_WINDOW and nchunk % 2 == 0 and n_pad == workers * npad * SC_WINDOW
    mesh = plsc.VectorSubcoreMesh(core_axis_name="c", subcore_axis_name="s")
    idx_t = lambda m: pltpu.VMEM((m, SC_WINDOW), jnp.int32)
    buf_t = pltpu.VMEM((SC_WINDOW, d), h.dtype)

    @functools.partial(
        pl.kernel, mesh=mesh,
        out_type=jax.ShapeDtypeStruct((2 * n + n_pad, d), h.dtype),
        scratch_types=[idx_t(nchunk), idx_t(nchunk), idx_t(npad), buf_t, buf_t, buf_t]
                      + [pltpu.SemaphoreType.DMA] * 6,
    )
    def dispatch_kernel(h_hbm, d0_hbm, d1_hbm, pad_hbm, zero_hbm, out_hbm, d0_v, d1_v, pad_v, buf0, buf1, zbuf,
                        r0, r1, w0, w1, x0, x1):
        wid = lax.axis_index("s") * nc + lax.axis_index("c")
        base = wid * per_worker
        pltpu.sync_copy(d0_hbm.at[wid], d0_v)
        pltpu.sync_copy(d1_hbm.at[wid], d1_v)
        pltpu.sync_copy(pad_hbm.at[wid], pad_v)
        pltpu.sync_copy(zero_hbm, zbuf)
        bufs, rsem, wsem, xsem = (buf0, buf1), (r0, r1), (w0, w1), (x0, x1)

        def read(c, slot):
            return pltpu.make_async_copy(h_hbm.at[pl.ds(base + c * SC_WINDOW, SC_WINDOW)], bufs[slot], rsem[slot])

        read(0, 0).start()
        read(1, 1).start()
        for pc in range(npad):
            pltpu.sync_copy(zbuf, out_hbm.at[pad_v.at[pc]])

        @pl.loop(0, nchunk, step=2)
        def _(c):
            for slot in range(2):
                cc = c + slot
                read(cc, slot).wait()
                first = pltpu.make_async_copy(bufs[slot], out_hbm.at[d0_v.at[cc]], wsem[slot])
                second = pltpu.make_async_copy(bufs[slot], out_hbm.at[d1_v.at[cc]], xsem[slot])
                first.start()
                second.start()
                first.wait()
                second.wait()

                @pl.when(cc + 2 < nchunk)
                def _():
                    read(cc + 2, slot).start()

    split = lambda z, m: z.reshape(workers, m, SC_WINDOW)
    return dispatch_kernel(h, split(dest0, nchunk), split(dest1, nchunk), split(pad_rows, npad),
                           jnp.zeros((SC_WINDOW, d), h.dtype))


def _gather_rows(src, idx):
    info = plsc.get_sparse_core_info()
    nc, ns = info.num_cores, info.num_subcores
    workers = nc * ns
    m, d = idx.shape[0], src.shape[1]
    per_worker = m // workers
    nchunk = per_worker // SC_WINDOW
    assert m == workers * nchunk * SC_WINDOW and nchunk % 2 == 0, "row count must split evenly over subcores"
    mesh = plsc.VectorSubcoreMesh(core_axis_name="c", subcore_axis_name="s")

    @functools.partial(
        pl.kernel, mesh=mesh,
        out_type=jax.ShapeDtypeStruct((m, d), src.dtype),
        scratch_types=[pltpu.VMEM((nchunk, SC_WINDOW), jnp.int32),
                       pltpu.VMEM((SC_WINDOW, d), src.dtype), pltpu.VMEM((SC_WINDOW, d), src.dtype),
                       pltpu.SemaphoreType.DMA, pltpu.SemaphoreType.DMA,
                       pltpu.SemaphoreType.DMA, pltpu.SemaphoreType.DMA],
    )
    def gather_kernel(src_hbm, idx_hbm, out_hbm, idx_v, buf0, buf1, g0, g1, w0, w1):
        wid = lax.axis_index("s") * nc + lax.axis_index("c")
        base = wid * per_worker
        pltpu.sync_copy(idx_hbm.at[wid], idx_v)
        bufs, gsem, wsem = (buf0, buf1), (g0, g1), (w0, w1)

        def gather(c, slot):
            return pltpu.make_async_copy(src_hbm.at[idx_v.at[c]], bufs[slot], gsem[slot])

        def write(c, slot):
            return pltpu.make_async_copy(bufs[slot], out_hbm.at[pl.ds(base + c * SC_WINDOW, SC_WINDOW)], wsem[slot])

        gather(0, 0).start()
        gather(1, 1).start()

        @pl.loop(0, nchunk, step=2)
        def _(c):
            for slot in range(2):
                cc = c + slot
                gather(cc, slot).wait()
                write(cc, slot).start()
                write(cc, slot).wait()

                @pl.when(cc + 2 < nchunk)
                def _():
                    gather(cc + 2, slot).start()

    return gather_kernel(src, idx.reshape(workers, nchunk, SC_WINDOW))


def _pack_bf16_pairs(y):
    bits = lambda z: pltpu.bitcast(z.astype(BF16).astype(F32), jnp.uint32)
    words = []
    for t in range(0, y.shape[1] // LANES, 2):
        lo = bits(y[:, t * LANES:(t + 1) * LANES])
        hi = bits(y[:, (t + 1) * LANES:(t + 2) * LANES])
        words.append((hi & jnp.uint32(0xFFFF0000)) | (lo >> 16))
    return jnp.concatenate(words, axis=1)


def _unpack_bf16_pairs(words):
    tiles = []
    for t in range(words.shape[1] // LANES):
        w = words[:, t * LANES:(t + 1) * LANES]
        tiles += [pltpu.bitcast(w << 16, F32), pltpu.bitcast(w & jnp.uint32(0xFFFF0000), F32)]
    return jnp.concatenate(tiles, axis=1)


def _expert_kernel(be_ref, nu_ref, x_ref, wg_ref, wu_ref, wd_ref, o_ref, h_ref, acc_ref):
    i, j = pl.program_id(0), pl.program_id(1)
    last = pl.num_programs(1) - 1

    @pl.when(i < nu_ref[0])
    def _():
        @pl.when(j == 0)
        def _():
            h_ref[...] = _bf(_unpack_bf16_pairs(x_ref[...]))

        h = h_ref[...]
        act = jax.nn.silu(_dot(h, wg_ref[0])) * _dot(h, wu_ref[0])
        part = _dot(_bf(act), wd_ref[0])

        @pl.when(j == 0)
        def _():
            acc_ref[...] = part

        @pl.when((j > 0) & (j < last))
        def _():
            acc_ref[...] += part

        @pl.when(j == last)
        def _():
            o_ref[...] = _pack_bf16_pairs(acc_ref[...] + part)

    @pl.when((i >= nu_ref[0]) & (j == 0))
    def _():
        o_ref[...] = jnp.zeros_like(o_ref)


def _expert_swiglu(xb, block_expert, n_used, wg, wu, wd):
    n_rows, d = xb.shape[0], 2 * xb.shape[1]
    d_ff = wg.shape[2]
    tf = _ffn_tile(d_ff, 2)
    nf = d_ff // tf
    assert nf >= 2, "the packed store happens on a d_ff step after the first"
    nblk = n_rows // MOE_BLOCK

    def row(i, j, be, nu):
        return jnp.minimum(i, nu[0] - 1)

    def col(i, j, be, nu):
        return jnp.where(i < nu[0], j, nf - 1)

    grid_spec = pltpu.PrefetchScalarGridSpec(
        num_scalar_prefetch=2,
        grid=(nblk, nf),
        in_specs=[pl.BlockSpec((MOE_BLOCK, d // 2), lambda i, j, be, nu: (row(i, j, be, nu), 0)),
                  pl.BlockSpec((1, d, tf), lambda i, j, be, nu: (be[row(i, j, be, nu)], 0, col(i, j, be, nu))),
                  pl.BlockSpec((1, d, tf), lambda i, j, be, nu: (be[row(i, j, be, nu)], 0, col(i, j, be, nu))),
                  pl.BlockSpec((1, tf, d), lambda i, j, be, nu: (be[row(i, j, be, nu)], col(i, j, be, nu), 0))],
        out_specs=pl.BlockSpec((MOE_BLOCK, d // 2), lambda i, j, be, nu: (i, 0)),
        scratch_shapes=[pltpu.VMEM((MOE_BLOCK, d), BF16), pltpu.VMEM((MOE_BLOCK, d), F32)],
    )
    return pl.pallas_call(
        _expert_kernel,
        grid_spec=grid_spec,
        out_shape=jax.ShapeDtypeStruct((n_rows, d // 2), jnp.uint32),
        compiler_params=_cparams("arbitrary", "arbitrary"),
        name="expert_swiglu",
    )(block_expert, n_used, xb, _bf(wg), _bf(wu), _bf(wd))


def _combine_kernel(x_ref, y1_ref, y2_ref, meta_ref, g_ref, o_ref):
    meta = meta_ref[...]
    lane = lax.broadcasted_iota(jnp.int32, meta.shape, 1)
    g1 = jnp.sum(jnp.where(lane == META_G, meta, 0.0), axis=-1, keepdims=True)
    g2 = jnp.sum(jnp.where(lane == META_G + 1, meta, 0.0), axis=-1, keepdims=True)
    y = x_ref[...] + (g1 * _unpack_bf16_pairs(y1_ref[...]) + g2 * _unpack_bf16_pairs(y2_ref[...]))
    o_ref[...] = _rmsnorm(y, g_ref[...])


def _combine_final_norm(x2, yg, meta, gain):
    n, d = x2.shape
    nblk = n // ROW_TILE
    return pl.pallas_call(
        _combine_kernel,
        grid=(nblk,),
        in_specs=[pl.BlockSpec((ROW_TILE, d), lambda i: (i, 0)),
                  pl.BlockSpec((ROW_TILE, d // 2), lambda i: (i, 0)),
                  pl.BlockSpec((ROW_TILE, d // 2), lambda i: (i + nblk, 0)),
                  pl.BlockSpec((ROW_TILE, LANES), lambda i: (i, 0)),
                  pl.BlockSpec((1, d), lambda i: (0, 0))],
        out_specs=pl.BlockSpec((ROW_TILE, d), lambda i: (i, 0)),
        out_shape=jax.ShapeDtypeStruct((n, d), F32),
        compiler_params=_cparams("parallel"),
        name="moe_combine_norm",
    )(x2, yg, yg, meta, gain.reshape(1, d))


def _moe_layer(mix_args, gain, w_router, wg, wu, wd, final_gain):
    x2, h, meta, counts = _out_router(mix_args, gain, w_router)
    n, d = x2.shape
    expert = meta[:, META_E:META_E + 2].astype(jnp.int32)
    rank = meta[:, META_RANK:META_RANK + 2].astype(jnp.int32)
    count = counts[0, :N_EXPERTS].astype(jnp.int32)
    padded = (count + MOE_BLOCK - 1) // MOE_BLOCK * MOE_BLOCK
    pad_end = jnp.cumsum(padded)
    pad_start = pad_end - padded
    dest = pad_start[expert] + rank
    n_rows = (2 * n // MOE_BLOCK + N_EXPERTS) * MOE_BLOCK
    empties = padded - count
    e_end = jnp.cumsum(empties)
    i = jnp.arange(n_rows - 2 * n, dtype=jnp.int32)
    grp = jnp.sum(i[:, None] >= e_end[None, :], axis=1).astype(jnp.int32)
    first_empty = jnp.concatenate([pad_start + count, pad_end[-1:]])
    first_index = jnp.concatenate([e_end - empties, e_end[-1:]])
    pad_rows = (first_empty[grp] + i - first_index[grp]).astype(jnp.int32)
    block_start = jnp.arange(n_rows // MOE_BLOCK, dtype=jnp.int32) * MOE_BLOCK
    block_expert = jnp.minimum(jnp.sum(block_start[:, None] >= pad_end[None, :], axis=1), N_EXPERTS - 1)
    block_expert = block_expert.astype(jnp.int32)
    n_used = (pad_end[-1:] // MOE_BLOCK).astype(jnp.int32)
    xb = _dispatch_rows(h, dest[:, 0], dest[:, 1], pad_rows)
    ys = _expert_swiglu(xb, block_expert, n_used, wg, wu, wd)
    yg = _gather_rows(ys, jnp.concatenate([dest[:, 0], dest[:, 1]]))
    return _combine_final_norm(x2, yg, meta, final_gain)


def _mixer_layer(x2, batch, norm_mix, w_in, w_out, shift_mu, gm_ln_w, gm_ln_b, gm_ws, gm_bs,
                 rw_w_up, rw_w0, rw_a_up, rw_a0, rw_g_up, rw_k_k, rw_k_a, rw_r_k, rw_lnx_w, rw_lnx_b):
    n, d = x2.shape
    gm2 = 2 * gm_ln_w.shape[0]
    w_in = _bf(w_in)
    y_gm, r, c, lw, k, v, a, b, g, bonus = _mixer_in(
        x2.reshape(batch, n // batch, d), norm_mix, w_in[:, :gm2], w_in[:, gm2:], gm_ln_w, gm_ln_b, gm_ws, gm_bs,
        shift_mu, rw_w_up, rw_w0, rw_a_up, rw_a0, rw_g_up, rw_k_k, rw_k_a, rw_r_k.reshape(-1))
    y_scan = _wkv(r, c, lw, k, v, a, b)
    flat = lambda z: z.reshape(n, -1)
    return x2, flat(y_gm), flat(y_scan), flat(bonus), flat(g), rw_lnx_w, rw_lnx_b, w_out


def kernel(x, norm_mix, w_in, w_out, shift_mu, gm_ln_w, gm_ln_b, gm_ws, gm_bs, rw_w_up, rw_w0, rw_a_up, rw_a0,
           rw_g_up, rw_k_k, rw_k_a, rw_r_k, rw_lnx_w, rw_lnx_b, norm_ffn, ffn_w_gate, ffn_w_up, ffn_w_down,
           moe_router, moe_w_gate, moe_w_up, moe_w_down, norm_final):
    batch, t, d = x.shape
    depth = norm_mix.shape[0]
    assert depth == 2 and t % ROW_TILE == 0, "two layers (dense then MoE), sequence a multiple of the row tile"
    x2 = x.reshape(batch * t, d)
    for i in range(depth):
        mixed = _mixer_layer(x2, batch, norm_mix[i], w_in[i], w_out[i], shift_mu[i], gm_ln_w[i], gm_ln_b[i],
                             gm_ws[i], gm_bs[i], rw_w_up[i], rw_w0[i], rw_a_up[i], rw_a0[i], rw_g_up[i],
                             rw_k_k[i], rw_k_a[i], rw_r_k[i], rw_lnx_w[i], rw_lnx_b[i])
        if i % 2 == 0:
            x2 = _out_swiglu(mixed, norm_ffn[i], ffn_w_gate[i // 2], ffn_w_up[i // 2], ffn_w_down[i // 2])
        else:
            x2 = _moe_layer(mixed, norm_ffn[i], moe_router[i // 2], moe_w_gate[i // 2], moe_w_up[i // 2],
                            moe_w_down[i // 2], norm_final)
    return x2.reshape(batch, t, d)
```

```python
import functools

import jax
import jax.numpy as jnp
from jax import lax
from jax.experimental import pallas as pl
from jax.experimental.pallas import tpu as pltpu
from jax.experimental.pallas import tpu_sc as plsc

F32 = jnp.float32
BF16 = jnp.bfloat16

HEAD_DIM = 64
LANES = 128
MXU_TILE = 256
GM_CHUNK = 128
WKV_CHUNK = 64
N_EXPERTS = 8
RMS_EPS = 1e-6
LN_EPS = 1e-5
GN_EPS = 64e-5
NEG_BIG = -1e30

ROW_TILE = 512
MIX_TILE = 1024
WKV_STEP = 512
MOE_BLOCK = 512
SC_WINDOW = 32
VMEM_LIMIT = 56 * 1024 * 1024


def _cparams(*sem):
    return pltpu.CompilerParams(dimension_semantics=sem, vmem_limit_bytes=VMEM_LIMIT)


def _bf(x):
    return x.astype(BF16)


def _dot(a, b):
    return jnp.dot(a, b, preferred_element_type=F32)


def _dot_nt(a, b):
    return lax.dot_general(a, b, (((1,), (1,)), ((), ())), preferred_element_type=F32)


def _dot_tn(a, b):
    return lax.dot_general(a, b, (((0,), (0,)), ((), ())), preferred_element_type=F32)


def _split_dot_left(m, x, parts=2):
    acc = None
    rem = x
    for p in range(parts):
        hi = _bf(rem)
        d = _dot(m, hi)
        acc = d if acc is None else acc + d
        if p + 1 < parts:
            rem = rem - hi.astype(F32)
    return acc


def _rmsnorm(x, g):
    return x * lax.rsqrt(jnp.mean(x * x, axis=-1, keepdims=True) + RMS_EPS) * g


def _gelu(x):
    return 0.5 * x * (1.0 + lax.erf(x * (2.0 ** -0.5)))


def _head_avg_matrix():
    r = lax.broadcasted_iota(jnp.int32, (MXU_TILE, MXU_TILE), 0) // HEAD_DIM
    c = lax.broadcasted_iota(jnp.int32, (MXU_TILE, MXU_TILE), 1) // HEAD_DIM
    return jnp.where(r == c, 1.0 / HEAD_DIM, 0.0).astype(BF16)


def _head_dot(x, m):
    return jnp.concatenate([_dot(x[:, i:i + MXU_TILE], m) for i in range(0, x.shape[1], MXU_TILE)], axis=1)


def _head_layernorm(x, avg, w, b, eps):
    mu = _head_dot(_bf(x), avg)
    xc = x - mu
    var = _head_dot(_bf(xc * xc), avg)
    return xc * lax.rsqrt(var + eps) * w + b


def _resident(shape):
    return pl.BlockSpec(shape, lambda *_: (0,) * len(shape), pipeline_mode=pl.Buffered(1))


def _gmlp_chunk(p, avg, lnw, lnb, w2, bias, head0):
    width = p.shape[1] // 2
    u = _gelu(p[:, :width])
    vn = _head_layernorm(_gelu(p[:, width:]), avg, lnw, lnb, LN_EPS)
    mixed = []
    for pair, w in enumerate(w2):
        vp = vn[:, pair * LANES:(pair + 1) * LANES]
        v_st = _bf(jnp.concatenate([jnp.where(head0, vp, 0.0), jnp.where(head0, 0.0, vp)], axis=0))
        mixed.append(_dot(w, v_st))
    return u * (jnp.concatenate(mixed, axis=1) + bias)


def _mixer_in_kernel(x_ref, gain_ref, wgm_ref, wrw_ref, lnw_ref, lnb_ref, ws_ref, bias_ref,
                     mu_ref, wup_ref, w0_ref, aup_ref, a0_ref, gup_ref, kk_ref, ka_ref, rk_ref,
                     ygm_ref, r_ref, c_ref, lw_ref, k_ref, v_ref, a_ref, b_ref, g_ref, bonus_ref, carry_ref):
    width = r_ref.shape[2]
    rows = x_ref.shape[1]

    @pl.when(pl.program_id(1) == 0)
    def _():
        carry_ref[...] = jnp.zeros_like(carry_ref)

    avg = _head_avg_matrix()
    ones = avg * HEAD_DIM
    t = lax.broadcasted_iota(jnp.int32, (GM_CHUNK, GM_CHUNK), 0)
    s = lax.broadcasted_iota(jnp.int32, (GM_CHUNK, GM_CHUNK), 1)
    causal = s <= t
    head0 = lax.broadcasted_iota(jnp.int32, (GM_CHUNK, LANES), 1) < HEAD_DIM
    w2 = [_bf(jnp.concatenate([jnp.where(causal, ws_ref[2 * pair], 0.0),
                               jnp.where(causal, ws_ref[2 * pair + 1], 0.0)], axis=1))
          for pair in range(width // LANES)]
    t = lax.broadcasted_iota(jnp.int32, (MXU_TILE, MXU_TILE), 0)
    s = lax.broadcasted_iota(jnp.int32, (MXU_TILE, MXU_TILE), 1)
    tri = ((t // WKV_CHUNK == s // WKV_CHUNK) & (s <= t)).astype(BF16)
    first = lax.broadcasted_iota(jnp.int32, (ROW_TILE, wrw_ref.shape[1]), 0) == 0

    subs = [slice(i * ROW_TILE, (i + 1) * ROW_TILE) for i in range(rows // ROW_TILE)]
    proj = []
    for rs in subs:
        h = _bf(_rmsnorm(x_ref[0, rs, :], gain_ref[...]))
        proj.append((_dot(h, wgm_ref[...]), _dot(h, wrw_ref[...])))

    last_row = carry_ref[0:1, :]
    for rs, (p_gm, p) in zip(subs, proj):
        for ch in range(ROW_TILE // GM_CHUNK):
            ts = slice(ch * GM_CHUNK, (ch + 1) * GM_CHUNK)
            out_rows = slice(rs.start + ts.start, rs.start + ts.stop)
            ygm_ref[0, out_rows, :] = _bf(_gmlp_chunk(p_gm[ts], avg, lnw_ref[...], lnb_ref[...], w2, bias_ref[...],
                                                      head0))
        prev = jnp.where(first, last_row, pltpu.roll(p, 1, axis=0))
        last_row = p[ROW_TILE - 1:ROW_TILE, :]
        ps = p + (prev - p) * mu_ref[...]
        r = ps[:, :width]
        k = ps[:, width:2 * width]
        v = ps[:, 2 * width:3 * width]
        lora_in = ps[:, 3 * width:3 * width + LANES]
        gd = ps[:, 3 * width + LANES:]
        lw = -(jnp.exp(-0.5)) * jax.nn.sigmoid(w0_ref[...] + _dot(_bf(jnp.tanh(lora_in)), wup_ref[...]))
        a_lr = jax.nn.sigmoid(a0_ref[...] + _dot(_bf(lora_in), aup_ref[...]))
        kk = k * kk_ref[...]
        kk = kk * lax.rsqrt(jnp.maximum(_head_dot(_bf(kk * kk), ones), 1e-24))
        kmod = k * (1.0 + (a_lr - 1.0) * ka_ref[...])
        lw_ref[0, rs, :] = lw
        c_ref[0, rs, :] = jnp.concatenate([_split_dot_left(tri, lw[i:i + MXU_TILE])
                                           for i in range(0, ROW_TILE, MXU_TILE)], axis=0)
        r_ref[0, rs, :] = _bf(r)
        k_ref[0, rs, :] = _bf(kmod)
        v_ref[0, rs, :] = _bf(v)
        a_ref[0, rs, :] = _bf(-kk)
        b_ref[0, rs, :] = _bf(kk * a_lr)
        g_ref[0, rs, :] = _bf(_dot(_bf(jax.nn.sigmoid(gd)), gup_ref[...]))
        bonus_ref[0, rs, :] = _bf(_head_dot(_bf(r * kmod * rk_ref[...]), ones) * v)
    carry_ref[0:1, :] = last_row


def _mixer_in(x3, gain, w_gm, w_rw, ln_w, ln_b, ws, bs, mu, w_up, w0, a_up, a0, g_up, k_k, k_a, r_k):
    batch, t, d = x3.shape
    width = w0.shape[0]
    rank = w_up.shape[0]
    zeros = jnp.zeros((LANES - rank, width), F32)
    wup_pad = _bf(jnp.concatenate([w_up, zeros], axis=0))
    aup_pad = _bf(jnp.concatenate([zeros, a_up], axis=0))
    bias = jnp.repeat(bs.T, HEAD_DIM, axis=1)
    row = lambda z: z.reshape(1, -1)
    params = (row(gain), w_gm, w_rw, row(ln_w), row(ln_b), ws, bias, row(mu), wup_pad, row(w0), aup_pad, row(a0),
              _bf(g_up), row(k_k), row(k_a), row(r_k))
    tile = min(MIX_TILE, t)
    out = pl.BlockSpec((1, tile, width), lambda b, i: (b, i, 0))
    sds = lambda dtype: jax.ShapeDtypeStruct((batch, t, width), dtype)
    return pl.pallas_call(
        _mixer_in_kernel,
        grid=(batch, t // tile),
        in_specs=[pl.BlockSpec((1, tile, d), lambda b, i: (b, i, 0))] + [_resident(z.shape) for z in params],
        out_specs=[out] * 10,
        out_shape=[sds(BF16), sds(BF16), sds(F32), sds(F32)] + [sds(BF16)] * 6,
        scratch_shapes=[pltpu.VMEM((8, w_rw.shape[1]), F32)],
        compiler_params=_cparams("parallel", "arbitrary"),
        name="mixer_in",
    )(x3, *params)


def _wkv_kernel(r_ref, c_ref, lw_ref, k_ref, v_ref, a_ref, b_ref, y_ref, h_ref,
                q_s, v_s, kc_s, ark_s, bcrb_s, mz_s, rhs_s, sol_s, t_s, rh_s, rkv_s, kv_s, dec_s, hc_s, yc_s):
    @pl.when(pl.program_id(1) == 0)
    def _():
        h_ref[...] = jnp.zeros_like(h_ref)

    n = WKV_CHUNK
    two = 2 * n
    head0 = lax.broadcasted_iota(jnp.int32, (n, LANES), 1) < HEAD_DIM
    row = lax.broadcasted_iota(jnp.int32, (two, two), 0)
    col = lax.broadcasted_iota(jnp.int32, (two, two), 1)
    eye = row == col
    row2 = lax.broadcasted_iota(jnp.int32, (two, 2 * two), 0)
    col2 = lax.broadcasted_iota(jnp.int32, (two, 2 * two), 1) % two
    same2 = (row2 // n) == (col2 // n)
    strict2 = same2 & ((col2 % n) < (row2 % n))
    incl2 = same2 & ((col2 % n) <= (row2 % n))
    npairs = y_ref.shape[2] // LANES
    nchunks = y_ref.shape[1] // n
    items = [(ch, pair) for ch in range(nchunks) for pair in range(npairs)]

    def window(ch, pair):
        return slice(ch * n, (ch + 1) * n), slice(pair * LANES, (pair + 1) * LANES)

    def stack(z):
        return jnp.concatenate([jnp.where(head0, z, 0.0), jnp.where(head0, 0.0, z)], axis=0)

    def dup(z):
        zb = _bf(z)
        return jnp.concatenate([zb, zb], axis=0)

    def scores(it):
        ts, ls = window(*items[it])
        r, k, v, a, b = (ref[0, ts, ls].astype(F32) for ref in (r_ref, k_ref, v_ref, a_ref, b_ref))
        c = c_ref[0, ts, ls]
        cex = c - lw_ref[0, ts, ls]
        mid = c[n // 2 - 1:n // 2, :]
        last = c[n - 1:n, :]
        g_inv = jnp.exp(mid - c)
        g_end = jnp.exp(last - c)
        ar_st = _bf(jnp.concatenate([stack(a * jnp.exp(cex - mid)), stack(r * jnp.exp(c - mid))], axis=0))
        bk_dup = jnp.concatenate([dup(b * g_inv), dup(k * g_inv)], axis=0)
        sc = _dot_nt(ar_st, bk_dup)
        top = jnp.where(strict2, sc[:two], 0.0)
        bot = jnp.where(incl2, sc[two:], 0.0)
        t_s[it] = jnp.where(eye, 1.0, top[:, :two])
        q_s[it] = _bf(top[:, :two])
        bcrb_s[it] = _bf(jnp.concatenate([stack(b * g_end).T, bot[:, :two]], axis=0))
        ark_s[it] = _bf(jnp.concatenate([top[:, two:], bot[:, two:]], axis=0))
        v_s[it] = _bf(stack(v))
        kc_s[it] = _bf(stack(k * g_end))
        rh_s[it] = stack(r * jnp.exp(c))
        rhs_s[it, :, :LANES] = _bf(stack(a * jnp.exp(cex)))
        e_col = jnp.sum(jnp.where(eye, jnp.exp(last), 0.0), axis=1, keepdims=True)
        dec_s[it] = jnp.broadcast_to(e_col, (two, LANES))

    def values(it):
        v_st = v_s[it]
        both = _dot(ark_s[it], v_st)
        rhs_s[it, :, LANES:] = _bf(both[:two])
        rkv_s[it] = both[two:]
        kv_s[it] = _dot_tn(kc_s[it], v_st)

    levels = n.bit_length() - 1

    def inverse_level(level, it):
        q = q_s[it]
        t_acc = t_s[it]
        if level == 1:
            q = _bf(_dot(q, q))
        if level + 1 < levels:
            prod = _dot(q, jnp.concatenate([_bf(t_acc), q], axis=1))
            q_s[it] = _bf(prod[:, two:])
            t_s[it] = t_acc + prod[:, :two]
        else:
            t_s[it] = t_acc + _dot(q, _bf(t_acc))

    def solve(it):
        sol_s[it] = _bf(_dot(_bf(t_s[it]), rhs_s[it]))

    def fold(it):
        prod = _dot(bcrb_s[it], sol_s[it])
        mz_s[it] = _bf(jnp.concatenate([prod[:two, :LANES], rh_s[it] + prod[two:, :LANES]], axis=0))
        hc_s[it] = prod[:two, LANES:] + kv_s[it]
        yc_s[it] = prod[two:, LANES:] + rkv_s[it]

    def advance(it):
        ts, ls = window(*items[it])
        pair = items[it][1]
        h = h_ref[pair]
        gmat = _dot(mz_s[it], _bf(h))
        h_ref[pair] = dec_s[it] * h + gmat[:two] + hc_s[it]
        y_st = gmat[two:] + yc_s[it]
        y_ref[0, ts, ls] = y_st[:n] + y_st[n:]

    stages = ([scores, values] + [functools.partial(inverse_level, level) for level in range(1, levels)]
              + [solve, fold, advance])
    for stage in stages:
        for it in range(len(items)):
            stage(it)


def _wkv(r, c, lw, k, v, a, b):
    batch, t, width = r.shape
    npairs = width // LANES
    items = (WKV_STEP // WKV_CHUNK) * npairs
    spec = pl.BlockSpec((1, WKV_STEP, width), lambda bi, i: (bi, i, 0))
    sq = lambda dtype: pltpu.VMEM((items, LANES, LANES), dtype)
    return pl.pallas_call(
        _wkv_kernel,
        grid=(batch, t // WKV_STEP),
        in_specs=[spec] * 7,
        out_specs=spec,
        out_shape=jax.ShapeDtypeStruct((batch, t, width), F32),
        scratch_shapes=[pltpu.VMEM((npairs, LANES, LANES), F32)] + [sq(BF16)] * 3
                       + [pltpu.VMEM((items, 2 * LANES, LANES), BF16)] * 3
                       + [pltpu.VMEM((items, LANES, 2 * LANES), BF16)] * 2 + [sq(F32)] * 7,
        compiler_params=_cparams("parallel", "arbitrary"),
        name="wkv7",
    )(r, c, lw, k, v, a, b)


N_MIX_OUT_REFS = 9


def _mixer_out(x_ref, ygm_ref, ys_ref, bonus_ref, g_ref, lnw_ref, lnb_ref, wtop_ref, wbot_ref):
    width = ys_ref.shape[1]
    yn = _head_layernorm(ys_ref[...], _head_avg_matrix(), lnw_ref[...], lnb_ref[...], GN_EPS)
    y_rw = (yn + bonus_ref[...]) * g_ref[...]
    return x_ref[...] + _dot(ygm_ref[...], wtop_ref[...]) + _dot(_bf(y_rw), wbot_ref[...])


def _mixer_out_operands(x2, y_gm, y_scan, bonus, g, lnx_w, lnx_b, w_out):
    d = x2.shape[1]
    width = y_gm.shape[1]
    rows = lambda w: pl.BlockSpec((ROW_TILE, w), lambda i: (i, 0))
    w_out = _bf(w_out)
    specs = [rows(d), rows(width), rows(width), rows(width), rows(width),
             _resident((1, width)), _resident((1, width)), _resident((width, d)), _resident((width, d))]
    args = (x2, y_gm, y_scan, bonus, g, lnx_w.reshape(1, width), lnx_b.reshape(1, width), w_out[:width], w_out[width:])
    assert len(specs) == len(args) == N_MIX_OUT_REFS
    return specs, args


def _out_swiglu_kernel(*refs):
    mix_refs, (gain_ref, wg_ref, wu_ref, wd_ref, o_ref) = refs[:N_MIX_OUT_REFS], refs[N_MIX_OUT_REFS:]
    x = _mixer_out(*mix_refs)
    h = _bf(_rmsnorm(x, gain_ref[...]))
    act = jax.nn.silu(_dot(h, wg_ref[...])) * _dot(h, wu_ref[...])
    o_ref[...] = x + _dot(_bf(act), wd_ref[...])


def _ffn_tile(d_ff, parts):
    assert d_ff % (parts * LANES) == 0, f"d_ff={d_ff} does not split into {parts} lane-aligned tiles"
    return d_ff // parts


def _out_swiglu(mix_args, gain, wg, wu, wd):
    specs, args = _mixer_out_operands(*mix_args)
    n, d = args[0].shape
    d_ff = wg.shape[1]
    return pl.pallas_call(
        _out_swiglu_kernel,
        grid=(n // ROW_TILE,),
        in_specs=specs + [_resident((1, d)), _resident((d, d_ff)), _resident((d, d_ff)), _resident((d_ff, d))],
        out_specs=pl.BlockSpec((ROW_TILE, d), lambda i: (i, 0)),
        out_shape=jax.ShapeDtypeStruct((n, d), F32),
        compiler_params=_cparams("parallel"),
        name="out_dense_swiglu",
    )(*args, gain.reshape(1, d), _bf(wg), _bf(wu), _bf(wd))


META_E, META_G, META_RANK = 0, 2, 4


def _out_router_kernel(*refs):
    mix_refs = refs[:N_MIX_OUT_REFS]
    g_ref, whi_ref, wlo_ref, x_ref, h_ref, meta_ref, cnt_ref, run_ref = refs[N_MIX_OUT_REFS:]

    @pl.when(pl.program_id(0) == 0)
    def _():
        run_ref[...] = jnp.zeros_like(run_ref)

    x = _mixer_out(*mix_refs)
    x_ref[...] = x
    h = _rmsnorm(x, g_ref[...])
    h_ref[...] = _pack_bf16_pairs(h)
    rows = h.shape[0]
    lane = lax.broadcasted_iota(jnp.int32, (rows, LANES), 1)
    h_hi = _bf(h)
    h_lo = _bf(h - h_hi.astype(F32))
    logits = _dot(h_hi, whi_ref[...]) + _dot(h_lo, whi_ref[...]) + _dot(h_hi, wlo_ref[...])
    logits = jnp.where(lane < N_EXPERTS, logits, NEG_BIG)
    v1 = jnp.max(logits, axis=-1, keepdims=True)
    e1 = jnp.min(jnp.where(logits == v1, lane, LANES), axis=-1, keepdims=True)
    oh1 = lane == e1
    rest = jnp.where(oh1, NEG_BIG, logits)
    v2 = jnp.max(rest, axis=-1, keepdims=True)
    e2 = jnp.min(jnp.where(rest == v2, lane, LANES), axis=-1, keepdims=True)
    oh2 = lane == e2
    ex = jnp.exp(v2 - v1)
    g1 = 1.0 / (1.0 + ex)
    g2 = ex / (1.0 + ex)
    cnt = jnp.where(oh1 | oh2, 1.0, 0.0)
    t = lax.broadcasted_iota(jnp.int32, (rows, rows), 0)
    s = lax.broadcasted_iota(jnp.int32, (rows, rows), 1)
    before = _dot(jnp.where(s < t, 1.0, 0.0).astype(BF16), _bf(cnt)) + run_ref[0:1, :]
    rank1 = jnp.sum(jnp.where(oh1, before, 0.0), axis=-1, keepdims=True)
    rank2 = jnp.sum(jnp.where(oh2, before, 0.0), axis=-1, keepdims=True)
    fields = (e1.astype(F32), e2.astype(F32), g1, g2, rank1, rank2)
    meta = jnp.zeros((rows, LANES), F32)
    for idx, val in enumerate(fields):
        meta = jnp.where(lane == idx, val, meta)
    meta_ref[...] = meta
    run = run_ref[0:1, :] + jnp.sum(cnt, axis=0, keepdims=True)
    run_ref[...] = jnp.broadcast_to(run, run_ref.shape)
    cnt_ref[...] = jnp.broadcast_to(run, cnt_ref.shape)


def _out_router(mix_args, gain, w_router):
    specs, args = _mixer_out_operands(*mix_args)
    n, d = args[0].shape
    wr = jnp.concatenate([w_router, jnp.zeros((d, LANES - w_router.shape[1]), F32)], axis=1)
    wr_hi = _bf(wr)
    wr_lo = _bf(wr - wr_hi.astype(F32))
    rows = lambda w: pl.BlockSpec((ROW_TILE, w), lambda i: (i, 0))
    return pl.pallas_call(
        _out_router_kernel,
        grid=(n // ROW_TILE,),
        in_specs=specs + [_resident((1, d)), _resident((d, LANES)), _resident((d, LANES))],
        out_specs=[rows(d), rows(d // 2), rows(LANES), pl.BlockSpec((8, LANES), lambda i: (0, 0))],
        out_shape=[jax.ShapeDtypeStruct((n, d), F32), jax.ShapeDtypeStruct((n, d // 2), jnp.uint32),
                   jax.ShapeDtypeStruct((n, LANES), F32), jax.ShapeDtypeStruct((8, LANES), F32)],
        scratch_shapes=[pltpu.VMEM((8, LANES), F32)],
        compiler_params=_cparams("arbitrary"),
        name="out_moe_router",
    )(*args, gain.reshape(1, d), wr_hi, wr_lo)


def _sc_workers():
    info = plsc.get_sparse_core_info()
    return info.num_cores, info.num_cores * info.num_subcores


def _dispatch_rows(h, dest0, dest1, pad_rows):
    nc, workers = _sc_workers()
    n, d = h.shape
    n_pad = pad_rows.shape[0]
    per_worker = n // workers
    nchunk = per_worker // SC_WINDOW
    npad = n_pad // (workers * SC_WINDOW)
    assert n == workers * nchunk * SC_WINDOW and nchunk % 2 == 0 and n_pad == workers * npad * SC_WINDOW
    mesh = plsc.VectorSubcoreMesh(core_axis_name="c", subcore_axis_name="s")
    idx_t = lambda m: pltpu.VMEM((m, SC_WINDOW), jnp.int32)
    buf_t = pltpu.VMEM((SC_WINDOW, d), h.dtype)

    @functools.partial(
        pl.kernel, mesh=mesh,
        out_type=jax.ShapeDtypeStruct((2 * n + n_pad, d), h.dtype),
        scratch_types=[idx_t(nchunk), idx_t(nchunk), idx_t(npad), buf_t, buf_t, buf_t]
                      + [pltpu.SemaphoreType.DMA] * 6,
    )
    def dispatch_kernel(h_hbm, d0_hbm, d1_hbm, pad_hbm, zero_hbm, out_hbm, d0_v, d1_v, pad_v, buf0, buf1, zbuf,
                        r0, r1, w0, w1, x0, x1):
        wid = lax.axis_index("s") * nc + lax.axis_index("c")
        base = wid * per_worker
        pltpu.sync_copy(d0_hbm.at[wid], d0_v)
        pltpu.sync_copy(d1_hbm.at[wid], d1_v)
        pltpu.sync_copy(pad_hbm.at[wid], pad_v)
        pltpu.sync_copy(zero_hbm, zbuf)
        bufs, rsem, wsem, xsem = (buf0, buf1), (r0, r1), (w0, w1), (x0, x1)

        def read(c, slot):
            return pltpu.make_async_copy(h_hbm.at[pl.ds(base + c * SC_WINDOW, SC_WINDOW)], bufs[slot], rsem[slot])

        read(0, 0).start()
        read(1, 1).start()
        for pc in range(npad):
            pltpu.sync_copy(zbuf, out_hbm.at[pad_v.at[pc]])

        @pl.loop(0, nchunk, step=2)
        def _(c):
            for slot in range(2):
                cc = c + slot
                read(cc, slot).wait()
                first = pltpu.make_async_copy(bufs[slot], out_hbm.at[d0_v.at[cc]], wsem[slot])
                second = pltpu.make_async_copy(bufs[slot], out_hbm.at[d1_v.at[cc]], xsem[slot])
                first.start()
                second.start()
                first.wait()
                second.wait()

                @pl.when(cc + 2 < nchunk)
                def _():
                    read(cc + 2, slot).start()

    split = lambda z, m: z.reshape(workers, m, SC_WINDOW)
    return dispatch_kernel(h, split(dest0, nchunk), split(dest1, nchunk), split(pad_rows, npad),
                           jnp.zeros((SC_WINDOW, d), h.dtype))


def _gather_rows(src, idx):
    nc, workers = _sc_workers()
    m, d = idx.shape[0], src.shape[1]
    per_worker = m // workers
    nchunk = per_worker // SC_WINDOW
    assert m == workers * nchunk * SC_WINDOW and nchunk % 2 == 0, "row count must split evenly over subcores"
    mesh = plsc.VectorSubcoreMesh(core_axis_name="c", subcore_axis_name="s")

    @functools.partial(
        pl.kernel, mesh=mesh,
        out_type=jax.ShapeDtypeStruct((m, d), src.dtype),
        scratch_types=[pltpu.VMEM((nchunk, SC_WINDOW), jnp.int32),
                       pltpu.VMEM((SC_WINDOW, d), src.dtype), pltpu.VMEM((SC_WINDOW, d), src.dtype),
                       pltpu.SemaphoreType.DMA, pltpu.SemaphoreType.DMA,
                       pltpu.SemaphoreType.DMA, pltpu.SemaphoreType.DMA],
    )
    def gather_kernel(src_hbm, idx_hbm, out_hbm, idx_v, buf0, buf1, g0, g1, w0, w1):
        wid = lax.axis_index("s") * nc + lax.axis_index("c")
        base = wid * per_worker
        pltpu.sync_copy(idx_hbm.at[wid], idx_v)
        bufs, gsem, wsem = (buf0, buf1), (g0, g1), (w0, w1)

        def gather(c, slot):
            return pltpu.make_async_copy(src_hbm.at[idx_v.at[c]], bufs[slot], gsem[slot])

        def write(c, slot):
            return pltpu.make_async_copy(bufs[slot], out_hbm.at[pl.ds(base + c * SC_WINDOW, SC_WINDOW)], wsem[slot])

        gather(0, 0).start()
        gather(1, 1).start()

        @pl.loop(0, nchunk, step=2)
        def _(c):
            for slot in range(2):
                cc = c + slot
                gather(cc, slot).wait()
                write(cc, slot).start()
                write(cc, slot).wait()

                @pl.when(cc + 2 < nchunk)
                def _():
                    gather(cc + 2, slot).start()

    return gather_kernel(src, idx.reshape(workers, nchunk, SC_WINDOW))


def _pack_bf16_pairs(y):
    bits = lambda z: pltpu.bitcast(z.astype(BF16).astype(F32), jnp.uint32)
    words = []
    for t in range(0, y.shape[1] // LANES, 2):
        lo = bits(y[:, t * LANES:(t + 1) * LANES])
        hi = bits(y[:, (t + 1) * LANES:(t + 2) * LANES])
        words.append((hi & jnp.uint32(0xFFFF0000)) | (lo >> 16))
    return jnp.concatenate(words, axis=1)


def _unpack_bf16_pairs(words):
    tiles = []
    for t in range(words.shape[1] // LANES):
        w = words[:, t * LANES:(t + 1) * LANES]
        tiles += [pltpu.bitcast(w << 16, F32), pltpu.bitcast(w & jnp.uint32(0xFFFF0000), F32)]
    return jnp.concatenate(tiles, axis=1)


def _expert_kernel(be_ref, nu_ref, x_ref, wg_ref, wu_ref, wd_ref, o_ref, h_ref, acc_ref):
    i, j = pl.program_id(0), pl.program_id(1)
    last = pl.num_programs(1) - 1

    @pl.when(i < nu_ref[0])
    def _():
        @pl.when(j == 0)
        def _():
            h_ref[...] = _bf(_unpack_bf16_pairs(x_ref[...]))

        h = h_ref[...]
        act = jax.nn.silu(_dot(h, wg_ref[0])) * _dot(h, wu_ref[0])
        part = _dot(_bf(act), wd_ref[0])

        @pl.when(j == 0)
        def _():
            acc_ref[...] = part

        @pl.when((j > 0) & (j < last))
        def _():
            acc_ref[...] += part

        @pl.when(j == last)
        def _():
            o_ref[...] = _pack_bf16_pairs(acc_ref[...] + part)

    @pl.when((i >= nu_ref[0]) & (j == 0))
    def _():
        o_ref[...] = jnp.zeros_like(o_ref)


def _expert_swiglu(xb, block_expert, n_used, wg, wu, wd):
    n_rows, d = xb.shape[0], 2 * xb.shape[1]
    d_ff = wg.shape[2]
    tf = _ffn_tile(d_ff, 2)
    nf = d_ff // tf
    assert nf >= 2, "the packed store happens on a d_ff step after the first"
    nblk = n_rows // MOE_BLOCK

    def row(i, j, be, nu):
        return jnp.minimum(i, nu[0] - 1)

    def col(i, j, be, nu):
        return jnp.where(i < nu[0], j, nf - 1)

    grid_spec = pltpu.PrefetchScalarGridSpec(
        num_scalar_prefetch=2,
        grid=(nblk, nf),
        in_specs=[pl.BlockSpec((MOE_BLOCK, d // 2), lambda i, j, be, nu: (row(i, j, be, nu), 0)),
                  pl.BlockSpec((1, d, tf), lambda i, j, be, nu: (be[row(i, j, be, nu)], 0, col(i, j, be, nu))),
                  pl.BlockSpec((1, d, tf), lambda i, j, be, nu: (be[row(i, j, be, nu)], 0, col(i, j, be, nu))),
                  pl.BlockSpec((1, tf, d), lambda i, j, be, nu: (be[row(i, j, be, nu)], col(i, j, be, nu), 0))],
        out_specs=pl.BlockSpec((MOE_BLOCK, d // 2), lambda i, j, be, nu: (i, 0)),
        scratch_shapes=[pltpu.VMEM((MOE_BLOCK, d), BF16), pltpu.VMEM((MOE_BLOCK, d), F32)],
    )
    return pl.pallas_call(
        _expert_kernel,
        grid_spec=grid_spec,
        out_shape=jax.ShapeDtypeStruct((n_rows, d // 2), jnp.uint32),
        compiler_params=_cparams("arbitrary", "arbitrary"),
        name="expert_swiglu",
    )(block_expert, n_used, xb, _bf(wg), _bf(wu), _bf(wd))


def _combine_kernel(x_ref, y1_ref, y2_ref, meta_ref, g_ref, o_ref):
    meta = meta_ref[...]
    lane = lax.broadcasted_iota(jnp.int32, meta.shape, 1)
    g1 = jnp.sum(jnp.where(lane == META_G, meta, 0.0), axis=-1, keepdims=True)
    g2 = jnp.sum(jnp.where(lane == META_G + 1, meta, 0.0), axis=-1, keepdims=True)
    y = x_ref[...] + (g1 * _unpack_bf16_pairs(y1_ref[...]) + g2 * _unpack_bf16_pairs(y2_ref[...]))
    o_ref[...] = _rmsnorm(y, g_ref[...])


def _combine_final_norm(x2, yg, meta, gain):
    n, d = x2.shape
    nblk = n // ROW_TILE
    return pl.pallas_call(
        _combine_kernel,
        grid=(nblk,),
        in_specs=[pl.BlockSpec((ROW_TILE, d), lambda i: (i, 0)),
                  pl.BlockSpec((ROW_TILE, d // 2), lambda i: (i, 0)),
                  pl.BlockSpec((ROW_TILE, d // 2), lambda i: (i + nblk, 0)),
                  pl.BlockSpec((ROW_TILE, LANES), lambda i: (i, 0)),
                  pl.BlockSpec((1, d), lambda i: (0, 0))],
        out_specs=pl.BlockSpec((ROW_TILE, d), lambda i: (i, 0)),
        out_shape=jax.ShapeDtypeStruct((n, d), F32),
        compiler_params=_cparams("parallel"),
        name="moe_combine_norm",
    )(x2, yg, yg, meta, gain.reshape(1, d))


def _moe_layer(mix_args, gain, w_router, wg, wu, wd, final_gain):
    x2, h, meta, counts = _out_router(mix_args, gain, w_router)
    n, d = x2.shape
    expert = meta[:, META_E:META_E + 2].astype(jnp.int32)
    rank = meta[:, META_RANK:META_RANK + 2].astype(jnp.int32)
    count = counts[0, :N_EXPERTS].astype(jnp.int32)
    padded = (count + MOE_BLOCK - 1) // MOE_BLOCK * MOE_BLOCK
    pad_end = jnp.cumsum(padded)
    pad_start = pad_end - padded
    dest = pad_start[expert] + rank
    n_rows = (2 * n // MOE_BLOCK + N_EXPERTS) * MOE_BLOCK
    empties = padded - count
    e_end = jnp.cumsum(empties)
    i = jnp.arange(n_rows - 2 * n, dtype=jnp.int32)
    grp = jnp.sum(i[:, None] >= e_end[None, :], axis=1).astype(jnp.int32)
    first_empty = jnp.concatenate([pad_start + count, pad_end[-1:]])
    first_index = jnp.concatenate([e_end - empties, e_end[-1:]])
    pad_rows = (first_empty[grp] + i - first_index[grp]).astype(jnp.int32)
    block_start = jnp.arange(n_rows // MOE_BLOCK, dtype=jnp.int32) * MOE_BLOCK
    block_expert = jnp.minimum(jnp.sum(block_start[:, None] >= pad_end[None, :], axis=1), N_EXPERTS - 1)
    block_expert = block_expert.astype(jnp.int32)
    n_used = (pad_end[-1:] // MOE_BLOCK).astype(jnp.int32)
    xb = _dispatch_rows(h, dest[:, 0], dest[:, 1], pad_rows)
    ys = _expert_swiglu(xb, block_expert, n_used, wg, wu, wd)
    yg = _gather_rows(ys, jnp.concatenate([dest[:, 0], dest[:, 1]]))
    return _combine_final_norm(x2, yg, meta, final_gain)


def _mixer_layer(x2, batch, norm_mix, w_in, w_out, shift_mu, gm_ln_w, gm_ln_b, gm_ws, gm_bs,
                 rw_w_up, rw_w0, rw_a_up, rw_a0, rw_g_up, rw_k_k, rw_k_a, rw_r_k, rw_lnx_w, rw_lnx_b):
    n, d = x2.shape
    gm2 = 2 * gm_ln_w.shape[0]
    w_in = _bf(w_in)
    y_gm, r, c, lw, k, v, a, b, g, bonus = _mixer_in(
        x2.reshape(batch, n // batch, d), norm_mix, w_in[:, :gm2], w_in[:, gm2:], gm_ln_w, gm_ln_b, gm_ws, gm_bs,
        shift_mu, rw_w_up, rw_w0, rw_a_up, rw_a0, rw_g_up, rw_k_k, rw_k_a, rw_r_k.reshape(-1))
    y_scan = _wkv(r, c, lw, k, v, a, b)
    flat = lambda z: z.reshape(n, -1)
    return x2, flat(y_gm), flat(y_scan), flat(bonus), flat(g), rw_lnx_w, rw_lnx_b, w_out


def kernel(x, norm_mix, w_in, w_out, shift_mu, gm_ln_w, gm_ln_b, gm_ws, gm_bs, rw_w_up, rw_w0, rw_a_up, rw_a0,
           rw_g_up, rw_k_k, rw_k_a, rw_r_k, rw_lnx_w, rw_lnx_b, norm_ffn, ffn_w_gate, ffn_w_up, ffn_w_down,
           moe_router, moe_w_gate, moe_w_up, moe_w_down, norm_final):
    batch, t, d = x.shape
    depth = norm_mix.shape[0]
    assert depth == 2 and t % ROW_TILE == 0, "two layers (dense then MoE), sequence a multiple of the row tile"
    x2 = x.reshape(batch * t, d)
    for i in range(depth):
        mixed = _mixer_layer(x2, batch, norm_mix[i], w_in[i], w_out[i], shift_mu[i], gm_ln_w[i], gm_ln_b[i],
                             gm_ws[i], gm_bs[i], rw_w_up[i], rw_w0[i], rw_a_up[i], rw_a0[i], rw_g_up[i],
                             rw_k_k[i], rw_k_a[i], rw_r_k[i], rw_lnx_w[i], rw_lnx_b[i])
        if i % 2 == 0:
            x2 = _out_swiglu(mixed, norm_ffn[i], ffn_w_gate[i // 2], ffn_w_up[i // 2], ffn_w_down[i // 2])
        else:
            x2 = _moe_layer(mixed, norm_ffn[i], moe_router[i // 2], moe_w_gate[i // 2], moe_w_up[i // 2],
                            moe_w_down[i // 2], norm_final)
    return x2.reshape(batch, t, d)
```

```python
import functools

import jax
import jax.numpy as jnp
from jax import lax
from jax.experimental import pallas as pl
from jax.experimental.pallas import tpu as pltpu
from jax.experimental.pallas import tpu_sc as plsc

F32 = jnp.float32
BF16 = jnp.bfloat16

HEAD_DIM = 64
LANES = 128
MXU_TILE = 256
GM_CHUNK = 128
WKV_CHUNK = 64
N_EXPERTS = 8
RMS_EPS = 1e-6
LN_EPS = 1e-5
GN_EPS = 64e-5
NEG_BIG = -1e30

ROW_TILE = 512
MIX_TILE = 1024
WKV_STEP = 512
MOE_BLOCK = 512
SC_WINDOW = 32
VMEM_LIMIT = 56 * 1024 * 1024


def _cparams(*sem):
    return pltpu.CompilerParams(dimension_semantics=sem, vmem_limit_bytes=VMEM_LIMIT)


def _bf(x):
    return x.astype(BF16)


def _dot(a, b):
    return jnp.dot(a, b, preferred_element_type=F32)


def _dot_nt(a, b):
    return lax.dot_general(a, b, (((1,), (1,)), ((), ())), preferred_element_type=F32)


def _split_dot_left(m, x, parts=2):
    acc = None
    rem = x
    for p in range(parts):
        hi = _bf(rem)
        d = _dot(m, hi)
        acc = d if acc is None else acc + d
        if p + 1 < parts:
            rem = rem - hi.astype(F32)
    return acc


def _rmsnorm(x, g):
    return x * lax.rsqrt(jnp.mean(x * x, axis=-1, keepdims=True) + RMS_EPS) * g


def _gelu(x):
    return 0.5 * x * (1.0 + lax.erf(x * (2.0 ** -0.5)))


def _head_avg_matrix():
    r = lax.broadcasted_iota(jnp.int32, (MXU_TILE, MXU_TILE), 0) // HEAD_DIM
    c = lax.broadcasted_iota(jnp.int32, (MXU_TILE, MXU_TILE), 1) // HEAD_DIM
    return jnp.where(r == c, 1.0 / HEAD_DIM, 0.0).astype(BF16)


def _head_dot(x, m):
    return jnp.concatenate([_dot(x[:, i:i + MXU_TILE], m) for i in range(0, x.shape[1], MXU_TILE)], axis=1)


def _head_layernorm(x, avg, w, b, eps):
    mu = _head_dot(_bf(x), avg)
    xc = x - mu
    var = _head_dot(_bf(xc * xc), avg)
    return xc * lax.rsqrt(var + eps) * w + b


def _resident(shape):
    return pl.BlockSpec(shape, lambda *_: (0,) * len(shape), pipeline_mode=pl.Buffered(1))


def _gmlp_chunk(p, avg, lnw, lnb, w2, bias, head0):
    width = p.shape[1] // 2
    u = _gelu(p[:, :width])
    vn = _head_layernorm(_gelu(p[:, width:]), avg, lnw, lnb, LN_EPS)
    mixed = []
    for pair, w in enumerate(w2):
        vp = vn[:, pair * LANES:(pair + 1) * LANES]
        v_st = _bf(jnp.concatenate([jnp.where(head0, vp, 0.0), jnp.where(head0, 0.0, vp)], axis=0))
        mixed.append(_dot(w, v_st))
    return u * (jnp.concatenate(mixed, axis=1) + bias)


def _mixer_in_kernel(x_ref, gain_ref, wgm_ref, wrw_ref, lnw_ref, lnb_ref, ws_ref, bias_ref,
                     mu_ref, wup_ref, w0_ref, aup_ref, a0_ref, gup_ref, kk_ref, ka_ref, rk_ref,
                     ygm_ref, r_ref, c_ref, lw_ref, k_ref, v_ref, a_ref, b_ref, g_ref, bonus_ref, carry_ref):
    width = r_ref.shape[2]
    rows = x_ref.shape[1]

    @pl.when(pl.program_id(1) == 0)
    def _():
        carry_ref[...] = jnp.zeros_like(carry_ref)

    avg = _head_avg_matrix()
    ones = avg * HEAD_DIM
    t = lax.broadcasted_iota(jnp.int32, (GM_CHUNK, GM_CHUNK), 0)
    s = lax.broadcasted_iota(jnp.int32, (GM_CHUNK, GM_CHUNK), 1)
    causal = s <= t
    head0 = lax.broadcasted_iota(jnp.int32, (GM_CHUNK, LANES), 1) < HEAD_DIM
    w2 = [_bf(jnp.concatenate([jnp.where(causal, ws_ref[2 * pair], 0.0),
                               jnp.where(causal, ws_ref[2 * pair + 1], 0.0)], axis=1))
          for pair in range(width // LANES)]
    t = lax.broadcasted_iota(jnp.int32, (MXU_TILE, MXU_TILE), 0)
    s = lax.broadcasted_iota(jnp.int32, (MXU_TILE, MXU_TILE), 1)
    tri = ((t // WKV_CHUNK == s // WKV_CHUNK) & (s <= t)).astype(BF16)
    first = lax.broadcasted_iota(jnp.int32, (ROW_TILE, wrw_ref.shape[1]), 0) == 0

    subs = [slice(i * ROW_TILE, (i + 1) * ROW_TILE) for i in range(rows // ROW_TILE)]
    proj = []
    for rs in subs:
        h = _bf(_rmsnorm(x_ref[0, rs, :], gain_ref[...]))
        proj.append((_dot(h, wgm_ref[...]), _dot(h, wrw_ref[...])))

    last_row = carry_ref[0:1, :]
    for rs, (p_gm, p) in zip(subs, proj):
        for ch in range(ROW_TILE // GM_CHUNK):
            ts = slice(ch * GM_CHUNK, (ch + 1) * GM_CHUNK)
            out_rows = slice(rs.start + ts.start, rs.start + ts.stop)
            ygm_ref[0, out_rows, :] = _bf(_gmlp_chunk(p_gm[ts], avg, lnw_ref[...], lnb_ref[...], w2, bias_ref[...],
                                                      head0))
        prev = jnp.where(first, last_row, pltpu.roll(p, 1, axis=0))
        last_row = p[ROW_TILE - 1:ROW_TILE, :]
        ps = p + (prev - p) * mu_ref[...]
        r = ps[:, :width]
        k = ps[:, width:2 * width]
        v = ps[:, 2 * width:3 * width]
        lora_in = ps[:, 3 * width:3 * width + LANES]
        gd = ps[:, 3 * width + LANES:]
        lw = -(jnp.exp(-0.5)) * jax.nn.sigmoid(w0_ref[...] + _dot(_bf(jnp.tanh(lora_in)), wup_ref[...]))
        a_lr = jax.nn.sigmoid(a0_ref[...] + _dot(_bf(lora_in), aup_ref[...]))
        kk = k * kk_ref[...]
        kk = kk * lax.rsqrt(jnp.maximum(_head_dot(_bf(kk * kk), ones), 1e-24))
        kmod = k * (1.0 + (a_lr - 1.0) * ka_ref[...])
        lw_ref[0, rs, :] = lw
        c_ref[0, rs, :] = jnp.concatenate([_split_dot_left(tri, lw[i:i + MXU_TILE])
                                           for i in range(0, ROW_TILE, MXU_TILE)], axis=0)
        r_ref[0, rs, :] = _bf(r)
        k_ref[0, rs, :] = _bf(kmod)
        v_ref[0, rs, :] = _bf(v)
        a_ref[0, rs, :] = _bf(-kk)
        b_ref[0, rs, :] = _bf(kk * a_lr)
        g_ref[0, rs, :] = _bf(_dot(_bf(jax.nn.sigmoid(gd)), gup_ref[...]))
        bonus_ref[0, rs, :] = _bf(_head_dot(_bf(r * kmod * rk_ref[...]), ones) * v)
    carry_ref[0:1, :] = last_row


def _mixer_in(x3, gain, w_gm, w_rw, ln_w, ln_b, ws, bs, mu, w_up, w0, a_up, a0, g_up, k_k, k_a, r_k):
    batch, t, d = x3.shape
    width = w0.shape[0]
    rank = w_up.shape[0]
    zeros = jnp.zeros((LANES - rank, width), F32)
    wup_pad = _bf(jnp.concatenate([w_up, zeros], axis=0))
    aup_pad = _bf(jnp.concatenate([zeros, a_up], axis=0))
    bias = jnp.repeat(bs.T, HEAD_DIM, axis=1)
    row = lambda z: z.reshape(1, -1)
    params = (row(gain), w_gm, w_rw, row(ln_w), row(ln_b), ws, bias, row(mu), wup_pad, row(w0), aup_pad, row(a0),
              _bf(g_up), row(k_k), row(k_a), row(r_k))
    tile = min(MIX_TILE, t)
    out = pl.BlockSpec((1, tile, width), lambda b, i: (b, i, 0))
    sds = lambda dtype: jax.ShapeDtypeStruct((batch, t, width), dtype)
    return pl.pallas_call(
        _mixer_in_kernel,
        grid=(batch, t // tile),
        in_specs=[pl.BlockSpec((1, tile, d), lambda b, i: (b, i, 0))] + [_resident(z.shape) for z in params],
        out_specs=[out] * 10,
        out_shape=[sds(BF16), sds(BF16), sds(F32), sds(F32)] + [sds(BF16)] * 6,
        scratch_shapes=[pltpu.VMEM((8, w_rw.shape[1]), F32)],
        compiler_params=_cparams("parallel", "arbitrary"),
        name="mixer_in",
    )(x3, *params)


def _wkv_kernel(r_ref, c_ref, lw_ref, k_ref, v_ref, a_ref, b_ref, y_ref, h_ref,
                q_s, v_s, ark_s, bcrb_s, mz_s, rhs_s, sol_s, t_s, rh_s, rkv_s, kv_s, dec_s, hc_s, yc_s):
    @pl.when(pl.program_id(1) == 0)
    def _():
        h_ref[...] = jnp.zeros_like(h_ref)

    n = WKV_CHUNK
    two = 2 * n
    head0 = lax.broadcasted_iota(jnp.int32, (n, LANES), 1) < HEAD_DIM
    row = lax.broadcasted_iota(jnp.int32, (two, two), 0)
    col = lax.broadcasted_iota(jnp.int32, (two, two), 1)
    eye = row == col
    row2 = lax.broadcasted_iota(jnp.int32, (two, 2 * two), 0)
    col2 = lax.broadcasted_iota(jnp.int32, (two, 2 * two), 1) % two
    same2 = (row2 // n) == (col2 // n)
    strict2 = same2 & ((col2 % n) < (row2 % n))
    incl2 = same2 & ((col2 % n) <= (row2 % n))
    npairs = y_ref.shape[2] // LANES
    nchunks = y_ref.shape[1] // n
    items = [(ch, pair) for ch in range(nchunks) for pair in range(npairs)]

    def window(ch, pair):
        return slice(ch * n, (ch + 1) * n), slice(pair * LANES, (pair + 1) * LANES)

    def stack(z):
        return jnp.concatenate([jnp.where(head0, z, 0.0), jnp.where(head0, 0.0, z)], axis=0)

    def dup(z):
        zb = _bf(z)
        return jnp.concatenate([zb, zb], axis=0)

    def scores(it):
        ts, ls = window(*items[it])
        r, k, v, a, b = (ref[0, ts, ls].astype(F32) for ref in (r_ref, k_ref, v_ref, a_ref, b_ref))
        c = c_ref[0, ts, ls]
        cex = c - lw_ref[0, ts, ls]
        mid = c[n // 2 - 1:n // 2, :]
        last = c[n - 1:n, :]
        g_inv = jnp.exp(mid - c)
        g_end = jnp.exp(last - c)
        ar_st = _bf(jnp.concatenate([stack(a * jnp.exp(cex - mid)), stack(r * jnp.exp(c - mid))], axis=0))
        bk_dup = jnp.concatenate([dup(b * g_inv), dup(k * g_inv)], axis=0)
        sc = _dot_nt(ar_st, bk_dup)
        top = jnp.where(strict2, sc[:two], 0.0)
        bot = jnp.where(incl2, sc[two:], 0.0)
        t_s[it] = jnp.where(eye, 1.0, top[:, :two])
        q_s[it] = _bf(top[:, :two])
        bcrb_s[it] = _bf(jnp.concatenate([stack(b * g_end).T, bot[:, :two]], axis=0))
        ark_s[it] = _bf(jnp.concatenate([top[:, two:], bot[:, two:], stack(k * g_end).T], axis=0))
        v_s[it] = _bf(stack(v))
        rh_s[it] = stack(r * jnp.exp(c))
        rhs_s[it, :, :LANES] = _bf(stack(a * jnp.exp(cex)))
        e_col = jnp.sum(jnp.where(eye, jnp.exp(last), 0.0), axis=1, keepdims=True)
        dec_s[it] = jnp.broadcast_to(e_col, (two, LANES))

    def values(it):
        v_st = v_s[it]
        prod = _dot(ark_s[it], v_st)
        rhs_s[it, :, LANES:] = _bf(prod[:two])
        rkv_s[it] = prod[two:2 * two]
        kv_s[it] = prod[2 * two:]

    levels = n.bit_length() - 1

    def inverse_level(level, it):
        q = q_s[it]
        t_acc = t_s[it]
        if level == 1:
            q = _bf(_dot(q, q))
        if level + 1 < levels:
            prod = _dot(q, jnp.concatenate([_bf(t_acc), q], axis=1))
            q_s[it] = _bf(prod[:, two:])
            t_s[it] = t_acc + prod[:, :two]
        else:
            t_s[it] = t_acc + _dot(q, _bf(t_acc))

    def solve(it):
        sol_s[it] = _bf(_dot(_bf(t_s[it]), rhs_s[it]))

    def fold(it):
        prod = _dot(bcrb_s[it], sol_s[it])
        mz_s[it] = _bf(jnp.concatenate([prod[:two, :LANES], rh_s[it] + prod[two:, :LANES]], axis=0))
        hc_s[it] = prod[:two, LANES:] + kv_s[it]
        yc_s[it] = prod[two:, LANES:] + rkv_s[it]

    def advance(it):
        ts, ls = window(*items[it])
        pair = items[it][1]
        h = h_ref[pair]
        gmat = _dot(mz_s[it], _bf(h))
        h_ref[pair] = dec_s[it] * h + gmat[:two] + hc_s[it]
        y_st = gmat[two:] + yc_s[it]
        y_ref[0, ts, ls] = y_st[:n] + y_st[n:]

    stages = ([scores, values] + [functools.partial(inverse_level, level) for level in range(1, levels)]
              + [solve, fold, advance])
    for stage in stages:
        for it in range(len(items)):
            stage(it)


def _wkv(r, c, lw, k, v, a, b):
    batch, t, width = r.shape
    npairs = width // LANES
    items = (WKV_STEP // WKV_CHUNK) * npairs
    spec = pl.BlockSpec((1, WKV_STEP, width), lambda bi, i: (bi, i, 0))
    sq = lambda dtype: pltpu.VMEM((items, LANES, LANES), dtype)
    return pl.pallas_call(
        _wkv_kernel,
        grid=(batch, t // WKV_STEP),
        in_specs=[spec] * 7,
        out_specs=spec,
        out_shape=jax.ShapeDtypeStruct((batch, t, width), F32),
        scratch_shapes=[pltpu.VMEM((npairs, LANES, LANES), F32)] + [sq(BF16)] * 2
                       + [pltpu.VMEM((items, 3 * LANES, LANES), BF16)]
                       + [pltpu.VMEM((items, 2 * LANES, LANES), BF16)] * 2
                       + [pltpu.VMEM((items, LANES, 2 * LANES), BF16)] * 2 + [sq(F32)] * 7,
        compiler_params=_cparams("parallel", "arbitrary"),
        name="wkv7",
    )(r, c, lw, k, v, a, b)


N_MIX_OUT_REFS = 9


def _mixer_out(x_ref, ygm_ref, ys_ref, bonus_ref, g_ref, lnw_ref, lnb_ref, wtop_ref, wbot_ref, rows=slice(None)):
    yn = _head_layernorm(ys_ref[rows, :], _head_avg_matrix(), lnw_ref[...], lnb_ref[...], GN_EPS)
    y_rw = (yn + bonus_ref[rows, :]) * g_ref[rows, :]
    return x_ref[rows, :] + _dot(ygm_ref[rows, :], wtop_ref[...]) + _dot(_bf(y_rw), wbot_ref[...])


def _mixer_out_operands(x2, y_gm, y_scan, bonus, g, lnx_w, lnx_b, w_out, tile=ROW_TILE):
    d = x2.shape[1]
    width = y_gm.shape[1]
    rows = lambda w: pl.BlockSpec((tile, w), lambda i: (i, 0))
    w_out = _bf(w_out)
    specs = [rows(d), rows(width), rows(width), rows(width), rows(width),
             _resident((1, width)), _resident((1, width)), _resident((width, d)), _resident((width, d))]
    args = (x2, y_gm, y_scan, bonus, g, lnx_w.reshape(1, width), lnx_b.reshape(1, width), w_out[:width], w_out[width:])
    assert len(specs) == len(args) == N_MIX_OUT_REFS
    return specs, args


def _out_swiglu_kernel(*refs):
    mix_refs, (gain_ref, wg_ref, wu_ref, wd_ref, o_ref) = refs[:N_MIX_OUT_REFS], refs[N_MIX_OUT_REFS:]
    x = _mixer_out(*mix_refs)
    h = _bf(_rmsnorm(x, gain_ref[...]))
    act = jax.nn.silu(_dot(h, wg_ref[...])) * _dot(h, wu_ref[...])
    o_ref[...] = x + _dot(_bf(act), wd_ref[...])


def _ffn_tile(d_ff, parts):
    assert d_ff % (parts * LANES) == 0, f"d_ff={d_ff} does not split into {parts} lane-aligned tiles"
    return d_ff // parts


def _out_swiglu(mix_args, gain, wg, wu, wd):
    specs, args = _mixer_out_operands(*mix_args)
    n, d = args[0].shape
    d_ff = wg.shape[1]
    return pl.pallas_call(
        _out_swiglu_kernel,
        grid=(n // ROW_TILE,),
        in_specs=specs + [_resident((1, d)), _resident((d, d_ff)), _resident((d, d_ff)), _resident((d_ff, d))],
        out_specs=pl.BlockSpec((ROW_TILE, d), lambda i: (i, 0)),
        out_shape=jax.ShapeDtypeStruct((n, d), F32),
        compiler_params=_cparams("parallel"),
        name="out_dense_swiglu",
    )(*args, gain.reshape(1, d), _bf(wg), _bf(wu), _bf(wd))


META_E, META_G, META_RANK = 0, 2, 4


def _out_router_kernel(*refs):
    mix_refs = refs[:N_MIX_OUT_REFS]
    g_ref, whi_ref, wlo_ref, x_ref, h_ref, meta_ref, cnt_ref, run_ref = refs[N_MIX_OUT_REFS:]

    @pl.when(pl.program_id(0) == 0)
    def _():
        run_ref[...] = jnp.zeros_like(run_ref)

    lane = lax.broadcasted_iota(jnp.int32, (ROW_TILE, LANES), 1)
    t = lax.broadcasted_iota(jnp.int32, (ROW_TILE, ROW_TILE), 0)
    s = lax.broadcasted_iota(jnp.int32, (ROW_TILE, ROW_TILE), 1)
    earlier = jnp.where(s < t, 1.0, 0.0).astype(BF16)

    subs = [slice(i * ROW_TILE, (i + 1) * ROW_TILE) for i in range(x_ref.shape[0] // ROW_TILE)]
    xs = [_mixer_out(*mix_refs, rows=rs) for rs in subs]

    run = run_ref[0:1, :]
    for rs, x in zip(subs, xs):
        x_ref[rs, :] = x
        h = _rmsnorm(x, g_ref[...])
        h_ref[rs, :] = _pack_bf16_pairs(h)
        h_hi = _bf(h)
        h_lo = _bf(h - h_hi.astype(F32))
        logits = _dot(h_hi, whi_ref[...]) + _dot(h_lo, whi_ref[...]) + _dot(h_hi, wlo_ref[...])
        logits = jnp.where(lane < N_EXPERTS, logits, NEG_BIG)
        v1 = jnp.max(logits, axis=-1, keepdims=True)
        e1 = jnp.min(jnp.where(logits == v1, lane, LANES), axis=-1, keepdims=True)
        oh1 = lane == e1
        rest = jnp.where(oh1, NEG_BIG, logits)
        v2 = jnp.max(rest, axis=-1, keepdims=True)
        e2 = jnp.min(jnp.where(rest == v2, lane, LANES), axis=-1, keepdims=True)
        oh2 = lane == e2
        ex = jnp.exp(v2 - v1)
        g1 = 1.0 / (1.0 + ex)
        g2 = ex / (1.0 + ex)
        cnt = jnp.where(oh1 | oh2, 1.0, 0.0)
        before = _dot(earlier, _bf(cnt)) + run
        rank1 = jnp.sum(jnp.where(oh1, before, 0.0), axis=-1, keepdims=True)
        rank2 = jnp.sum(jnp.where(oh2, before, 0.0), axis=-1, keepdims=True)
        fields = (e1.astype(F32), e2.astype(F32), g1, g2, rank1, rank2)
        meta = jnp.zeros((ROW_TILE, LANES), F32)
        for idx, val in enumerate(fields):
            meta = jnp.where(lane == idx, val, meta)
        meta_ref[rs, :] = meta
        run = run + jnp.sum(cnt, axis=0, keepdims=True)
    run_ref[...] = jnp.broadcast_to(run, run_ref.shape)
    cnt_ref[...] = jnp.broadcast_to(run, cnt_ref.shape)


def _out_router(mix_args, gain, w_router):
    n, d = mix_args[0].shape
    tile = min(MIX_TILE, n)
    specs, args = _mixer_out_operands(*mix_args, tile=tile)
    wr = jnp.concatenate([w_router, jnp.zeros((d, LANES - w_router.shape[1]), F32)], axis=1)
    wr_hi = _bf(wr)
    wr_lo = _bf(wr - wr_hi.astype(F32))
    rows = lambda w: pl.BlockSpec((tile, w), lambda i: (i, 0))
    return pl.pallas_call(
        _out_router_kernel,
        grid=(n // tile,),
        in_specs=specs + [_resident((1, d)), _resident((d, LANES)), _resident((d, LANES))],
        out_specs=[rows(d), rows(d // 2), rows(LANES), pl.BlockSpec((8, LANES), lambda i: (0, 0))],
        out_shape=[jax.ShapeDtypeStruct((n, d), F32), jax.ShapeDtypeStruct((n, d // 2), jnp.uint32),
                   jax.ShapeDtypeStruct((n, LANES), F32), jax.ShapeDtypeStruct((8, LANES), F32)],
        scratch_shapes=[pltpu.VMEM((8, LANES), F32)],
        compiler_params=_cparams("arbitrary"),
        name="out_moe_router",
    )(*args, gain.reshape(1, d), wr_hi, wr_lo)


def _sc_workers():
    info = plsc.get_sparse_core_info()
    return info.num_cores, info.num_cores * info.num_subcores


def _dispatch_rows(h, dest0, dest1, pad_rows):
    nc, workers = _sc_workers()
    n, d = h.shape
    n_pad = pad_rows.shape[0]
    per_worker = n // workers
    nchunk = per_worker // SC_WINDOW
    npad = n_pad // (workers * SC_WINDOW)
    assert n == workers * nchunk * SC_WINDOW and nchunk % 2 == 0 and n_pad == workers * npad * SC_WINDOW
    mesh = plsc.VectorSubcoreMesh(core_axis_name="c", subcore_axis_name="s")
    idx_t = lambda m: pltpu.VMEM((m, SC_WINDOW), jnp.int32)
    buf_t = pltpu.VMEM((SC_WINDOW, d), h.dtype)

    @functools.partial(
        pl.kernel, mesh=mesh,
        out_type=jax.ShapeDtypeStruct((2 * n + n_pad, d), h.dtype),
        scratch_types=[idx_t(nchunk), idx_t(nchunk), idx_t(npad), buf_t, buf_t, buf_t]
                      + [pltpu.SemaphoreType.DMA] * 6,
    )
    def dispatch_kernel(h_hbm, d0_hbm, d1_hbm, pad_hbm, zero_hbm, out_hbm, d0_v, d1_v, pad_v, buf0, buf1, zbuf,
                        r0, r1, w0, w1, x0, x1):
        wid = lax.axis_index("s") * nc + lax.axis_index("c")
        base = wid * per_worker
        pltpu.sync_copy(d0_hbm.at[wid], d0_v)
        pltpu.sync_copy(d1_hbm.at[wid], d1_v)
        pltpu.sync_copy(pad_hbm.at[wid], pad_v)
        pltpu.sync_copy(zero_hbm, zbuf)
        bufs, rsem, wsem, xsem = (buf0, buf1), (r0, r1), (w0, w1), (x0, x1)

        def read(c, slot):
            return pltpu.make_async_copy(h_hbm.at[pl.ds(base + c * SC_WINDOW, SC_WINDOW)], bufs[slot], rsem[slot])

        read(0, 0).start()
        read(1, 1).start()
        for pc in range(npad):
            pltpu.sync_copy(zbuf, out_hbm.at[pad_v.at[pc]])

        @pl.loop(0, nchunk, step=2)
        def _(c):
            for slot in range(2):
                cc = c + slot
                read(cc, slot).wait()
                first = pltpu.make_async_copy(bufs[slot], out_hbm.at[d0_v.at[cc]], wsem[slot])
                second = pltpu.make_async_copy(bufs[slot], out_hbm.at[d1_v.at[cc]], xsem[slot])
                first.start()
                second.start()
                first.wait()
                second.wait()

                @pl.when(cc + 2 < nchunk)
                def _():
                    read(cc + 2, slot).start()

    split = lambda z, m: z.reshape(workers, m, SC_WINDOW)
    return dispatch_kernel(h, split(dest0, nchunk), split(dest1, nchunk), split(pad_rows, npad),
                           jnp.zeros((SC_WINDOW, d), h.dtype))


def _gather_rows(src, idx):
    nc, workers = _sc_workers()
    m, d = idx.shape[0], src.shape[1]
    per_worker = m // workers
    nchunk = per_worker // SC_WINDOW
    assert m == workers * nchunk * SC_WINDOW and nchunk % 2 == 0, "row count must split evenly over subcores"
    mesh = plsc.VectorSubcoreMesh(core_axis_name="c", subcore_axis_name="s")

    @functools.partial(
        pl.kernel, mesh=mesh,
        out_type=jax.ShapeDtypeStruct((m, d), src.dtype),
        scratch_types=[pltpu.VMEM((nchunk, SC_WINDOW), jnp.int32),
                       pltpu.VMEM((SC_WINDOW, d), src.dtype), pltpu.VMEM((SC_WINDOW, d), src.dtype),
                       pltpu.SemaphoreType.DMA, pltpu.SemaphoreType.DMA,
                       pltpu.SemaphoreType.DMA, pltpu.SemaphoreType.DMA],
    )
    def gather_kernel(src_hbm, idx_hbm, out_hbm, idx_v, buf0, buf1, g0, g1, w0, w1):
        wid = lax.axis_index("s") * nc + lax.axis_index("c")
        base = wid * per_worker
        pltpu.sync_copy(idx_hbm.at[wid], idx_v)
        bufs, gsem, wsem = (buf0, buf1), (g0, g1), (w0, w1)

        def gather(c, slot):
            return pltpu.make_async_copy(src_hbm.at[idx_v.at[c]], bufs[slot], gsem[slot])

        def write(c, slot):
            return pltpu.make_async_copy(bufs[slot], out_hbm.at[pl.ds(base + c * SC_WINDOW, SC_WINDOW)], wsem[slot])

        gather(0, 0).start()
        gather(1, 1).start()

        @pl.loop(0, nchunk, step=2)
        def _(c):
            for slot in range(2):
                cc = c + slot
                gather(cc, slot).wait()
                write(cc, slot).start()
                write(cc, slot).wait()

                @pl.when(cc + 2 < nchunk)
                def _():
                    gather(cc + 2, slot).start()

    return gather_kernel(src, idx.reshape(workers, nchunk, SC_WINDOW))


def _pack_bf16_pairs(y):
    bits = lambda z: pltpu.bitcast(z.astype(BF16).astype(F32), jnp.uint32)
    words = []
    for t in range(0, y.shape[1] // LANES, 2):
        lo = bits(y[:, t * LANES:(t + 1) * LANES])
        hi = bits(y[:, (t + 1) * LANES:(t + 2) * LANES])
        words.append((hi & jnp.uint32(0xFFFF0000)) | (lo >> 16))
    return jnp.concatenate(words, axis=1)


def _unpack_bf16_pairs(words):
    tiles = []
    for t in range(words.shape[1] // LANES):
        w = words[:, t * LANES:(t + 1) * LANES]
        tiles += [pltpu.bitcast(w << 16, F32), pltpu.bitcast(w & jnp.uint32(0xFFFF0000), F32)]
    return jnp.concatenate(tiles, axis=1)


def _expert_kernel(be_ref, nu_ref, x_ref, wg_ref, wu_ref, wd_ref, o_ref, h_ref, acc_ref):
    i, j = pl.program_id(0), pl.program_id(1)

    @pl.when((i == 0) & (j == 0))
    def _():
        acc_ref[...] = jnp.zeros_like(acc_ref)

    @pl.when(i < nu_ref[0])
    def _():
        @pl.when(j == 0)
        def _():
            h_ref[...] = _bf(_unpack_bf16_pairs(x_ref[...]))

        h = h_ref[...]
        act = jax.nn.silu(_dot(h, wg_ref[0])) * _dot(h, wu_ref[0])
        total = jnp.where(j == 0, 0.0, acc_ref[...]) + _dot(_bf(act), wd_ref[0])
        acc_ref[...] = total
        o_ref[...] = _pack_bf16_pairs(total)

    @pl.when((i >= nu_ref[0]) & (j == 0))
    def _():
        o_ref[...] = jnp.zeros_like(o_ref)


def _expert_swiglu(xb, block_expert, n_used, wg, wu, wd):
    n_rows, d = xb.shape[0], 2 * xb.shape[1]
    d_ff = wg.shape[2]
    tf = _ffn_tile(d_ff, 2)
    nf = d_ff // tf
    nblk = n_rows // MOE_BLOCK

    def row(i, j, be, nu):
        return jnp.minimum(i, nu[0] - 1)

    def col(i, j, be, nu):
        return jnp.where(i < nu[0], j, nf - 1)

    grid_spec = pltpu.PrefetchScalarGridSpec(
        num_scalar_prefetch=2,
        grid=(nblk, nf),
        in_specs=[pl.BlockSpec((MOE_BLOCK, d // 2), lambda i, j, be, nu: (row(i, j, be, nu), 0)),
                  pl.BlockSpec((1, d, tf), lambda i, j, be, nu: (be[row(i, j, be, nu)], 0, col(i, j, be, nu))),
                  pl.BlockSpec((1, d, tf), lambda i, j, be, nu: (be[row(i, j, be, nu)], 0, col(i, j, be, nu))),
                  pl.BlockSpec((1, tf, d), lambda i, j, be, nu: (be[row(i, j, be, nu)], col(i, j, be, nu), 0))],
        out_specs=pl.BlockSpec((MOE_BLOCK, d // 2), lambda i, j, be, nu: (i, 0)),
        scratch_shapes=[pltpu.VMEM((MOE_BLOCK, d), BF16), pltpu.VMEM((MOE_BLOCK, d), F32)],
    )
    return pl.pallas_call(
        _expert_kernel,
        grid_spec=grid_spec,
        out_shape=jax.ShapeDtypeStruct((n_rows, d // 2), jnp.uint32),
        compiler_params=_cparams("arbitrary", "arbitrary"),
        name="expert_swiglu",
    )(block_expert, n_used, xb, _bf(wg), _bf(wu), _bf(wd))


def _combine_kernel(x_ref, y1_ref, y2_ref, meta_ref, g_ref, o_ref):
    meta = meta_ref[...]
    lane = lax.broadcasted_iota(jnp.int32, meta.shape, 1)
    g1 = jnp.sum(jnp.where(lane == META_G, meta, 0.0), axis=-1, keepdims=True)
    g2 = jnp.sum(jnp.where(lane == META_G + 1, meta, 0.0), axis=-1, keepdims=True)
    y = x_ref[...] + (g1 * _unpack_bf16_pairs(y1_ref[...]) + g2 * _unpack_bf16_pairs(y2_ref[...]))
    o_ref[...] = _rmsnorm(y, g_ref[...])


def _combine_final_norm(x2, yg, meta, gain):
    n, d = x2.shape
    nblk = n // ROW_TILE
    return pl.pallas_call(
        _combine_kernel,
        grid=(nblk,),
        in_specs=[pl.BlockSpec((ROW_TILE, d), lambda i: (i, 0)),
                  pl.BlockSpec((ROW_TILE, d // 2), lambda i: (i, 0)),
                  pl.BlockSpec((ROW_TILE, d // 2), lambda i: (i + nblk, 0)),
                  pl.BlockSpec((ROW_TILE, LANES), lambda i: (i, 0)),
                  pl.BlockSpec((1, d), lambda i: (0, 0))],
        out_specs=pl.BlockSpec((ROW_TILE, d), lambda i: (i, 0)),
        out_shape=jax.ShapeDtypeStruct((n, d), F32),
        compiler_params=_cparams("parallel"),
        name="moe_combine_norm",
    )(x2, yg, yg, meta, gain.reshape(1, d))


def _moe_layer(mix_args, gain, w_router, wg, wu, wd, final_gain):
    x2, h, meta, counts = _out_router(mix_args, gain, w_router)
    n, d = x2.shape
    expert = meta[:, META_E:META_E + 2].astype(jnp.int32)
    rank = meta[:, META_RANK:META_RANK + 2].astype(jnp.int32)
    count = counts[0, :N_EXPERTS].astype(jnp.int32)
    padded = (count + MOE_BLOCK - 1) // MOE_BLOCK * MOE_BLOCK
    pad_end = jnp.cumsum(padded)
    pad_start = pad_end - padded
    dest = pad_start[expert] + rank
    n_rows = (2 * n // MOE_BLOCK + N_EXPERTS) * MOE_BLOCK
    empties = padded - count
    e_end = jnp.cumsum(empties)
    i = jnp.arange(n_rows - 2 * n, dtype=jnp.int32)
    grp = jnp.sum(i[:, None] >= e_end[None, :], axis=1).astype(jnp.int32)
    first_empty = jnp.concatenate([pad_start + count, pad_end[-1:]])
    first_index = jnp.concatenate([e_end - empties, e_end[-1:]])
    pad_rows = (first_empty[grp] + i - first_index[grp]).astype(jnp.int32)
    block_start = jnp.arange(n_rows // MOE_BLOCK, dtype=jnp.int32) * MOE_BLOCK
    block_expert = jnp.minimum(jnp.sum(block_start[:, None] >= pad_end[None, :], axis=1), N_EXPERTS - 1)
    block_expert = block_expert.astype(jnp.int32)
    n_used = (pad_end[-1:] // MOE_BLOCK).astype(jnp.int32)
    xb = _dispatch_rows(h, dest[:, 0], dest[:, 1], pad_rows)
    ys = _expert_swiglu(xb, block_expert, n_used, wg, wu, wd)
    yg = _gather_rows(ys, jnp.concatenate([dest[:, 0], dest[:, 1]]))
    return _combine_final_norm(x2, yg, meta, final_gain)


def _mixer_layer(x2, batch, norm_mix, w_in, w_out, shift_mu, gm_ln_w, gm_ln_b, gm_ws, gm_bs,
                 rw_w_up, rw_w0, rw_a_up, rw_a0, rw_g_up, rw_k_k, rw_k_a, rw_r_k, rw_lnx_w, rw_lnx_b):
    n, d = x2.shape
    gm2 = 2 * gm_ln_w.shape[0]
    w_in = _bf(w_in)
    y_gm, r, c, lw, k, v, a, b, g, bonus = _mixer_in(
        x2.reshape(batch, n // batch, d), norm_mix, w_in[:, :gm2], w_in[:, gm2:], gm_ln_w, gm_ln_b, gm_ws, gm_bs,
        shift_mu, rw_w_up, rw_w0, rw_a_up, rw_a0, rw_g_up, rw_k_k, rw_k_a, rw_r_k.reshape(-1))
    y_scan = _wkv(r, c, lw, k, v, a, b)
    flat = lambda z: z.reshape(n, -1)
    return x2, flat(y_gm), flat(y_scan), flat(bonus), flat(g), rw_lnx_w, rw_lnx_b, w_out


def kernel(x, norm_mix, w_in, w_out, shift_mu, gm_ln_w, gm_ln_b, gm_ws, gm_bs, rw_w_up, rw_w0, rw_a_up, rw_a0,
           rw_g_up, rw_k_k, rw_k_a, rw_r_k, rw_lnx_w, rw_lnx_b, norm_ffn, ffn_w_gate, ffn_w_up, ffn_w_down,
           moe_router, moe_w_gate, moe_w_up, moe_w_down, norm_final):
    batch, t, d = x.shape
    depth = norm_mix.shape[0]
    assert depth == 2 and t % ROW_TILE == 0, "two layers (dense then MoE), sequence a multiple of the row tile"
    x2 = x.reshape(batch * t, d)
    for i in range(depth):
        mixed = _mixer_layer(x2, batch, norm_mix[i], w_in[i], w_out[i], shift_mu[i], gm_ln_w[i], gm_ln_b[i],
                             gm_ws[i], gm_bs[i], rw_w_up[i], rw_w0[i], rw_a_up[i], rw_a0[i], rw_g_up[i],
                             rw_k_k[i], rw_k_a[i], rw_r_k[i], rw_lnx_w[i], rw_lnx_b[i])
        if i % 2 == 0:
            x2 = _out_swiglu(mixed, norm_ffn[i], ffn_w_gate[i // 2], ffn_w_up[i // 2], ffn_w_down[i // 2])
        else:
            x2 = _moe_layer(mixed, norm_ffn[i], moe_router[i // 2], moe_w_gate[i // 2], moe_w_up[i // 2],
                            moe_w_down[i // 2], norm_final)
    return x2.reshape(batch, t, d)
```

```python
import functools

import jax
import jax.numpy as jnp
from jax import lax
from jax.experimental import pallas as pl
from jax.experimental.pallas import tpu as pltpu
from jax.experimental.pallas import tpu_sc as plsc

F32 = jnp.float32
BF16 = jnp.bfloat16

HEAD_DIM = 64
LANES = 128
MXU_TILE = 256
GM_CHUNK = 128
WKV_CHUNK = 64
N_EXPERTS = 8
RMS_EPS = 1e-6
LN_EPS = 1e-5
GN_EPS = 64e-5
NEG_BIG = -1e30

ROW_TILE = 512
MIX_TILE = 1024
WKV_STEP = 512
MOE_BLOCK = 512
SC_WINDOW = 32
VMEM_LIMIT = 56 * 1024 * 1024


def _cparams(*sem):
    return pltpu.CompilerParams(dimension_semantics=sem, vmem_limit_bytes=VMEM_LIMIT)


def _bf(x):
    return x.astype(BF16)


def _dot(a, b):
    return jnp.dot(a, b, preferred_element_type=F32)


def _dot_nt(a, b):
    return lax.dot_general(a, b, (((1,), (1,)), ((), ())), preferred_element_type=F32)


def _split_dot_left(m, x, parts=2):
    acc = None
    rem = x
    for p in range(parts):
        hi = _bf(rem)
        d = _dot(m, hi)
        acc = d if acc is None else acc + d
        if p + 1 < parts:
            rem = rem - hi.astype(F32)
    return acc


def _rmsnorm(x, g):
    return x * lax.rsqrt(jnp.mean(x * x, axis=-1, keepdims=True) + RMS_EPS) * g


def _gelu(x):
    return 0.5 * x * (1.0 + lax.erf(x * (2.0 ** -0.5)))


def _head_avg_matrix():
    r = lax.broadcasted_iota(jnp.int32, (MXU_TILE, MXU_TILE), 0) // HEAD_DIM
    c = lax.broadcasted_iota(jnp.int32, (MXU_TILE, MXU_TILE), 1) // HEAD_DIM
    return jnp.where(r == c, 1.0 / HEAD_DIM, 0.0).astype(BF16)


def _head_dot(x, m):
    return jnp.concatenate([_dot(x[:, i:i + MXU_TILE], m) for i in range(0, x.shape[1], MXU_TILE)], axis=1)


def _head_layernorm(x, avg, w, b, eps):
    mu = _head_dot(_bf(x), avg)
    xc = x - mu
    var = _head_dot(_bf(xc * xc), avg)
    return xc * lax.rsqrt(var + eps) * w + b


def _resident(shape):
    return pl.BlockSpec(shape, lambda *_: (0,) * len(shape), pipeline_mode=pl.Buffered(1))


def _gmlp_chunk(p, avg, lnw, lnb, w2, bias, head0):
    width = p.shape[1] // 2
    u = _gelu(p[:, :width])
    vn = _head_layernorm(_gelu(p[:, width:]), avg, lnw, lnb, LN_EPS)
    mixed = []
    for pair, w in enumerate(w2):
        vp = vn[:, pair * LANES:(pair + 1) * LANES]
        v_st = _bf(jnp.concatenate([jnp.where(head0, vp, 0.0), jnp.where(head0, 0.0, vp)], axis=0))
        mixed.append(_dot(w, v_st))
    return u * (jnp.concatenate(mixed, axis=1) + bias)


def _mixer_in_kernel(x_ref, gain_ref, wgm_ref, wrw_ref, lnw_ref, lnb_ref, ws_ref, bias_ref,
                     mu_ref, wup_ref, w0_ref, aup_ref, a0_ref, gup_ref, kk_ref, ka_ref, rk_ref,
                     ygm_ref, r_ref, c_ref, lw_ref, k_ref, v_ref, a_ref, b_ref, g_ref, bonus_ref, carry_ref):
    width = r_ref.shape[2]
    rows = x_ref.shape[1]

    @pl.when(pl.program_id(1) == 0)
    def _():
        carry_ref[...] = jnp.zeros_like(carry_ref)

    avg = _head_avg_matrix()
    ones = avg * HEAD_DIM
    t = lax.broadcasted_iota(jnp.int32, (GM_CHUNK, GM_CHUNK), 0)
    s = lax.broadcasted_iota(jnp.int32, (GM_CHUNK, GM_CHUNK), 1)
    causal = s <= t
    head0 = lax.broadcasted_iota(jnp.int32, (GM_CHUNK, LANES), 1) < HEAD_DIM
    w2 = [_bf(jnp.concatenate([jnp.where(causal, ws_ref[2 * pair], 0.0),
                               jnp.where(causal, ws_ref[2 * pair + 1], 0.0)], axis=1))
          for pair in range(width // LANES)]
    t = lax.broadcasted_iota(jnp.int32, (MXU_TILE, MXU_TILE), 0)
    s = lax.broadcasted_iota(jnp.int32, (MXU_TILE, MXU_TILE), 1)
    tri = ((t // WKV_CHUNK == s // WKV_CHUNK) & (s <= t)).astype(BF16)
    first = lax.broadcasted_iota(jnp.int32, (ROW_TILE, wrw_ref.shape[1]), 0) == 0

    subs = [slice(i * ROW_TILE, (i + 1) * ROW_TILE) for i in range(rows // ROW_TILE)]
    proj = []
    for rs in subs:
        h = _bf(_rmsnorm(x_ref[0, rs, :], gain_ref[...]))
        proj.append((_dot(h, wgm_ref[...]), _dot(h, wrw_ref[...])))

    last_row = carry_ref[0:1, :]
    for rs, (p_gm, p) in zip(subs, proj):
        for ch in range(ROW_TILE // GM_CHUNK):
            ts = slice(ch * GM_CHUNK, (ch + 1) * GM_CHUNK)
            out_rows = slice(rs.start + ts.start, rs.start + ts.stop)
            ygm_ref[0, out_rows, :] = _bf(_gmlp_chunk(p_gm[ts], avg, lnw_ref[...], lnb_ref[...], w2, bias_ref[...],
                                                      head0))
        prev = jnp.where(first, last_row, pltpu.roll(p, 1, axis=0))
        last_row = p[ROW_TILE - 1:ROW_TILE, :]
        ps = p + (prev - p) * mu_ref[...]
        r = ps[:, :width]
        k = ps[:, width:2 * width]
        v = ps[:, 2 * width:3 * width]
        lora_in = ps[:, 3 * width:3 * width + LANES]
        gd = ps[:, 3 * width + LANES:]
        lw = -(jnp.exp(-0.5)) * jax.nn.sigmoid(w0_ref[...] + _dot(_bf(jnp.tanh(lora_in)), wup_ref[...]))
        a_lr = jax.nn.sigmoid(a0_ref[...] + _dot(_bf(lora_in), aup_ref[...]))
        kk = k * kk_ref[...]
        kk = kk * lax.rsqrt(jnp.maximum(_head_dot(_bf(kk * kk), ones), 1e-24))
        kmod = k * (1.0 + (a_lr - 1.0) * ka_ref[...])
        lw_ref[0, rs, :] = lw
        c_ref[0, rs, :] = jnp.concatenate([_split_dot_left(tri, lw[i:i + MXU_TILE])
                                           for i in range(0, ROW_TILE, MXU_TILE)], axis=0)
        r_ref[0, rs, :] = _bf(r)
        k_ref[0, rs, :] = _bf(kmod)
        v_ref[0, rs, :] = _bf(v)
        a_ref[0, rs, :] = _bf(-kk)
        b_ref[0, rs, :] = _bf(kk * a_lr)
        g_ref[0, rs, :] = _bf(_dot(_bf(jax.nn.sigmoid(gd)), gup_ref[...]))
        bonus_ref[0, rs, :] = _bf(_head_dot(_bf(r * kmod * rk_ref[...]), ones) * v)
    carry_ref[0:1, :] = last_row


def _mixer_in(x3, gain, w_gm, w_rw, ln_w, ln_b, ws, bs, mu, w_up, w0, a_up, a0, g_up, k_k, k_a, r_k):
    batch, t, d = x3.shape
    width = w0.shape[0]
    rank = w_up.shape[0]
    zeros = jnp.zeros((LANES - rank, width), F32)
    wup_pad = _bf(jnp.concatenate([w_up, zeros], axis=0))
    aup_pad = _bf(jnp.concatenate([zeros, a_up], axis=0))
    bias = jnp.repeat(bs.T, HEAD_DIM, axis=1)
    row = lambda z: z.reshape(1, -1)
    params = (row(gain), w_gm, w_rw, row(ln_w), row(ln_b), ws, bias, row(mu), wup_pad, row(w0), aup_pad, row(a0),
              _bf(g_up), row(k_k), row(k_a), row(r_k))
    tile = min(MIX_TILE, t)
    out = pl.BlockSpec((1, tile, width), lambda b, i: (b, i, 0))
    sds = lambda dtype: jax.ShapeDtypeStruct((batch, t, width), dtype)
    return pl.pallas_call(
        _mixer_in_kernel,
        grid=(batch, t // tile),
        in_specs=[pl.BlockSpec((1, tile, d), lambda b, i: (b, i, 0))] + [_resident(z.shape) for z in params],
        out_specs=[out] * 10,
        out_shape=[sds(BF16), sds(BF16), sds(F32), sds(F32)] + [sds(BF16)] * 6,
        scratch_shapes=[pltpu.VMEM((8, w_rw.shape[1]), F32)],
        compiler_params=_cparams("parallel", "arbitrary"),
        name="mixer_in",
    )(x3, *params)


def _wkv_kernel(r_ref, c_ref, lw_ref, k_ref, v_ref, a_ref, b_ref, y_ref, h_ref,
                q_s, v_s, ark_s, bcrb_s, mz_s, rhs_s, sol_s, t_s, rh_s, rkv_s, kv_s, dec_s, hc_s, yc_s):
    @pl.when(pl.program_id(1) == 0)
    def _():
        h_ref[...] = jnp.zeros_like(h_ref)

    n = WKV_CHUNK
    two = 2 * n
    head0 = lax.broadcasted_iota(jnp.int32, (n, LANES), 1) < HEAD_DIM
    row = lax.broadcasted_iota(jnp.int32, (two, two), 0)
    col = lax.broadcasted_iota(jnp.int32, (two, two), 1)
    eye = row == col
    row2 = lax.broadcasted_iota(jnp.int32, (two, 2 * two), 0)
    col2 = lax.broadcasted_iota(jnp.int32, (two, 2 * two), 1) % two
    same2 = (row2 // n) == (col2 // n)
    strict2 = same2 & ((col2 % n) < (row2 % n))
    incl2 = same2 & ((col2 % n) <= (row2 % n))
    npairs = y_ref.shape[2] // LANES
    nchunks = y_ref.shape[1] // n
    items = [(ch, pair) for ch in range(nchunks) for pair in range(npairs)]

    def window(ch, pair):
        return slice(ch * n, (ch + 1) * n), slice(pair * LANES, (pair + 1) * LANES)

    def stack(z):
        return jnp.concatenate([jnp.where(head0, z, 0.0), jnp.where(head0, 0.0, z)], axis=0)

    def dup(z):
        zb = _bf(z)
        return jnp.concatenate([zb, zb], axis=0)

    def scores(it):
        ts, ls = window(*items[it])
        r, k, v, a, b = (ref[0, ts, ls].astype(F32) for ref in (r_ref, k_ref, v_ref, a_ref, b_ref))
        c = c_ref[0, ts, ls]
        cex = c - lw_ref[0, ts, ls]
        mid = c[n // 2 - 1:n // 2, :]
        last = c[n - 1:n, :]
        g_inv = jnp.exp(mid - c)
        g_end = jnp.exp(last - c)
        ar_st = _bf(jnp.concatenate([stack(a * jnp.exp(cex - mid)), stack(r * jnp.exp(c - mid))], axis=0))
        bk_dup = jnp.concatenate([dup(b * g_inv), dup(k * g_inv)], axis=0)
        sc = _dot_nt(ar_st, bk_dup)
        top = jnp.where(strict2, sc[:two], 0.0)
        bot = jnp.where(incl2, sc[two:], 0.0)
        t_s[it] = jnp.where(eye, 1.0, top[:, :two])
        q_s[it] = _bf(top[:, :two])
        bcrb_s[it] = _bf(jnp.concatenate([stack(b * g_end).T, bot[:, :two]], axis=0))
        ark_s[it] = _bf(jnp.concatenate([top[:, two:], bot[:, two:], stack(k * g_end).T], axis=0))
        v_s[it] = _bf(stack(v))
        rh_s[it] = stack(r * jnp.exp(c))
        rhs_s[it, :, :LANES] = _bf(stack(a * jnp.exp(cex)))
        e_col = jnp.sum(jnp.where(eye, jnp.exp(last), 0.0), axis=1, keepdims=True)
        dec_s[it] = jnp.broadcast_to(e_col, (two, LANES))

    def values(it):
        v_st = v_s[it]
        prod = _dot(ark_s[it], v_st)
        rhs_s[it, :, LANES:] = _bf(prod[:two])
        rkv_s[it] = prod[two:2 * two]
        kv_s[it] = prod[2 * two:]

    levels = n.bit_length() - 1

    def inverse_level(level, it):
        q = q_s[it]
        t_acc = t_s[it]
        if level == 1:
            q = _bf(_dot(q, q))
        if level + 1 < levels:
            prod = _dot(q, jnp.concatenate([_bf(t_acc), q], axis=1))
            q_s[it] = _bf(prod[:, two:])
            t_s[it] = t_acc + prod[:, :two]
        else:
            t_s[it] = t_acc + _dot(q, _bf(t_acc))

    def solve(it):
        sol_s[it] = _bf(_dot(_bf(t_s[it]), rhs_s[it]))

    def fold(it):
        prod = _dot(bcrb_s[it], sol_s[it])
        mz_s[it] = _bf(jnp.concatenate([prod[:two, :LANES], rh_s[it] + prod[two:, :LANES]], axis=0))
        hc_s[it] = prod[:two, LANES:] + kv_s[it]
        yc_s[it] = prod[two:, LANES:] + rkv_s[it]

    def advance(it):
        ts, ls = window(*items[it])
        pair = items[it][1]
        h = h_ref[pair]
        gmat = _dot(mz_s[it], _bf(h))
        h_ref[pair] = dec_s[it] * h + gmat[:two] + hc_s[it]
        y_st = gmat[two:] + yc_s[it]
        y_ref[0, ts, ls] = y_st[:n] + y_st[n:]

    stages = ([scores, values] + [functools.partial(inverse_level, level) for level in range(1, levels)]
              + [solve, fold, advance])
    for stage in stages:
        for it in range(len(items)):
            stage(it)


def _wkv(r, c, lw, k, v, a, b):
    batch, t, width = r.shape
    npairs = width // LANES
    items = (WKV_STEP // WKV_CHUNK) * npairs
    spec = pl.BlockSpec((1, WKV_STEP, width), lambda bi, i: (bi, i, 0))
    sq = lambda dtype: pltpu.VMEM((items, LANES, LANES), dtype)
    return pl.pallas_call(
        _wkv_kernel,
        grid=(batch, t // WKV_STEP),
        in_specs=[spec] * 7,
        out_specs=spec,
        out_shape=jax.ShapeDtypeStruct((batch, t, width), F32),
        scratch_shapes=[pltpu.VMEM((npairs, LANES, LANES), F32)] + [sq(BF16)] * 2
                       + [pltpu.VMEM((items, 3 * LANES, LANES), BF16)]
                       + [pltpu.VMEM((items, 2 * LANES, LANES), BF16)] * 2
                       + [pltpu.VMEM((items, LANES, 2 * LANES), BF16)] * 2 + [sq(F32)] * 7,
        compiler_params=_cparams("parallel", "arbitrary"),
        name="wkv7",
    )(r, c, lw, k, v, a, b)


N_MIX_OUT_REFS = 9


def _mixer_out(x_ref, ygm_ref, ys_ref, bonus_ref, g_ref, lnw_ref, lnb_ref, wtop_ref, wbot_ref, rows=slice(None)):
    yn = _head_layernorm(ys_ref[rows, :], _head_avg_matrix(), lnw_ref[...], lnb_ref[...], GN_EPS)
    y_rw = (yn + bonus_ref[rows, :]) * g_ref[rows, :]
    return x_ref[rows, :] + _dot(ygm_ref[rows, :], wtop_ref[...]) + _dot(_bf(y_rw), wbot_ref[...])


def _mixer_out_operands(x2, y_gm, y_scan, bonus, g, lnx_w, lnx_b, w_out, tile=ROW_TILE):
    d = x2.shape[1]
    width = y_gm.shape[1]
    rows = lambda w: pl.BlockSpec((tile, w), lambda i: (i, 0))
    w_out = _bf(w_out)
    specs = [rows(d), rows(width), rows(width), rows(width), rows(width),
             _resident((1, width)), _resident((1, width)), _resident((width, d)), _resident((width, d))]
    args = (x2, y_gm, y_scan, bonus, g, lnx_w.reshape(1, width), lnx_b.reshape(1, width), w_out[:width], w_out[width:])
    assert len(specs) == len(args) == N_MIX_OUT_REFS
    return specs, args


def _out_swiglu_kernel(*refs):
    mix_refs, (gain_ref, wg_ref, wu_ref, wd_ref, o_ref) = refs[:N_MIX_OUT_REFS], refs[N_MIX_OUT_REFS:]
    x = _mixer_out(*mix_refs)
    h = _bf(_rmsnorm(x, gain_ref[...]))
    act = jax.nn.silu(_dot(h, wg_ref[...])) * _dot(h, wu_ref[...])
    o_ref[...] = x + _dot(_bf(act), wd_ref[...])


def _ffn_tile(d_ff, parts):
    assert d_ff % (parts * LANES) == 0, f"d_ff={d_ff} does not split into {parts} lane-aligned tiles"
    return d_ff // parts


def _out_swiglu(mix_args, gain, wg, wu, wd):
    specs, args = _mixer_out_operands(*mix_args)
    n, d = args[0].shape
    d_ff = wg.shape[1]
    return pl.pallas_call(
        _out_swiglu_kernel,
        grid=(n // ROW_TILE,),
        in_specs=specs + [_resident((1, d)), _resident((d, d_ff)), _resident((d, d_ff)), _resident((d_ff, d))],
        out_specs=pl.BlockSpec((ROW_TILE, d), lambda i: (i, 0)),
        out_shape=jax.ShapeDtypeStruct((n, d), F32),
        compiler_params=_cparams("parallel"),
        name="out_dense_swiglu",
    )(*args, gain.reshape(1, d), _bf(wg), _bf(wu), _bf(wd))


META_E, META_G, META_RANK = 0, 2, 4


def _out_router_kernel(*refs):
    mix_refs = refs[:N_MIX_OUT_REFS]
    g_ref, whi_ref, wlo_ref, x_ref, h_ref, meta_ref, cnt_ref, run_ref = refs[N_MIX_OUT_REFS:]

    @pl.when(pl.program_id(0) == 0)
    def _():
        run_ref[...] = jnp.zeros_like(run_ref)

    lane = lax.broadcasted_iota(jnp.int32, (ROW_TILE, LANES), 1)
    t = lax.broadcasted_iota(jnp.int32, (ROW_TILE, ROW_TILE), 0)
    s = lax.broadcasted_iota(jnp.int32, (ROW_TILE, ROW_TILE), 1)
    earlier = jnp.where(s < t, 1.0, 0.0).astype(BF16)

    subs = [slice(i * ROW_TILE, (i + 1) * ROW_TILE) for i in range(x_ref.shape[0] // ROW_TILE)]
    xs = [_mixer_out(*mix_refs, rows=rs) for rs in subs]

    run = run_ref[0:1, :]
    for rs, x in zip(subs, xs):
        x_ref[rs, :] = x
        h = _rmsnorm(x, g_ref[...])
        h_ref[rs, :] = _pack_bf16_pairs(h)
        h_hi = _bf(h)
        h_lo = _bf(h - h_hi.astype(F32))
        logits = _dot(h_hi, whi_ref[...]) + _dot(h_lo, whi_ref[...]) + _dot(h_hi, wlo_ref[...])
        logits = jnp.where(lane < N_EXPERTS, logits, NEG_BIG)
        v1 = jnp.max(logits, axis=-1, keepdims=True)
        e1 = jnp.min(jnp.where(logits == v1, lane, LANES), axis=-1, keepdims=True)
        oh1 = lane == e1
        rest = jnp.where(oh1, NEG_BIG, logits)
        v2 = jnp.max(rest, axis=-1, keepdims=True)
        e2 = jnp.min(jnp.where(rest == v2, lane, LANES), axis=-1, keepdims=True)
        oh2 = lane == e2
        ex = jnp.exp(v2 - v1)
        g1 = 1.0 / (1.0 + ex)
        g2 = ex / (1.0 + ex)
        cnt = jnp.where(oh1 | oh2, 1.0, 0.0)
        before = _dot(earlier, _bf(cnt)) + run
        rank1 = jnp.sum(jnp.where(oh1, before, 0.0), axis=-1, keepdims=True)
        rank2 = jnp.sum(jnp.where(oh2, before, 0.0), axis=-1, keepdims=True)
        fields = (e1.astype(F32), e2.astype(F32), g1, g2, rank1, rank2)
        meta = jnp.zeros((ROW_TILE, LANES), F32)
        for idx, val in enumerate(fields):
            meta = jnp.where(lane == idx, val, meta)
        meta_ref[rs, :] = meta
        run = run + jnp.sum(cnt, axis=0, keepdims=True)
    run_ref[...] = jnp.broadcast_to(run, run_ref.shape)
    cnt_ref[...] = jnp.broadcast_to(run, cnt_ref.shape)


def _out_router(mix_args, gain, w_router):
    n, d = mix_args[0].shape
    tile = min(MIX_TILE, n)
    specs, args = _mixer_out_operands(*mix_args, tile=tile)
    wr = jnp.concatenate([w_router, jnp.zeros((d, LANES - w_router.shape[1]), F32)], axis=1)
    wr_hi = _bf(wr)
    wr_lo = _bf(wr - wr_hi.astype(F32))
    rows = lambda w: pl.BlockSpec((tile, w), lambda i: (i, 0))
    return pl.pallas_call(
        _out_router_kernel,
        grid=(n // tile,),
        in_specs=specs + [_resident((1, d)), _resident((d, LANES)), _resident((d, LANES))],
        out_specs=[rows(d), rows(d // 2), rows(LANES), pl.BlockSpec((8, LANES), lambda i: (0, 0))],
        out_shape=[jax.ShapeDtypeStruct((n, d), F32), jax.ShapeDtypeStruct((n, d // 2), jnp.uint32),
                   jax.ShapeDtypeStruct((n, LANES), F32), jax.ShapeDtypeStruct((8, LANES), F32)],
        scratch_shapes=[pltpu.VMEM((8, LANES), F32)],
        compiler_params=_cparams("arbitrary"),
        name="out_moe_router",
    )(*args, gain.reshape(1, d), wr_hi, wr_lo)


def _sc_workers():
    info = plsc.get_sparse_core_info()
    return info.num_cores, info.num_cores * info.num_subcores


def _dispatch_rows(h, dest0, dest1, pad_rows):
    nc, workers = _sc_workers()
    n, d = h.shape
    n_pad = pad_rows.shape[0]
    per_worker = n // workers
    nchunk = per_worker // SC_WINDOW
    npad = n_pad // (workers * SC_WINDOW)
    assert n == workers * nchunk * SC_WINDOW and nchunk % 2 == 0 and n_pad == workers * npad * SC_WINDOW
    mesh = plsc.VectorSubcoreMesh(core_axis_name="c", subcore_axis_name="s")
    idx_t = lambda m: pltpu.VMEM((m, SC_WINDOW), jnp.int32)
    buf_t = pltpu.VMEM((SC_WINDOW, d), h.dtype)

    @functools.partial(
        pl.kernel, mesh=mesh,
        out_type=jax.ShapeDtypeStruct((2 * n + n_pad, d), h.dtype),
        scratch_types=[idx_t(nchunk), idx_t(nchunk), idx_t(npad), buf_t, buf_t, buf_t]
                      + [pltpu.SemaphoreType.DMA] * 6,
    )
    def dispatch_kernel(h_hbm, d0_hbm, d1_hbm, pad_hbm, zero_hbm, out_hbm, d0_v, d1_v, pad_v, buf0, buf1, zbuf,
                        r0, r1, w0, w1, x0, x1):
        wid = lax.axis_index("s") * nc + lax.axis_index("c")
        base = wid * per_worker
        pltpu.sync_copy(d0_hbm.at[wid], d0_v)
        pltpu.sync_copy(d1_hbm.at[wid], d1_v)
        pltpu.sync_copy(pad_hbm.at[wid], pad_v)
        pltpu.sync_copy(zero_hbm, zbuf)
        bufs, rsem, wsem, xsem = (buf0, buf1), (r0, r1), (w0, w1), (x0, x1)

        def read(c, slot):
            return pltpu.make_async_copy(h_hbm.at[pl.ds(base + c * SC_WINDOW, SC_WINDOW)], bufs[slot], rsem[slot])

        read(0, 0).start()
        read(1, 1).start()
        for pc in range(npad):
            pltpu.sync_copy(zbuf, out_hbm.at[pad_v.at[pc]])

        @pl.loop(0, nchunk, step=2)
        def _(c):
            for slot in range(2):
                cc = c + slot
                read(cc, slot).wait()
                first = pltpu.make_async_copy(bufs[slot], out_hbm.at[d0_v.at[cc]], wsem[slot])
                second = pltpu.make_async_copy(bufs[slot], out_hbm.at[d1_v.at[cc]], xsem[slot])
                first.start()
                second.start()
                first.wait()
                second.wait()

                @pl.when(cc + 2 < nchunk)
                def _():
                    read(cc + 2, slot).start()

    split = lambda z, m: z.reshape(workers, m, SC_WINDOW)
    return dispatch_kernel(h, split(dest0, nchunk), split(dest1, nchunk), split(pad_rows, npad),
                           jnp.zeros((SC_WINDOW, d), h.dtype))


def _gather_rows(src, idx):
    nc, workers = _sc_workers()
    m, d = idx.shape[0], src.shape[1]
    per_worker = m // workers
    nchunk = per_worker // SC_WINDOW
    assert m == workers * nchunk * SC_WINDOW and nchunk % 2 == 0, "row count must split evenly over subcores"
    mesh = plsc.VectorSubcoreMesh(core_axis_name="c", subcore_axis_name="s")

    @functools.partial(
        pl.kernel, mesh=mesh,
        out_type=jax.ShapeDtypeStruct((m, d), src.dtype),
        scratch_types=[pltpu.VMEM((nchunk, SC_WINDOW), jnp.int32),
                       pltpu.VMEM((SC_WINDOW, d), src.dtype), pltpu.VMEM((SC_WINDOW, d), src.dtype),
                       pltpu.SemaphoreType.DMA, pltpu.SemaphoreType.DMA,
                       pltpu.SemaphoreType.DMA, pltpu.SemaphoreType.DMA],
    )
    def gather_kernel(src_hbm, idx_hbm, out_hbm, idx_v, buf0, buf1, g0, g1, w0, w1):
        wid = lax.axis_index("s") * nc + lax.axis_index("c")
        base = wid * per_worker
        pltpu.sync_copy(idx_hbm.at[wid], idx_v)
        bufs, gsem, wsem = (buf0, buf1), (g0, g1), (w0, w1)

        def gather(c, slot):
            return pltpu.make_async_copy(src_hbm.at[idx_v.at[c]], bufs[slot], gsem[slot])

        def write(c, slot):
            return pltpu.make_async_copy(bufs[slot], out_hbm.at[pl.ds(base + c * SC_WINDOW, SC_WINDOW)], wsem[slot])

        gather(0, 0).start()
        gather(1, 1).start()

        @pl.loop(0, nchunk, step=2)
        def _(c):
            for slot in range(2):
                cc = c + slot
                gather(cc, slot).wait()
                write(cc, slot).start()
                write(cc, slot).wait()

                @pl.when(cc + 2 < nchunk)
                def _():
                    gather(cc + 2, slot).start()

    return gather_kernel(src, idx.reshape(workers, nchunk, SC_WINDOW))


def _pack_bf16_pairs(y):
    bits = lambda z: pltpu.bitcast(z.astype(BF16).astype(F32), jnp.uint32)
    words = []
    for t in range(0, y.shape[1] // LANES, 2):
        lo = bits(y[:, t * LANES:(t + 1) * LANES])
        hi = bits(y[:, (t + 1) * LANES:(t + 2) * LANES])
        words.append((hi & jnp.uint32(0xFFFF0000)) | (lo >> 16))
    return jnp.concatenate(words, axis=1)


def _unpack_bf16_pairs(words):
    tiles = []
    for t in range(words.shape[1] // LANES):
        w = words[:, t * LANES:(t + 1) * LANES]
        tiles += [pltpu.bitcast(w << 16, F32), pltpu.bitcast(w & jnp.uint32(0xFFFF0000), F32)]
    return jnp.concatenate(tiles, axis=1)


def _expert_kernel(be_ref, nu_ref, x_ref, wg_ref, wu_ref, wd_ref, o_ref, acc_ref):
    i, j = pl.program_id(0), pl.program_id(1)

    @pl.when((i == 0) & (j == 0))
    def _():
        acc_ref[...] = jnp.zeros_like(acc_ref)

    @pl.when(i < nu_ref[0])
    def _():
        h = _bf(_unpack_bf16_pairs(x_ref[...]))
        act = jax.nn.silu(_dot(h, wg_ref[0])) * _dot(h, wu_ref[0])
        total = jnp.where(j == 0, 0.0, acc_ref[...]) + _dot(_bf(act), wd_ref[0])
        acc_ref[...] = total
        o_ref[...] = _pack_bf16_pairs(total)

    @pl.when((i >= nu_ref[0]) & (j == 0))
    def _():
        o_ref[...] = jnp.zeros_like(o_ref)


def _expert_swiglu(xb, block_expert, n_used, wg, wu, wd):
    n_rows, d = xb.shape[0], 2 * xb.shape[1]
    d_ff = wg.shape[2]
    tf = _ffn_tile(d_ff, 2)
    nf = d_ff // tf
    nblk = n_rows // MOE_BLOCK

    def row(i, j, be, nu):
        return jnp.minimum(i, nu[0] - 1)

    def col(i, j, be, nu):
        return jnp.where(i < nu[0], j, nf - 1)

    grid_spec = pltpu.PrefetchScalarGridSpec(
        num_scalar_prefetch=2,
        grid=(nblk, nf),
        in_specs=[pl.BlockSpec((MOE_BLOCK, d // 2), lambda i, j, be, nu: (row(i, j, be, nu), 0)),
                  pl.BlockSpec((1, d, tf), lambda i, j, be, nu: (be[row(i, j, be, nu)], 0, col(i, j, be, nu))),
                  pl.BlockSpec((1, d, tf), lambda i, j, be, nu: (be[row(i, j, be, nu)], 0, col(i, j, be, nu))),
                  pl.BlockSpec((1, tf, d), lambda i, j, be, nu: (be[row(i, j, be, nu)], col(i, j, be, nu), 0))],
        out_specs=pl.BlockSpec((MOE_BLOCK, d // 2), lambda i, j, be, nu: (i, 0)),
        scratch_shapes=[pltpu.VMEM((MOE_BLOCK, d), F32)],
    )
    return pl.pallas_call(
        _expert_kernel,
        grid_spec=grid_spec,
        out_shape=jax.ShapeDtypeStruct((n_rows, d // 2), jnp.uint32),
        compiler_params=_cparams("arbitrary", "arbitrary"),
        name="expert_swiglu",
    )(block_expert, n_used, xb, _bf(wg), _bf(wu), _bf(wd))


def _combine_kernel(x_ref, y1_ref, y2_ref, meta_ref, g_ref, o_ref):
    meta = meta_ref[...]
    lane = lax.broadcasted_iota(jnp.int32, meta.shape, 1)
    g1 = jnp.sum(jnp.where(lane == META_G, meta, 0.0), axis=-1, keepdims=True)
    g2 = jnp.sum(jnp.where(lane == META_G + 1, meta, 0.0), axis=-1, keepdims=True)
    y = x_ref[...] + (g1 * _unpack_bf16_pairs(y1_ref[...]) + g2 * _unpack_bf16_pairs(y2_ref[...]))
    o_ref[...] = _rmsnorm(y, g_ref[...])


def _combine_final_norm(x2, yg, meta, gain):
    n, d = x2.shape
    nblk = n // ROW_TILE
    return pl.pallas_call(
        _combine_kernel,
        grid=(nblk,),
        in_specs=[pl.BlockSpec((ROW_TILE, d), lambda i: (i, 0)),
                  pl.BlockSpec((ROW_TILE, d // 2), lambda i: (i, 0)),
                  pl.BlockSpec((ROW_TILE, d // 2), lambda i: (i + nblk, 0)),
                  pl.BlockSpec((ROW_TILE, LANES), lambda i: (i, 0)),
                  pl.BlockSpec((1, d), lambda i: (0, 0))],
        out_specs=pl.BlockSpec((ROW_TILE, d), lambda i: (i, 0)),
        out_shape=jax.ShapeDtypeStruct((n, d), F32),
        compiler_params=_cparams("parallel"),
        name="moe_combine_norm",
    )(x2, yg, yg, meta, gain.reshape(1, d))


def _moe_layer(mix_args, gain, w_router, wg, wu, wd, final_gain):
    x2, h, meta, counts = _out_router(mix_args, gain, w_router)
    n, d = x2.shape
    expert = meta[:, META_E:META_E + 2].astype(jnp.int32)
    rank = meta[:, META_RANK:META_RANK + 2].astype(jnp.int32)
    count = counts[0, :N_EXPERTS].astype(jnp.int32)
    padded = (count + MOE_BLOCK - 1) // MOE_BLOCK * MOE_BLOCK
    pad_end = jnp.cumsum(padded)
    pad_start = pad_end - padded
    dest = pad_start[expert] + rank
    n_rows = (2 * n // MOE_BLOCK + N_EXPERTS) * MOE_BLOCK
    empties = padded - count
    e_end = jnp.cumsum(empties)
    i = jnp.arange(n_rows - 2 * n, dtype=jnp.int32)
    grp = jnp.sum(i[:, None] >= e_end[None, :], axis=1).astype(jnp.int32)
    first_empty = jnp.concatenate([pad_start + count, pad_end[-1:]])
    first_index = jnp.concatenate([e_end - empties, e_end[-1:]])
    pad_rows = (first_empty[grp] + i - first_index[grp]).astype(jnp.int32)
    block_start = jnp.arange(n_rows // MOE_BLOCK, dtype=jnp.int32) * MOE_BLOCK
    block_expert = jnp.minimum(jnp.sum(block_start[:, None] >= pad_end[None, :], axis=1), N_EXPERTS - 1)
    block_expert = block_expert.astype(jnp.int32)
    n_used = (pad_end[-1:] // MOE_BLOCK).astype(jnp.int32)
    xb = _dispatch_rows(h, dest[:, 0], dest[:, 1], pad_rows)
    ys = _expert_swiglu(xb, block_expert, n_used, wg, wu, wd)
    yg = _gather_rows(ys, jnp.concatenate([dest[:, 0], dest[:, 1]]))
    return _combine_final_norm(x2, yg, meta, final_gain)


def _mixer_layer(x2, batch, norm_mix, w_in, w_out, shift_mu, gm_ln_w, gm_ln_b, gm_ws, gm_bs,
                 rw_w_up, rw_w0, rw_a_up, rw_a0, rw_g_up, rw_k_k, rw_k_a, rw_r_k, rw_lnx_w, rw_lnx_b):
    n, d = x2.shape
    gm2 = 2 * gm_ln_w.shape[0]
    w_in = _bf(w_in)
    y_gm, r, c, lw, k, v, a, b, g, bonus = _mixer_in(
        x2.reshape(batch, n // batch, d), norm_mix, w_in[:, :gm2], w_in[:, gm2:], gm_ln_w, gm_ln_b, gm_ws, gm_bs,
        shift_mu, rw_w_up, rw_w0, rw_a_up, rw_a0, rw_g_up, rw_k_k, rw_k_a, rw_r_k.reshape(-1))
    y_scan = _wkv(r, c, lw, k, v, a, b)
    flat = lambda z: z.reshape(n, -1)
    return x2, flat(y_gm), flat(y_scan), flat(bonus), flat(g), rw_lnx_w, rw_lnx_b, w_out


def kernel(x, norm_mix, w_in, w_out, shift_mu, gm_ln_w, gm_ln_b, gm_ws, gm_bs, rw_w_up, rw_w0, rw_a_up, rw_a0,
           rw_g_up, rw_k_k, rw_k_a, rw_r_k, rw_lnx_w, rw_lnx_b, norm_ffn, ffn_w_gate, ffn_w_up, ffn_w_down,
           moe_router, moe_w_gate, moe_w_up, moe_w_down, norm_final):
    batch, t, d = x.shape
    depth = norm_mix.shape[0]
    assert depth == 2 and t % ROW_TILE == 0, "two layers (dense then MoE), sequence a multiple of the row tile"
    x2 = x.reshape(batch * t, d)
    for i in range(depth):
        mixed = _mixer_layer(x2, batch, norm_mix[i], w_in[i], w_out[i], shift_mu[i], gm_ln_w[i], gm_ln_b[i],
                             gm_ws[i], gm_bs[i], rw_w_up[i], rw_w0[i], rw_a_up[i], rw_a0[i], rw_g_up[i],
                             rw_k_k[i], rw_k_a[i], rw_r_k[i], rw_lnx_w[i], rw_lnx_b[i])
        if i % 2 == 0:
            x2 = _out_swiglu(mixed, norm_ffn[i], ffn_w_gate[i // 2], ffn_w_up[i // 2], ffn_w_down[i // 2])
        else:
            x2 = _moe_layer(mixed, norm_ffn[i], moe_router[i // 2], moe_w_gate[i // 2], moe_w_up[i // 2],
                            moe_w_down[i // 2], norm_final)
    return x2.reshape(batch, t, d)
```

```python
import functools

import jax
import jax.numpy as jnp
from jax import lax
from jax.experimental import pallas as pl
from jax.experimental.pallas import tpu as pltpu
from jax.experimental.pallas import tpu_sc as plsc

F32 = jnp.float32
BF16 = jnp.bfloat16

HEAD_DIM = 64
LANES = 128
MXU_TILE = 256
GM_CHUNK = 128
WKV_CHUNK = 64
N_EXPERTS = 8
RMS_EPS = 1e-6
LN_EPS = 1e-5
GN_EPS = 64e-5
NEG_BIG = -1e30

ROW_TILE = 512
MIX_TILE = 1024
WKV_STEP = 512
MOE_BLOCK = 512
SC_WINDOW = 32
VMEM_LIMIT = 56 * 1024 * 1024


def _cparams(*sem):
    return pltpu.CompilerParams(dimension_semantics=sem, vmem_limit_bytes=VMEM_LIMIT)


def _bf(x):
    return x.astype(BF16)


def _dot(a, b):
    return jnp.dot(a, b, preferred_element_type=F32)


def _dot_nt(a, b):
    return lax.dot_general(a, b, (((1,), (1,)), ((), ())), preferred_element_type=F32)


def _split_dot_left(m, x, parts=2):
    acc = None
    rem = x
    for p in range(parts):
        hi = _bf(rem)
        d = _dot(m, hi)
        acc = d if acc is None else acc + d
        if p + 1 < parts:
            rem = rem - hi.astype(F32)
    return acc


def _rmsnorm(x, g):
    return x * lax.rsqrt(jnp.mean(x * x, axis=-1, keepdims=True) + RMS_EPS) * g


def _gelu(x):
    return 0.5 * x * (1.0 + lax.erf(x * (2.0 ** -0.5)))


def _head_avg_matrix():
    r = lax.broadcasted_iota(jnp.int32, (MXU_TILE, MXU_TILE), 0) // HEAD_DIM
    c = lax.broadcasted_iota(jnp.int32, (MXU_TILE, MXU_TILE), 1) // HEAD_DIM
    return jnp.where(r == c, 1.0 / HEAD_DIM, 0.0).astype(BF16)


def _head_dot(x, m):
    return jnp.concatenate([_dot(x[:, i:i + MXU_TILE], m) for i in range(0, x.shape[1], MXU_TILE)], axis=1)


def _head_layernorm(x, avg, w, b, eps):
    mu = _head_dot(_bf(x), avg)
    xc = x - mu
    var = _head_dot(_bf(xc * xc), avg)
    return xc * lax.rsqrt(var + eps) * w + b


def _resident(shape):
    return pl.BlockSpec(shape, lambda *_: (0,) * len(shape), pipeline_mode=pl.Buffered(1))


def _gmlp_chunk(p, avg, lnw, lnb, w2, bias, head0):
    width = p.shape[1] // 2
    u = _gelu(p[:, :width])
    vn = _head_layernorm(_gelu(p[:, width:]), avg, lnw, lnb, LN_EPS)
    mixed = []
    for pair, w in enumerate(w2):
        vp = vn[:, pair * LANES:(pair + 1) * LANES]
        v_st = _bf(jnp.concatenate([jnp.where(head0, vp, 0.0), jnp.where(head0, 0.0, vp)], axis=0))
        mixed.append(_dot(w, v_st))
    return u * (jnp.concatenate(mixed, axis=1) + bias)


def _mixer_in_kernel(x_ref, gain_ref, wgm_ref, wrw_ref, lnw_ref, lnb_ref, ws_ref, bias_ref,
                     mu_ref, wup_ref, w0_ref, aup_ref, a0_ref, gup_ref, kk_ref, ka_ref, rk_ref,
                     ygm_ref, r_ref, c_ref, lw_ref, k_ref, v_ref, a_ref, b_ref, g_ref, bonus_ref, carry_ref):
    width = r_ref.shape[2]
    rows = x_ref.shape[1]

    @pl.when(pl.program_id(1) == 0)
    def _():
        carry_ref[...] = jnp.zeros_like(carry_ref)

    avg = _head_avg_matrix()
    ones = avg * HEAD_DIM
    t = lax.broadcasted_iota(jnp.int32, (GM_CHUNK, GM_CHUNK), 0)
    s = lax.broadcasted_iota(jnp.int32, (GM_CHUNK, GM_CHUNK), 1)
    causal = s <= t
    head0 = lax.broadcasted_iota(jnp.int32, (GM_CHUNK, LANES), 1) < HEAD_DIM
    w2 = [_bf(jnp.concatenate([jnp.where(causal, ws_ref[2 * pair], 0.0),
                               jnp.where(causal, ws_ref[2 * pair + 1], 0.0)], axis=1))
          for pair in range(width // LANES)]
    t = lax.broadcasted_iota(jnp.int32, (MXU_TILE, MXU_TILE), 0)
    s = lax.broadcasted_iota(jnp.int32, (MXU_TILE, MXU_TILE), 1)
    tri = ((t // WKV_CHUNK == s // WKV_CHUNK) & (s <= t)).astype(BF16)
    first = lax.broadcasted_iota(jnp.int32, (ROW_TILE, wrw_ref.shape[1]), 0) == 0

    subs = [slice(i * ROW_TILE, (i + 1) * ROW_TILE) for i in range(rows // ROW_TILE)]
    proj = []
    for rs in subs:
        h = _bf(_rmsnorm(x_ref[0, rs, :], gain_ref[...]))
        proj.append((_dot(h, wgm_ref[...]), _dot(h, wrw_ref[...])))

    last_row = carry_ref[0:1, :]
    for rs, (p_gm, p) in zip(subs, proj):
        for ch in range(ROW_TILE // GM_CHUNK):
            ts = slice(ch * GM_CHUNK, (ch + 1) * GM_CHUNK)
            out_rows = slice(rs.start + ts.start, rs.start + ts.stop)
            ygm_ref[0, out_rows, :] = _bf(_gmlp_chunk(p_gm[ts], avg, lnw_ref[...], lnb_ref[...], w2, bias_ref[...],
                                                      head0))
        prev = jnp.where(first, last_row, pltpu.roll(p, 1, axis=0))
        last_row = p[ROW_TILE - 1:ROW_TILE, :]
        ps = p + (prev - p) * mu_ref[...]
        r = ps[:, :width]
        k = ps[:, width:2 * width]
        v = ps[:, 2 * width:3 * width]
        lora_in = ps[:, 3 * width:3 * width + LANES]
        gd = ps[:, 3 * width + LANES:]
        lw = -(jnp.exp(-0.5)) * jax.nn.sigmoid(w0_ref[...] + _dot(_bf(jnp.tanh(lora_in)), wup_ref[...]))
        a_lr = jax.nn.sigmoid(a0_ref[...] + _dot(_bf(lora_in), aup_ref[...]))
        kk = k * kk_ref[...]
        kk = kk * lax.rsqrt(jnp.maximum(_head_dot(_bf(kk * kk), ones), 1e-24))
        kmod = k * (1.0 + (a_lr - 1.0) * ka_ref[...])
        lw_ref[0, rs, :] = lw
        c_ref[0, rs, :] = jnp.concatenate([_split_dot_left(tri, lw[i:i + MXU_TILE])
                                           for i in range(0, ROW_TILE, MXU_TILE)], axis=0)
        r_ref[0, rs, :] = _bf(r)
        k_ref[0, rs, :] = _bf(kmod)
        v_ref[0, rs, :] = _bf(v)
        a_ref[0, rs, :] = _bf(-kk)
        b_ref[0, rs, :] = _bf(kk * a_lr)
        g_ref[0, rs, :] = _bf(_dot(_bf(jax.nn.sigmoid(gd)), gup_ref[...]))
        bonus_ref[0, rs, :] = _bf(_head_dot(_bf(r * kmod * rk_ref[...]), ones) * v)
    carry_ref[0:1, :] = last_row


def _mixer_in(x3, gain, w_gm, w_rw, ln_w, ln_b, ws, bs, mu, w_up, w0, a_up, a0, g_up, k_k, k_a, r_k):
    batch, t, d = x3.shape
    width = w0.shape[0]
    rank = w_up.shape[0]
    zeros = jnp.zeros((LANES - rank, width), F32)
    wup_pad = _bf(jnp.concatenate([w_up, zeros], axis=0))
    aup_pad = _bf(jnp.concatenate([zeros, a_up], axis=0))
    bias = jnp.repeat(bs.T, HEAD_DIM, axis=1)
    row = lambda z: z.reshape(1, -1)
    params = (row(gain), w_gm, w_rw, row(ln_w), row(ln_b), ws, bias, row(mu), wup_pad, row(w0), aup_pad, row(a0),
              _bf(g_up), row(k_k), row(k_a), row(r_k))
    tile = min(MIX_TILE, t)
    out = pl.BlockSpec((1, tile, width), lambda b, i: (b, i, 0))
    sds = lambda dtype: jax.ShapeDtypeStruct((batch, t, width), dtype)
    return pl.pallas_call(
        _mixer_in_kernel,
        grid=(batch, t // tile),
        in_specs=[pl.BlockSpec((1, tile, d), lambda b, i: (b, i, 0))] + [_resident(z.shape) for z in params],
        out_specs=[out] * 10,
        out_shape=[sds(BF16), sds(BF16), sds(F32), sds(F32)] + [sds(BF16)] * 6,
        scratch_shapes=[pltpu.VMEM((8, w_rw.shape[1]), F32)],
        compiler_params=_cparams("parallel", "arbitrary"),
        name="mixer_in",
    )(x3, *params)


def _wkv_kernel(r_ref, c_ref, lw_ref, k_ref, v_ref, a_ref, b_ref, y_ref, h_ref,
                q_s, v_s, ark_s, bcrb_s, mz_s, rhs_s, sol_s, t_s, rh_s, rkv_s, kv_s, dec_s, hc_s, yc_s):
    @pl.when(pl.program_id(1) == 0)
    def _():
        h_ref[...] = jnp.zeros_like(h_ref)

    n = WKV_CHUNK
    two = 2 * n
    head0 = lax.broadcasted_iota(jnp.int32, (n, LANES), 1) < HEAD_DIM
    row = lax.broadcasted_iota(jnp.int32, (two, two), 0)
    col = lax.broadcasted_iota(jnp.int32, (two, two), 1)
    eye = row == col
    row2 = lax.broadcasted_iota(jnp.int32, (two, 2 * two), 0)
    col2 = lax.broadcasted_iota(jnp.int32, (two, 2 * two), 1) % two
    same2 = (row2 // n) == (col2 // n)
    strict2 = same2 & ((col2 % n) < (row2 % n))
    incl2 = same2 & ((col2 % n) <= (row2 % n))
    npairs = y_ref.shape[2] // LANES
    nchunks = y_ref.shape[1] // n
    items = [(ch, pair) for ch in range(nchunks) for pair in range(npairs)]

    def window(ch, pair):
        return slice(ch * n, (ch + 1) * n), slice(pair * LANES, (pair + 1) * LANES)

    def stack(z):
        return jnp.concatenate([jnp.where(head0, z, 0.0), jnp.where(head0, 0.0, z)], axis=0)

    def dup(z):
        zb = _bf(z)
        return jnp.concatenate([zb, zb], axis=0)

    def scores(it):
        ts, ls = window(*items[it])
        r, k, v, a, b = (ref[0, ts, ls].astype(F32) for ref in (r_ref, k_ref, v_ref, a_ref, b_ref))
        c = c_ref[0, ts, ls]
        cex = c - lw_ref[0, ts, ls]
        mid = c[n // 2 - 1:n // 2, :]
        last = c[n - 1:n, :]
        g_inv = jnp.exp(mid - c)
        g_end = jnp.exp(last - c)
        ar_st = _bf(jnp.concatenate([stack(a * jnp.exp(cex - mid)), stack(r * jnp.exp(c - mid))], axis=0))
        bk_dup = jnp.concatenate([dup(b * g_inv), dup(k * g_inv)], axis=0)
        sc = _dot_nt(ar_st, bk_dup)
        top = jnp.where(strict2, sc[:two], 0.0)
        bot = jnp.where(incl2, sc[two:], 0.0)
        t_s[it] = jnp.where(eye, 1.0, top[:, :two])
        q_s[it] = _bf(top[:, :two])
        bcrb_s[it] = _bf(jnp.concatenate([stack(b * g_end).T, bot[:, :two]], axis=0))
        ark_s[it] = _bf(jnp.concatenate([top[:, two:], bot[:, two:], stack(k * g_end).T], axis=0))
        v_s[it] = _bf(stack(v))
        rh_s[it] = stack(r * jnp.exp(c))
        rhs_s[it, :, :LANES] = _bf(stack(a * jnp.exp(cex)))
        e_col = jnp.sum(jnp.where(eye, jnp.exp(last), 0.0), axis=1, keepdims=True)
        dec_s[it] = jnp.broadcast_to(e_col, (two, LANES))

    def values(it):
        v_st = v_s[it]
        prod = _dot(ark_s[it], v_st)
        rhs_s[it, :, LANES:] = _bf(prod[:two])
        rkv_s[it] = prod[two:2 * two]
        kv_s[it] = prod[2 * two:]

    levels = n.bit_length() - 1

    def inverse_level(level, it):
        q = q_s[it]
        t_acc = t_s[it]
        if level == 1:
            q = _bf(_dot(q, q))
        if level + 1 < levels:
            prod = _dot(q, jnp.concatenate([_bf(t_acc), q], axis=1))
            q_s[it] = _bf(prod[:, two:])
            t_s[it] = t_acc + prod[:, :two]
        else:
            t_s[it] = t_acc + _dot(q, _bf(t_acc))

    def solve(it):
        sol_s[it] = _bf(_dot(_bf(t_s[it]), rhs_s[it]))

    def fold(it):
        prod = _dot(bcrb_s[it], sol_s[it])
        mz_s[it] = _bf(jnp.concatenate([prod[:two, :LANES], rh_s[it] + prod[two:, :LANES]], axis=0))
        hc_s[it] = prod[:two, LANES:] + kv_s[it]
        yc_s[it] = prod[two:, LANES:] + rkv_s[it]

    def advance(it):
        ts, ls = window(*items[it])
        pair = items[it][1]
        h = h_ref[pair]
        gmat = _dot(mz_s[it], _bf(h))
        h_ref[pair] = dec_s[it] * h + gmat[:two] + hc_s[it]
        y_st = gmat[two:] + yc_s[it]
        y_ref[0, ts, ls] = y_st[:n] + y_st[n:]

    stages = ([scores, values] + [functools.partial(inverse_level, level) for level in range(1, levels)]
              + [solve, fold, advance])
    for stage in stages:
        for it in range(len(items)):
            stage(it)


def _wkv(r, c, lw, k, v, a, b):
    batch, t, width = r.shape
    npairs = width // LANES
    items = (WKV_STEP // WKV_CHUNK) * npairs
    spec = pl.BlockSpec((1, WKV_STEP, width), lambda bi, i: (bi, i, 0))
    sq = lambda dtype: pltpu.VMEM((items, LANES, LANES), dtype)
    return pl.pallas_call(
        _wkv_kernel,
        grid=(batch, t // WKV_STEP),
        in_specs=[spec] * 7,
        out_specs=spec,
        out_shape=jax.ShapeDtypeStruct((batch, t, width), F32),
        scratch_shapes=[pltpu.VMEM((npairs, LANES, LANES), F32)] + [sq(BF16)] * 2
                       + [pltpu.VMEM((items, 3 * LANES, LANES), BF16)]
                       + [pltpu.VMEM((items, 2 * LANES, LANES), BF16)] * 2
                       + [pltpu.VMEM((items, LANES, 2 * LANES), BF16)] * 2 + [sq(F32)] * 7,
        compiler_params=_cparams("parallel", "arbitrary"),
        name="wkv7",
    )(r, c, lw, k, v, a, b)


N_MIX_OUT_REFS = 9


def _mixer_out(x_ref, ygm_ref, ys_ref, bonus_ref, g_ref, lnw_ref, lnb_ref, wtop_ref, wbot_ref, rows=slice(None)):
    yn = _head_layernorm(ys_ref[rows, :], _head_avg_matrix(), lnw_ref[...], lnb_ref[...], GN_EPS)
    y_rw = (yn + bonus_ref[rows, :]) * g_ref[rows, :]
    return x_ref[rows, :] + _dot(ygm_ref[rows, :], wtop_ref[...]) + _dot(_bf(y_rw), wbot_ref[...])


def _mixer_out_operands(x2, y_gm, y_scan, bonus, g, lnx_w, lnx_b, w_out, tile=ROW_TILE):
    d = x2.shape[1]
    width = y_gm.shape[1]
    rows = lambda w: pl.BlockSpec((tile, w), lambda i: (i, 0))
    w_out = _bf(w_out)
    specs = [rows(d), rows(width), rows(width), rows(width), rows(width),
             _resident((1, width)), _resident((1, width)), _resident((width, d)), _resident((width, d))]
    args = (x2, y_gm, y_scan, bonus, g, lnx_w.reshape(1, width), lnx_b.reshape(1, width), w_out[:width], w_out[width:])
    assert len(specs) == len(args) == N_MIX_OUT_REFS
    return specs, args


def _out_swiglu_kernel(*refs):
    mix_refs, (gain_ref, wg_ref, wu_ref, wd_ref, o_ref) = refs[:N_MIX_OUT_REFS], refs[N_MIX_OUT_REFS:]
    x = _mixer_out(*mix_refs)
    h = _bf(_rmsnorm(x, gain_ref[...]))
    act = jax.nn.silu(_dot(h, wg_ref[...])) * _dot(h, wu_ref[...])
    o_ref[...] = x + _dot(_bf(act), wd_ref[...])


def _out_swiglu(mix_args, gain, wg, wu, wd):
    specs, args = _mixer_out_operands(*mix_args)
    n, d = args[0].shape
    d_ff = wg.shape[1]
    return pl.pallas_call(
        _out_swiglu_kernel,
        grid=(n // ROW_TILE,),
        in_specs=specs + [_resident((1, d)), _resident((d, d_ff)), _resident((d, d_ff)), _resident((d_ff, d))],
        out_specs=pl.BlockSpec((ROW_TILE, d), lambda i: (i, 0)),
        out_shape=jax.ShapeDtypeStruct((n, d), F32),
        compiler_params=_cparams("parallel"),
        name="out_dense_swiglu",
    )(*args, gain.reshape(1, d), _bf(wg), _bf(wu), _bf(wd))


META_E, META_G, META_RANK = 0, 2, 4


def _out_router_kernel(*refs):
    mix_refs = refs[:N_MIX_OUT_REFS]
    g_ref, whi_ref, wlo_ref, x_ref, h_ref, meta_ref, cnt_ref, run_ref = refs[N_MIX_OUT_REFS:]

    @pl.when(pl.program_id(0) == 0)
    def _():
        run_ref[...] = jnp.zeros_like(run_ref)

    lane = lax.broadcasted_iota(jnp.int32, (ROW_TILE, LANES), 1)
    t = lax.broadcasted_iota(jnp.int32, (ROW_TILE, ROW_TILE), 0)
    s = lax.broadcasted_iota(jnp.int32, (ROW_TILE, ROW_TILE), 1)
    earlier = jnp.where(s < t, 1.0, 0.0).astype(BF16)

    subs = [slice(i * ROW_TILE, (i + 1) * ROW_TILE) for i in range(x_ref.shape[0] // ROW_TILE)]
    xs = [_mixer_out(*mix_refs, rows=rs) for rs in subs]

    run = run_ref[0:1, :]
    for rs, x in zip(subs, xs):
        x_ref[rs, :] = x
        h = _rmsnorm(x, g_ref[...])
        h_ref[rs, :] = _pack_bf16_pairs(h)
        h_hi = _bf(h)
        h_lo = _bf(h - h_hi.astype(F32))
        logits = _dot(h_hi, whi_ref[...]) + _dot(h_lo, whi_ref[...]) + _dot(h_hi, wlo_ref[...])
        logits = jnp.where(lane < N_EXPERTS, logits, NEG_BIG)
        v1 = jnp.max(logits, axis=-1, keepdims=True)
        e1 = jnp.min(jnp.where(logits == v1, lane, LANES), axis=-1, keepdims=True)
        oh1 = lane == e1
        rest = jnp.where(oh1, NEG_BIG, logits)
        v2 = jnp.max(rest, axis=-1, keepdims=True)
        e2 = jnp.min(jnp.where(rest == v2, lane, LANES), axis=-1, keepdims=True)
        oh2 = lane == e2
        ex = jnp.exp(v2 - v1)
        g1 = 1.0 / (1.0 + ex)
        g2 = ex / (1.0 + ex)
        cnt = jnp.where(oh1 | oh2, 1.0, 0.0)
        before = _dot(earlier, _bf(cnt)) + run
        rank1 = jnp.sum(jnp.where(oh1, before, 0.0), axis=-1, keepdims=True)
        rank2 = jnp.sum(jnp.where(oh2, before, 0.0), axis=-1, keepdims=True)
        fields = (e1.astype(F32), e2.astype(F32), g1, g2, rank1, rank2)
        meta = jnp.zeros((ROW_TILE, LANES), F32)
        for idx, val in enumerate(fields):
            meta = jnp.where(lane == idx, val, meta)
        meta_ref[rs, :] = meta
        run = run + jnp.sum(cnt, axis=0, keepdims=True)
    run_ref[...] = jnp.broadcast_to(run, run_ref.shape)
    cnt_ref[...] = jnp.broadcast_to(run, cnt_ref.shape)


def _out_router(mix_args, gain, w_router):
    n, d = mix_args[0].shape
    tile = min(MIX_TILE, n)
    specs, args = _mixer_out_operands(*mix_args, tile=tile)
    wr = jnp.concatenate([w_router, jnp.zeros((d, LANES - w_router.shape[1]), F32)], axis=1)
    wr_hi = _bf(wr)
    wr_lo = _bf(wr - wr_hi.astype(F32))
    rows = lambda w: pl.BlockSpec((tile, w), lambda i: (i, 0))
    return pl.pallas_call(
        _out_router_kernel,
        grid=(n // tile,),
        in_specs=specs + [_resident((1, d)), _resident((d, LANES)), _resident((d, LANES))],
        out_specs=[rows(d), rows(d // 2), rows(LANES), pl.BlockSpec((8, LANES), lambda i: (0, 0))],
        out_shape=[jax.ShapeDtypeStruct((n, d), F32), jax.ShapeDtypeStruct((n, d // 2), jnp.uint32),
                   jax.ShapeDtypeStruct((n, LANES), F32), jax.ShapeDtypeStruct((8, LANES), F32)],
        scratch_shapes=[pltpu.VMEM((8, LANES), F32)],
        compiler_params=_cparams("arbitrary"),
        name="out_moe_router",
    )(*args, gain.reshape(1, d), wr_hi, wr_lo)


def _sc_workers():
    info = plsc.get_sparse_core_info()
    return info.num_cores, info.num_cores * info.num_subcores


def _dispatch_rows(h, dest0, dest1, pad_rows):
    nc, workers = _sc_workers()
    n, d = h.shape
    n_pad = pad_rows.shape[0]
    per_worker = n // workers
    nchunk = per_worker // SC_WINDOW
    npad = n_pad // (workers * SC_WINDOW)
    assert n == workers * nchunk * SC_WINDOW and nchunk % 2 == 0 and n_pad == workers * npad * SC_WINDOW
    mesh = plsc.VectorSubcoreMesh(core_axis_name="c", subcore_axis_name="s")
    idx_t = lambda m: pltpu.VMEM((m, SC_WINDOW), jnp.int32)
    buf_t = pltpu.VMEM((SC_WINDOW, d), h.dtype)

    @functools.partial(
        pl.kernel, mesh=mesh,
        out_type=jax.ShapeDtypeStruct((2 * n + n_pad, d), h.dtype),
        scratch_types=[idx_t(nchunk), idx_t(nchunk), idx_t(npad), buf_t, buf_t, buf_t]
                      + [pltpu.SemaphoreType.DMA] * 6,
    )
    def dispatch_kernel(h_hbm, d0_hbm, d1_hbm, pad_hbm, zero_hbm, out_hbm, d0_v, d1_v, pad_v, buf0, buf1, zbuf,
                        r0, r1, w0, w1, x0, x1):
        wid = lax.axis_index("s") * nc + lax.axis_index("c")
        base = wid * per_worker
        pltpu.sync_copy(d0_hbm.at[wid], d0_v)
        pltpu.sync_copy(d1_hbm.at[wid], d1_v)
        pltpu.sync_copy(pad_hbm.at[wid], pad_v)
        pltpu.sync_copy(zero_hbm, zbuf)
        bufs, rsem, wsem, xsem = (buf0, buf1), (r0, r1), (w0, w1), (x0, x1)

        def read(c, slot):
            return pltpu.make_async_copy(h_hbm.at[pl.ds(base + c * SC_WINDOW, SC_WINDOW)], bufs[slot], rsem[slot])

        read(0, 0).start()
        read(1, 1).start()
        for pc in range(npad):
            pltpu.sync_copy(zbuf, out_hbm.at[pad_v.at[pc]])

        @pl.loop(0, nchunk, step=2)
        def _(c):
            for slot in range(2):
                cc = c + slot
                read(cc, slot).wait()
                first = pltpu.make_async_copy(bufs[slot], out_hbm.at[d0_v.at[cc]], wsem[slot])
                second = pltpu.make_async_copy(bufs[slot], out_hbm.at[d1_v.at[cc]], xsem[slot])
                first.start()
                second.start()
                first.wait()
                second.wait()

                @pl.when(cc + 2 < nchunk)
                def _():
                    read(cc + 2, slot).start()

    split = lambda z, m: z.reshape(workers, m, SC_WINDOW)
    return dispatch_kernel(h, split(dest0, nchunk), split(dest1, nchunk), split(pad_rows, npad),
                           jnp.zeros((SC_WINDOW, d), h.dtype))


def _gather_rows(src, idx):
    nc, workers = _sc_workers()
    m, d = idx.shape[0], src.shape[1]
    per_worker = m // workers
    nchunk = per_worker // SC_WINDOW
    assert m == workers * nchunk * SC_WINDOW and nchunk % 2 == 0, "row count must split evenly over subcores"
    mesh = plsc.VectorSubcoreMesh(core_axis_name="c", subcore_axis_name="s")

    @functools.partial(
        pl.kernel, mesh=mesh,
        out_type=jax.ShapeDtypeStruct((m, d), src.dtype),
        scratch_types=[pltpu.VMEM((nchunk, SC_WINDOW), jnp.int32),
                       pltpu.VMEM((SC_WINDOW, d), src.dtype), pltpu.VMEM((SC_WINDOW, d), src.dtype),
                       pltpu.SemaphoreType.DMA, pltpu.SemaphoreType.DMA,
                       pltpu.SemaphoreType.DMA, pltpu.SemaphoreType.DMA],
    )
    def gather_kernel(src_hbm, idx_hbm, out_hbm, idx_v, buf0, buf1, g0, g1, w0, w1):
        wid = lax.axis_index("s") * nc + lax.axis_index("c")
        base = wid * per_worker
        pltpu.sync_copy(idx_hbm.at[wid], idx_v)
        bufs, gsem, wsem = (buf0, buf1), (g0, g1), (w0, w1)

        def gather(c, slot):
            return pltpu.make_async_copy(src_hbm.at[idx_v.at[c]], bufs[slot], gsem[slot])

        def write(c, slot):
            return pltpu.make_async_copy(bufs[slot], out_hbm.at[pl.ds(base + c * SC_WINDOW, SC_WINDOW)], wsem[slot])

        gather(0, 0).start()
        gather(1, 1).start()

        @pl.loop(0, nchunk, step=2)
        def _(c):
            for slot in range(2):
                cc = c + slot
                gather(cc, slot).wait()
                write(cc, slot).start()
                write(cc, slot).wait()

                @pl.when(cc + 2 < nchunk)
                def _():
                    gather(cc + 2, slot).start()

    return gather_kernel(src, idx.reshape(workers, nchunk, SC_WINDOW))


def _pack_bf16_pairs(y):
    bits = lambda z: pltpu.bitcast(z.astype(BF16).astype(F32), jnp.uint32)
    words = []
    for t in range(0, y.shape[1] // LANES, 2):
        lo = bits(y[:, t * LANES:(t + 1) * LANES])
        hi = bits(y[:, (t + 1) * LANES:(t + 2) * LANES])
        words.append((hi & jnp.uint32(0xFFFF0000)) | (lo >> 16))
    return jnp.concatenate(words, axis=1)


def _unpack_bf16_pairs(words):
    tiles = []
    for t in range(words.shape[1] // LANES):
        w = words[:, t * LANES:(t + 1) * LANES]
        tiles += [pltpu.bitcast(w << 16, F32), pltpu.bitcast(w & jnp.uint32(0xFFFF0000), F32)]
    return jnp.concatenate(tiles, axis=1)


def _expert_kernel(be_ref, nu_ref, x_ref, wg_ref, wu_ref, wd_ref, o_ref):
    i = pl.program_id(0)

    @pl.when(i < nu_ref[0])
    def _():
        h = _bf(_unpack_bf16_pairs(x_ref[...]))
        act = jax.nn.silu(_dot(h, wg_ref[0])) * _dot(h, wu_ref[0])
        o_ref[...] = _pack_bf16_pairs(_dot(_bf(act), wd_ref[0]))

    @pl.when(i >= nu_ref[0])
    def _():
        o_ref[...] = jnp.zeros_like(o_ref)


def _expert_swiglu(xb, block_expert, n_used, wg, wu, wd):
    n_rows, d = xb.shape[0], 2 * xb.shape[1]
    d_ff = wg.shape[2]
    nblk = n_rows // MOE_BLOCK

    def row(i, be, nu):
        return jnp.minimum(i, nu[0] - 1)

    def expert(i, be, nu):
        return be[row(i, be, nu)], 0, 0

    resident = dict(pipeline_mode=pl.Buffered(1))
    grid_spec = pltpu.PrefetchScalarGridSpec(
        num_scalar_prefetch=2,
        grid=(nblk,),
        in_specs=[pl.BlockSpec((MOE_BLOCK, d // 2), lambda i, be, nu: (row(i, be, nu), 0)),
                  pl.BlockSpec((1, d, d_ff), expert, **resident),
                  pl.BlockSpec((1, d, d_ff), expert, **resident),
                  pl.BlockSpec((1, d_ff, d), expert, **resident)],
        out_specs=pl.BlockSpec((MOE_BLOCK, d // 2), lambda i, be, nu: (i, 0)),
    )
    return pl.pallas_call(
        _expert_kernel,
        grid_spec=grid_spec,
        out_shape=jax.ShapeDtypeStruct((n_rows, d // 2), jnp.uint32),
        compiler_params=_cparams("arbitrary"),
        name="expert_swiglu",
    )(block_expert, n_used, xb, _bf(wg), _bf(wu), _bf(wd))


def _combine_kernel(x_ref, y1_ref, y2_ref, meta_ref, g_ref, o_ref):
    meta = meta_ref[...]
    lane = lax.broadcasted_iota(jnp.int32, meta.shape, 1)
    g1 = jnp.sum(jnp.where(lane == META_G, meta, 0.0), axis=-1, keepdims=True)
    g2 = jnp.sum(jnp.where(lane == META_G + 1, meta, 0.0), axis=-1, keepdims=True)
    y = x_ref[...] + (g1 * _unpack_bf16_pairs(y1_ref[...]) + g2 * _unpack_bf16_pairs(y2_ref[...]))
    o_ref[...] = _rmsnorm(y, g_ref[...])


def _combine_final_norm(x2, yg, meta, gain):
    n, d = x2.shape
    nblk = n // ROW_TILE
    return pl.pallas_call(
        _combine_kernel,
        grid=(nblk,),
        in_specs=[pl.BlockSpec((ROW_TILE, d), lambda i: (i, 0)),
                  pl.BlockSpec((ROW_TILE, d // 2), lambda i: (i, 0)),
                  pl.BlockSpec((ROW_TILE, d // 2), lambda i: (i + nblk, 0)),
                  pl.BlockSpec((ROW_TILE, LANES), lambda i: (i, 0)),
                  pl.BlockSpec((1, d), lambda i: (0, 0))],
        out_specs=pl.BlockSpec((ROW_TILE, d), lambda i: (i, 0)),
        out_shape=jax.ShapeDtypeStruct((n, d), F32),
        compiler_params=_cparams("parallel"),
        name="moe_combine_norm",
    )(x2, yg, yg, meta, gain.reshape(1, d))


def _moe_layer(mix_args, gain, w_router, wg, wu, wd, final_gain):
    x2, h, meta, counts = _out_router(mix_args, gain, w_router)
    n, d = x2.shape
    expert = meta[:, META_E:META_E + 2].astype(jnp.int32)
    rank = meta[:, META_RANK:META_RANK + 2].astype(jnp.int32)
    count = counts[0, :N_EXPERTS].astype(jnp.int32)
    padded = (count + MOE_BLOCK - 1) // MOE_BLOCK * MOE_BLOCK
    pad_end = jnp.cumsum(padded)
    pad_start = pad_end - padded
    dest = pad_start[expert] + rank
    n_rows = (2 * n // MOE_BLOCK + N_EXPERTS) * MOE_BLOCK
    empties = padded - count
    e_end = jnp.cumsum(empties)
    i = jnp.arange(n_rows - 2 * n, dtype=jnp.int32)
    grp = jnp.sum(i[:, None] >= e_end[None, :], axis=1).astype(jnp.int32)
    first_empty = jnp.concatenate([pad_start + count, pad_end[-1:]])
    first_index = jnp.concatenate([e_end - empties, e_end[-1:]])
    pad_rows = (first_empty[grp] + i - first_index[grp]).astype(jnp.int32)
    block_start = jnp.arange(n_rows // MOE_BLOCK, dtype=jnp.int32) * MOE_BLOCK
    block_expert = jnp.minimum(jnp.sum(block_start[:, None] >= pad_end[None, :], axis=1), N_EXPERTS - 1)
    block_expert = block_expert.astype(jnp.int32)
    n_used = (pad_end[-1:] // MOE_BLOCK).astype(jnp.int32)
    xb = _dispatch_rows(h, dest[:, 0], dest[:, 1], pad_rows)
    ys = _expert_swiglu(xb, block_expert, n_used, wg, wu, wd)
    yg = _gather_rows(ys, jnp.concatenate([dest[:, 0], dest[:, 1]]))
    return _combine_final_norm(x2, yg, meta, final_gain)


def _mixer_layer(x2, batch, norm_mix, w_in, w_out, shift_mu, gm_ln_w, gm_ln_b, gm_ws, gm_bs,
                 rw_w_up, rw_w0, rw_a_up, rw_a0, rw_g_up, rw_k_k, rw_k_a, rw_r_k, rw_lnx_w, rw_lnx_b):
    n, d = x2.shape
    gm2 = 2 * gm_ln_w.shape[0]
    w_in = _bf(w_in)
    y_gm, r, c, lw, k, v, a, b, g, bonus = _mixer_in(
        x2.reshape(batch, n // batch, d), norm_mix, w_in[:, :gm2], w_in[:, gm2:], gm_ln_w, gm_ln_b, gm_ws, gm_bs,
        shift_mu, rw_w_up, rw_w0, rw_a_up, rw_a0, rw_g_up, rw_k_k, rw_k_a, rw_r_k.reshape(-1))
    y_scan = _wkv(r, c, lw, k, v, a, b)
    flat = lambda z: z.reshape(n, -1)
    return x2, flat(y_gm), flat(y_scan), flat(bonus), flat(g), rw_lnx_w, rw_lnx_b, w_out


def kernel(x, norm_mix, w_in, w_out, shift_mu, gm_ln_w, gm_ln_b, gm_ws, gm_bs, rw_w_up, rw_w0, rw_a_up, rw_a0,
           rw_g_up, rw_k_k, rw_k_a, rw_r_k, rw_lnx_w, rw_lnx_b, norm_ffn, ffn_w_gate, ffn_w_up, ffn_w_down,
           moe_router, moe_w_gate, moe_w_up, moe_w_down, norm_final):
    batch, t, d = x.shape
    depth = norm_mix.shape[0]
    assert depth == 2 and t % ROW_TILE == 0, "two layers (dense then MoE), sequence a multiple of the row tile"
    x2 = x.reshape(batch * t, d)
    for i in range(depth):
        mixed = _mixer_layer(x2, batch, norm_mix[i], w_in[i], w_out[i], shift_mu[i], gm_ln_w[i], gm_ln_b[i],
                             gm_ws[i], gm_bs[i], rw_w_up[i], rw_w0[i], rw_a_up[i], rw_a0[i], rw_g_up[i],
                             rw_k_k[i], rw_k_a[i], rw_r_k[i], rw_lnx_w[i], rw_lnx_b[i])
        if i % 2 == 0:
            x2 = _out_swiglu(mixed, norm_ffn[i], ffn_w_gate[i // 2], ffn_w_up[i // 2], ffn_w_down[i // 2])
        else:
            x2 = _moe_layer(mixed, norm_ffn[i], moe_router[i // 2], moe_w_gate[i // 2], moe_w_up[i // 2],
                            moe_w_down[i // 2], norm_final)
    return x2.reshape(batch, t, d)
```

```python
import functools

import jax
import jax.numpy as jnp
from jax import lax
from jax.experimental import pallas as pl
from jax.experimental.pallas import tpu as pltpu
from jax.experimental.pallas import tpu_sc as plsc

F32 = jnp.float32
BF16 = jnp.bfloat16

HEAD_DIM = 64
LANES = 128
MXU_TILE = 256
GM_CHUNK = 128
WKV_CHUNK = 64
N_EXPERTS = 8
RMS_EPS = 1e-6
LN_EPS = 1e-5
GN_EPS = 64e-5
NEG_BIG = -1e30

ROW_TILE = 512
MIX_TILE = 1024
WKV_STEP = 512
MOE_BLOCK = 512
SC_WINDOW = 32
VMEM_LIMIT = 56 * 1024 * 1024


def _cparams(*sem):
    return pltpu.CompilerParams(dimension_semantics=sem, vmem_limit_bytes=VMEM_LIMIT)


def _bf(x):
    return x.astype(BF16)


def _dot(a, b):
    return jnp.dot(a, b, preferred_element_type=F32)


def _dot_nt(a, b):
    return lax.dot_general(a, b, (((1,), (1,)), ((), ())), preferred_element_type=F32)


def _split_dot_left(m, x, parts=2):
    acc = None
    rem = x
    for p in range(parts):
        hi = _bf(rem)
        d = _dot(m, hi)
        acc = d if acc is None else acc + d
        if p + 1 < parts:
            rem = rem - hi.astype(F32)
    return acc


def _rmsnorm(x, g):
    return x * lax.rsqrt(jnp.mean(x * x, axis=-1, keepdims=True) + RMS_EPS) * g


def _gelu(x):
    return 0.5 * x * (1.0 + lax.erf(x * (2.0 ** -0.5)))


def _head_avg_matrix():
    r = lax.broadcasted_iota(jnp.int32, (MXU_TILE, MXU_TILE), 0) // HEAD_DIM
    c = lax.broadcasted_iota(jnp.int32, (MXU_TILE, MXU_TILE), 1) // HEAD_DIM
    return jnp.where(r == c, 1.0 / HEAD_DIM, 0.0).astype(BF16)


def _head_dot(x, m):
    return jnp.concatenate([_dot(x[:, i:i + MXU_TILE], m) for i in range(0, x.shape[1], MXU_TILE)], axis=1)


def _head_layernorm(x, avg, w, b, eps):
    mu = _head_dot(_bf(x), avg)
    xc = x - mu
    var = _head_dot(_bf(xc * xc), avg)
    return xc * lax.rsqrt(var + eps) * w + b


def _resident(shape):
    return pl.BlockSpec(shape, lambda *_: (0,) * len(shape), pipeline_mode=pl.Buffered(1))


def _gmlp_chunk(p, avg, lnw, lnb, w2, bias, head0):
    width = p.shape[1] // 2
    u = _gelu(p[:, :width])
    vn = _head_layernorm(_gelu(p[:, width:]), avg, lnw, lnb, LN_EPS)
    mixed = []
    for pair, w in enumerate(w2):
        vp = vn[:, pair * LANES:(pair + 1) * LANES]
        v_st = _bf(jnp.concatenate([jnp.where(head0, vp, 0.0), jnp.where(head0, 0.0, vp)], axis=0))
        mixed.append(_dot(w, v_st))
    return u * (jnp.concatenate(mixed, axis=1) + bias)


def _mixer_in_kernel(x_ref, gain_ref, wgm_ref, wrw_ref, lnw_ref, lnb_ref, ws_ref, bias_ref,
                     mu_ref, wup_ref, w0_ref, aup_ref, a0_ref, gup_ref, kk_ref, ka_ref, rk_ref,
                     ygm_ref, r_ref, c_ref, lw_ref, k_ref, v_ref, a_ref, b_ref, g_ref, bonus_ref, carry_ref):
    width = r_ref.shape[2]
    rows = x_ref.shape[1]

    @pl.when(pl.program_id(1) == 0)
    def _():
        carry_ref[...] = jnp.zeros_like(carry_ref)

    avg = _head_avg_matrix()
    ones = avg * HEAD_DIM
    t = lax.broadcasted_iota(jnp.int32, (GM_CHUNK, GM_CHUNK), 0)
    s = lax.broadcasted_iota(jnp.int32, (GM_CHUNK, GM_CHUNK), 1)
    causal = s <= t
    head0 = lax.broadcasted_iota(jnp.int32, (GM_CHUNK, LANES), 1) < HEAD_DIM
    w2 = [_bf(jnp.concatenate([jnp.where(causal, ws_ref[2 * pair], 0.0),
                               jnp.where(causal, ws_ref[2 * pair + 1], 0.0)], axis=1))
          for pair in range(width // LANES)]
    t = lax.broadcasted_iota(jnp.int32, (MXU_TILE, MXU_TILE), 0)
    s = lax.broadcasted_iota(jnp.int32, (MXU_TILE, MXU_TILE), 1)
    tri = ((t // WKV_CHUNK == s // WKV_CHUNK) & (s <= t)).astype(BF16)
    first = lax.broadcasted_iota(jnp.int32, (ROW_TILE, wrw_ref.shape[1]), 0) == 0

    subs = [slice(i * ROW_TILE, (i + 1) * ROW_TILE) for i in range(rows // ROW_TILE)]
    proj = []
    for rs in subs:
        h = _bf(_rmsnorm(x_ref[0, rs, :], gain_ref[...]))
        proj.append((_dot(h, wgm_ref[...]), _dot(h, wrw_ref[...])))

    last_row = carry_ref[0:1, :]
    for rs, (p_gm, p) in zip(subs, proj):
        for ch in range(ROW_TILE // GM_CHUNK):
            ts = slice(ch * GM_CHUNK, (ch + 1) * GM_CHUNK)
            out_rows = slice(rs.start + ts.start, rs.start + ts.stop)
            ygm_ref[0, out_rows, :] = _bf(_gmlp_chunk(p_gm[ts], avg, lnw_ref[...], lnb_ref[...], w2, bias_ref[...],
                                                      head0))
        prev = jnp.where(first, last_row, pltpu.roll(p, 1, axis=0))
        last_row = p[ROW_TILE - 1:ROW_TILE, :]
        ps = p + (prev - p) * mu_ref[...]
        r = ps[:, :width]
        k = ps[:, width:2 * width]
        v = ps[:, 2 * width:3 * width]
        lora_in = ps[:, 3 * width:3 * width + LANES]
        gd = ps[:, 3 * width + LANES:]
        lw = -(jnp.exp(-0.5)) * jax.nn.sigmoid(w0_ref[...] + _dot(_bf(jnp.tanh(lora_in)), wup_ref[...]))
        a_lr = jax.nn.sigmoid(a0_ref[...] + _dot(_bf(lora_in), aup_ref[...]))
        kk = k * kk_ref[...]
        kk = kk * lax.rsqrt(jnp.maximum(_head_dot(_bf(kk * kk), ones), 1e-24))
        kmod = k * (1.0 + (a_lr - 1.0) * ka_ref[...])
        lw_ref[0, rs, :] = lw
        c_ref[0, rs, :] = jnp.concatenate([_split_dot_left(tri, lw[i:i + MXU_TILE])
                                           for i in range(0, ROW_TILE, MXU_TILE)], axis=0)
        r_ref[0, rs, :] = _bf(r)
        k_ref[0, rs, :] = _bf(kmod)
        v_ref[0, rs, :] = _bf(v)
        a_ref[0, rs, :] = _bf(-kk)
        b_ref[0, rs, :] = _bf(kk * a_lr)
        g_ref[0, rs, :] = _bf(_dot(_bf(jax.nn.sigmoid(gd)), gup_ref[...]))
        bonus_ref[0, rs, :] = _bf(_head_dot(_bf(r * kmod * rk_ref[...]), ones) * v)
    carry_ref[0:1, :] = last_row


def _mixer_in(x3, gain, w_gm, w_rw, ln_w, ln_b, ws, bs, mu, w_up, w0, a_up, a0, g_up, k_k, k_a, r_k):
    batch, t, d = x3.shape
    width = w0.shape[0]
    rank = w_up.shape[0]
    zeros = jnp.zeros((LANES - rank, width), F32)
    wup_pad = _bf(jnp.concatenate([w_up, zeros], axis=0))
    aup_pad = _bf(jnp.concatenate([zeros, a_up], axis=0))
    bias = jnp.repeat(bs.T, HEAD_DIM, axis=1)
    row = lambda z: z.reshape(1, -1)
    params = (row(gain), w_gm, w_rw, row(ln_w), row(ln_b), ws, bias, row(mu), wup_pad, row(w0), aup_pad, row(a0),
              _bf(g_up), row(k_k), row(k_a), row(r_k))
    tile = min(MIX_TILE, t)
    out = pl.BlockSpec((1, tile, width), lambda b, i: (b, i, 0))
    sds = lambda dtype: jax.ShapeDtypeStruct((batch, t, width), dtype)
    return pl.pallas_call(
        _mixer_in_kernel,
        grid=(batch, t // tile),
        in_specs=[pl.BlockSpec((1, tile, d), lambda b, i: (b, i, 0))] + [_resident(z.shape) for z in params],
        out_specs=[out] * 10,
        out_shape=[sds(BF16), sds(BF16), sds(F32), sds(F32)] + [sds(BF16)] * 6,
        scratch_shapes=[pltpu.VMEM((8, w_rw.shape[1]), F32)],
        compiler_params=_cparams("parallel", "arbitrary"),
        name="mixer_in",
    )(x3, *params)


def _wkv_kernel(r_ref, c_ref, lw_ref, k_ref, v_ref, a_ref, b_ref, y_ref, h_ref,
                q_s, v_s, ark_s, bcrb_s, mz_s, rhs_s, sol_s, t_s, rh_s, rkv_s, kv_s, dec_s, hc_s, yc_s):
    @pl.when(pl.program_id(1) == 0)
    def _():
        h_ref[...] = jnp.zeros_like(h_ref)

    n = WKV_CHUNK
    two = 2 * n
    head0 = lax.broadcasted_iota(jnp.int32, (n, LANES), 1) < HEAD_DIM
    row = lax.broadcasted_iota(jnp.int32, (two, two), 0)
    col = lax.broadcasted_iota(jnp.int32, (two, two), 1)
    eye = row == col
    row2 = lax.broadcasted_iota(jnp.int32, (two, 2 * two), 0)
    col2 = lax.broadcasted_iota(jnp.int32, (two, 2 * two), 1) % two
    same2 = (row2 // n) == (col2 // n)
    strict2 = same2 & ((col2 % n) < (row2 % n))
    incl2 = same2 & ((col2 % n) <= (row2 % n))
    npairs = y_ref.shape[2] // LANES
    nchunks = y_ref.shape[1] // n
    items = [(ch, pair) for ch in range(nchunks) for pair in range(npairs)]

    def window(ch, pair):
        return slice(ch * n, (ch + 1) * n), slice(pair * LANES, (pair + 1) * LANES)

    def stack(z):
        return jnp.concatenate([jnp.where(head0, z, 0.0), jnp.where(head0, 0.0, z)], axis=0)

    def dup(z):
        zb = _bf(z)
        return jnp.concatenate([zb, zb], axis=0)

    def scores(it):
        ts, ls = window(*items[it])
        r, k, v, a, b = (ref[0, ts, ls].astype(F32) for ref in (r_ref, k_ref, v_ref, a_ref, b_ref))
        c = c_ref[0, ts, ls]
        cex = c - lw_ref[0, ts, ls]
        mid = c[n // 2 - 1:n // 2, :]
        last = c[n - 1:n, :]
        g_inv = jnp.exp(mid - c)
        g_end = jnp.exp(last - c)
        ar_st = _bf(jnp.concatenate([stack(a * jnp.exp(cex - mid)), stack(r * jnp.exp(c - mid))], axis=0))
        bk_dup = jnp.concatenate([dup(b * g_inv), dup(k * g_inv)], axis=0)
        sc = _dot_nt(ar_st, bk_dup)
        top = jnp.where(strict2, sc[:two], 0.0)
        bot = jnp.where(incl2, sc[two:], 0.0)
        t_s[it] = jnp.where(eye, 1.0, top[:, :two])
        q_s[it] = _bf(top[:, :two])
        bcrb_s[it] = _bf(jnp.concatenate([stack(b * g_end).T, bot[:, :two]], axis=0))
        ark_s[it] = _bf(jnp.concatenate([top[:, two:], bot[:, two:], stack(k * g_end).T], axis=0))
        v_s[it] = _bf(stack(v))
        rh_s[it] = stack(r * jnp.exp(c))
        rhs_s[it, :, :LANES] = _bf(stack(a * jnp.exp(cex)))
        e_col = jnp.sum(jnp.where(eye, jnp.exp(last), 0.0), axis=1, keepdims=True)
        dec_s[it] = jnp.broadcast_to(e_col, (two, LANES))

    def values(it):
        v_st = v_s[it]
        prod = _dot(ark_s[it], v_st)
        rhs_s[it, :, LANES:] = _bf(prod[:two])
        rkv_s[it] = prod[two:2 * two]
        kv_s[it] = prod[2 * two:]

    levels = n.bit_length() - 1

    def inverse_level(level, it):
        q = q_s[it]
        t_acc = t_s[it]
        if level == 1:
            q = _bf(_dot(q, q))
        if level + 1 < levels:
            prod = _dot(q, jnp.concatenate([_bf(t_acc), q], axis=1))
            q_s[it] = _bf(prod[:, two:])
            t_s[it] = t_acc + prod[:, :two]
        else:
            t_s[it] = t_acc + _dot(q, _bf(t_acc))

    def solve(it):
        sol_s[it] = _bf(_dot(_bf(t_s[it]), rhs_s[it]))

    def fold(it):
        prod = _dot(bcrb_s[it], sol_s[it])
        mz_s[it] = _bf(jnp.concatenate([prod[:two, :LANES], rh_s[it] + prod[two:, :LANES]], axis=0))
        hc_s[it] = prod[:two, LANES:] + kv_s[it]
        yc_s[it] = prod[two:, LANES:] + rkv_s[it]

    def advance(it):
        ts, ls = window(*items[it])
        pair = items[it][1]
        h = h_ref[pair]
        gmat = _dot(mz_s[it], _bf(h))
        h_ref[pair] = dec_s[it] * h + gmat[:two] + hc_s[it]
        y_st = gmat[two:] + yc_s[it]
        y_ref[0, ts, ls] = y_st[:n] + y_st[n:]

    stages = ([scores, values] + [functools.partial(inverse_level, level) for level in range(1, levels)]
              + [solve, fold, advance])
    for stage in stages:
        for it in range(len(items)):
            stage(it)


def _wkv(r, c, lw, k, v, a, b):
    batch, t, width = r.shape
    npairs = width // LANES
    items = (WKV_STEP // WKV_CHUNK) * npairs
    spec = pl.BlockSpec((1, WKV_STEP, width), lambda bi, i: (bi, i, 0))
    sq = lambda dtype: pltpu.VMEM((items, LANES, LANES), dtype)
    return pl.pallas_call(
        _wkv_kernel,
        grid=(batch, t // WKV_STEP),
        in_specs=[spec] * 7,
        out_specs=spec,
        out_shape=jax.ShapeDtypeStruct((batch, t, width), F32),
        scratch_shapes=[pltpu.VMEM((npairs, LANES, LANES), F32)] + [sq(BF16)] * 2
                       + [pltpu.VMEM((items, 3 * LANES, LANES), BF16)]
                       + [pltpu.VMEM((items, 2 * LANES, LANES), BF16)] * 2
                       + [pltpu.VMEM((items, LANES, 2 * LANES), BF16)] * 2 + [sq(F32)] * 7,
        compiler_params=_cparams("parallel", "arbitrary"),
        name="wkv7",
    )(r, c, lw, k, v, a, b)


N_MIX_OUT_REFS = 9


def _mixer_out(x_ref, ygm_ref, ys_ref, bonus_ref, g_ref, lnw_ref, lnb_ref, wtop_ref, wbot_ref, rows=slice(None)):
    yn = _head_layernorm(ys_ref[rows, :], _head_avg_matrix(), lnw_ref[...], lnb_ref[...], GN_EPS)
    y_rw = (yn + bonus_ref[rows, :]) * g_ref[rows, :]
    return x_ref[rows, :] + _dot(ygm_ref[rows, :], wtop_ref[...]) + _dot(_bf(y_rw), wbot_ref[...])


def _mixer_out_operands(x2, y_gm, y_scan, bonus, g, lnx_w, lnx_b, w_out, tile=ROW_TILE):
    d = x2.shape[1]
    width = y_gm.shape[1]
    rows = lambda w: pl.BlockSpec((tile, w), lambda i: (i, 0))
    w_out = _bf(w_out)
    specs = [rows(d), rows(width), rows(width), rows(width), rows(width),
             _resident((1, width)), _resident((1, width)), _resident((width, d)), _resident((width, d))]
    args = (x2, y_gm, y_scan, bonus, g, lnx_w.reshape(1, width), lnx_b.reshape(1, width), w_out[:width], w_out[width:])
    assert len(specs) == len(args) == N_MIX_OUT_REFS
    return specs, args


def _out_swiglu_kernel(*refs):
    mix_refs, (gain_ref, wg_ref, wu_ref, wd_ref, o_ref) = refs[:N_MIX_OUT_REFS], refs[N_MIX_OUT_REFS:]
    x = _mixer_out(*mix_refs)
    h = _bf(_rmsnorm(x, gain_ref[...]))
    act = jax.nn.silu(_dot(h, wg_ref[...])) * _dot(h, wu_ref[...])
    o_ref[...] = x + _dot(_bf(act), wd_ref[...])


def _out_swiglu(mix_args, gain, wg, wu, wd):
    specs, args = _mixer_out_operands(*mix_args)
    n, d = args[0].shape
    d_ff = wg.shape[1]
    return pl.pallas_call(
        _out_swiglu_kernel,
        grid=(n // ROW_TILE,),
        in_specs=specs + [_resident((1, d)), _resident((d, d_ff)), _resident((d, d_ff)), _resident((d_ff, d))],
        out_specs=pl.BlockSpec((ROW_TILE, d), lambda i: (i, 0)),
        out_shape=jax.ShapeDtypeStruct((n, d), F32),
        compiler_params=_cparams("parallel"),
        name="out_dense_swiglu",
    )(*args, gain.reshape(1, d), _bf(wg), _bf(wu), _bf(wd))


META_E, META_G, META_RANK = 0, 2, 4


def _out_router_kernel(*refs):
    mix_refs = refs[:N_MIX_OUT_REFS]
    g_ref, whi_ref, wlo_ref, x_ref, h_ref, meta_ref, cnt_ref, run_ref = refs[N_MIX_OUT_REFS:]

    @pl.when(pl.program_id(0) == 0)
    def _():
        run_ref[...] = jnp.zeros_like(run_ref)

    lane = lax.broadcasted_iota(jnp.int32, (ROW_TILE, LANES), 1)
    t = lax.broadcasted_iota(jnp.int32, (ROW_TILE, ROW_TILE), 0)
    s = lax.broadcasted_iota(jnp.int32, (ROW_TILE, ROW_TILE), 1)
    earlier = jnp.where(s < t, 1.0, 0.0).astype(BF16)

    subs = [slice(i * ROW_TILE, (i + 1) * ROW_TILE) for i in range(x_ref.shape[0] // ROW_TILE)]
    xs = [_mixer_out(*mix_refs, rows=rs) for rs in subs]

    run = run_ref[0:1, :]
    for rs, x in zip(subs, xs):
        x_ref[rs, :] = x
        h = _rmsnorm(x, g_ref[...])
        h_ref[rs, :] = _pack_bf16_pairs(h)
        h_hi = _bf(h)
        h_lo = _bf(h - h_hi.astype(F32))
        logits = _dot(h_hi, whi_ref[...]) + _dot(h_lo, whi_ref[...]) + _dot(h_hi, wlo_ref[...])
        logits = jnp.where(lane < N_EXPERTS, logits, NEG_BIG)
        v1 = jnp.max(logits, axis=-1, keepdims=True)
        e1 = jnp.min(jnp.where(logits == v1, lane, LANES), axis=-1, keepdims=True)
        oh1 = lane == e1
        rest = jnp.where(oh1, NEG_BIG, logits)
        v2 = jnp.max(rest, axis=-1, keepdims=True)
        e2 = jnp.min(jnp.where(rest == v2, lane, LANES), axis=-1, keepdims=True)
        oh2 = lane == e2
        ex = jnp.exp(v2 - v1)
        g1 = 1.0 / (1.0 + ex)
        g2 = ex / (1.0 + ex)
        cnt = jnp.where(oh1 | oh2, 1.0, 0.0)
        before = _dot(earlier, _bf(cnt)) + run
        rank1 = jnp.sum(jnp.where(oh1, before, 0.0), axis=-1, keepdims=True)
        rank2 = jnp.sum(jnp.where(oh2, before, 0.0), axis=-1, keepdims=True)
        fields = (e1.astype(F32), e2.astype(F32), g1, g2, rank1, rank2)
        meta = jnp.zeros((ROW_TILE, LANES), F32)
        for idx, val in enumerate(fields):
            meta = jnp.where(lane == idx, val, meta)
        meta_ref[rs, :] = meta
        run = run + jnp.sum(cnt, axis=0, keepdims=True)
    run_ref[...] = jnp.broadcast_to(run, run_ref.shape)
    cnt_ref[...] = jnp.broadcast_to(run, cnt_ref.shape)


def _out_router(mix_args, gain, w_router):
    n, d = mix_args[0].shape
    tile = min(MIX_TILE, n)
    specs, args = _mixer_out_operands(*mix_args, tile=tile)
    wr = jnp.concatenate([w_router, jnp.zeros((d, LANES - w_router.shape[1]), F32)], axis=1)
    wr_hi = _bf(wr)
    wr_lo = _bf(wr - wr_hi.astype(F32))
    rows = lambda w: pl.BlockSpec((tile, w), lambda i: (i, 0))
    return pl.pallas_call(
        _out_router_kernel,
        grid=(n // tile,),
        in_specs=specs + [_resident((1, d)), _resident((d, LANES)), _resident((d, LANES))],
        out_specs=[rows(d), rows(d // 2), rows(LANES), pl.BlockSpec((8, LANES), lambda i: (0, 0))],
        out_shape=[jax.ShapeDtypeStruct((n, d), F32), jax.ShapeDtypeStruct((n, d // 2), jnp.uint32),
                   jax.ShapeDtypeStruct((n, LANES), F32), jax.ShapeDtypeStruct((8, LANES), F32)],
        scratch_shapes=[pltpu.VMEM((8, LANES), F32)],
        compiler_params=_cparams("arbitrary"),
        name="out_moe_router",
    )(*args, gain.reshape(1, d), wr_hi, wr_lo)


def _sc_workers():
    info = plsc.get_sparse_core_info()
    return info.num_cores, info.num_cores * info.num_subcores


def _dispatch_rows(h, dest0, dest1, pad_rows):
    nc, workers = _sc_workers()
    n, d = h.shape
    n_pad = pad_rows.shape[0]
    per_worker = n // workers
    nchunk = per_worker // SC_WINDOW
    npad = n_pad // (workers * SC_WINDOW)
    assert n == workers * nchunk * SC_WINDOW and nchunk % 2 == 0 and n_pad == workers * npad * SC_WINDOW
    mesh = plsc.VectorSubcoreMesh(core_axis_name="c", subcore_axis_name="s")
    idx_t = lambda m: pltpu.VMEM((m, SC_WINDOW), jnp.int32)
    buf_t = pltpu.VMEM((SC_WINDOW, d), h.dtype)

    @functools.partial(
        pl.kernel, mesh=mesh,
        out_type=jax.ShapeDtypeStruct((2 * n + n_pad, d), h.dtype),
        scratch_types=[idx_t(nchunk), idx_t(nchunk), idx_t(npad), buf_t, buf_t, buf_t]
                      + [pltpu.SemaphoreType.DMA] * 6,
    )
    def dispatch_kernel(h_hbm, d0_hbm, d1_hbm, pad_hbm, zero_hbm, out_hbm, d0_v, d1_v, pad_v, buf0, buf1, zbuf,
                        r0, r1, w0, w1, x0, x1):
        wid = lax.axis_index("s") * nc + lax.axis_index("c")
        base = wid * per_worker
        pltpu.sync_copy(d0_hbm.at[wid], d0_v)
        pltpu.sync_copy(d1_hbm.at[wid], d1_v)
        pltpu.sync_copy(pad_hbm.at[wid], pad_v)
        pltpu.sync_copy(zero_hbm, zbuf)
        bufs, rsem, wsem, xsem = (buf0, buf1), (r0, r1), (w0, w1), (x0, x1)

        def read(c, slot):
            return pltpu.make_async_copy(h_hbm.at[pl.ds(base + c * SC_WINDOW, SC_WINDOW)], bufs[slot], rsem[slot])

        read(0, 0).start()
        read(1, 1).start()
        for pc in range(npad):
            pltpu.sync_copy(zbuf, out_hbm.at[pad_v.at[pc]])

        @pl.loop(0, nchunk, step=2)
        def _(c):
            for slot in range(2):
                cc = c + slot
                read(cc, slot).wait()
                first = pltpu.make_async_copy(bufs[slot], out_hbm.at[d0_v.at[cc]], wsem[slot])
                second = pltpu.make_async_copy(bufs[slot], out_hbm.at[d1_v.at[cc]], xsem[slot])
                first.start()
                second.start()
                first.wait()
                second.wait()

                @pl.when(cc + 2 < nchunk)
                def _():
                    read(cc + 2, slot).start()

    split = lambda z, m: z.reshape(workers, m, SC_WINDOW)
    return dispatch_kernel(h, split(dest0, nchunk), split(dest1, nchunk), split(pad_rows, npad),
                           jnp.zeros((SC_WINDOW, d), h.dtype))


def _gather_rows(src, idx):
    nc, workers = _sc_workers()
    m, d = idx.shape[0], src.shape[1]
    per_worker = m // workers
    nchunk = per_worker // SC_WINDOW
    assert m == workers * nchunk * SC_WINDOW and nchunk % 2 == 0, "row count must split evenly over subcores"
    mesh = plsc.VectorSubcoreMesh(core_axis_name="c", subcore_axis_name="s")

    @functools.partial(
        pl.kernel, mesh=mesh,
        out_type=jax.ShapeDtypeStruct((m, d), src.dtype),
        scratch_types=[pltpu.VMEM((nchunk, SC_WINDOW), jnp.int32),
                       pltpu.VMEM((SC_WINDOW, d), src.dtype), pltpu.VMEM((SC_WINDOW, d), src.dtype),
                       pltpu.SemaphoreType.DMA, pltpu.SemaphoreType.DMA,
                       pltpu.SemaphoreType.DMA, pltpu.SemaphoreType.DMA],
    )
    def gather_kernel(src_hbm, idx_hbm, out_hbm, idx_v, buf0, buf1, g0, g1, w0, w1):
        wid = lax.axis_index("s") * nc + lax.axis_index("c")
        base = wid * per_worker
        pltpu.sync_copy(idx_hbm.at[wid], idx_v)
        bufs, gsem, wsem = (buf0, buf1), (g0, g1), (w0, w1)

        def gather(c, slot):
            return pltpu.make_async_copy(src_hbm.at[idx_v.at[c]], bufs[slot], gsem[slot])

        def write(c, slot):
            return pltpu.make_async_copy(bufs[slot], out_hbm.at[pl.ds(base + c * SC_WINDOW, SC_WINDOW)], wsem[slot])

        gather(0, 0).start()
        gather(1, 1).start()

        @pl.loop(0, nchunk, step=2)
        def _(c):
            for slot in range(2):
                cc = c + slot
                gather(cc, slot).wait()
                write(cc, slot).start()
                write(cc, slot).wait()

                @pl.when(cc + 2 < nchunk)
                def _():
                    gather(cc + 2, slot).start()

    return gather_kernel(src, idx.reshape(workers, nchunk, SC_WINDOW))


def _pack_bf16_pairs(y):
    bits = lambda z: pltpu.bitcast(z.astype(BF16).astype(F32), jnp.uint32)
    words = []
    for t in range(0, y.shape[1] // LANES, 2):
        lo = bits(y[:, t * LANES:(t + 1) * LANES])
        hi = bits(y[:, (t + 1) * LANES:(t + 2) * LANES])
        words.append((hi & jnp.uint32(0xFFFF0000)) | (lo >> 16))
    return jnp.concatenate(words, axis=1)


def _unpack_bf16_pairs(words):
    tiles = []
    for t in range(words.shape[1] // LANES):
        w = words[:, t * LANES:(t + 1) * LANES]
        tiles += [pltpu.bitcast(w << 16, F32), pltpu.bitcast(w & jnp.uint32(0xFFFF0000), F32)]
    return jnp.concatenate(tiles, axis=1)


def _expert_kernel(be_ref, nu_ref, x_ref, wg_ref, wu_ref, wd_ref, o_ref):
    i = pl.program_id(0)

    @pl.when(i < nu_ref[0])
    def _():
        h = _bf(_unpack_bf16_pairs(x_ref[...]))
        act = jax.nn.silu(_dot(h, wg_ref[0])) * _dot(h, wu_ref[0])
        o_ref[...] = _pack_bf16_pairs(_dot(_bf(act), wd_ref[0]))

    @pl.when(i >= nu_ref[0])
    def _():
        o_ref[...] = jnp.zeros_like(o_ref)


def _expert_swiglu(xb, block_expert, n_used, wg, wu, wd):
    n_rows, d = xb.shape[0], 2 * xb.shape[1]
    d_ff = wg.shape[2]
    nblk = n_rows // MOE_BLOCK

    def row(i, be, nu):
        return jnp.minimum(i, nu[0] - 1)

    def expert(i, be, nu):
        return be[row(i, be, nu)], 0, 0

    grid_spec = pltpu.PrefetchScalarGridSpec(
        num_scalar_prefetch=2,
        grid=(nblk,),
        in_specs=[pl.BlockSpec((MOE_BLOCK, d // 2), lambda i, be, nu: (row(i, be, nu), 0)),
                  pl.BlockSpec((1, d, d_ff), expert),
                  pl.BlockSpec((1, d, d_ff), expert),
                  pl.BlockSpec((1, d_ff, d), expert)],
        out_specs=pl.BlockSpec((MOE_BLOCK, d // 2), lambda i, be, nu: (i, 0)),
    )
    return pl.pallas_call(
        _expert_kernel,
        grid_spec=grid_spec,
        out_shape=jax.ShapeDtypeStruct((n_rows, d // 2), jnp.uint32),
        compiler_params=_cparams("arbitrary"),
        name="expert_swiglu",
    )(block_expert, n_used, xb, _bf(wg), _bf(wu), _bf(wd))


def _combine_kernel(x_ref, y1_ref, y2_ref, meta_ref, g_ref, o_ref):
    meta = meta_ref[...]
    lane = lax.broadcasted_iota(jnp.int32, meta.shape, 1)
    g1 = jnp.sum(jnp.where(lane == META_G, meta, 0.0), axis=-1, keepdims=True)
    g2 = jnp.sum(jnp.where(lane == META_G + 1, meta, 0.0), axis=-1, keepdims=True)
    y = x_ref[...] + (g1 * _unpack_bf16_pairs(y1_ref[...]) + g2 * _unpack_bf16_pairs(y2_ref[...]))
    o_ref[...] = _rmsnorm(y, g_ref[...])


def _combine_final_norm(x2, yg, meta, gain):
    n, d = x2.shape
    nblk = n // ROW_TILE
    return pl.pallas_call(
        _combine_kernel,
        grid=(nblk,),
        in_specs=[pl.BlockSpec((ROW_TILE, d), lambda i: (i, 0)),
                  pl.BlockSpec((ROW_TILE, d // 2), lambda i: (i, 0)),
                  pl.BlockSpec((ROW_TILE, d // 2), lambda i: (i + nblk, 0)),
                  pl.BlockSpec((ROW_TILE, LANES), lambda i: (i, 0)),
                  pl.BlockSpec((1, d), lambda i: (0, 0))],
        out_specs=pl.BlockSpec((ROW_TILE, d), lambda i: (i, 0)),
        out_shape=jax.ShapeDtypeStruct((n, d), F32),
        compiler_params=_cparams("parallel"),
        name="moe_combine_norm",
    )(x2, yg, yg, meta, gain.reshape(1, d))


def _moe_layer(mix_args, gain, w_router, wg, wu, wd, final_gain):
    x2, h, meta, counts = _out_router(mix_args, gain, w_router)
    n, d = x2.shape
    expert = meta[:, META_E:META_E + 2].astype(jnp.int32)
    rank = meta[:, META_RANK:META_RANK + 2].astype(jnp.int32)
    count = counts[0, :N_EXPERTS].astype(jnp.int32)
    padded = (count + MOE_BLOCK - 1) // MOE_BLOCK * MOE_BLOCK
    pad_end = jnp.cumsum(padded)
    pad_start = pad_end - padded
    dest = pad_start[expert] + rank
    n_rows = (2 * n // MOE_BLOCK + N_EXPERTS) * MOE_BLOCK
    empties = padded - count
    e_end = jnp.cumsum(empties)
    i = jnp.arange(n_rows - 2 * n, dtype=jnp.int32)
    grp = jnp.sum(i[:, None] >= e_end[None, :], axis=1).astype(jnp.int32)
    first_empty = jnp.concatenate([pad_start + count, pad_end[-1:]])
    first_index = jnp.concatenate([e_end - empties, e_end[-1:]])
    pad_rows = (first_empty[grp] + i - first_index[grp]).astype(jnp.int32)
    block_start = jnp.arange(n_rows // MOE_BLOCK, dtype=jnp.int32) * MOE_BLOCK
    block_expert = jnp.minimum(jnp.sum(block_start[:, None] >= pad_end[None, :], axis=1), N_EXPERTS - 1)
    block_expert = block_expert.astype(jnp.int32)
    n_used = (pad_end[-1:] // MOE_BLOCK).astype(jnp.int32)
    xb = _dispatch_rows(h, dest[:, 0], dest[:, 1], pad_rows)
    ys = _expert_swiglu(xb, block_expert, n_used, wg, wu, wd)
    yg = _gather_rows(ys, jnp.concatenate([dest[:, 0], dest[:, 1]]))
    return _combine_final_norm(x2, yg, meta, final_gain)


def _mixer_layer(x2, batch, norm_mix, w_in, w_out, shift_mu, gm_ln_w, gm_ln_b, gm_ws, gm_bs,
                 rw_w_up, rw_w0, rw_a_up, rw_a0, rw_g_up, rw_k_k, rw_k_a, rw_r_k, rw_lnx_w, rw_lnx_b):
    n, d = x2.shape
    gm2 = 2 * gm_ln_w.shape[0]
    w_in = _bf(w_in)
    y_gm, r, c, lw, k, v, a, b, g, bonus = _mixer_in(
        x2.reshape(batch, n // batch, d), norm_mix, w_in[:, :gm2], w_in[:, gm2:], gm_ln_w, gm_ln_b, gm_ws, gm_bs,
        shift_mu, rw_w_up, rw_w0, rw_a_up, rw_a0, rw_g_up, rw_k_k, rw_k_a, rw_r_k.reshape(-1))
    y_scan = _wkv(r, c, lw, k, v, a, b)
    flat = lambda z: z.reshape(n, -1)
    return x2, flat(y_gm), flat(y_scan), flat(bonus), flat(g), rw_lnx_w, rw_lnx_b, w_out


def kernel(x, norm_mix, w_in, w_out, shift_mu, gm_ln_w, gm_ln_b, gm_ws, gm_bs, rw_w_up, rw_w0, rw_a_up, rw_a0,
           rw_g_up, rw_k_k, rw_k_a, rw_r_k, rw_lnx_w, rw_lnx_b, norm_ffn, ffn_w_gate, ffn_w_up, ffn_w_down,
           moe_router, moe_w_gate, moe_w_up, moe_w_down, norm_final):
    batch, t, d = x.shape
    depth = norm_mix.shape[0]
    assert depth == 2 and t % ROW_TILE == 0, "two layers (dense then MoE), sequence a multiple of the row tile"
    x2 = x.reshape(batch * t, d)
    for i in range(depth):
        mixed = _mixer_layer(x2, batch, norm_mix[i], w_in[i], w_out[i], shift_mu[i], gm_ln_w[i], gm_ln_b[i],
                             gm_ws[i], gm_bs[i], rw_w_up[i], rw_w0[i], rw_a_up[i], rw_a0[i], rw_g_up[i],
                             rw_k_k[i], rw_k_a[i], rw_r_k[i], rw_lnx_w[i], rw_lnx_b[i])
        if i % 2 == 0:
            x2 = _out_swiglu(mixed, norm_ffn[i], ffn_w_gate[i // 2], ffn_w_up[i // 2], ffn_w_down[i // 2])
        else:
            x2 = _moe_layer(mixed, norm_ffn[i], moe_router[i // 2], moe_w_gate[i // 2], moe_w_up[i // 2],
                            moe_w_down[i // 2], norm_final)
    return x2.reshape(batch, t, d)
```

```python
import functools

import jax
import jax.numpy as jnp
from jax import lax
from jax.experimental import pallas as pl
from jax.experimental.pallas import tpu as pltpu
from jax.experimental.pallas import tpu_sc as plsc

F32 = jnp.float32
BF16 = jnp.bfloat16

HEAD_DIM = 64
LANES = 128
MXU_TILE = 256
GM_CHUNK = 128
WKV_CHUNK = 64
N_EXPERTS = 8
RMS_EPS = 1e-6
LN_EPS = 1e-5
GN_EPS = 64e-5
NEG_BIG = -1e30

ROW_TILE = 512
MIX_TILE = 1024
WKV_STEP = 512
MOE_BLOCK = 512
SC_WINDOW = 32
VMEM_LIMIT = 58 * 1024 * 1024


def _cparams(*sem):
    return pltpu.CompilerParams(dimension_semantics=sem, vmem_limit_bytes=VMEM_LIMIT)


def _bf(x):
    return x.astype(BF16)


def _dot(a, b):
    return jnp.dot(a, b, preferred_element_type=F32)


def _dot_nt(a, b):
    return lax.dot_general(a, b, (((1,), (1,)), ((), ())), preferred_element_type=F32)


def _split_dot_left(m, x, parts=2):
    acc = None
    rem = x
    for p in range(parts):
        hi = _bf(rem)
        d = _dot(m, hi)
        acc = d if acc is None else acc + d
        if p + 1 < parts:
            rem = rem - hi.astype(F32)
    return acc


def _rmsnorm(x, g):
    return x * lax.rsqrt(jnp.mean(x * x, axis=-1, keepdims=True) + RMS_EPS) * g


def _gelu(x):
    return 0.5 * x * (1.0 + lax.erf(x * (2.0 ** -0.5)))


def _head_avg_matrix():
    r = lax.broadcasted_iota(jnp.int32, (MXU_TILE, MXU_TILE), 0) // HEAD_DIM
    c = lax.broadcasted_iota(jnp.int32, (MXU_TILE, MXU_TILE), 1) // HEAD_DIM
    return jnp.where(r == c, 1.0 / HEAD_DIM, 0.0).astype(BF16)


def _head_dot(x, m):
    return jnp.concatenate([_dot(x[:, i:i + MXU_TILE], m) for i in range(0, x.shape[1], MXU_TILE)], axis=1)


def _head_layernorm(x, avg, w, b, eps):
    mu = _head_dot(_bf(x), avg)
    xc = x - mu
    var = _head_dot(_bf(xc * xc), avg)
    return xc * lax.rsqrt(var + eps) * w + b


def _resident(shape):
    return pl.BlockSpec(shape, lambda *_: (0,) * len(shape), pipeline_mode=pl.Buffered(1))


def _gmlp_chunk(p, avg, lnw, lnb, w2, bias, head0):
    width = p.shape[1] // 2
    u = _gelu(p[:, :width])
    vn = _head_layernorm(_gelu(p[:, width:]), avg, lnw, lnb, LN_EPS)
    mixed = []
    for pair, w in enumerate(w2):
        vp = vn[:, pair * LANES:(pair + 1) * LANES]
        v_st = _bf(jnp.concatenate([jnp.where(head0, vp, 0.0), jnp.where(head0, 0.0, vp)], axis=0))
        mixed.append(_dot(w, v_st))
    return u * (jnp.concatenate(mixed, axis=1) + bias)


def _mixer_in_kernel(x_ref, gain_ref, wgm_ref, wrw_ref, lnw_ref, lnb_ref, ws_ref, bias_ref,
                     mu_ref, wup_ref, w0_ref, aup_ref, a0_ref, gup_ref, kk_ref, ka_ref, rk_ref,
                     ygm_ref, r_ref, c_ref, lw_ref, k_ref, v_ref, a_ref, b_ref, g_ref, bonus_ref, carry_ref):
    width = r_ref.shape[2]
    rows = x_ref.shape[1]

    @pl.when(pl.program_id(1) == 0)
    def _():
        carry_ref[...] = jnp.zeros_like(carry_ref)

    avg = _head_avg_matrix()
    ones = avg * HEAD_DIM
    t = lax.broadcasted_iota(jnp.int32, (GM_CHUNK, GM_CHUNK), 0)
    s = lax.broadcasted_iota(jnp.int32, (GM_CHUNK, GM_CHUNK), 1)
    causal = s <= t
    head0 = lax.broadcasted_iota(jnp.int32, (GM_CHUNK, LANES), 1) < HEAD_DIM
    w2 = [_bf(jnp.concatenate([jnp.where(causal, ws_ref[2 * pair], 0.0),
                               jnp.where(causal, ws_ref[2 * pair + 1], 0.0)], axis=1))
          for pair in range(width // LANES)]
    t = lax.broadcasted_iota(jnp.int32, (MXU_TILE, MXU_TILE), 0)
    s = lax.broadcasted_iota(jnp.int32, (MXU_TILE, MXU_TILE), 1)
    tri = ((t // WKV_CHUNK == s // WKV_CHUNK) & (s <= t)).astype(BF16)
    first = lax.broadcasted_iota(jnp.int32, (ROW_TILE, wrw_ref.shape[1]), 0) == 0

    subs = [slice(i * ROW_TILE, (i + 1) * ROW_TILE) for i in range(rows // ROW_TILE)]
    proj = []
    for rs in subs:
        h = _bf(_rmsnorm(x_ref[0, rs, :], gain_ref[...]))
        proj.append((_dot(h, wgm_ref[...]), _dot(h, wrw_ref[...])))

    last_row = carry_ref[0:1, :]
    for rs, (p_gm, p) in zip(subs, proj):
        for ch in range(ROW_TILE // GM_CHUNK):
            ts = slice(ch * GM_CHUNK, (ch + 1) * GM_CHUNK)
            out_rows = slice(rs.start + ts.start, rs.start + ts.stop)
            ygm_ref[0, out_rows, :] = _bf(_gmlp_chunk(p_gm[ts], avg, lnw_ref[...], lnb_ref[...], w2, bias_ref[...],
                                                      head0))
        prev = jnp.where(first, last_row, pltpu.roll(p, 1, axis=0))
        last_row = p[ROW_TILE - 1:ROW_TILE, :]
        ps = p + (prev - p) * mu_ref[...]
        r = ps[:, :width]
        k = ps[:, width:2 * width]
        v = ps[:, 2 * width:3 * width]
        lora_in = ps[:, 3 * width:3 * width + LANES]
        gd = ps[:, 3 * width + LANES:]
        lw = -(jnp.exp(-0.5)) * jax.nn.sigmoid(w0_ref[...] + _dot(_bf(jnp.tanh(lora_in)), wup_ref[...]))
        a_lr = jax.nn.sigmoid(a0_ref[...] + _dot(_bf(lora_in), aup_ref[...]))
        kk = k * kk_ref[...]
        kk = kk * lax.rsqrt(jnp.maximum(_head_dot(_bf(kk * kk), ones), 1e-24))
        kmod = k * (1.0 + (a_lr - 1.0) * ka_ref[...])
        lw_ref[0, rs, :] = lw
        c_ref[0, rs, :] = jnp.concatenate([_split_dot_left(tri, lw[i:i + MXU_TILE])
                                           for i in range(0, ROW_TILE, MXU_TILE)], axis=0)
        r_ref[0, rs, :] = _bf(r)
        k_ref[0, rs, :] = _bf(kmod)
        v_ref[0, rs, :] = _bf(v)
        a_ref[0, rs, :] = _bf(-kk)
        b_ref[0, rs, :] = _bf(kk * a_lr)
        g_ref[0, rs, :] = _bf(_dot(_bf(jax.nn.sigmoid(gd)), gup_ref[...]))
        bonus_ref[0, rs, :] = _bf(_head_dot(_bf(r * kmod * rk_ref[...]), ones) * v)
    carry_ref[0:1, :] = last_row


def _mixer_in(x3, gain, w_gm, w_rw, ln_w, ln_b, ws, bs, mu, w_up, w0, a_up, a0, g_up, k_k, k_a, r_k):
    batch, t, d = x3.shape
    width = w0.shape[0]
    rank = w_up.shape[0]
    zeros = jnp.zeros((LANES - rank, width), F32)
    wup_pad = _bf(jnp.concatenate([w_up, zeros], axis=0))
    aup_pad = _bf(jnp.concatenate([zeros, a_up], axis=0))
    bias = jnp.repeat(bs.T, HEAD_DIM, axis=1)
    row = lambda z: z.reshape(1, -1)
    params = (row(gain), w_gm, w_rw, row(ln_w), row(ln_b), ws, bias, row(mu), wup_pad, row(w0), aup_pad, row(a0),
              _bf(g_up), row(k_k), row(k_a), row(r_k))
    tile = min(MIX_TILE, t)
    out = pl.BlockSpec((1, tile, width), lambda b, i: (b, i, 0))
    sds = lambda dtype: jax.ShapeDtypeStruct((batch, t, width), dtype)
    return pl.pallas_call(
        _mixer_in_kernel,
        grid=(batch, t // tile),
        in_specs=[pl.BlockSpec((1, tile, d), lambda b, i: (b, i, 0))] + [_resident(z.shape) for z in params],
        out_specs=[out] * 10,
        out_shape=[sds(BF16), sds(BF16), sds(F32), sds(F32)] + [sds(BF16)] * 6,
        scratch_shapes=[pltpu.VMEM((8, w_rw.shape[1]), F32)],
        compiler_params=_cparams("parallel", "arbitrary"),
        name="mixer_in",
    )(x3, *params)


def _wkv_kernel(r_ref, c_ref, lw_ref, k_ref, v_ref, a_ref, b_ref, y_ref, h_ref,
                q_s, v_s, ark_s, bcrb_s, mz_s, rhs_s, sol_s, t_s, rh_s, rkv_s, kv_s, dec_s, hc_s, yc_s):
    @pl.when(pl.program_id(1) == 0)
    def _():
        h_ref[...] = jnp.zeros_like(h_ref)

    n = WKV_CHUNK
    two = 2 * n
    head0 = lax.broadcasted_iota(jnp.int32, (n, LANES), 1) < HEAD_DIM
    row = lax.broadcasted_iota(jnp.int32, (two, two), 0)
    col = lax.broadcasted_iota(jnp.int32, (two, two), 1)
    eye = row == col
    row2 = lax.broadcasted_iota(jnp.int32, (two, 2 * two), 0)
    col2 = lax.broadcasted_iota(jnp.int32, (two, 2 * two), 1) % two
    same2 = (row2 // n) == (col2 // n)
    strict2 = same2 & ((col2 % n) < (row2 % n))
    incl2 = same2 & ((col2 % n) <= (row2 % n))
    npairs = y_ref.shape[2] // LANES
    nchunks = y_ref.shape[1] // n
    items = [(ch, pair) for ch in range(nchunks) for pair in range(npairs)]

    def window(ch, pair):
        return slice(ch * n, (ch + 1) * n), slice(pair * LANES, (pair + 1) * LANES)

    def stack(z):
        return jnp.concatenate([jnp.where(head0, z, 0.0), jnp.where(head0, 0.0, z)], axis=0)

    def dup(z):
        zb = _bf(z)
        return jnp.concatenate([zb, zb], axis=0)

    def scores(it):
        ts, ls = window(*items[it])
        r, k, v, a, b = (ref[0, ts, ls].astype(F32) for ref in (r_ref, k_ref, v_ref, a_ref, b_ref))
        c = c_ref[0, ts, ls]
        cex = c - lw_ref[0, ts, ls]
        mid = c[n // 2 - 1:n // 2, :]
        last = c[n - 1:n, :]
        g_inv = jnp.exp(mid - c)
        g_end = jnp.exp(last - c)
        ar_st = _bf(jnp.concatenate([stack(a * jnp.exp(cex - mid)), stack(r * jnp.exp(c - mid))], axis=0))
        bk_dup = jnp.concatenate([dup(b * g_inv), dup(k * g_inv)], axis=0)
        sc = _dot_nt(ar_st, bk_dup)
        top = jnp.where(strict2, sc[:two], 0.0)
        bot = jnp.where(incl2, sc[two:], 0.0)
        t_s[it] = jnp.where(eye, 1.0, top[:, :two])
        q_s[it] = _bf(top[:, :two])
        bcrb_s[it] = _bf(jnp.concatenate([stack(b * g_end).T, bot[:, :two]], axis=0))
        ark_s[it] = _bf(jnp.concatenate([top[:, two:], bot[:, two:], stack(k * g_end).T], axis=0))
        v_s[it] = _bf(stack(v))
        rh_s[it] = stack(r * jnp.exp(c))
        rhs_s[it, :, :LANES] = _bf(stack(a * jnp.exp(cex)))
        e_col = jnp.sum(jnp.where(eye, jnp.exp(last), 0.0), axis=1, keepdims=True)
        dec_s[it] = jnp.broadcast_to(e_col, (two, LANES))

    def values(it):
        v_st = v_s[it]
        prod = _dot(ark_s[it], v_st)
        rhs_s[it, :, LANES:] = _bf(prod[:two])
        rkv_s[it] = prod[two:2 * two]
        kv_s[it] = prod[2 * two:]

    levels = n.bit_length() - 1

    def inverse_level(level, it):
        q = q_s[it]
        t_acc = t_s[it]
        if level == 1:
            q = _bf(_dot(q, q))
        if level + 1 < levels:
            prod = _dot(q, jnp.concatenate([_bf(t_acc), q], axis=1))
            q_s[it] = _bf(prod[:, two:])
            t_s[it] = t_acc + prod[:, :two]
        else:
            t_s[it] = t_acc + _dot(q, _bf(t_acc))

    def solve(it):
        sol_s[it] = _bf(_dot(_bf(t_s[it]), rhs_s[it]))

    def fold(it):
        prod = _dot(bcrb_s[it], sol_s[it])
        mz_s[it] = _bf(jnp.concatenate([prod[:two, :LANES], rh_s[it] + prod[two:, :LANES]], axis=0))
        hc_s[it] = prod[:two, LANES:] + kv_s[it]
        yc_s[it] = prod[two:, LANES:] + rkv_s[it]

    def advance(it):
        ts, ls = window(*items[it])
        pair = items[it][1]
        h = h_ref[pair]
        gmat = _dot(mz_s[it], _bf(h))
        h_ref[pair] = dec_s[it] * h + gmat[:two] + hc_s[it]
        y_st = gmat[two:] + yc_s[it]
        y_ref[0, ts, ls] = y_st[:n] + y_st[n:]

    stages = ([scores, values] + [functools.partial(inverse_level, level) for level in range(1, levels)]
              + [solve, fold, advance])
    for stage in stages:
        for it in range(len(items)):
            stage(it)


def _wkv(r, c, lw, k, v, a, b):
    batch, t, width = r.shape
    npairs = width // LANES
    items = (WKV_STEP // WKV_CHUNK) * npairs
    spec = pl.BlockSpec((1, WKV_STEP, width), lambda bi, i: (bi, i, 0))
    sq = lambda dtype: pltpu.VMEM((items, LANES, LANES), dtype)
    return pl.pallas_call(
        _wkv_kernel,
        grid=(batch, t // WKV_STEP),
        in_specs=[spec] * 7,
        out_specs=spec,
        out_shape=jax.ShapeDtypeStruct((batch, t, width), F32),
        scratch_shapes=[pltpu.VMEM((npairs, LANES, LANES), F32)] + [sq(BF16)] * 2
                       + [pltpu.VMEM((items, 3 * LANES, LANES), BF16)]
                       + [pltpu.VMEM((items, 2 * LANES, LANES), BF16)] * 2
                       + [pltpu.VMEM((items, LANES, 2 * LANES), BF16)] * 2 + [sq(F32)] * 7,
        compiler_params=_cparams("parallel", "arbitrary"),
        name="wkv7",
    )(r, c, lw, k, v, a, b)


N_MIX_OUT_REFS = 9


def _mixer_out(x_ref, ygm_ref, ys_ref, bonus_ref, g_ref, lnw_ref, lnb_ref, wtop_ref, wbot_ref, rows=slice(None)):
    yn = _head_layernorm(ys_ref[rows, :], _head_avg_matrix(), lnw_ref[...], lnb_ref[...], GN_EPS)
    y_rw = (yn + bonus_ref[rows, :]) * g_ref[rows, :]
    return x_ref[rows, :] + _dot(ygm_ref[rows, :], wtop_ref[...]) + _dot(_bf(y_rw), wbot_ref[...])


def _mixer_out_operands(x2, y_gm, y_scan, bonus, g, lnx_w, lnx_b, w_out, tile=ROW_TILE):
    d = x2.shape[1]
    width = y_gm.shape[1]
    rows = lambda w: pl.BlockSpec((tile, w), lambda i: (i, 0))
    w_out = _bf(w_out)
    specs = [rows(d), rows(width), rows(width), rows(width), rows(width),
             _resident((1, width)), _resident((1, width)), _resident((width, d)), _resident((width, d))]
    args = (x2, y_gm, y_scan, bonus, g, lnx_w.reshape(1, width), lnx_b.reshape(1, width), w_out[:width], w_out[width:])
    assert len(specs) == len(args) == N_MIX_OUT_REFS
    return specs, args


def _out_swiglu_kernel(*refs):
    mix_refs, (gain_ref, wg_ref, wu_ref, wd_ref, o_ref) = refs[:N_MIX_OUT_REFS], refs[N_MIX_OUT_REFS:]
    subs = [slice(i * ROW_TILE, (i + 1) * ROW_TILE) for i in range(o_ref.shape[0] // ROW_TILE)]
    xs = [_mixer_out(*mix_refs, rows=rs) for rs in subs]
    for rs, x in zip(subs, xs):
        h = _bf(_rmsnorm(x, gain_ref[...]))
        act = jax.nn.silu(_dot(h, wg_ref[...])) * _dot(h, wu_ref[...])
        o_ref[rs, :] = x + _dot(_bf(act), wd_ref[...])


def _out_swiglu(mix_args, gain, wg, wu, wd):
    n, d = mix_args[0].shape
    tile = min(MIX_TILE, n)
    specs, args = _mixer_out_operands(*mix_args, tile=tile)
    d_ff = wg.shape[1]
    return pl.pallas_call(
        _out_swiglu_kernel,
        grid=(n // tile,),
        in_specs=specs + [_resident((1, d)), _resident((d, d_ff)), _resident((d, d_ff)), _resident((d_ff, d))],
        out_specs=pl.BlockSpec((tile, d), lambda i: (i, 0)),
        out_shape=jax.ShapeDtypeStruct((n, d), F32),
        compiler_params=_cparams("parallel"),
        name="out_dense_swiglu",
    )(*args, gain.reshape(1, d), _bf(wg), _bf(wu), _bf(wd))


META_E, META_G, META_RANK = 0, 2, 4


def _out_router_kernel(*refs):
    mix_refs = refs[:N_MIX_OUT_REFS]
    g_ref, whi_ref, wlo_ref, x_ref, h_ref, meta_ref, cnt_ref, run_ref = refs[N_MIX_OUT_REFS:]

    @pl.when(pl.program_id(0) == 0)
    def _():
        run_ref[...] = jnp.zeros_like(run_ref)

    lane = lax.broadcasted_iota(jnp.int32, (ROW_TILE, LANES), 1)
    t = lax.broadcasted_iota(jnp.int32, (ROW_TILE, ROW_TILE), 0)
    s = lax.broadcasted_iota(jnp.int32, (ROW_TILE, ROW_TILE), 1)
    earlier = jnp.where(s < t, 1.0, 0.0).astype(BF16)

    subs = [slice(i * ROW_TILE, (i + 1) * ROW_TILE) for i in range(x_ref.shape[0] // ROW_TILE)]
    xs = [_mixer_out(*mix_refs, rows=rs) for rs in subs]

    run = run_ref[0:1, :]
    for rs, x in zip(subs, xs):
        x_ref[rs, :] = x
        h = _rmsnorm(x, g_ref[...])
        h_ref[rs, :] = _pack_bf16_pairs(h)
        h_hi = _bf(h)
        h_lo = _bf(h - h_hi.astype(F32))
        logits = _dot(h_hi, whi_ref[...]) + _dot(h_lo, whi_ref[...]) + _dot(h_hi, wlo_ref[...])
        logits = jnp.where(lane < N_EXPERTS, logits, NEG_BIG)
        v1 = jnp.max(logits, axis=-1, keepdims=True)
        e1 = jnp.min(jnp.where(logits == v1, lane, LANES), axis=-1, keepdims=True)
        oh1 = lane == e1
        rest = jnp.where(oh1, NEG_BIG, logits)
        v2 = jnp.max(rest, axis=-1, keepdims=True)
        e2 = jnp.min(jnp.where(rest == v2, lane, LANES), axis=-1, keepdims=True)
        oh2 = lane == e2
        ex = jnp.exp(v2 - v1)
        g1 = 1.0 / (1.0 + ex)
        g2 = ex / (1.0 + ex)
        cnt = jnp.where(oh1 | oh2, 1.0, 0.0)
        before = _dot(earlier, _bf(cnt)) + run
        rank1 = jnp.sum(jnp.where(oh1, before, 0.0), axis=-1, keepdims=True)
        rank2 = jnp.sum(jnp.where(oh2, before, 0.0), axis=-1, keepdims=True)
        fields = (e1.astype(F32), e2.astype(F32), g1, g2, rank1, rank2)
        meta = jnp.zeros((ROW_TILE, LANES), F32)
        for idx, val in enumerate(fields):
            meta = jnp.where(lane == idx, val, meta)
        meta_ref[rs, :] = meta
        run = run + jnp.sum(cnt, axis=0, keepdims=True)
    run_ref[...] = jnp.broadcast_to(run, run_ref.shape)
    cnt_ref[...] = jnp.broadcast_to(run, cnt_ref.shape)


def _out_router(mix_args, gain, w_router):
    n, d = mix_args[0].shape
    tile = min(MIX_TILE, n)
    specs, args = _mixer_out_operands(*mix_args, tile=tile)
    wr = jnp.concatenate([w_router, jnp.zeros((d, LANES - w_router.shape[1]), F32)], axis=1)
    wr_hi = _bf(wr)
    wr_lo = _bf(wr - wr_hi.astype(F32))
    rows = lambda w: pl.BlockSpec((tile, w), lambda i: (i, 0))
    return pl.pallas_call(
        _out_router_kernel,
        grid=(n // tile,),
        in_specs=specs + [_resident((1, d)), _resident((d, LANES)), _resident((d, LANES))],
        out_specs=[rows(d), rows(d // 2), rows(LANES), pl.BlockSpec((8, LANES), lambda i: (0, 0))],
        out_shape=[jax.ShapeDtypeStruct((n, d), F32), jax.ShapeDtypeStruct((n, d // 2), jnp.uint32),
                   jax.ShapeDtypeStruct((n, LANES), F32), jax.ShapeDtypeStruct((8, LANES), F32)],
        scratch_shapes=[pltpu.VMEM((8, LANES), F32)],
        compiler_params=_cparams("arbitrary"),
        name="out_moe_router",
    )(*args, gain.reshape(1, d), wr_hi, wr_lo)


def _sc_workers():
    info = plsc.get_sparse_core_info()
    return info.num_cores, info.num_cores * info.num_subcores


def _dispatch_rows(h, dest0, dest1, pad_rows):
    nc, workers = _sc_workers()
    n, d = h.shape
    n_pad = pad_rows.shape[0]
    per_worker = n // workers
    nchunk = per_worker // SC_WINDOW
    npad = n_pad // (workers * SC_WINDOW)
    assert n == workers * nchunk * SC_WINDOW and nchunk % 2 == 0 and n_pad == workers * npad * SC_WINDOW
    mesh = plsc.VectorSubcoreMesh(core_axis_name="c", subcore_axis_name="s")
    idx_t = lambda m: pltpu.VMEM((m, SC_WINDOW), jnp.int32)
    buf_t = pltpu.VMEM((SC_WINDOW, d), h.dtype)

    @functools.partial(
        pl.kernel, mesh=mesh,
        out_type=jax.ShapeDtypeStruct((2 * n + n_pad, d), h.dtype),
        scratch_types=[idx_t(nchunk), idx_t(nchunk), idx_t(npad), buf_t, buf_t, buf_t]
                      + [pltpu.SemaphoreType.DMA] * 6,
    )
    def dispatch_kernel(h_hbm, d0_hbm, d1_hbm, pad_hbm, zero_hbm, out_hbm, d0_v, d1_v, pad_v, buf0, buf1, zbuf,
                        r0, r1, w0, w1, x0, x1):
        wid = lax.axis_index("s") * nc + lax.axis_index("c")
        base = wid * per_worker
        pltpu.sync_copy(d0_hbm.at[wid], d0_v)
        pltpu.sync_copy(d1_hbm.at[wid], d1_v)
        pltpu.sync_copy(pad_hbm.at[wid], pad_v)
        pltpu.sync_copy(zero_hbm, zbuf)
        bufs, rsem, wsem, xsem = (buf0, buf1), (r0, r1), (w0, w1), (x0, x1)

        def read(c, slot):
            return pltpu.make_async_copy(h_hbm.at[pl.ds(base + c * SC_WINDOW, SC_WINDOW)], bufs[slot], rsem[slot])

        read(0, 0).start()
        read(1, 1).start()
        for pc in range(npad):
            pltpu.sync_copy(zbuf, out_hbm.at[pad_v.at[pc]])

        @pl.loop(0, nchunk, step=2)
        def _(c):
            for slot in range(2):
                cc = c + slot
                read(cc, slot).wait()
                first = pltpu.make_async_copy(bufs[slot], out_hbm.at[d0_v.at[cc]], wsem[slot])
                second = pltpu.make_async_copy(bufs[slot], out_hbm.at[d1_v.at[cc]], xsem[slot])
                first.start()
                second.start()
                first.wait()
                second.wait()

                @pl.when(cc + 2 < nchunk)
                def _():
                    read(cc + 2, slot).start()

    split = lambda z, m: z.reshape(workers, m, SC_WINDOW)
    return dispatch_kernel(h, split(dest0, nchunk), split(dest1, nchunk), split(pad_rows, npad),
                           jnp.zeros((SC_WINDOW, d), h.dtype))


def _gather_rows(src, idx):
    nc, workers = _sc_workers()
    m, d = idx.shape[0], src.shape[1]
    per_worker = m // workers
    nchunk = per_worker // SC_WINDOW
    assert m == workers * nchunk * SC_WINDOW and nchunk % 2 == 0, "row count must split evenly over subcores"
    mesh = plsc.VectorSubcoreMesh(core_axis_name="c", subcore_axis_name="s")

    @functools.partial(
        pl.kernel, mesh=mesh,
        out_type=jax.ShapeDtypeStruct((m, d), src.dtype),
        scratch_types=[pltpu.VMEM((nchunk, SC_WINDOW), jnp.int32),
                       pltpu.VMEM((SC_WINDOW, d), src.dtype), pltpu.VMEM((SC_WINDOW, d), src.dtype),
                       pltpu.SemaphoreType.DMA, pltpu.SemaphoreType.DMA,
                       pltpu.SemaphoreType.DMA, pltpu.SemaphoreType.DMA],
    )
    def gather_kernel(src_hbm, idx_hbm, out_hbm, idx_v, buf0, buf1, g0, g1, w0, w1):
        wid = lax.axis_index("s") * nc + lax.axis_index("c")
        base = wid * per_worker
        pltpu.sync_copy(idx_hbm.at[wid], idx_v)
        bufs, gsem, wsem = (buf0, buf1), (g0, g1), (w0, w1)

        def gather(c, slot):
            return pltpu.make_async_copy(src_hbm.at[idx_v.at[c]], bufs[slot], gsem[slot])

        def write(c, slot):
            return pltpu.make_async_copy(bufs[slot], out_hbm.at[pl.ds(base + c * SC_WINDOW, SC_WINDOW)], wsem[slot])

        gather(0, 0).start()
        gather(1, 1).start()

        @pl.loop(0, nchunk, step=2)
        def _(c):
            for slot in range(2):
                cc = c + slot
                gather(cc, slot).wait()
                write(cc, slot).start()
                write(cc, slot).wait()

                @pl.when(cc + 2 < nchunk)
                def _():
                    gather(cc + 2, slot).start()

    return gather_kernel(src, idx.reshape(workers, nchunk, SC_WINDOW))


def _pack_bf16_pairs(y):
    bits = lambda z: pltpu.bitcast(z.astype(BF16).astype(F32), jnp.uint32)
    words = []
    for t in range(0, y.shape[1] // LANES, 2):
        lo = bits(y[:, t * LANES:(t + 1) * LANES])
        hi = bits(y[:, (t + 1) * LANES:(t + 2) * LANES])
        words.append((hi & jnp.uint32(0xFFFF0000)) | (lo >> 16))
    return jnp.concatenate(words, axis=1)


def _unpack_bf16_pairs(words):
    tiles = []
    for t in range(words.shape[1] // LANES):
        w = words[:, t * LANES:(t + 1) * LANES]
        tiles += [pltpu.bitcast(w << 16, F32), pltpu.bitcast(w & jnp.uint32(0xFFFF0000), F32)]
    return jnp.concatenate(tiles, axis=1)


def _expert_kernel(be_ref, nu_ref, x_ref, wg_ref, wu_ref, wd_ref, o_ref):
    i = pl.program_id(0)

    @pl.when(i < nu_ref[0])
    def _():
        h = _bf(_unpack_bf16_pairs(x_ref[...]))
        act = jax.nn.silu(_dot(h, wg_ref[0])) * _dot(h, wu_ref[0])
        o_ref[...] = _pack_bf16_pairs(_dot(_bf(act), wd_ref[0]))

    @pl.when(i >= nu_ref[0])
    def _():
        o_ref[...] = jnp.zeros_like(o_ref)


def _expert_swiglu(xb, block_expert, n_used, wg, wu, wd):
    n_rows, d = xb.shape[0], 2 * xb.shape[1]
    d_ff = wg.shape[2]
    nblk = n_rows // MOE_BLOCK

    def row(i, be, nu):
        return jnp.minimum(i, nu[0] - 1)

    def expert(i, be, nu):
        return be[row(i, be, nu)], 0, 0

    grid_spec = pltpu.PrefetchScalarGridSpec(
        num_scalar_prefetch=2,
        grid=(nblk,),
        in_specs=[pl.BlockSpec((MOE_BLOCK, d // 2), lambda i, be, nu: (row(i, be, nu), 0)),
                  pl.BlockSpec((1, d, d_ff), expert),
                  pl.BlockSpec((1, d, d_ff), expert),
                  pl.BlockSpec((1, d_ff, d), expert)],
        out_specs=pl.BlockSpec((MOE_BLOCK, d // 2), lambda i, be, nu: (i, 0)),
    )
    return pl.pallas_call(
        _expert_kernel,
        grid_spec=grid_spec,
        out_shape=jax.ShapeDtypeStruct((n_rows, d // 2), jnp.uint32),
        compiler_params=_cparams("arbitrary"),
        name="expert_swiglu",
    )(block_expert, n_used, xb, _bf(wg), _bf(wu), _bf(wd))


def _combine_kernel(x_ref, y1_ref, y2_ref, meta_ref, g_ref, o_ref):
    meta = meta_ref[...]
    lane = lax.broadcasted_iota(jnp.int32, meta.shape, 1)
    g1 = jnp.sum(jnp.where(lane == META_G, meta, 0.0), axis=-1, keepdims=True)
    g2 = jnp.sum(jnp.where(lane == META_G + 1, meta, 0.0), axis=-1, keepdims=True)
    y = x_ref[...] + (g1 * _unpack_bf16_pairs(y1_ref[...]) + g2 * _unpack_bf16_pairs(y2_ref[...]))
    o_ref[...] = _rmsnorm(y, g_ref[...])


def _combine_final_norm(x2, yg, meta, gain):
    n, d = x2.shape
    nblk = n // ROW_TILE
    return pl.pallas_call(
        _combine_kernel,
        grid=(nblk,),
        in_specs=[pl.BlockSpec((ROW_TILE, d), lambda i: (i, 0)),
                  pl.BlockSpec((ROW_TILE, d // 2), lambda i: (i, 0)),
                  pl.BlockSpec((ROW_TILE, d // 2), lambda i: (i + nblk, 0)),
                  pl.BlockSpec((ROW_TILE, LANES), lambda i: (i, 0)),
                  pl.BlockSpec((1, d), lambda i: (0, 0))],
        out_specs=pl.BlockSpec((ROW_TILE, d), lambda i: (i, 0)),
        out_shape=jax.ShapeDtypeStruct((n, d), F32),
        compiler_params=_cparams("parallel"),
        name="moe_combine_norm",
    )(x2, yg, yg, meta, gain.reshape(1, d))


def _moe_layer(mix_args, gain, w_router, wg, wu, wd, final_gain):
    x2, h, meta, counts = _out_router(mix_args, gain, w_router)
    n, d = x2.shape
    expert = meta[:, META_E:META_E + 2].astype(jnp.int32)
    rank = meta[:, META_RANK:META_RANK + 2].astype(jnp.int32)
    count = counts[0, :N_EXPERTS].astype(jnp.int32)
    padded = (count + MOE_BLOCK - 1) // MOE_BLOCK * MOE_BLOCK
    pad_end = jnp.cumsum(padded)
    pad_start = pad_end - padded
    dest = pad_start[expert] + rank
    n_rows = (2 * n // MOE_BLOCK + N_EXPERTS) * MOE_BLOCK
    empties = padded - count
    e_end = jnp.cumsum(empties)
    i = jnp.arange(n_rows - 2 * n, dtype=jnp.int32)
    grp = jnp.sum(i[:, None] >= e_end[None, :], axis=1).astype(jnp.int32)
    first_empty = jnp.concatenate([pad_start + count, pad_end[-1:]])
    first_index = jnp.concatenate([e_end - empties, e_end[-1:]])
    pad_rows = (first_empty[grp] + i - first_index[grp]).astype(jnp.int32)
    block_start = jnp.arange(n_rows // MOE_BLOCK, dtype=jnp.int32) * MOE_BLOCK
    block_expert = jnp.minimum(jnp.sum(block_start[:, None] >= pad_end[None, :], axis=1), N_EXPERTS - 1)
    block_expert = block_expert.astype(jnp.int32)
    n_used = (pad_end[-1:] // MOE_BLOCK).astype(jnp.int32)
    xb = _dispatch_rows(h, dest[:, 0], dest[:, 1], pad_rows)
    ys = _expert_swiglu(xb, block_expert, n_used, wg, wu, wd)
    yg = _gather_rows(ys, jnp.concatenate([dest[:, 0], dest[:, 1]]))
    return _combine_final_norm(x2, yg, meta, final_gain)


def _mixer_layer(x2, batch, norm_mix, w_in, w_out, shift_mu, gm_ln_w, gm_ln_b, gm_ws, gm_bs,
                 rw_w_up, rw_w0, rw_a_up, rw_a0, rw_g_up, rw_k_k, rw_k_a, rw_r_k, rw_lnx_w, rw_lnx_b):
    n, d = x2.shape
    gm2 = 2 * gm_ln_w.shape[0]
    w_in = _bf(w_in)
    y_gm, r, c, lw, k, v, a, b, g, bonus = _mixer_in(
        x2.reshape(batch, n // batch, d), norm_mix, w_in[:, :gm2], w_in[:, gm2:], gm_ln_w, gm_ln_b, gm_ws, gm_bs,
        shift_mu, rw_w_up, rw_w0, rw_a_up, rw_a0, rw_g_up, rw_k_k, rw_k_a, rw_r_k.reshape(-1))
    y_scan = _wkv(r, c, lw, k, v, a, b)
    flat = lambda z: z.reshape(n, -1)
    return x2, flat(y_gm), flat(y_scan), flat(bonus), flat(g), rw_lnx_w, rw_lnx_b, w_out


def kernel(x, norm_mix, w_in, w_out, shift_mu, gm_ln_w, gm_ln_b, gm_ws, gm_bs, rw_w_up, rw_w0, rw_a_up, rw_a0,
           rw_g_up, rw_k_k, rw_k_a, rw_r_k, rw_lnx_w, rw_lnx_b, norm_ffn, ffn_w_gate, ffn_w_up, ffn_w_down,
           moe_router, moe_w_gate, moe_w_up, moe_w_down, norm_final):
    batch, t, d = x.shape
    depth = norm_mix.shape[0]
    assert depth == 2 and t % ROW_TILE == 0, "two layers (dense then MoE), sequence a multiple of the row tile"
    x2 = x.reshape(batch * t, d)
    for i in range(depth):
        mixed = _mixer_layer(x2, batch, norm_mix[i], w_in[i], w_out[i], shift_mu[i], gm_ln_w[i], gm_ln_b[i],
                             gm_ws[i], gm_bs[i], rw_w_up[i], rw_w0[i], rw_a_up[i], rw_a0[i], rw_g_up[i],
                             rw_k_k[i], rw_k_a[i], rw_r_k[i], rw_lnx_w[i], rw_lnx_b[i])
        if i % 2 == 0:
            x2 = _out_swiglu(mixed, norm_ffn[i], ffn_w_gate[i // 2], ffn_w_up[i // 2], ffn_w_down[i // 2])
        else:
            x2 = _moe_layer(mixed, norm_ffn[i], moe_router[i // 2], moe_w_gate[i // 2], moe_w_up[i // 2],
                            moe_w_down[i // 2], norm_final)
    return x2.reshape(batch, t, d)
```

```python
import functools

import jax
import jax.numpy as jnp
from jax import lax
from jax.experimental import pallas as pl
from jax.experimental.pallas import tpu as pltpu
from jax.experimental.pallas import tpu_sc as plsc

F32 = jnp.float32
BF16 = jnp.bfloat16

HEAD_DIM = 64
LANES = 128
MXU_TILE = 256
GM_CHUNK = 128
WKV_CHUNK = 64
N_EXPERTS = 8
RMS_EPS = 1e-6
LN_EPS = 1e-5
GN_EPS = 64e-5
NEG_BIG = -1e30

ROW_TILE = 512
MIX_TILE = 1024
WKV_STEP = 512
MOE_BLOCK = 512
SC_WINDOW = 32
VMEM_LIMIT = 58 * 1024 * 1024


def _cparams(*sem):
    return pltpu.CompilerParams(dimension_semantics=sem, vmem_limit_bytes=VMEM_LIMIT)


def _bf(x):
    return x.astype(BF16)


def _dot(a, b):
    return jnp.dot(a, b, preferred_element_type=F32)


def _dot_nt(a, b):
    return lax.dot_general(a, b, (((1,), (1,)), ((), ())), preferred_element_type=F32)


def _split_dot_left(m, x, parts=2):
    acc = None
    rem = x
    for p in range(parts):
        hi = _bf(rem)
        d = _dot(m, hi)
        acc = d if acc is None else acc + d
        if p + 1 < parts:
            rem = rem - hi.astype(F32)
    return acc


def _rmsnorm(x, g):
    return x * lax.rsqrt(jnp.mean(x * x, axis=-1, keepdims=True) + RMS_EPS) * g


def _gelu(x):
    return 0.5 * x * (1.0 + lax.erf(x * (2.0 ** -0.5)))


def _head_avg_matrix():
    r = lax.broadcasted_iota(jnp.int32, (MXU_TILE, MXU_TILE), 0) // HEAD_DIM
    c = lax.broadcasted_iota(jnp.int32, (MXU_TILE, MXU_TILE), 1) // HEAD_DIM
    return jnp.where(r == c, 1.0 / HEAD_DIM, 0.0).astype(BF16)


def _head_dot(x, m):
    return jnp.concatenate([_dot(x[:, i:i + MXU_TILE], m) for i in range(0, x.shape[1], MXU_TILE)], axis=1)


def _head_layernorm(x, avg, w, b, eps):
    mu = _head_dot(_bf(x), avg)
    xc = x - mu
    var = _head_dot(_bf(xc * xc), avg)
    return xc * lax.rsqrt(var + eps) * w + b


def _resident(shape):
    return pl.BlockSpec(shape, lambda *_: (0,) * len(shape), pipeline_mode=pl.Buffered(1))


def _gmlp_chunk(p, avg, lnw, lnb, w2, bias, head0):
    width = p.shape[1] // 2
    u = _gelu(p[:, :width])
    vn = _head_layernorm(_gelu(p[:, width:]), avg, lnw, lnb, LN_EPS)
    mixed = []
    for pair, w in enumerate(w2):
        vp = vn[:, pair * LANES:(pair + 1) * LANES]
        v_st = _bf(jnp.concatenate([jnp.where(head0, vp, 0.0), jnp.where(head0, 0.0, vp)], axis=0))
        mixed.append(_dot(w, v_st))
    return u * (jnp.concatenate(mixed, axis=1) + bias)


def _mixer_in_kernel(x_ref, gain_ref, wgm_ref, wrw_ref, lnw_ref, lnb_ref, ws_ref, bias_ref,
                     mu_ref, wup_ref, w0_ref, aup_ref, a0_ref, gup_ref, kk_ref, ka_ref, rk_ref,
                     ygm_ref, r_ref, c_ref, lw_ref, k_ref, v_ref, a_ref, b_ref, g_ref, bonus_ref, carry_ref):
    width = r_ref.shape[2]
    rows = x_ref.shape[1]

    @pl.when(pl.program_id(1) == 0)
    def _():
        carry_ref[...] = jnp.zeros_like(carry_ref)

    avg = _head_avg_matrix()
    ones = avg * HEAD_DIM
    t = lax.broadcasted_iota(jnp.int32, (GM_CHUNK, GM_CHUNK), 0)
    s = lax.broadcasted_iota(jnp.int32, (GM_CHUNK, GM_CHUNK), 1)
    causal = s <= t
    head0 = lax.broadcasted_iota(jnp.int32, (GM_CHUNK, LANES), 1) < HEAD_DIM
    w2 = [_bf(jnp.concatenate([jnp.where(causal, ws_ref[2 * pair], 0.0),
                               jnp.where(causal, ws_ref[2 * pair + 1], 0.0)], axis=1))
          for pair in range(width // LANES)]
    t = lax.broadcasted_iota(jnp.int32, (MXU_TILE, MXU_TILE), 0)
    s = lax.broadcasted_iota(jnp.int32, (MXU_TILE, MXU_TILE), 1)
    tri = ((t // WKV_CHUNK == s // WKV_CHUNK) & (s <= t)).astype(BF16)
    first = lax.broadcasted_iota(jnp.int32, (ROW_TILE, wrw_ref.shape[1]), 0) == 0

    subs = [slice(i * ROW_TILE, (i + 1) * ROW_TILE) for i in range(rows // ROW_TILE)]
    proj = []
    for rs in subs:
        h = _bf(_rmsnorm(x_ref[0, rs, :], gain_ref[...]))
        proj.append((_dot(h, wgm_ref[...]), _dot(h, wrw_ref[...])))

    last_row = carry_ref[0:1, :]
    for rs, (p_gm, p) in zip(subs, proj):
        for ch in range(ROW_TILE // GM_CHUNK):
            ts = slice(ch * GM_CHUNK, (ch + 1) * GM_CHUNK)
            out_rows = slice(rs.start + ts.start, rs.start + ts.stop)
            ygm_ref[0, out_rows, :] = _bf(_gmlp_chunk(p_gm[ts], avg, lnw_ref[...], lnb_ref[...], w2, bias_ref[...],
                                                      head0))
        prev = jnp.where(first, last_row, pltpu.roll(p, 1, axis=0))
        last_row = p[ROW_TILE - 1:ROW_TILE, :]
        ps = p + (prev - p) * mu_ref[...]
        r = ps[:, :width]
        k = ps[:, width:2 * width]
        v = ps[:, 2 * width:3 * width]
        lora_in = ps[:, 3 * width:3 * width + LANES]
        gd = ps[:, 3 * width + LANES:]
        lw = -(jnp.exp(-0.5)) * jax.nn.sigmoid(w0_ref[...] + _dot(_bf(jnp.tanh(lora_in)), wup_ref[...]))
        a_lr = jax.nn.sigmoid(a0_ref[...] + _dot(_bf(lora_in), aup_ref[...]))
        kk = k * kk_ref[...]
        kk = kk * lax.rsqrt(jnp.maximum(_head_dot(_bf(kk * kk), ones), 1e-24))
        kmod = k * (1.0 + (a_lr - 1.0) * ka_ref[...])
        lw_ref[0, rs, :] = lw
        c_ref[0, rs, :] = jnp.concatenate([_split_dot_left(tri, lw[i:i + MXU_TILE])
                                           for i in range(0, ROW_TILE, MXU_TILE)], axis=0)
        r_ref[0, rs, :] = _bf(r)
        k_ref[0, rs, :] = _bf(kmod)
        v_ref[0, rs, :] = _bf(v)
        a_ref[0, rs, :] = _bf(-kk)
        b_ref[0, rs, :] = _bf(kk * a_lr)
        g_ref[0, rs, :] = _bf(_dot(_bf(jax.nn.sigmoid(gd)), gup_ref[...]))
        bonus_ref[0, rs, :] = _bf(_head_dot(_bf(r * kmod * rk_ref[...]), ones) * v)
    carry_ref[0:1, :] = last_row


def _mixer_in(x3, gain, w_gm, w_rw, ln_w, ln_b, ws, bs, mu, w_up, w0, a_up, a0, g_up, k_k, k_a, r_k):
    batch, t, d = x3.shape
    width = w0.shape[0]
    rank = w_up.shape[0]
    zeros = jnp.zeros((LANES - rank, width), F32)
    wup_pad = _bf(jnp.concatenate([w_up, zeros], axis=0))
    aup_pad = _bf(jnp.concatenate([zeros, a_up], axis=0))
    bias = jnp.repeat(bs.T, HEAD_DIM, axis=1)
    row = lambda z: z.reshape(1, -1)
    params = (row(gain), w_gm, w_rw, row(ln_w), row(ln_b), ws, bias, row(mu), wup_pad, row(w0), aup_pad, row(a0),
              _bf(g_up), row(k_k), row(k_a), row(r_k))
    tile = min(MIX_TILE, t)
    out = pl.BlockSpec((1, tile, width), lambda b, i: (b, i, 0))
    sds = lambda dtype: jax.ShapeDtypeStruct((batch, t, width), dtype)
    return pl.pallas_call(
        _mixer_in_kernel,
        grid=(batch, t // tile),
        in_specs=[pl.BlockSpec((1, tile, d), lambda b, i: (b, i, 0))] + [_resident(z.shape) for z in params],
        out_specs=[out] * 10,
        out_shape=[sds(BF16), sds(BF16), sds(F32), sds(F32)] + [sds(BF16)] * 6,
        scratch_shapes=[pltpu.VMEM((8, w_rw.shape[1]), F32)],
        compiler_params=_cparams("parallel", "arbitrary"),
        name="mixer_in",
    )(x3, *params)


def _wkv_kernel(r_ref, c_ref, lw_ref, k_ref, v_ref, a_ref, b_ref, y_ref, h_ref,
                q_s, v_s, ark_s, bcrb_s, mz_s, rhs_s, sol_s, t_s, rh_s, rkv_s, kv_s, dec_s, hc_s, yc_s):
    @pl.when(pl.program_id(1) == 0)
    def _():
        h_ref[...] = jnp.zeros_like(h_ref)

    n = WKV_CHUNK
    two = 2 * n
    head0 = lax.broadcasted_iota(jnp.int32, (n, LANES), 1) < HEAD_DIM
    row = lax.broadcasted_iota(jnp.int32, (two, two), 0)
    col = lax.broadcasted_iota(jnp.int32, (two, two), 1)
    eye = row == col
    row2 = lax.broadcasted_iota(jnp.int32, (two, 2 * two), 0)
    col2 = lax.broadcasted_iota(jnp.int32, (two, 2 * two), 1) % two
    same2 = (row2 // n) == (col2 // n)
    strict2 = same2 & ((col2 % n) < (row2 % n))
    incl2 = same2 & ((col2 % n) <= (row2 % n))
    npairs = y_ref.shape[2] // LANES
    nchunks = y_ref.shape[1] // n
    items = [(ch, pair) for ch in range(nchunks) for pair in range(npairs)]

    def window(ch, pair):
        return slice(ch * n, (ch + 1) * n), slice(pair * LANES, (pair + 1) * LANES)

    def stack(z):
        return jnp.concatenate([jnp.where(head0, z, 0.0), jnp.where(head0, 0.0, z)], axis=0)

    def dup(z):
        zb = _bf(z)
        return jnp.concatenate([zb, zb], axis=0)

    def scores(it):
        ts, ls = window(*items[it])
        r, k, v, a, b = (ref[0, ts, ls].astype(F32) for ref in (r_ref, k_ref, v_ref, a_ref, b_ref))
        c = c_ref[0, ts, ls]
        cex = c - lw_ref[0, ts, ls]
        mid = c[n // 2 - 1:n // 2, :]
        last = c[n - 1:n, :]
        g_inv = jnp.exp(mid - c)
        g_end = jnp.exp(last - c)
        ar_st = _bf(jnp.concatenate([stack(a * jnp.exp(cex - mid)), stack(r * jnp.exp(c - mid))], axis=0))
        bk_dup = jnp.concatenate([dup(b * g_inv), dup(k * g_inv)], axis=0)
        sc = _dot_nt(ar_st, bk_dup)
        top = jnp.where(strict2, sc[:two], 0.0)
        bot = jnp.where(incl2, sc[two:], 0.0)
        t_s[it] = jnp.where(eye, 1.0, top[:, :two])
        q_s[it] = _bf(top[:, :two])
        bcrb_s[it] = _bf(jnp.concatenate([stack(b * g_end).T, bot[:, :two]], axis=0))
        ark_s[it] = _bf(jnp.concatenate([top[:, two:], bot[:, two:], stack(k * g_end).T], axis=0))
        v_s[it] = _bf(stack(v))
        rh_s[it] = stack(r * jnp.exp(c))
        rhs_s[it, :, :LANES] = _bf(stack(a * jnp.exp(cex)))
        e_col = jnp.sum(jnp.where(eye, jnp.exp(last), 0.0), axis=1, keepdims=True)
        dec_s[it] = jnp.broadcast_to(e_col, (two, LANES))

    def values(it):
        v_st = v_s[it]
        prod = _dot(ark_s[it], v_st)
        rhs_s[it, :, LANES:] = _bf(prod[:two])
        rkv_s[it] = prod[two:2 * two]
        kv_s[it] = prod[2 * two:]

    levels = n.bit_length() - 1

    def inverse_level(level, it):
        q = q_s[it]
        t_acc = t_s[it]
        if level == 1:
            q = _bf(_dot(q, q))
        if level + 1 < levels:
            prod = _dot(q, jnp.concatenate([_bf(t_acc), q], axis=1))
            q_s[it] = _bf(prod[:, two:])
            t_s[it] = t_acc + prod[:, :two]
        else:
            t_s[it] = t_acc + _dot(q, _bf(t_acc))

    def solve(it):
        sol_s[it] = _bf(_dot(_bf(t_s[it]), rhs_s[it]))

    def fold(it):
        prod = _dot(bcrb_s[it], sol_s[it])
        mz_s[it] = _bf(jnp.concatenate([prod[:two, :LANES], rh_s[it] + prod[two:, :LANES]], axis=0))
        hc_s[it] = prod[:two, LANES:] + kv_s[it]
        yc_s[it] = prod[two:, LANES:] + rkv_s[it]

    def advance(it):
        ts, ls = window(*items[it])
        pair = items[it][1]
        h = h_ref[pair]
        gmat = _dot(mz_s[it], _bf(h))
        h_ref[pair] = dec_s[it] * h + gmat[:two] + hc_s[it]
        y_st = gmat[two:] + yc_s[it]
        y_ref[0, ts, ls] = y_st[:n] + y_st[n:]

    stages = ([scores, values] + [functools.partial(inverse_level, level) for level in range(1, levels)]
              + [solve, fold, advance])
    for stage in stages:
        for it in range(len(items)):
            stage(it)


def _wkv(r, c, lw, k, v, a, b):
    batch, t, width = r.shape
    npairs = width // LANES
    items = (WKV_STEP // WKV_CHUNK) * npairs
    spec = pl.BlockSpec((1, WKV_STEP, width), lambda bi, i: (bi, i, 0))
    sq = lambda dtype: pltpu.VMEM((items, LANES, LANES), dtype)
    return pl.pallas_call(
        _wkv_kernel,
        grid=(batch, t // WKV_STEP),
        in_specs=[spec] * 7,
        out_specs=spec,
        out_shape=jax.ShapeDtypeStruct((batch, t, width), F32),
        scratch_shapes=[pltpu.VMEM((npairs, LANES, LANES), F32)] + [sq(BF16)] * 2
                       + [pltpu.VMEM((items, 3 * LANES, LANES), BF16)]
                       + [pltpu.VMEM((items, 2 * LANES, LANES), BF16)] * 2
                       + [pltpu.VMEM((items, LANES, 2 * LANES), BF16)] * 2 + [sq(F32)] * 7,
        compiler_params=_cparams("parallel", "arbitrary"),
        name="wkv7",
    )(r, c, lw, k, v, a, b)


N_MIX_OUT_REFS = 9


def _mixer_out(x_ref, ygm_ref, ys_ref, bonus_ref, g_ref, lnw_ref, lnb_ref, wtop_ref, wbot_ref, rows=slice(None)):
    yn = _head_layernorm(ys_ref[rows, :], _head_avg_matrix(), lnw_ref[...], lnb_ref[...], GN_EPS)
    y_rw = (yn + bonus_ref[rows, :]) * g_ref[rows, :]
    return x_ref[rows, :] + _dot(ygm_ref[rows, :], wtop_ref[...]) + _dot(_bf(y_rw), wbot_ref[...])


def _mixer_out_operands(x2, y_gm, y_scan, bonus, g, lnx_w, lnx_b, w_out, tile=ROW_TILE):
    d = x2.shape[1]
    width = y_gm.shape[1]
    rows = lambda w: pl.BlockSpec((tile, w), lambda i: (i, 0))
    w_out = _bf(w_out)
    specs = [rows(d), rows(width), rows(width), rows(width), rows(width),
             _resident((1, width)), _resident((1, width)), _resident((width, d)), _resident((width, d))]
    args = (x2, y_gm, y_scan, bonus, g, lnx_w.reshape(1, width), lnx_b.reshape(1, width), w_out[:width], w_out[width:])
    assert len(specs) == len(args) == N_MIX_OUT_REFS
    return specs, args


def _out_swiglu_kernel(*refs):
    mix_refs, (gain_ref, wg_ref, wu_ref, wd_ref, o_ref) = refs[:N_MIX_OUT_REFS], refs[N_MIX_OUT_REFS:]
    subs = [slice(i * ROW_TILE, (i + 1) * ROW_TILE) for i in range(o_ref.shape[0] // ROW_TILE)]
    xs = [_mixer_out(*mix_refs, rows=rs) for rs in subs]
    for rs, x in zip(subs, xs):
        h = _bf(_rmsnorm(x, gain_ref[...]))
        act = jax.nn.silu(_dot(h, wg_ref[...])) * _dot(h, wu_ref[...])
        o_ref[rs, :] = x + _dot(_bf(act), wd_ref[...])


def _out_swiglu(mix_args, gain, wg, wu, wd):
    n, d = mix_args[0].shape
    tile = min(MIX_TILE, n)
    specs, args = _mixer_out_operands(*mix_args, tile=tile)
    d_ff = wg.shape[1]
    return pl.pallas_call(
        _out_swiglu_kernel,
        grid=(n // tile,),
        in_specs=specs + [_resident((1, d)), _resident((d, d_ff)), _resident((d, d_ff)), _resident((d_ff, d))],
        out_specs=pl.BlockSpec((tile, d), lambda i: (i, 0)),
        out_shape=jax.ShapeDtypeStruct((n, d), F32),
        compiler_params=_cparams("parallel"),
        name="out_dense_swiglu",
    )(*args, gain.reshape(1, d), _bf(wg), _bf(wu), _bf(wd))


META_E, META_G, META_RANK = 0, 2, 4


def _out_router_kernel(*refs):
    mix_refs = refs[:N_MIX_OUT_REFS]
    g_ref, whi_ref, wlo_ref, x_ref, h_ref, meta_ref, cnt_ref, run_ref = refs[N_MIX_OUT_REFS:]

    @pl.when(pl.program_id(0) == 0)
    def _():
        run_ref[...] = jnp.zeros_like(run_ref)

    lane = lax.broadcasted_iota(jnp.int32, (ROW_TILE, LANES), 1)
    t = lax.broadcasted_iota(jnp.int32, (ROW_TILE, ROW_TILE), 0)
    s = lax.broadcasted_iota(jnp.int32, (ROW_TILE, ROW_TILE), 1)
    earlier = jnp.where(s < t, 1.0, 0.0).astype(BF16)

    subs = [slice(i * ROW_TILE, (i + 1) * ROW_TILE) for i in range(x_ref.shape[0] // ROW_TILE)]
    xs = [_mixer_out(*mix_refs, rows=rs) for rs in subs]

    run = run_ref[0:1, :]
    for rs, x in zip(subs, xs):
        x_ref[rs, :] = x
        h = _rmsnorm(x, g_ref[...])
        h_ref[rs, :] = _pack_bf16_pairs(h)
        h_hi = _bf(h)
        h_lo = _bf(h - h_hi.astype(F32))
        logits = _dot(h_hi, whi_ref[...]) + _dot(h_lo, whi_ref[...]) + _dot(h_hi, wlo_ref[...])
        logits = jnp.where(lane < N_EXPERTS, logits, NEG_BIG)
        v1 = jnp.max(logits, axis=-1, keepdims=True)
        e1 = jnp.min(jnp.where(logits == v1, lane, LANES), axis=-1, keepdims=True)
        oh1 = lane == e1
        rest = jnp.where(oh1, NEG_BIG, logits)
        v2 = jnp.max(rest, axis=-1, keepdims=True)
        e2 = jnp.min(jnp.where(rest == v2, lane, LANES), axis=-1, keepdims=True)
        oh2 = lane == e2
        ex = jnp.exp(v2 - v1)
        g1 = 1.0 / (1.0 + ex)
        g2 = ex / (1.0 + ex)
        cnt = jnp.where(oh1 | oh2, 1.0, 0.0)
        before = _dot(earlier, _bf(cnt)) + run
        rank1 = jnp.sum(jnp.where(oh1, before, 0.0), axis=-1, keepdims=True)
        rank2 = jnp.sum(jnp.where(oh2, before, 0.0), axis=-1, keepdims=True)
        fields = (e1.astype(F32), e2.astype(F32), g1, g2, rank1, rank2)
        meta = jnp.zeros((ROW_TILE, LANES), F32)
        for idx, val in enumerate(fields):
            meta = jnp.where(lane == idx, val, meta)
        meta_ref[rs, :] = meta
        run = run + jnp.sum(cnt, axis=0, keepdims=True)
    run_ref[...] = jnp.broadcast_to(run, run_ref.shape)
    cnt_ref[...] = jnp.broadcast_to(run, cnt_ref.shape)


def _out_router(mix_args, gain, w_router):
    n, d = mix_args[0].shape
    tile = min(MIX_TILE, n)
    specs, args = _mixer_out_operands(*mix_args, tile=tile)
    wr = jnp.concatenate([w_router, jnp.zeros((d, LANES - w_router.shape[1]), F32)], axis=1)
    wr_hi = _bf(wr)
    wr_lo = _bf(wr - wr_hi.astype(F32))
    rows = lambda w: pl.BlockSpec((tile, w), lambda i: (i, 0))
    return pl.pallas_call(
        _out_router_kernel,
        grid=(n // tile,),
        in_specs=specs + [_resident((1, d)), _resident((d, LANES)), _resident((d, LANES))],
        out_specs=[rows(d), rows(d // 2), rows(LANES), pl.BlockSpec((8, LANES), lambda i: (0, 0))],
        out_shape=[jax.ShapeDtypeStruct((n, d), F32), jax.ShapeDtypeStruct((n, d // 2), jnp.uint32),
                   jax.ShapeDtypeStruct((n, LANES), F32), jax.ShapeDtypeStruct((8, LANES), F32)],
        scratch_shapes=[pltpu.VMEM((8, LANES), F32)],
        compiler_params=_cparams("arbitrary"),
        name="out_moe_router",
    )(*args, gain.reshape(1, d), wr_hi, wr_lo)


def _sc_workers():
    info = plsc.get_sparse_core_info()
    return info.num_cores, info.num_cores * info.num_subcores


def _dispatch_rows(h, dest0, dest1, pad_rows):
    nc, workers = _sc_workers()
    n, d = h.shape
    n_pad = pad_rows.shape[0]
    per_worker = n // workers
    nchunk = per_worker // SC_WINDOW
    npad = n_pad // (workers * SC_WINDOW)
    assert n == workers * nchunk * SC_WINDOW and nchunk % 2 == 0 and n_pad == workers * npad * SC_WINDOW
    mesh = plsc.VectorSubcoreMesh(core_axis_name="c", subcore_axis_name="s")
    idx_t = lambda m: pltpu.VMEM((m, SC_WINDOW), jnp.int32)
    buf_t = pltpu.VMEM((SC_WINDOW, d), h.dtype)

    @functools.partial(
        pl.kernel, mesh=mesh,
        out_type=jax.ShapeDtypeStruct((2 * n + n_pad, d), h.dtype),
        scratch_types=[idx_t(nchunk), idx_t(nchunk), idx_t(npad), buf_t, buf_t, buf_t]
                      + [pltpu.SemaphoreType.DMA] * 6,
    )
    def dispatch_kernel(h_hbm, d0_hbm, d1_hbm, pad_hbm, zero_hbm, out_hbm, d0_v, d1_v, pad_v, buf0, buf1, zbuf,
                        r0, r1, w0, w1, x0, x1):
        wid = lax.axis_index("s") * nc + lax.axis_index("c")
        base = wid * per_worker
        pltpu.sync_copy(d0_hbm.at[wid], d0_v)
        pltpu.sync_copy(d1_hbm.at[wid], d1_v)
        pltpu.sync_copy(pad_hbm.at[wid], pad_v)
        pltpu.sync_copy(zero_hbm, zbuf)
        bufs, rsem, wsem, xsem = (buf0, buf1), (r0, r1), (w0, w1), (x0, x1)

        def read(c, slot):
            return pltpu.make_async_copy(h_hbm.at[pl.ds(base + c * SC_WINDOW, SC_WINDOW)], bufs[slot], rsem[slot])

        read(0, 0).start()
        read(1, 1).start()
        for pc in range(npad):
            pltpu.sync_copy(zbuf, out_hbm.at[pad_v.at[pc]])

        @pl.loop(0, nchunk, step=2)
        def _(c):
            for slot in range(2):
                cc = c + slot
                read(cc, slot).wait()
                first = pltpu.make_async_copy(bufs[slot], out_hbm.at[d0_v.at[cc]], wsem[slot])
                second = pltpu.make_async_copy(bufs[slot], out_hbm.at[d1_v.at[cc]], xsem[slot])
                first.start()
                second.start()
                first.wait()
                second.wait()

                @pl.when(cc + 2 < nchunk)
                def _():
                    read(cc + 2, slot).start()

    split = lambda z, m: z.reshape(workers, m, SC_WINDOW)
    return dispatch_kernel(h, split(dest0, nchunk), split(dest1, nchunk), split(pad_rows, npad),
                           jnp.zeros((SC_WINDOW, d), h.dtype))


def _gather_rows(src, idx):
    nc, workers = _sc_workers()
    m, d = idx.shape[0], src.shape[1]
    per_worker = m // workers
    nchunk = per_worker // SC_WINDOW
    assert m == workers * nchunk * SC_WINDOW and nchunk % 2 == 0, "row count must split evenly over subcores"
    mesh = plsc.VectorSubcoreMesh(core_axis_name="c", subcore_axis_name="s")

    @functools.partial(
        pl.kernel, mesh=mesh,
        out_type=jax.ShapeDtypeStruct((m, d), src.dtype),
        scratch_types=[pltpu.VMEM((nchunk, SC_WINDOW), jnp.int32),
                       pltpu.VMEM((SC_WINDOW, d), src.dtype), pltpu.VMEM((SC_WINDOW, d), src.dtype),
                       pltpu.SemaphoreType.DMA, pltpu.SemaphoreType.DMA,
                       pltpu.SemaphoreType.DMA, pltpu.SemaphoreType.DMA],
    )
    def gather_kernel(src_hbm, idx_hbm, out_hbm, idx_v, buf0, buf1, g0, g1, w0, w1):
        wid = lax.axis_index("s") * nc + lax.axis_index("c")
        base = wid * per_worker
        pltpu.sync_copy(idx_hbm.at[wid], idx_v)
        bufs, gsem, wsem = (buf0, buf1), (g0, g1), (w0, w1)

        def gather(c, slot):
            return pltpu.make_async_copy(src_hbm.at[idx_v.at[c]], bufs[slot], gsem[slot])

        def write(c, slot):
            return pltpu.make_async_copy(bufs[slot], out_hbm.at[pl.ds(base + c * SC_WINDOW, SC_WINDOW)], wsem[slot])

        gather(0, 0).start()
        gather(1, 1).start()

        @pl.loop(0, nchunk, step=2)
        def _(c):
            for slot in range(2):
                cc = c + slot
                gather(cc, slot).wait()
                write(cc, slot).start()
                write(cc, slot).wait()

                @pl.when(cc + 2 < nchunk)
                def _():
                    gather(cc + 2, slot).start()

    return gather_kernel(src, idx.reshape(workers, nchunk, SC_WINDOW))


def _pack_bf16_pairs(y):
    bits = lambda z: pltpu.bitcast(z.astype(BF16).astype(F32), jnp.uint32)
    words = []
    for t in range(0, y.shape[1] // LANES, 2):
        lo = bits(y[:, t * LANES:(t + 1) * LANES])
        hi = bits(y[:, (t + 1) * LANES:(t + 2) * LANES])
        words.append((hi & jnp.uint32(0xFFFF0000)) | (lo >> 16))
    return jnp.concatenate(words, axis=1)


def _unpack_bf16_pairs(words):
    tiles = []
    for t in range(words.shape[1] // LANES):
        w = words[:, t * LANES:(t + 1) * LANES]
        tiles += [pltpu.bitcast(w << 16, F32), pltpu.bitcast(w & jnp.uint32(0xFFFF0000), F32)]
    return jnp.concatenate(tiles, axis=1)


def _expert_kernel(be_ref, nu_ref, x_ref, wg_ref, wu_ref, wd_ref, o_ref):
    i = pl.program_id(0)

    @pl.when(i < nu_ref[0])
    def _():
        h = _bf(_unpack_bf16_pairs(x_ref[...]))
        act = jax.nn.silu(_dot(h, wg_ref[0])) * _dot(h, wu_ref[0])
        o_ref[...] = _pack_bf16_pairs(_dot(_bf(act), wd_ref[0]))

    @pl.when(i >= nu_ref[0])
    def _():
        o_ref[...] = jnp.zeros_like(o_ref)


def _expert_swiglu(xb, block_expert, n_used, wg, wu, wd):
    n_rows, d = xb.shape[0], 2 * xb.shape[1]
    d_ff = wg.shape[2]
    nblk = n_rows // MOE_BLOCK

    def row(i, be, nu):
        return jnp.minimum(i, nu[0] - 1)

    def expert(i, be, nu):
        return be[row(i, be, nu)], 0, 0

    grid_spec = pltpu.PrefetchScalarGridSpec(
        num_scalar_prefetch=2,
        grid=(nblk,),
        in_specs=[pl.BlockSpec((MOE_BLOCK, d // 2), lambda i, be, nu: (row(i, be, nu), 0)),
                  pl.BlockSpec((1, d, d_ff), expert),
                  pl.BlockSpec((1, d, d_ff), expert),
                  pl.BlockSpec((1, d_ff, d), expert)],
        out_specs=pl.BlockSpec((MOE_BLOCK, d // 2), lambda i, be, nu: (i, 0)),
    )
    return pl.pallas_call(
        _expert_kernel,
        grid_spec=grid_spec,
        out_shape=jax.ShapeDtypeStruct((n_rows, d // 2), jnp.uint32),
        compiler_params=_cparams("arbitrary"),
        name="expert_swiglu",
    )(block_expert, n_used, xb, _bf(wg), _bf(wu), _bf(wd))


def _combine_kernel(x_ref, y1_ref, y2_ref, meta_ref, g_ref, o_ref):
    meta = meta_ref[...]
    lane = lax.broadcasted_iota(jnp.int32, meta.shape, 1)
    g1 = jnp.sum(jnp.where(lane == META_G, meta, 0.0), axis=-1, keepdims=True)
    g2 = jnp.sum(jnp.where(lane == META_G + 1, meta, 0.0), axis=-1, keepdims=True)
    y = x_ref[...] + (g1 * _unpack_bf16_pairs(y1_ref[...]) + g2 * _unpack_bf16_pairs(y2_ref[...]))
    o_ref[...] = _rmsnorm(y, g_ref[...])


def _combine_final_norm(x2, yg, meta, gain):
    n, d = x2.shape
    nblk = n // ROW_TILE
    deep = dict(pipeline_mode=pl.Buffered(3))

    def pipelined(x_hbm, yg_hbm, meta_hbm, g_ref, o_hbm):
        def body(x_ref, y1_ref, y2_ref, meta_ref, o_ref):
            _combine_kernel(x_ref, y1_ref, y2_ref, meta_ref, g_ref, o_ref)

        pltpu.emit_pipeline(
            body,
            grid=(nblk,),
            in_specs=[pl.BlockSpec((ROW_TILE, d), lambda i: (i, 0), **deep),
                      pl.BlockSpec((ROW_TILE, d // 2), lambda i: (i, 0), **deep),
                      pl.BlockSpec((ROW_TILE, d // 2), lambda i: (i + nblk, 0), **deep),
                      pl.BlockSpec((ROW_TILE, LANES), lambda i: (i, 0))],
            out_specs=[pl.BlockSpec((ROW_TILE, d), lambda i: (i, 0))],
        )(x_hbm, yg_hbm, yg_hbm, meta_hbm, o_hbm)

    hbm = pl.BlockSpec(memory_space=pl.ANY)
    return pl.pallas_call(
        pipelined,
        in_specs=[hbm, hbm, hbm, pl.BlockSpec(memory_space=pltpu.VMEM)],
        out_specs=hbm,
        out_shape=jax.ShapeDtypeStruct((n, d), F32),
        compiler_params=pltpu.CompilerParams(vmem_limit_bytes=VMEM_LIMIT),
        name="moe_combine_norm",
    )(x2, yg, meta, gain.reshape(1, d))


def _moe_layer(mix_args, gain, w_router, wg, wu, wd, final_gain):
    x2, h, meta, counts = _out_router(mix_args, gain, w_router)
    n, d = x2.shape
    expert = meta[:, META_E:META_E + 2].astype(jnp.int32)
    rank = meta[:, META_RANK:META_RANK + 2].astype(jnp.int32)
    count = counts[0, :N_EXPERTS].astype(jnp.int32)
    padded = (count + MOE_BLOCK - 1) // MOE_BLOCK * MOE_BLOCK
    pad_end = jnp.cumsum(padded)
    pad_start = pad_end - padded
    dest = pad_start[expert] + rank
    n_rows = (2 * n // MOE_BLOCK + N_EXPERTS) * MOE_BLOCK
    empties = padded - count
    e_end = jnp.cumsum(empties)
    i = jnp.arange(n_rows - 2 * n, dtype=jnp.int32)
    grp = jnp.sum(i[:, None] >= e_end[None, :], axis=1).astype(jnp.int32)
    first_empty = jnp.concatenate([pad_start + count, pad_end[-1:]])
    first_index = jnp.concatenate([e_end - empties, e_end[-1:]])
    pad_rows = (first_empty[grp] + i - first_index[grp]).astype(jnp.int32)
    block_start = jnp.arange(n_rows // MOE_BLOCK, dtype=jnp.int32) * MOE_BLOCK
    block_expert = jnp.minimum(jnp.sum(block_start[:, None] >= pad_end[None, :], axis=1), N_EXPERTS - 1)
    block_expert = block_expert.astype(jnp.int32)
    n_used = (pad_end[-1:] // MOE_BLOCK).astype(jnp.int32)
    xb = _dispatch_rows(h, dest[:, 0], dest[:, 1], pad_rows)
    ys = _expert_swiglu(xb, block_expert, n_used, wg, wu, wd)
    yg = _gather_rows(ys, jnp.concatenate([dest[:, 0], dest[:, 1]]))
    return _combine_final_norm(x2, yg, meta, final_gain)


def _mixer_layer(x2, batch, norm_mix, w_in, w_out, shift_mu, gm_ln_w, gm_ln_b, gm_ws, gm_bs,
                 rw_w_up, rw_w0, rw_a_up, rw_a0, rw_g_up, rw_k_k, rw_k_a, rw_r_k, rw_lnx_w, rw_lnx_b):
    n, d = x2.shape
    gm2 = 2 * gm_ln_w.shape[0]
    w_in = _bf(w_in)
    y_gm, r, c, lw, k, v, a, b, g, bonus = _mixer_in(
        x2.reshape(batch, n // batch, d), norm_mix, w_in[:, :gm2], w_in[:, gm2:], gm_ln_w, gm_ln_b, gm_ws, gm_bs,
        shift_mu, rw_w_up, rw_w0, rw_a_up, rw_a0, rw_g_up, rw_k_k, rw_k_a, rw_r_k.reshape(-1))
    y_scan = _wkv(r, c, lw, k, v, a, b)
    flat = lambda z: z.reshape(n, -1)
    return x2, flat(y_gm), flat(y_scan), flat(bonus), flat(g), rw_lnx_w, rw_lnx_b, w_out


def kernel(x, norm_mix, w_in, w_out, shift_mu, gm_ln_w, gm_ln_b, gm_ws, gm_bs, rw_w_up, rw_w0, rw_a_up, rw_a0,
           rw_g_up, rw_k_k, rw_k_a, rw_r_k, rw_lnx_w, rw_lnx_b, norm_ffn, ffn_w_gate, ffn_w_up, ffn_w_down,
           moe_router, moe_w_gate, moe_w_up, moe_w_down, norm_final):
    batch, t, d = x.shape
    depth = norm_mix.shape[0]
    assert depth == 2 and t % ROW_TILE == 0, "two layers (dense then MoE), sequence a multiple of the row tile"
    x2 = x.reshape(batch * t, d)
    for i in range(depth):
        mixed = _mixer_layer(x2, batch, norm_mix[i], w_in[i], w_out[i], shift_mu[i], gm_ln_w[i], gm_ln_b[i],
                             gm_ws[i], gm_bs[i], rw_w_up[i], rw_w0[i], rw_a_up[i], rw_a0[i], rw_g_up[i],
                             rw_k_k[i], rw_k_a[i], rw_r_k[i], rw_lnx_w[i], rw_lnx_b[i])
        if i % 2 == 0:
            x2 = _out_swiglu(mixed, norm_ffn[i], ffn_w_gate[i // 2], ffn_w_up[i // 2], ffn_w_down[i // 2])
        else:
            x2 = _moe_layer(mixed, norm_ffn[i], moe_router[i // 2], moe_w_gate[i // 2], moe_w_up[i // 2],
                            moe_w_down[i // 2], norm_final)
    return x2.reshape(batch, t, d)
```
